```python
import math
import jax, jax.numpy as jnp
from jax import lax
import numpy as np

D_MODEL = 1024
BATCH = 16
SEQ = 4096
DEPTH = 2
DEC_BATCH = 2
DEC_SEQ = 8192
PAST_LEN = 128

N_META = 16
N_MIXERS = 2
N_HYENA_LAYERS = (DEPTH + 1) // 2
N_ATTN_LAYERS = DEPTH // 2
HY_ORDER = 2
HY_SHORT = 3
HY_EMB = 33
HY_BANDS = (HY_EMB - 1) // 2
HY_HIDDEN = 64
HY_FAST_DECAY = 0.3
HY_SLOW_DECAY = 1.5
HY_DECAY_TARGET = 1e-2
N_HEADS = 16
N_KV_HEADS = 4
HEAD_DIM = D_MODEL // N_HEADS
GROUP = N_HEADS // N_KV_HEADS
WINDOW = 128
BLOCK = 128
D_FF = 4 * D_MODEL
RMS_EPS = 1e-6

kernel_name = "hyena_swa_hybrid_encoder"


def rmsnorm(x, g):
    xf = x.astype(jnp.float32)
    y = xf * lax.rsqrt(jnp.mean(xf * xf, axis=-1, keepdims=True) + RMS_EPS)
    return (y * g.astype(jnp.float32)).astype(x.dtype)


def alibi_slopes():
    return jnp.exp2(-8.0 * jnp.arange(1, N_HEADS + 1, dtype=jnp.float32) / N_HEADS)


def hyena_filters(L, f_w1, f_b1, f_w2, f_b2, f_w3, f_b3, f_wout, f_freq):
    f32 = jnp.float32
    t = jnp.linspace(0.0, 1.0, L, dtype=f32)[:, None]
    w = 2.0 * math.pi * jnp.arange(L, dtype=f32)[:, None] / L
    f = jnp.linspace(1e-4, HY_BANDS - 1, HY_BANDS, dtype=f32)[None, :]
    z = jnp.concatenate([t, jnp.cos(f * w), -jnp.sin(f * w)], axis=-1)
    freq = f_freq.astype(f32)
    h = jnp.sin(freq * (z @ f_w1.astype(f32) + f_b1.astype(f32)))
    h = jnp.sin(freq * (h @ f_w2.astype(f32) + f_b2.astype(f32)))
    h = jnp.sin(freq * (h @ f_w3.astype(f32) + f_b3.astype(f32)))
    h = h @ f_wout.astype(f32)
    max_decay = math.log(HY_DECAY_TARGET) / HY_FAST_DECAY
    min_decay = math.log(HY_DECAY_TARGET) / HY_SLOW_DECAY
    deltas = jnp.linspace(min_decay, max_decay, D_MODEL, dtype=f32)
    decay = jnp.exp(-t * jnp.abs(deltas)[None, :])
    return h.reshape(L, HY_ORDER, 2, D_MODEL) * decay[:, None, None, :]


def bidir_long_conv(u, h_fwd, h_bwd, skip):
    L = u.shape[1]
    n = 2 * L
    c = jnp.concatenate([h_fwd, jnp.zeros((1, D_MODEL), jnp.float32), h_bwd[:0:-1]], axis=0)
    uf = u.astype(jnp.float32)
    y = jnp.fft.irfft(jnp.fft.rfft(uf, n=n, axis=1) * jnp.fft.rfft(c, axis=0)[None], n=n, axis=1)[:, :L]
    return (y + uf * skip.astype(jnp.float32)).astype(u.dtype)


def short_conv(p, w, b):
    L = p.shape[1]
    pp = jnp.pad(p, ((0, 0), (1, 1), (0, 0)))
    return pp[:, :L] * w[0] + pp[:, 1:L + 1] * w[1] + pp[:, 2:] * w[2] + b


def hyena_mixer(u, w_in, conv_w, conv_b, filt, skip, w_out, b_out):
    p = short_conv(u @ w_in, conv_w, conv_b)
    v, g1, g2 = jnp.split(p, 3, axis=-1)
    z = v
    for o, g in enumerate((g1, g2)):
        z = g * bidir_long_conv(z, filt[:, o, 0], filt[:, o, 1], skip[o])
    return z @ w_out + b_out


def window_attention(u, w_qkv, sink, w_o):
    f32 = jnp.float32
    B, L, _ = u.shape
    pad = BLOCK - N_META
    Lp = L + pad
    nb = Lp // BLOCK
    qkv = jnp.pad(u @ w_qkv, ((0, 0), (pad, 0), (0, 0)))
    q, k, v = jnp.split(qkv, [N_HEADS * HEAD_DIM, (N_HEADS + N_KV_HEADS) * HEAD_DIM], axis=-1)
    q = q.reshape(B, nb, BLOCK, N_KV_HEADS, GROUP, HEAD_DIM).astype(f32) * (HEAD_DIM ** -0.5)
    k = k.reshape(B, Lp, N_KV_HEADS, HEAD_DIM)
    v = v.reshape(B, Lp, N_KV_HEADS, HEAD_DIM)
    k_meta = k[:, pad:pad + N_META]
    v_meta = v[:, pad:pad + N_META]

    def band(a):
        ab = jnp.pad(a, ((0, 0), (BLOCK, BLOCK), (0, 0), (0, 0))).reshape(B, nb + 2, BLOCK, N_KV_HEADS, HEAD_DIM)
        return jnp.concatenate([ab[:, :-2], ab[:, 1:-1], ab[:, 2:]], axis=2)

    kb, vb = band(k), band(v)
    q_pos = jnp.arange(Lp).reshape(nb, BLOCK)
    k_pos = jnp.arange(nb)[:, None] * BLOCK - BLOCK + jnp.arange(3 * BLOCK)[None, :]
    m_pos = pad + jnp.arange(N_META)
    dist_band = jnp.abs(q_pos[:, :, None] - k_pos[:, None, :])
    valid = (k_pos >= pad + N_META) & (k_pos < Lp)
    mask = valid[:, None, :] & (dist_band <= WINDOW)
    dist_meta = jnp.abs(q_pos[:, :, None] - m_pos[None, None, :])
    slopes = alibi_slopes().reshape(N_KV_HEADS, GROUP)[None, None, :, :, None, None]

    s_band = jnp.einsum('bnqhgd,bnkhd->bnhgqk', q, kb.astype(f32))
    s_band = s_band - slopes * dist_band.astype(f32)[None, :, None, None]
    s_band = jnp.where(mask[None, :, None, None], s_band, -jnp.inf)
    s_meta = jnp.einsum('bnqhgd,bmhd->bnhgqm', q, k_meta.astype(f32))
    s_meta = s_meta - slopes * dist_meta.astype(f32)[None, :, None, None]
    s_sink = jnp.broadcast_to(sink.astype(f32).reshape(N_KV_HEADS, GROUP)[None, None, :, :, None, None],
                              s_meta.shape[:-1] + (1,))
    p = jax.nn.softmax(jnp.concatenate([s_band, s_meta, s_sink], axis=-1), axis=-1)
    o = (jnp.einsum('bnhgqk,bnkhd->bnqhgd', p[..., :3 * BLOCK].astype(v.dtype), vb)
         + jnp.einsum('bnhgqm,bmhd->bnqhgd', p[..., 3 * BLOCK:3 * BLOCK + N_META].astype(v.dtype), v_meta))
    o = o.reshape(B, Lp, N_HEADS * HEAD_DIM)[:, pad:]
    return o @ w_o


def encoder(x, meta_tokens, norm_mix, norm_mlp, norm_final,
            hy_w_in, hy_conv_w, hy_conv_b, hy_f_w1, hy_f_b1, hy_f_w2, hy_f_b2, hy_f_w3, hy_f_b3,
            hy_f_wout, hy_f_freq, hy_skip, hy_w_out, hy_b_out,
            at_w_qkv, at_sink, at_w_o, mlp_w1, mlp_w2):
    B, S, _ = x.shape
    L = S + N_META
    h = jnp.concatenate([jnp.broadcast_to(meta_tokens[None].astype(x.dtype), (B, N_META, D_MODEL)), x], axis=1)
    for i in range(DEPTH):
        j = i // N_MIXERS
        u = rmsnorm(h, norm_mix[i])
        if i % N_MIXERS == 0:
            filt = hyena_filters(L, hy_f_w1[j], hy_f_b1[j], hy_f_w2[j], hy_f_b2[j], hy_f_w3[j], hy_f_b3[j],
                                 hy_f_wout[j], hy_f_freq[j])
            h = h + hyena_mixer(u, hy_w_in[j], hy_conv_w[j], hy_conv_b[j], filt, hy_skip[j],
                                hy_w_out[j], hy_b_out[j])
        else:
            h = h + window_attention(u, at_w_qkv[j], at_sink[j], at_w_o[j])
        u = rmsnorm(h, norm_mlp[i])
        h = h + jnp.square(jax.nn.relu(u @ mlp_w1[i])) @ mlp_w2[i]
    return rmsnorm(h, norm_final)[:, N_META:]


def setup_inputs(seed: int = 0) -> dict:
    key = jax.random.key(seed)
    ks = jax.random.split(key, 32)
    f32 = jnp.float32
    D = D_MODEL
    NH, NA = N_HYENA_LAYERS, N_ATTN_LAYERS

    def nrm(k, shape, scale):
        return jax.random.normal(k, shape, f32) * scale

    return {
        "x_prompt": nrm(ks[0], (BATCH, SEQ, D), 1.0),
        "x_sample": nrm(ks[1], (DEC_BATCH, DEC_SEQ, D), 1.0),
        "meta_tokens": nrm(ks[2], (N_META, D), 1.0),
        "norm_mix": 1.0 + nrm(ks[3], (DEPTH, D), 0.02),
        "norm_mlp": 1.0 + nrm(ks[4], (DEPTH, D), 0.02),
        "norm_final": 1.0 + nrm(ks[5], (D,), 0.02),
        "hy_w_in": nrm(ks[6], (NH, D, 3 * D), D ** -0.5),
        "hy_conv_w": nrm(ks[7], (NH, HY_SHORT, 3 * D), HY_SHORT ** -0.5),
        "hy_conv_b": nrm(ks[8], (NH, 3 * D), 0.01),
        "hy_f_w1": nrm(ks[9], (NH, HY_EMB, HY_HIDDEN), HY_EMB ** -0.5),
        "hy_f_b1": nrm(ks[10], (NH, HY_HIDDEN), 0.1),
        "hy_f_w2": nrm(ks[11], (NH, HY_HIDDEN, HY_HIDDEN), HY_HIDDEN ** -0.5),
        "hy_f_b2": nrm(ks[12], (NH, HY_HIDDEN), 0.1),
        "hy_f_w3": nrm(ks[13], (NH, HY_HIDDEN, HY_HIDDEN), HY_HIDDEN ** -0.5),
        "hy_f_b3": nrm(ks[14], (NH, HY_HIDDEN), 0.1),
        "hy_f_wout": nrm(ks[15], (NH, HY_HIDDEN, HY_ORDER * 2 * D), 0.005),
        "hy_f_freq": 1.0 + nrm(ks[16], (NH, HY_HIDDEN), 0.01),
        "hy_skip": nrm(ks[17], (NH, HY_ORDER, D), 0.5),
        "hy_w_out": nrm(ks[18], (NH, D, D), D ** -0.5),
        "hy_b_out": nrm(ks[19], (NH, D), 0.01),
        "at_w_qkv": nrm(ks[20], (NA, D, (N_HEADS + 2 * N_KV_HEADS) * HEAD_DIM), D ** -0.5),
        "at_sink": nrm(ks[21], (NA, N_HEADS), 0.5),
        "at_w_o": nrm(ks[22], (NA, N_HEADS * HEAD_DIM, D), (N_HEADS * HEAD_DIM) ** -0.5),
        "mlp_w1": nrm(ks[23], (DEPTH, D, D_FF), D ** -0.5),
        "mlp_w2": nrm(ks[24], (DEPTH, D_FF, D), D_FF ** -0.5),
    }


def reference(x_prompt, x_sample, meta_tokens, norm_mix, norm_mlp, norm_final,
              hy_w_in, hy_conv_w, hy_conv_b, hy_f_w1, hy_f_b1, hy_f_w2, hy_f_b2, hy_f_w3, hy_f_b3,
              hy_f_wout, hy_f_freq, hy_skip, hy_w_out, hy_b_out,
              at_w_qkv, at_sink, at_w_o, mlp_w1, mlp_w2):
    y_prompt = encoder(x_prompt, meta_tokens, norm_mix, norm_mlp, norm_final,
                       hy_w_in, hy_conv_w, hy_conv_b, hy_f_w1, hy_f_b1, hy_f_w2, hy_f_b2, hy_f_w3, hy_f_b3,
                       hy_f_wout, hy_f_freq, hy_skip, hy_w_out, hy_b_out,
                       at_w_qkv, at_sink, at_w_o, mlp_w1, mlp_w2)
    y_sample = encoder(x_sample, meta_tokens, norm_mix, norm_mlp, norm_final,
                       hy_w_in, hy_conv_w, hy_conv_b, hy_f_w1, hy_f_b1, hy_f_w2, hy_f_b2, hy_f_w3, hy_f_b3,
                       hy_f_wout, hy_f_freq, hy_skip, hy_w_out, hy_b_out,
                       at_w_qkv, at_sink, at_w_o, mlp_w1, mlp_w2)
    return (y_prompt, y_sample)
```

```python
import functools
import math

import jax
import jax.numpy as jnp
from jax import lax
from jax.experimental import pallas as pl
from jax.experimental.pallas import tpu as pltpu

F32 = jnp.float32
BF16 = jnp.bfloat16

N_META = 16
RMS_EPS = 1e-6
HY_BANDS = 16
HY_EMB_PAD = 40
HY_FAST_DECAY = 0.3
HY_SLOW_DECAY = 1.5
HY_DECAY_TARGET = 1e-2
ATT_BLOCK = 128
HEAD_DIM = 64
GROUP = 4
MASK_VALUE = -1e30
FF_CHUNK = 1024
LANE_TILE = 128
N2_CHUNK = 16
FFT_MID_LANES = 512
VMEM_LIMIT = 56 * 1024 * 1024
HIGHEST = lax.Precision.HIGHEST


def _cparams(sem):
    return pltpu.CompilerParams(dimension_semantics=sem, vmem_limit_bytes=VMEM_LIMIT)


def _rms(x, g):
    return x * lax.rsqrt(jnp.mean(x * x, axis=-1, keepdims=True) + RMS_EPS) * g


def _norm_matmul_kernel(x_ref, g_ref, w_ref, o_ref):
    u = _rms(x_ref[...], g_ref[...]).astype(BF16)
    o_ref[...] = jnp.dot(u, w_ref[...], preferred_element_type=F32).astype(o_ref.dtype)


def _norm_matmul(x, g, w, tm):
    rows, d = x.shape
    n = w.shape[1]
    return pl.pallas_call(
        _norm_matmul_kernel,
        grid=(rows // tm,),
        in_specs=[pl.BlockSpec((tm, d), lambda i: (i, 0)),
                  pl.BlockSpec((1, d), lambda i: (0, 0)),
                  pl.BlockSpec((d, n), lambda i: (0, 0))],
        out_specs=pl.BlockSpec((tm, n), lambda i: (i, 0)),
        out_shape=jax.ShapeDtypeStruct((rows, n), BF16),
        compiler_params=_cparams(("parallel",)),
        name="norm_matmul",
    )(x, g.reshape(1, d), w)


def _mixer_out_mlp_kernel(h_ref, z_ref, wp_ref, bp_ref, g_ref, w1_ref, w2_ref, gf_ref, o_ref, *, final_norm):
    h = h_ref[...] + jnp.dot(z_ref[...], wp_ref[...], preferred_element_type=F32) + bp_ref[...]
    u = _rms(h, g_ref[...]).astype(BF16)
    acc = h
    d_ff = w1_ref.shape[1]
    for c in range(d_ff // FF_CHUNK):
        a = jnp.dot(u, w1_ref[:, c * FF_CHUNK:(c + 1) * FF_CHUNK], preferred_element_type=F32)
        a = jnp.square(jnp.maximum(a, 0.0)).astype(BF16)
        acc = acc + jnp.dot(a, w2_ref[c * FF_CHUNK:(c + 1) * FF_CHUNK, :], preferred_element_type=F32)
    if final_norm:
        acc = _rms(acc, gf_ref[...])
    o_ref[...] = acc


def _mixer_out_mlp(h, z, wp, bp, g, w1, w2, gf, tm, final_norm):
    rows, d = h.shape
    dz = z.shape[1]
    d_ff = w1.shape[1]
    const = lambda i: (0, 0)
    return pl.pallas_call(
        functools.partial(_mixer_out_mlp_kernel, final_norm=final_norm),
        grid=(rows // tm,),
        in_specs=[pl.BlockSpec((tm, d), lambda i: (i, 0)),
                  pl.BlockSpec((tm, dz), lambda i: (i, 0)),
                  pl.BlockSpec((dz, d), const),
                  pl.BlockSpec((1, d), const),
                  pl.BlockSpec((1, d), const),
                  pl.BlockSpec((d, d_ff), const),
                  pl.BlockSpec((d_ff, d), const),
                  pl.BlockSpec((1, d), const)],
        out_specs=pl.BlockSpec((tm, d), lambda i: (i, 0)),
        out_shape=jax.ShapeDtypeStruct((rows, d), F32),
        compiler_params=_cparams(("parallel",)),
        name="mixer_out_mlp",
    )(h, z, wp, bp.reshape(1, d), g.reshape(1, d), w1, w2, gf.reshape(1, d))


def _filter_kernel(z_ref, t_ref, m_ref, w1_ref, b1_ref, w2_ref, b2_ref, w3_ref, b3_ref, fr_ref, wo_ref,
                   ad_ref, o_ref):
    d = ad_ref.shape[1]
    dot = functools.partial(jnp.dot, precision=HIGHEST, preferred_element_type=F32)
    fr = fr_ref[...]
    h = jnp.sin(fr * (dot(z_ref[...], w1_ref[...]) + b1_ref[...]))
    h = jnp.sin(fr * (dot(h, w2_ref[...]) + b2_ref[...]))
    h = jnp.sin(fr * (dot(h, w3_ref[...]) + b3_ref[...]))
    ho = dot(h, wo_ref[...])
    decay = jnp.exp(-t_ref[...] * ad_ref[...])
    fwd = m_ref[...] > 0.5
    for o in range(2):
        o_ref[o] = jnp.where(fwd, ho[:, 2 * o * d:(2 * o + 1) * d], ho[:, (2 * o + 1) * d:(2 * o + 2) * d]) * decay


def _filter_rows(lag, is_fwd, seq_len, fp, d):
    r = lag.shape[0]
    tr = min(r, 512)
    lagf = lag.astype(F32)
    t = lagf / (seq_len - 1)
    w = 2.0 * math.pi * lagf / seq_len
    f = jnp.linspace(1e-4, HY_BANDS - 1, HY_BANDS, dtype=F32)[None, :]
    z = jnp.concatenate([t[:, None], jnp.cos(f * w[:, None]), -jnp.sin(f * w[:, None]),
                         jnp.zeros((r, HY_EMB_PAD - 2 * HY_BANDS - 1), F32)], axis=-1)
    w1 = jnp.pad(fp["w1"], ((0, HY_EMB_PAD - fp["w1"].shape[0]), (0, 0)))
    hid = w1.shape[1]
    max_decay = math.log(HY_DECAY_TARGET) / HY_FAST_DECAY
    min_decay = math.log(HY_DECAY_TARGET) / HY_SLOW_DECAY
    adel = jnp.abs(jnp.linspace(min_decay, max_decay, d, dtype=F32))[None, :]
    const = lambda i: (0, 0)
    row = lambda i: (i, 0)
    return pl.pallas_call(
        _filter_kernel,
        grid=(r // tr,),
        in_specs=[pl.BlockSpec((tr, HY_EMB_PAD), row), pl.BlockSpec((tr, 1), row), pl.BlockSpec((tr, 1), row),
                  pl.BlockSpec((HY_EMB_PAD, hid), const), pl.BlockSpec((1, hid), const),
                  pl.BlockSpec((hid, hid), const), pl.BlockSpec((1, hid), const),
                  pl.BlockSpec((hid, hid), const), pl.BlockSpec((1, hid), const),
                  pl.BlockSpec((1, hid), const), pl.BlockSpec((hid, 4 * d), const),
                  pl.BlockSpec((1, d), const)],
        out_specs=pl.BlockSpec((2, tr, d), lambda i: (0, i, 0)),
        out_shape=jax.ShapeDtypeStruct((2, r, d), F32),
        compiler_params=_cparams(("parallel",)),
        name="hyena_filter",
    )(z, t[:, None], is_fwd.astype(F32)[:, None], w1, fp["b1"][None], fp["w2"], fp["b2"][None],
      fp["w3"], fp["b3"][None], fp["freq"][None], fp["wout"], adel)


def _cplx_block(re, im):
    return jnp.concatenate([jnp.concatenate([re, -im], axis=-1), jnp.concatenate([im, re], axis=-1)], axis=-2)


def _fft_tables(n1, n2):
    n = n1 * n2
    n1h = n1 // 2
    pad = 8 - 1
    k1 = jnp.arange(n1, dtype=jnp.int32)
    c2 = jnp.arange(n2, dtype=jnp.int32)
    cols = jnp.concatenate([jnp.arange(n1h, dtype=jnp.int32), jnp.array([n1 - 1], jnp.int32)])
    pos = n2 * cols[None, None, :] + c2[:, None, None]
    ang = ((k1[None, :, None] * pos) % n).astype(F32) * (-2.0 * math.pi / n)
    wr = jnp.pad(jnp.cos(ang), ((0, 0), (0, 0), (0, pad)))
    wi = jnp.pad(jnp.sin(ang), ((0, 0), (0, 0), (0, pad)))
    fwd1 = _cplx_block(wr, wi)
    inv1 = _cplx_block(jnp.swapaxes(wr, 1, 2), -jnp.swapaxes(wi, 1, 2)) / n
    posf = n2 * k1[None, None, :] + c2[:, None, None]
    angf = ((k1[None, :, None] * posf) % n).astype(F32) * (-2.0 * math.pi / n)
    fil1 = jnp.concatenate([jnp.cos(angf), jnp.sin(angf)], axis=1)
    ang2 = ((c2[:, None] * c2[None, :]) % n2).astype(F32) * (-2.0 * math.pi / n2)
    fr, fi = jnp.cos(ang2), jnp.sin(ang2)
    return dict(fwd1=fwd1.astype(BF16), inv1=inv1.astype(BF16), fil1=fil1,
                fwd2=_cplx_block(fr, fi).astype(BF16), inv2=_cplx_block(fr, -fi).astype(BF16),
                fwd2_f32=_cplx_block(fr, fi))


def _ordered_short_conv(meta, real, cw_ref, cb_ref, seq_ref):
    s = real.shape[0]
    dt = real.shape[1]
    seq_ref[0:8, :] = jnp.zeros((8, dt), F32)
    seq_ref[8:8 + N_META, :] = meta.astype(F32)
    seq_ref[8 + N_META:8 + N_META + s, :] = real.astype(F32)
    seq_ref[8 + N_META + s:16 + N_META + s, :] = jnp.zeros((8, dt), F32)
    x = seq_ref[...]
    rows = x.shape[0]
    cw = cw_ref[...].astype(F32)
    y = (pltpu.roll(x, 1, axis=0) * cw[0:1] + x * cw[1:2] + pltpu.roll(x, rows - 1, axis=0) * cw[2:3]
         + cb_ref[...].astype(F32))
    return y[8:8 + N_META], y[8 + N_META:8 + N_META + s]


def _fft_in_kernel(*refs, n1, n2, short_conv):
    if short_conv:
        (xr_ref, xm_ref, cw_ref, cb_ref, f_ref, are_ref, aim_ref, vr_ref, vm_ref,
         xs_ref, mp_ref, seq_ref) = refs
    else:
        xr_ref, xm_ref, f_ref, are_ref, aim_ref, xs_ref, mp_ref = refs
    n1h = n1 // 2
    chunk = pl.program_id(2)

    @pl.when(chunk == 0)
    def _prepare():
        for e in range(2):
            if short_conv:
                meta, real = _ordered_short_conv(xm_ref[e], xr_ref[e], cw_ref, cb_ref, seq_ref)
                vr_ref[e] = real.astype(vr_ref.dtype)
                vm_ref[e] = meta.astype(vm_ref.dtype)
            else:
                meta, real = xm_ref[e].astype(F32), xr_ref[e].astype(F32)
            xs_ref[e] = real
            mp_ref[e] = jnp.zeros(mp_ref.shape[1:], F32)
            for j in range(N_META):
                mp_ref[e, pl.ds(8 * (n2 - N_META + j), 1), :] = meta[j:j + 1]

    for i in range(N2_CHUNK):
        c2 = chunk * N2_CHUNK + i
        parts = []
        for e in range(2):
            parts.append(xs_ref[e, pl.ds(c2, n1h, stride=n2), :])
            parts.append(mp_ref[e, pl.ds(pl.multiple_of(c2 * 8, 8), 8), :])
        rhs = jnp.concatenate(parts, axis=0).astype(BF16)
        out = jnp.dot(f_ref[i], rhs, preferred_element_type=F32)
        are_ref[i] = out[:n1]
        aim_ref[i] = out[n1:]


def _fft_in(xr, xm, col_off, conv_w, conv_b, tab, *, b, s, d, n1, n2, dt):
    short_conv = conv_w is not None
    pairs = b // 2
    c = xr.shape[1]
    cb0 = col_off // dt
    xr4 = xr.reshape(pairs, 2, s, c)
    xm4 = xm.reshape(pairs, 2, N_META, c)
    kk = tab["fwd1"].shape[2]
    in_specs = [pl.BlockSpec((None, 2, s, dt), lambda p, j, t: (p, 0, 0, cb0 + j)),
                pl.BlockSpec((None, 2, N_META, dt), lambda p, j, t: (p, 0, 0, cb0 + j))]
    args = [xr4, xm4]
    if short_conv:
        in_specs += [pl.BlockSpec((8, dt), lambda p, j, t: (0, cb0 + j)),
                     pl.BlockSpec((1, dt), lambda p, j, t: (0, cb0 + j))]
        args += [conv_w, conv_b]
    in_specs.append(pl.BlockSpec((N2_CHUNK, 2 * n1, kk), lambda p, j, t: (t, 0, 0)))
    args.append(tab["fwd1"])
    a_spec = pl.BlockSpec((None, N2_CHUNK, n1, dt), lambda p, j, t: (p, t, 0, j))
    a_shape = jax.ShapeDtypeStruct((pairs, n2, n1, d), F32)
    out_specs = [a_spec, a_spec]
    out_shape = [a_shape, a_shape]
    scratch = [pltpu.VMEM((2, s, dt), F32), pltpu.VMEM((2, 8 * n2, dt), F32)]
    if short_conv:
        out_specs += [pl.BlockSpec((None, 2, s, dt), lambda p, j, t: (p, 0, 0, j)),
                      pl.BlockSpec((None, 2, N_META, dt), lambda p, j, t: (p, 0, 0, j))]
        out_shape += [jax.ShapeDtypeStruct((pairs, 2, s, d), BF16),
                      jax.ShapeDtypeStruct((pairs, 2, N_META, d), BF16)]
        scratch.append(pltpu.VMEM((s + N_META + 16, dt), F32))
    return pl.pallas_call(
        functools.partial(_fft_in_kernel, n1=n1, n2=n2, short_conv=short_conv),
        grid=(pairs, d // dt, n2 // N2_CHUNK),
        in_specs=in_specs, out_specs=out_specs, out_shape=out_shape, scratch_shapes=scratch,
        compiler_params=_cparams(("parallel", "parallel", "arbitrary")),
        name="fft_in",
    )(*args)


def _fft_mid_kernel(are_ref, aim_ref, kre_ref, kim_ref, f_ref, g_ref, bre_ref, bim_ref, *, n2, slabs):
    for i in range(slabs):
        x = jnp.concatenate([are_ref[:, i, :], aim_ref[:, i, :]], axis=0).astype(BF16)
        z = jnp.dot(f_ref[...], x, preferred_element_type=F32)
        zr, zi = z[:n2], z[n2:]
        kr, ki = kre_ref[:, i, :], kim_ref[:, i, :]
        p = jnp.concatenate([zr * kr - zi * ki, zr * ki + zi * kr], axis=0).astype(BF16)
        y = jnp.dot(g_ref[...], p, preferred_element_type=F32)
        bre_ref[:, i, :] = y[:n2]
        bim_ref[:, i, :] = y[n2:]


def _fft_mid(are, aim, kre, kim, tab, *, dt, slabs):
    pairs, n2, n1, d = are.shape
    blk = (None, n2, slabs, dt)
    amap = lambda j, kb, p: (p, 0, kb, j)
    kmap = lambda j, kb, p: (0, kb, j)
    const = lambda j, kb, p: (0, 0)
    shp = jax.ShapeDtypeStruct((pairs, n2, n1, d), F32)
    return pl.pallas_call(
        functools.partial(_fft_mid_kernel, n2=n2, slabs=slabs),
        grid=(d // dt, n1 // slabs, pairs),
        in_specs=[pl.BlockSpec(blk, amap), pl.BlockSpec(blk, amap),
                  pl.BlockSpec((n2, slabs, dt), kmap), pl.BlockSpec((n2, slabs, dt), kmap),
                  pl.BlockSpec((2 * n2, 2 * n2), const), pl.BlockSpec((2 * n2, 2 * n2), const)],
        out_specs=[pl.BlockSpec(blk, amap), pl.BlockSpec(blk, amap)],
        out_shape=[shp, shp],
        compiler_params=_cparams(("parallel", "parallel", "arbitrary")),
        name="fft_mid",
    )(are, aim, kre, kim, tab["fwd2"], tab["inv2"])


def _filter_fft_in_kernel(c_ref, f_ref, are_ref, aim_ref, *, n1, n2):
    chunk = pl.program_id(2)
    for i in range(N2_CHUNK):
        rhs = c_ref[pl.ds(chunk * N2_CHUNK + i, n1, stride=n2), :]
        out = jnp.dot(f_ref[i], rhs, precision=HIGHEST, preferred_element_type=F32)
        are_ref[i] = out[:n1]
        aim_ref[i] = out[n1:]


def _filter_fft_mid_kernel(are_ref, aim_ref, f_ref, kre_ref, kim_ref, *, n2, slabs):
    for i in range(slabs):
        x = jnp.concatenate([are_ref[:, i, :], aim_ref[:, i, :]], axis=0)
        z = jnp.dot(f_ref[...], x, precision=HIGHEST, preferred_element_type=F32)
        kre_ref[:, i, :] = z[:n2]
        kim_ref[:, i, :] = z[n2:]


def _filter_spectrum(circ, tab, *, n1, n2, dt, slabs):
    orders, n, d = circ.shape
    shp = [jax.ShapeDtypeStruct((orders, n2, n1, d), F32)] * 2
    are, aim = pl.pallas_call(
        functools.partial(_filter_fft_in_kernel, n1=n1, n2=n2),
        grid=(orders, d // dt, n2 // N2_CHUNK),
        in_specs=[pl.BlockSpec((None, n, dt), lambda o, j, t: (o, 0, j)),
                  pl.BlockSpec((N2_CHUNK, 2 * n1, n1), lambda o, j, t: (t, 0, 0))],
        out_specs=[pl.BlockSpec((None, N2_CHUNK, n1, dt), lambda o, j, t: (o, t, 0, j))] * 2,
        out_shape=shp,
        compiler_params=_cparams(("parallel", "parallel", "arbitrary")),
        name="filter_fft_in",
    )(circ, tab["fil1"])
    blk = (None, n2, slabs, dt)
    amap = lambda o, kb, j: (o, 0, kb, j)
    return pl.pallas_call(
        functools.partial(_filter_fft_mid_kernel, n2=n2, slabs=slabs),
        grid=(orders, n1 // slabs, d // dt),
        in_specs=[pl.BlockSpec(blk, amap), pl.BlockSpec(blk, amap),
                  pl.BlockSpec((2 * n2, 2 * n2), lambda o, kb, j: (0, 0))],
        out_specs=[pl.BlockSpec(blk, amap)] * 2,
        out_shape=shp,
        compiler_params=_cparams(("parallel", "parallel", "parallel")),
        name="filter_fft_mid",
    )(are, aim, tab["fwd2_f32"])


def _alias_patch(tail, u_meta, u_last):
    dfw = tail[0:16] - tail[16:32]
    dbw = tail[32:48] - tail[48:64]
    ridx = lax.broadcasted_iota(jnp.int32, dfw.shape, 0)
    real_fix = jnp.zeros_like(dfw)
    meta_fix = jnp.zeros_like(dfw)
    for o in range(15):
        shifted = jnp.where(ridx >= o + 1, pltpu.roll(dfw, o + 1, axis=0), 0.0)
        real_fix = real_fix + shifted * u_meta[o:o + 1]
    for c in range(16):
        src = u_last if c == 0 else pltpu.roll(u_last, 16 - c, axis=0)
        meta_fix = meta_fix + jnp.where(ridx + c <= 15, src, 0.0) * dbw[c:c + 1]
    return real_fix, meta_fix


def _fft_out_kernel(bre_ref, bim_ref, g_ref, gr_ref, gm_ref, cw_ref, cb_ref, ur_ref, um_ref, skip_ref,
                    tail_ref, zr_ref, zm_ref, ys_ref, yp_ref, seq_ref, *, n1, n2):
    n1h = n1 // 2
    kk = n1h + 8
    s = ys_ref.shape[1]
    chunk = pl.program_id(2)

    for i in range(N2_CHUNK):
        c2 = chunk * N2_CHUNK + i
        rhs = jnp.concatenate([bre_ref[i], bim_ref[i]], axis=0).astype(BF16)
        y = jnp.dot(g_ref[i], rhs, preferred_element_type=F32)
        for e in range(2):
            ys_ref[e, pl.ds(c2, n1h, stride=n2), :] = y[e * kk:e * kk + n1h]
            yp_ref[e, pl.ds(pl.multiple_of(c2 * 8, 8), 8), :] = y[e * kk + n1h:(e + 1) * kk]

    @pl.when(chunk == n2 // N2_CHUNK - 1)
    def _gate():
        tail = tail_ref[...]
        skip = skip_ref[...]
        for e in range(2):
            y_meta = jnp.concatenate(
                [yp_ref[e, pl.ds(8 * (n2 - N_META + j), 1), :] for j in range(N_META)], axis=0)
            u_real = ur_ref[e].astype(F32)
            u_meta = um_ref[e].astype(F32)
            real_fix, meta_fix = _alias_patch(tail, u_meta, u_real[s - 16:s])
            ys_ref[e, s - 16:s, :] = ys_ref[e, s - 16:s, :] + real_fix
            g_meta, g_real = _ordered_short_conv(gm_ref[e], gr_ref[e], cw_ref, cb_ref, seq_ref)
            zr_ref[e] = (g_real * (ys_ref[e] + skip * u_real)).astype(zr_ref.dtype)
            zm_ref[e] = (g_meta * (y_meta + meta_fix + skip * u_meta)).astype(zm_ref.dtype)


def _fft_out(bre, bim, tab, gr, gm, gate_off, conv_w, conv_b, ur, um, skip, tail, *, b, s, d, n1, n2, dt):
    pairs = b // 2
    c = gr.shape[1]
    gb0 = gate_off // dt
    kk2 = tab["inv1"].shape[1]
    gr4 = gr.reshape(pairs, 2, s, c)
    gm4 = gm.reshape(pairs, 2, N_META, c)
    ur4 = ur.reshape(pairs, 2, s, d)
    um4 = um.reshape(pairs, 2, N_META, d)
    b_spec = pl.BlockSpec((None, N2_CHUNK, n1, dt), lambda p, j, t: (p, t, 0, j))
    zr, zm = pl.pallas_call(
        functools.partial(_fft_out_kernel, n1=n1, n2=n2),
        grid=(pairs, d // dt, n2 // N2_CHUNK),
        in_specs=[b_spec, b_spec,
                  pl.BlockSpec((N2_CHUNK, kk2, 2 * n1), lambda p, j, t: (t, 0, 0)),
                  pl.BlockSpec((None, 2, s, dt), lambda p, j, t: (p, 0, 0, gb0 + j)),
                  pl.BlockSpec((None, 2, N_META, dt), lambda p, j, t: (p, 0, 0, gb0 + j)),
                  pl.BlockSpec((8, dt), lambda p, j, t: (0, gb0 + j)),
                  pl.BlockSpec((1, dt), lambda p, j, t: (0, gb0 + j)),
                  pl.BlockSpec((None, 2, s, dt), lambda p, j, t: (p, 0, 0, j)),
                  pl.BlockSpec((None, 2, N_META, dt), lambda p, j, t: (p, 0, 0, j)),
                  pl.BlockSpec((1, dt), lambda p, j, t: (0, j)),
                  pl.BlockSpec((64, dt), lambda p, j, t: (0, j))],
        out_specs=[pl.BlockSpec((None, 2, s, dt), lambda p, j, t: (p, 0, 0, j)),
                   pl.BlockSpec((None, 2, N_META, dt), lambda p, j, t: (p, 0, 0, j))],
        out_shape=[jax.ShapeDtypeStruct((pairs, 2, s, d), BF16),
                   jax.ShapeDtypeStruct((pairs, 2, N_META, d), BF16)],
        scratch_shapes=[pltpu.VMEM((2, s, dt), F32), pltpu.VMEM((2, 8 * n2, dt), F32),
                        pltpu.VMEM((s + N_META + 16, dt), F32)],
        compiler_params=_cparams(("parallel", "parallel", "arbitrary")),
        name="fft_out",
    )(bre, bim, tab["inv1"], gr4, gm4, conv_w, conv_b, ur4, um4, skip.reshape(1, d), tail)
    return zr.reshape(b * s, d), zm.reshape(b * N_META, d)


def _attn_kernel(own_ref, prev_ref, next_ref, meta_ref, bias_ref, shift_ref, sink_ref, rep_ref, o_ref, *, groups):
    blk = pl.program_id(1).astype(F32)
    scale = HEAD_DIM ** -0.5
    gw = GROUP * HEAD_DIM
    stride = gw + 2 * HEAD_DIM
    lane_head = lax.broadcasted_iota(jnp.int32, (1, gw), 1) // HEAD_DIM
    col = lax.broadcasted_iota(jnp.int32, (1, 4 * ATT_BLOCK), 1)
    is_meta = jnp.logical_and(col >= 3 * ATT_BLOCK, col < 3 * ATT_BLOCK + N_META)
    pad_rows = jnp.zeros((ATT_BLOCK - N_META, 2 * HEAD_DIM), BF16)
    for g in range(groups):
        q = own_ref[:, g * stride:g * stride + gw]
        kv_cols = slice(g * stride + gw, (g + 1) * stride)
        kv = jnp.concatenate([prev_ref[:, kv_cols], own_ref[:, kv_cols], next_ref[:, kv_cols],
                              meta_ref[:, kv_cols], pad_rows], axis=0)
        kv_rep = jnp.dot(kv, rep_ref[...], preferred_element_type=F32).astype(BF16)
        k_rep, v_rep = kv_rep[:, :gw], kv_rep[:, gw:]
        acc = jnp.zeros((ATT_BLOCK, gw), F32)
        for hh in range(GROUP):
            head = g * GROUP + hh
            mine = lane_head == hh
            qh = jnp.where(mine, q, jnp.zeros_like(q))
            sc = lax.dot_general(qh, k_rep, (((1,), (1,)), ((), ())), preferred_element_type=F32)
            sc = sc * scale + bias_ref[head] - jnp.where(is_meta, shift_ref[head][:, 0:1] * blk, 0.0)
            sink = sink_ref[head][:, 0:1]
            m = jnp.maximum(jnp.max(sc, axis=1, keepdims=True), sink)
            p = jnp.exp(sc - m)
            den = jnp.sum(p, axis=1, keepdims=True) + jnp.exp(sink - m)
            vh = jnp.where(mine, v_rep, jnp.zeros_like(v_rep))
            acc = acc + jnp.dot(p.astype(BF16), vh, preferred_element_type=F32) * (1.0 / den)
        o_ref[:, g * gw:(g + 1) * gw] = acc.astype(o_ref.dtype)


def _attention_tables(n_heads, sink):
    slopes = jnp.exp2(-8.0 * jnp.arange(1, n_heads + 1, dtype=F32) / n_heads)
    i = jnp.arange(ATT_BLOCK, dtype=jnp.int32)[:, None]
    c = jnp.arange(4 * ATT_BLOCK, dtype=jnp.int32)[None, :]
    rel = c - ATT_BLOCK - i
    dist = jnp.abs(rel)
    key_blk = c // ATT_BLOCK
    in_band = jnp.logical_and(c < 3 * ATT_BLOCK, dist <= ATT_BLOCK)
    meta_col = jnp.logical_and(c >= 3 * ATT_BLOCK, c < 3 * ATT_BLOCK + N_META)
    meta_dist = N_META + i - (c - 3 * ATT_BLOCK)
    tables = []
    for drop in (None, 0, 2):
        ok = in_band if drop is None else jnp.logical_and(in_band, key_blk != drop)
        d_eff = jnp.where(ok, dist, jnp.where(meta_col, meta_dist, 0)).astype(F32)
        live = jnp.logical_or(ok, meta_col)
        tables.append(jnp.where(live[None], -slopes[:, None, None] * d_eff[None], MASK_VALUE))
    bias = jnp.stack(tables, axis=0)
    shift = jnp.broadcast_to((slopes * ATT_BLOCK)[:, None, None], (n_heads, 1, LANE_TILE))
    sinkb = jnp.broadcast_to(sink.astype(F32)[:, None, None], (n_heads, 1, LANE_TILE))
    r = jnp.arange(2 * HEAD_DIM, dtype=jnp.int32)[:, None]
    cc = jnp.arange(2 * GROUP * HEAD_DIM, dtype=jnp.int32)[None, :]
    rep = jnp.logical_and(r // HEAD_DIM == cc // (GROUP * HEAD_DIM), r % HEAD_DIM == cc % HEAD_DIM)
    return bias, shift, sinkb, rep.astype(BF16)


def _attention(qkv_r, qkv_m, sink, *, b, s, n_heads):
    groups = n_heads // GROUP
    nblk = s // ATT_BLOCK
    width = qkv_r.shape[1]
    d_out = n_heads * HEAD_DIM
    bias, shift, sinkb, rep = _attention_tables(n_heads, sink)

    def variant(j):
        return jnp.where(j == 0, 1, jnp.where(j == nblk - 1, 2, 0))

    return pl.pallas_call(
        functools.partial(_attn_kernel, groups=groups),
        grid=(b, nblk),
        in_specs=[pl.BlockSpec((ATT_BLOCK, width), lambda i, j: (i * nblk + j, 0)),
                  pl.BlockSpec((ATT_BLOCK, width), lambda i, j: (i * nblk + jnp.maximum(j - 1, 0), 0)),
                  pl.BlockSpec((ATT_BLOCK, width), lambda i, j: (i * nblk + jnp.minimum(j + 1, nblk - 1), 0)),
                  pl.BlockSpec((N_META, width), lambda i, j: (i, 0)),
                  pl.BlockSpec((None, n_heads, ATT_BLOCK, 4 * ATT_BLOCK), lambda i, j: (variant(j), 0, 0, 0)),
                  pl.BlockSpec((n_heads, 1, LANE_TILE), lambda i, j: (0, 0, 0)),
                  pl.BlockSpec((n_heads, 1, LANE_TILE), lambda i, j: (0, 0, 0)),
                  pl.BlockSpec((2 * HEAD_DIM, 2 * GROUP * HEAD_DIM), lambda i, j: (0, 0))],
        out_specs=pl.BlockSpec((ATT_BLOCK, d_out), lambda i, j: (i * nblk + j, 0)),
        out_shape=jax.ShapeDtypeStruct((b * s, d_out), BF16),
        compiler_params=_cparams(("parallel", "arbitrary")),
        name="window_attention",
    )(qkv_r, qkv_r, qkv_r, qkv_m, bias, shift, sinkb, rep)


def _fft_split(s):
    n2 = 128 if s >= 1024 else 32
    return (2 * s) // n2, n2


def _hyena_conv(xr, xm, col_off, conv_w, conv_b, kre, kim, tab, gr, gm, gate_off, skip, tail, *, dims):
    b, s, d, n1, n2 = dims
    kw = dict(b=b, s=s, d=d, n1=n1, n2=n2, dt=LANE_TILE)
    if col_off is not None:
        are, aim, ur4, um4 = _fft_in(xr, xm, col_off, conv_w, conv_b, tab, **kw)
        ur, um = ur4.reshape(b * s, d), um4.reshape(b * N_META, d)
    else:
        are, aim = _fft_in(xr, xm, 0, None, None, tab, **kw)
        ur, um = xr, xm
    bre, bim = _fft_mid(are, aim, kre, kim, tab, dt=min(d, FFT_MID_LANES), slabs=8)
    return _fft_out(bre, bim, tab, gr, gm, gate_off, conv_w, conv_b, ur, um, skip, tail, **kw)


def _hyena_layer(streams, fp, g_mix, w_in, conv_w, conv_b, skip, tm):
    outs = []
    d = w_in.shape[0]
    for st in streams:
        b, s = st["b"], st["s"]
        n1, n2 = _fft_split(s)
        seq_len = s + N_META
        n = 2 * s
        tab = _fft_tables(n1, n2)
        r = jnp.arange(n, dtype=jnp.int32)
        circ = _filter_rows(jnp.where(r <= s, r, n - r), r <= s, seq_len, fp, d)
        kre, kim = _filter_spectrum(circ, tab, n1=n1, n2=n2, dt=LANE_TILE, slabs=8)
        a = jnp.arange(16, dtype=jnp.int32)
        tail_lag = jnp.concatenate([jnp.minimum(s + 1 + a, seq_len - 1), s - 1 - a, s + a, s - a])
        tail_fwd = jnp.concatenate([jnp.ones(16, bool), jnp.zeros(32, bool), jnp.ones(16, bool)])
        tail = _filter_rows(tail_lag, tail_fwd, seq_len, fp, d)
        pr = _norm_matmul(st["hr"], g_mix, w_in, tm)
        pm = _norm_matmul(st["hm"], g_mix, w_in, st["hm"].shape[0])
        dims = (b, s, d, n1, n2)
        z1r, z1m = _hyena_conv(pr, pm, 0, conv_w, conv_b, kre[0], kim[0], tab, pr, pm, d, skip[0], tail[0], dims=dims)
        z2r, z2m = _hyena_conv(z1r, z1m, None, conv_w, conv_b, kre[1], kim[1], tab, pr, pm, 2 * d, skip[1], tail[1],
                               dims=dims)
        outs.append((z2r, z2m))
    return outs


def _group_qkv_columns(n_heads):
    groups = n_heads // GROUP
    qd = n_heads * HEAD_DIM
    idx = []
    for g in range(groups):
        idx += list(range(g * GROUP * HEAD_DIM, (g + 1) * GROUP * HEAD_DIM))
        idx += list(range(qd + g * HEAD_DIM, qd + (g + 1) * HEAD_DIM))
        idx += list(range(qd + groups * HEAD_DIM + g * HEAD_DIM, qd + groups * HEAD_DIM + (g + 1) * HEAD_DIM))
    return jnp.array(idx, jnp.int32)


def _encoder_pair(x_prompt, x_sample, meta_tokens, norm_mix, norm_mlp, norm_final,
                  hy_w_in, hy_conv_w, hy_conv_b, fps, hy_skip, hy_w_out, hy_b_out,
                  at_w_qkv, at_sink, at_w_o, mlp_w1, mlp_w2, *, n_heads, tm):
    d = x_prompt.shape[-1]
    streams = []
    for x in (x_prompt, x_sample):
        b, s, _ = x.shape
        streams.append(dict(b=b, s=s, hr=x.reshape(b * s, d),
                            hm=jnp.tile(meta_tokens.astype(F32), (b, 1))))
    zeros_d = jnp.zeros((d,), F32)

    conv_w = jnp.pad(hy_conv_w[0], ((0, 5), (0, 0)))
    conv_b = hy_conv_b[0][None, :]
    zs = _hyena_layer(streams, fps[0], norm_mix[0], hy_w_in[0].astype(BF16), conv_w, conv_b, hy_skip[0], tm)
    w_out = hy_w_out[0].astype(BF16)
    w1 = [w.astype(BF16) for w in mlp_w1]
    w2 = [w.astype(BF16) for w in mlp_w2]
    for st, (zr, zm) in zip(streams, zs):
        st["hr"] = _mixer_out_mlp(st["hr"], zr, w_out, hy_b_out[0], norm_mlp[0], w1[0], w2[0], zeros_d, tm, False)
        st["hm"] = _mixer_out_mlp(st["hm"], zm, w_out, hy_b_out[0], norm_mlp[0], w1[0], w2[0], zeros_d,
                                  st["hm"].shape[0], False)

    w_qkv = at_w_qkv[0][:, _group_qkv_columns(n_heads)].astype(BF16)
    w_o = at_w_o[0].astype(BF16)
    outs = []
    for st in streams:
        qkv_r = _norm_matmul(st["hr"], norm_mix[1], w_qkv, tm)
        qkv_m = _norm_matmul(st["hm"], norm_mix[1], w_qkv, st["hm"].shape[0])
        att = _attention(qkv_r, qkv_m, at_sink[0], b=st["b"], s=st["s"], n_heads=n_heads)
        y = _mixer_out_mlp(st["hr"], att, w_o, zeros_d, norm_mlp[1], w1[1], w2[1], norm_final, tm, True)
        outs.append(y.reshape(st["b"], st["s"], d))
    return tuple(outs)


def kernel(x_prompt, x_sample, meta_tokens, norm_mix, norm_mlp, norm_final, hy_w_in, hy_conv_w, hy_conv_b,
           hy_f_w1, hy_f_b1, hy_f_w2, hy_f_b2, hy_f_w3, hy_f_b3, hy_f_wout, hy_f_freq, hy_skip, hy_w_out,
           hy_b_out, at_w_qkv, at_sink, at_w_o, mlp_w1, mlp_w2):
    fps = [dict(w1=hy_f_w1[j], b1=hy_f_b1[j], w2=hy_f_w2[j], b2=hy_f_b2[j], w3=hy_f_w3[j], b3=hy_f_b3[j],
                wout=hy_f_wout[j], freq=hy_f_freq[j]) for j in range(hy_f_w1.shape[0])]
    n_heads = at_sink.shape[1]
    return _encoder_pair(x_prompt, x_sample, meta_tokens, norm_mix, norm_mlp, norm_final,
                         hy_w_in, hy_conv_w, hy_conv_b, fps, hy_skip, hy_w_out, hy_b_out,
                         at_w_qkv, at_sink, at_w_o, mlp_w1, mlp_w2, n_heads=n_heads, tm=512)
```

```python
import functools
import math

import jax
import jax.numpy as jnp
from jax import lax
from jax.experimental import pallas as pl
from jax.experimental.pallas import tpu as pltpu

F32 = jnp.float32
BF16 = jnp.bfloat16

N_META = 16
RMS_EPS = 1e-6
HY_BANDS = 16
HY_EMB_PAD = 40
HY_FAST_DECAY = 0.3
HY_SLOW_DECAY = 1.5
HY_DECAY_TARGET = 1e-2
ATT_BLOCK = 128
HEAD_DIM = 64
GROUP = 4
MASK_VALUE = -1e30
FF_CHUNK = 1024
LANE_TILE = 128
N2_CHUNK = 16
FFT_MID_LANES = 512
VMEM_LIMIT = 56 * 1024 * 1024
HIGHEST = lax.Precision.HIGHEST


def _cparams(sem):
    return pltpu.CompilerParams(dimension_semantics=sem, vmem_limit_bytes=VMEM_LIMIT)


def _rms(x, g):
    return x * lax.rsqrt(jnp.mean(x * x, axis=-1, keepdims=True) + RMS_EPS) * g


def _norm_matmul_kernel(x_ref, g_ref, w_ref, *rest):
    u = _rms(x_ref[...], g_ref[...]).astype(BF16)
    if len(rest) == 1:
        (o_ref,) = rest
    else:
        wt_ref, o_ref, ot_ref = rest
        ot_ref[...] = lax.dot_general(wt_ref[...], u, (((1,), (1,)), ((), ())),
                                      preferred_element_type=F32).astype(ot_ref.dtype)
    o_ref[...] = jnp.dot(u, w_ref[...], preferred_element_type=F32).astype(o_ref.dtype)


def _norm_matmul(x, g, w, tm, wt=None):
    rows, d = x.shape
    n = w.shape[1]
    in_specs = [pl.BlockSpec((tm, d), lambda i: (i, 0)),
                pl.BlockSpec((1, d), lambda i: (0, 0)),
                pl.BlockSpec((d, n), lambda i: (0, 0))]
    out_specs = pl.BlockSpec((tm, n), lambda i: (i, 0))
    out_shape = jax.ShapeDtypeStruct((rows, n), BF16)
    args = [x, g.reshape(1, d), w]
    if wt is not None:
        m = wt.shape[0]
        in_specs.append(pl.BlockSpec((m, d), lambda i: (0, 0)))
        out_specs = [out_specs, pl.BlockSpec((m, tm), lambda i: (0, i))]
        out_shape = [out_shape, jax.ShapeDtypeStruct((m, rows), BF16)]
        args.append(wt)
    return pl.pallas_call(
        _norm_matmul_kernel,
        grid=(rows // tm,),
        in_specs=in_specs, out_specs=out_specs, out_shape=out_shape,
        compiler_params=_cparams(("parallel",)),
        name="norm_matmul",
    )(*args)


def _mixer_out_mlp_kernel(h_ref, z_ref, wp_ref, bp_ref, g_ref, w1_ref, w2_ref, gf_ref, o_ref, *, final_norm):
    h = h_ref[...] + jnp.dot(z_ref[...], wp_ref[...], preferred_element_type=F32) + bp_ref[...]
    u = _rms(h, g_ref[...]).astype(BF16)
    acc = h
    d_ff = w1_ref.shape[1]
    for c in range(d_ff // FF_CHUNK):
        a = jnp.dot(u, w1_ref[:, c * FF_CHUNK:(c + 1) * FF_CHUNK], preferred_element_type=F32)
        a = jnp.square(jnp.maximum(a, 0.0)).astype(BF16)
        acc = acc + jnp.dot(a, w2_ref[c * FF_CHUNK:(c + 1) * FF_CHUNK, :], preferred_element_type=F32)
    if final_norm:
        acc = _rms(acc, gf_ref[...])
    o_ref[...] = acc


def _mixer_out_mlp(h, z, wp, bp, g, w1, w2, gf, tm, final_norm):
    rows, d = h.shape
    dz = z.shape[1]
    d_ff = w1.shape[1]
    const = lambda i: (0, 0)
    return pl.pallas_call(
        functools.partial(_mixer_out_mlp_kernel, final_norm=final_norm),
        grid=(rows // tm,),
        in_specs=[pl.BlockSpec((tm, d), lambda i: (i, 0)),
                  pl.BlockSpec((tm, dz), lambda i: (i, 0)),
                  pl.BlockSpec((dz, d), const),
                  pl.BlockSpec((1, d), const),
                  pl.BlockSpec((1, d), const),
                  pl.BlockSpec((d, d_ff), const),
                  pl.BlockSpec((d_ff, d), const),
                  pl.BlockSpec((1, d), const)],
        out_specs=pl.BlockSpec((tm, d), lambda i: (i, 0)),
        out_shape=jax.ShapeDtypeStruct((rows, d), F32),
        compiler_params=_cparams(("parallel",)),
        name="mixer_out_mlp",
    )(h, z, wp, bp.reshape(1, d), g.reshape(1, d), w1, w2, gf.reshape(1, d))


def _split_bf16(x):
    hi = x.astype(BF16)
    return hi, (x - hi.astype(F32)).astype(BF16)


def _dot3(a_hi, a_lo, b_hi, b_lo):
    dot = functools.partial(jnp.dot, preferred_element_type=F32)
    return dot(a_hi, b_hi) + (dot(a_hi, b_lo) + dot(a_lo, b_hi))


def _filter_kernel(z_ref, t_ref, m_ref, w1_ref, b1_ref, w2_ref, b2_ref, w3_ref, b3_ref, fr_ref, woh_ref, wol_ref,
                   ad_ref, o_ref):
    d = ad_ref.shape[1]
    dot = functools.partial(jnp.dot, precision=HIGHEST, preferred_element_type=F32)
    fr = fr_ref[...]
    h = jnp.sin(fr * (dot(z_ref[...], w1_ref[...]) + b1_ref[...]))
    h = jnp.sin(fr * (dot(h, w2_ref[...]) + b2_ref[...]))
    h = jnp.sin(fr * (dot(h, w3_ref[...]) + b3_ref[...]))
    ho = _dot3(*_split_bf16(h), woh_ref[...], wol_ref[...])
    decay = jnp.exp(-t_ref[...] * ad_ref[...])
    fwd = m_ref[...] > 0.5
    for o in range(2):
        o_ref[o] = jnp.where(fwd, ho[:, 2 * o * d:(2 * o + 1) * d], ho[:, (2 * o + 1) * d:(2 * o + 2) * d]) * decay


def _filter_rows(lag, is_fwd, seq_len, fp, d):
    r = lag.shape[0]
    tr = min(r, 512)
    lagf = lag.astype(F32)
    t = lagf / (seq_len - 1)
    w = 2.0 * math.pi * lagf / seq_len
    f = jnp.linspace(1e-4, HY_BANDS - 1, HY_BANDS, dtype=F32)[None, :]
    z = jnp.concatenate([t[:, None], jnp.cos(f * w[:, None]), -jnp.sin(f * w[:, None]),
                         jnp.zeros((r, HY_EMB_PAD - 2 * HY_BANDS - 1), F32)], axis=-1)
    w1 = jnp.pad(fp["w1"], ((0, HY_EMB_PAD - fp["w1"].shape[0]), (0, 0)))
    hid = w1.shape[1]
    max_decay = math.log(HY_DECAY_TARGET) / HY_FAST_DECAY
    min_decay = math.log(HY_DECAY_TARGET) / HY_SLOW_DECAY
    adel = jnp.abs(jnp.linspace(min_decay, max_decay, d, dtype=F32))[None, :]
    const = lambda i: (0, 0)
    row = lambda i: (i, 0)
    wo_hi, wo_lo = _split_bf16(fp["wout"])
    return pl.pallas_call(
        _filter_kernel,
        grid=(r // tr,),
        in_specs=[pl.BlockSpec((tr, HY_EMB_PAD), row), pl.BlockSpec((tr, 1), row), pl.BlockSpec((tr, 1), row),
                  pl.BlockSpec((HY_EMB_PAD, hid), const), pl.BlockSpec((1, hid), const),
                  pl.BlockSpec((hid, hid), const), pl.BlockSpec((1, hid), const),
                  pl.BlockSpec((hid, hid), const), pl.BlockSpec((1, hid), const),
                  pl.BlockSpec((1, hid), const), pl.BlockSpec((hid, 4 * d), const), pl.BlockSpec((hid, 4 * d), const),
                  pl.BlockSpec((1, d), const)],
        out_specs=pl.BlockSpec((2, tr, d), lambda i: (0, i, 0)),
        out_shape=jax.ShapeDtypeStruct((2, r, d), F32),
        compiler_params=_cparams(("parallel",)),
        name="hyena_filter",
    )(z, t[:, None], is_fwd.astype(F32)[:, None], w1, fp["b1"][None], fp["w2"], fp["b2"][None],
      fp["w3"], fp["b3"][None], fp["freq"][None], wo_hi, wo_lo, adel)


def _cplx_block(re, im):
    return jnp.concatenate([jnp.concatenate([re, -im], axis=-1), jnp.concatenate([im, re], axis=-1)], axis=-2)


def _fft_tables(n1, n2):
    n = n1 * n2
    n1h = n1 // 2
    pad = 8 - 1
    k1 = jnp.arange(n1, dtype=jnp.int32)
    c2 = jnp.arange(n2, dtype=jnp.int32)
    cols = jnp.concatenate([jnp.arange(n1h, dtype=jnp.int32), jnp.array([n1 - 1], jnp.int32)])
    pos = n2 * cols[None, None, :] + c2[:, None, None]
    ang = ((k1[None, :, None] * pos) % n).astype(F32) * (-2.0 * math.pi / n)
    wr = jnp.pad(jnp.cos(ang), ((0, 0), (0, 0), (0, pad)))
    wi = jnp.pad(jnp.sin(ang), ((0, 0), (0, 0), (0, pad)))
    fwd1 = _cplx_block(wr, wi)
    inv1 = _cplx_block(jnp.swapaxes(wr, 1, 2), -jnp.swapaxes(wi, 1, 2)) / n
    posf = n2 * k1[None, None, :] + c2[:, None, None]
    angf = ((k1[None, :, None] * posf) % n).astype(F32) * (-2.0 * math.pi / n)
    fil1 = jnp.concatenate([jnp.cos(angf), jnp.sin(angf)], axis=1)
    ang2 = ((c2[:, None] * c2[None, :]) % n2).astype(F32) * (-2.0 * math.pi / n2)
    fr, fi = jnp.cos(ang2), jnp.sin(ang2)
    fwd2 = _cplx_block(fr, fi)
    return dict(fwd1=fwd1.astype(BF16), inv1=inv1.astype(BF16), fil1=_split_bf16(fil1),
                fwd2=fwd2.astype(BF16), inv2=_cplx_block(fr, -fi).astype(BF16), fwd2_split=_split_bf16(fwd2))


def _ordered_short_conv(meta, real, cw_ref, cb_ref, seq_ref):
    s = real.shape[0]
    dt = real.shape[1]
    seq_ref[0:8, :] = jnp.zeros((8, dt), F32)
    seq_ref[8:8 + N_META, :] = meta.astype(F32)
    seq_ref[8 + N_META:8 + N_META + s, :] = real.astype(F32)
    seq_ref[8 + N_META + s:16 + N_META + s, :] = jnp.zeros((8, dt), F32)
    x = seq_ref[...]
    rows = x.shape[0]
    cw = cw_ref[...].astype(F32)
    y = (pltpu.roll(x, 1, axis=0) * cw[0:1] + x * cw[1:2] + pltpu.roll(x, rows - 1, axis=0) * cw[2:3]
         + cb_ref[...].astype(F32))
    return y[8:8 + N_META], y[8 + N_META:8 + N_META + s]


def _fft_in_kernel(*refs, n1, n2, short_conv):
    if short_conv:
        (xr_ref, xm_ref, cw_ref, cb_ref, f_ref, are_ref, aim_ref, vr_ref, vm_ref,
         xs_ref, mp_ref, seq_ref) = refs
    else:
        xr_ref, xm_ref, f_ref, are_ref, aim_ref, xs_ref, mp_ref = refs
    n1h = n1 // 2
    chunk = pl.program_id(2)

    @pl.when(chunk == 0)
    def _prepare():
        for e in range(2):
            if short_conv:
                meta, real = _ordered_short_conv(xm_ref[e], xr_ref[e], cw_ref, cb_ref, seq_ref)
                vr_ref[e] = real.astype(vr_ref.dtype)
                vm_ref[e] = meta.astype(vm_ref.dtype)
            else:
                meta, real = xm_ref[e].astype(F32), xr_ref[e].astype(F32)
            xs_ref[e] = real
            mp_ref[e] = jnp.zeros(mp_ref.shape[1:], F32)
            for j in range(N_META):
                mp_ref[e, pl.ds(8 * (n2 - N_META + j), 1), :] = meta[j:j + 1]

    for i in range(N2_CHUNK):
        c2 = chunk * N2_CHUNK + i
        parts = []
        for e in range(2):
            parts.append(xs_ref[e, pl.ds(c2, n1h, stride=n2), :])
            parts.append(mp_ref[e, pl.ds(pl.multiple_of(c2 * 8, 8), 8), :])
        rhs = jnp.concatenate(parts, axis=0).astype(BF16)
        out = jnp.dot(f_ref[i], rhs, preferred_element_type=F32)
        are_ref[i] = out[:n1]
        aim_ref[i] = out[n1:]


def _fft_in(xr, xm, col_off, conv_w, conv_b, tab, *, b, s, d, n1, n2, dt):
    short_conv = conv_w is not None
    pairs = b // 2
    c = xr.shape[1]
    cb0 = col_off // dt
    xr4 = xr.reshape(pairs, 2, s, c)
    xm4 = xm.reshape(pairs, 2, N_META, c)
    kk = tab["fwd1"].shape[2]
    in_specs = [pl.BlockSpec((None, 2, s, dt), lambda p, j, t: (p, 0, 0, cb0 + j)),
                pl.BlockSpec((None, 2, N_META, dt), lambda p, j, t: (p, 0, 0, cb0 + j))]
    args = [xr4, xm4]
    if short_conv:
        in_specs += [pl.BlockSpec((8, dt), lambda p, j, t: (0, cb0 + j)),
                     pl.BlockSpec((1, dt), lambda p, j, t: (0, cb0 + j))]
        args += [conv_w, conv_b]
    in_specs.append(pl.BlockSpec((N2_CHUNK, 2 * n1, kk), lambda p, j, t: (t, 0, 0)))
    args.append(tab["fwd1"])
    a_spec = pl.BlockSpec((None, N2_CHUNK, n1, dt), lambda p, j, t: (p, t, 0, j))
    a_shape = jax.ShapeDtypeStruct((pairs, n2, n1, d), F32)
    out_specs = [a_spec, a_spec]
    out_shape = [a_shape, a_shape]
    scratch = [pltpu.VMEM((2, s, dt), F32), pltpu.VMEM((2, 8 * n2, dt), F32)]
    if short_conv:
        out_specs += [pl.BlockSpec((None, 2, s, dt), lambda p, j, t: (p, 0, 0, j)),
                      pl.BlockSpec((None, 2, N_META, dt), lambda p, j, t: (p, 0, 0, j))]
        out_shape += [jax.ShapeDtypeStruct((pairs, 2, s, d), BF16),
                      jax.ShapeDtypeStruct((pairs, 2, N_META, d), BF16)]
        scratch.append(pltpu.VMEM((s + N_META + 16, dt), F32))
    return pl.pallas_call(
        functools.partial(_fft_in_kernel, n1=n1, n2=n2, short_conv=short_conv),
        grid=(pairs, d // dt, n2 // N2_CHUNK),
        in_specs=in_specs, out_specs=out_specs, out_shape=out_shape, scratch_shapes=scratch,
        compiler_params=_cparams(("parallel", "parallel", "arbitrary")),
        name="fft_in",
    )(*args)


def _fft_mid_kernel(are_ref, aim_ref, kre_ref, kim_ref, f_ref, g_ref, bre_ref, bim_ref, *, n2, slabs):
    for i in range(slabs):
        x = jnp.concatenate([are_ref[:, i, :], aim_ref[:, i, :]], axis=0).astype(BF16)
        z = jnp.dot(f_ref[...], x, preferred_element_type=F32)
        zr, zi = z[:n2], z[n2:]
        kr, ki = kre_ref[:, i, :], kim_ref[:, i, :]
        p = jnp.concatenate([zr * kr - zi * ki, zr * ki + zi * kr], axis=0).astype(BF16)
        y = jnp.dot(g_ref[...], p, preferred_element_type=F32)
        bre_ref[:, i, :] = y[:n2]
        bim_ref[:, i, :] = y[n2:]


def _fft_mid(are, aim, kre, kim, tab, *, dt, slabs):
    pairs, n2, n1, d = are.shape
    blk = (None, n2, slabs, dt)
    amap = lambda j, kb, p: (p, 0, kb, j)
    kmap = lambda j, kb, p: (0, kb, j)
    const = lambda j, kb, p: (0, 0)
    shp = jax.ShapeDtypeStruct((pairs, n2, n1, d), F32)
    return pl.pallas_call(
        functools.partial(_fft_mid_kernel, n2=n2, slabs=slabs),
        grid=(d // dt, n1 // slabs, pairs),
        in_specs=[pl.BlockSpec(blk, amap), pl.BlockSpec(blk, amap),
                  pl.BlockSpec((n2, slabs, dt), kmap), pl.BlockSpec((n2, slabs, dt), kmap),
                  pl.BlockSpec((2 * n2, 2 * n2), const), pl.BlockSpec((2 * n2, 2 * n2), const)],
        out_specs=[pl.BlockSpec(blk, amap), pl.BlockSpec(blk, amap)],
        out_shape=[shp, shp],
        compiler_params=_cparams(("parallel", "parallel", "arbitrary")),
        name="fft_mid",
    )(are, aim, kre, kim, tab["fwd2"], tab["inv2"])


def _filter_fft_in_kernel(c_ref, fh_ref, fl_ref, are_ref, aim_ref, *, n1, n2):
    chunk = pl.program_id(2)
    for i in range(N2_CHUNK):
        rhs = c_ref[pl.ds(chunk * N2_CHUNK + i, n1, stride=n2), :]
        out = _dot3(fh_ref[i], fl_ref[i], *_split_bf16(rhs))
        are_ref[i] = out[:n1]
        aim_ref[i] = out[n1:]


def _filter_fft_mid_kernel(are_ref, aim_ref, fh_ref, fl_ref, kre_ref, kim_ref, *, n2, slabs):
    for i in range(slabs):
        x = jnp.concatenate([are_ref[:, i, :], aim_ref[:, i, :]], axis=0)
        z = _dot3(fh_ref[...], fl_ref[...], *_split_bf16(x))
        kre_ref[:, i, :] = z[:n2]
        kim_ref[:, i, :] = z[n2:]


def _filter_spectrum(circ, tab, *, n1, n2, dt, slabs):
    orders, n, d = circ.shape
    shp = [jax.ShapeDtypeStruct((orders, n2, n1, d), F32)] * 2
    tab_spec = pl.BlockSpec((N2_CHUNK, 2 * n1, n1), lambda o, j, t: (t, 0, 0))
    lt = LANE_TILE
    are, aim = pl.pallas_call(
        functools.partial(_filter_fft_in_kernel, n1=n1, n2=n2),
        grid=(orders, d // lt, n2 // N2_CHUNK),
        in_specs=[pl.BlockSpec((None, n, lt), lambda o, j, t: (o, 0, j)), tab_spec, tab_spec],
        out_specs=[pl.BlockSpec((None, N2_CHUNK, n1, lt), lambda o, j, t: (o, t, 0, j))] * 2,
        out_shape=shp,
        compiler_params=_cparams(("parallel", "parallel", "arbitrary")),
        name="filter_fft_in",
    )(circ, *tab["fil1"])
    blk = (None, n2, slabs, dt)
    amap = lambda o, kb, j: (o, 0, kb, j)
    f_spec = pl.BlockSpec((2 * n2, 2 * n2), lambda o, kb, j: (0, 0))
    return pl.pallas_call(
        functools.partial(_filter_fft_mid_kernel, n2=n2, slabs=slabs),
        grid=(orders, n1 // slabs, d // dt),
        in_specs=[pl.BlockSpec(blk, amap), pl.BlockSpec(blk, amap), f_spec, f_spec],
        out_specs=[pl.BlockSpec(blk, amap)] * 2,
        out_shape=shp,
        compiler_params=_cparams(("parallel", "parallel", "parallel")),
        name="filter_fft_mid",
    )(are, aim, *tab["fwd2_split"])


def _alias_patch(tail, u_meta, u_last):
    dfw = tail[0:16] - tail[16:32]
    dbw = tail[32:48] - tail[48:64]
    ridx = lax.broadcasted_iota(jnp.int32, dfw.shape, 0)
    real_fix = jnp.zeros_like(dfw)
    meta_fix = jnp.zeros_like(dfw)
    for o in range(15):
        shifted = jnp.where(ridx >= o + 1, pltpu.roll(dfw, o + 1, axis=0), 0.0)
        real_fix = real_fix + shifted * u_meta[o:o + 1]
    for c in range(16):
        src = u_last if c == 0 else pltpu.roll(u_last, 16 - c, axis=0)
        meta_fix = meta_fix + jnp.where(ridx + c <= 15, src, 0.0) * dbw[c:c + 1]
    return real_fix, meta_fix


def _fft_out_kernel(bre_ref, bim_ref, g_ref, gr_ref, gm_ref, cw_ref, cb_ref, ur_ref, um_ref, skip_ref,
                    tail_ref, zr_ref, zm_ref, ys_ref, yp_ref, seq_ref, *, n1, n2):
    n1h = n1 // 2
    kk = n1h + 8
    s = ys_ref.shape[1]
    chunk = pl.program_id(2)

    for i in range(N2_CHUNK):
        c2 = chunk * N2_CHUNK + i
        rhs = jnp.concatenate([bre_ref[i], bim_ref[i]], axis=0).astype(BF16)
        y = jnp.dot(g_ref[i], rhs, preferred_element_type=F32)
        for e in range(2):
            ys_ref[e, pl.ds(c2, n1h, stride=n2), :] = y[e * kk:e * kk + n1h]
            yp_ref[e, pl.ds(pl.multiple_of(c2 * 8, 8), 8), :] = y[e * kk + n1h:(e + 1) * kk]

    @pl.when(chunk == n2 // N2_CHUNK - 1)
    def _gate():
        tail = tail_ref[...]
        skip = skip_ref[...]
        for e in range(2):
            y_meta = jnp.concatenate(
                [yp_ref[e, pl.ds(8 * (n2 - N_META + j), 1), :] for j in range(N_META)], axis=0)
            u_real = ur_ref[e].astype(F32)
            u_meta = um_ref[e].astype(F32)
            real_fix, meta_fix = _alias_patch(tail, u_meta, u_real[s - 16:s])
            ys_ref[e, s - 16:s, :] = ys_ref[e, s - 16:s, :] + real_fix
            g_meta, g_real = _ordered_short_conv(gm_ref[e], gr_ref[e], cw_ref, cb_ref, seq_ref)
            zr_ref[e] = (g_real * (ys_ref[e] + skip * u_real)).astype(zr_ref.dtype)
            zm_ref[e] = (g_meta * (y_meta + meta_fix + skip * u_meta)).astype(zm_ref.dtype)


def _fft_out(bre, bim, tab, gr, gm, gate_off, conv_w, conv_b, ur, um, skip, tail, *, b, s, d, n1, n2, dt):
    pairs = b // 2
    c = gr.shape[1]
    gb0 = gate_off // dt
    kk2 = tab["inv1"].shape[1]
    gr4 = gr.reshape(pairs, 2, s, c)
    gm4 = gm.reshape(pairs, 2, N_META, c)
    ur4 = ur.reshape(pairs, 2, s, d)
    um4 = um.reshape(pairs, 2, N_META, d)
    b_spec = pl.BlockSpec((None, N2_CHUNK, n1, dt), lambda p, j, t: (p, t, 0, j))
    zr, zm = pl.pallas_call(
        functools.partial(_fft_out_kernel, n1=n1, n2=n2),
        grid=(pairs, d // dt, n2 // N2_CHUNK),
        in_specs=[b_spec, b_spec,
                  pl.BlockSpec((N2_CHUNK, kk2, 2 * n1), lambda p, j, t: (t, 0, 0)),
                  pl.BlockSpec((None, 2, s, dt), lambda p, j, t: (p, 0, 0, gb0 + j)),
                  pl.BlockSpec((None, 2, N_META, dt), lambda p, j, t: (p, 0, 0, gb0 + j)),
                  pl.BlockSpec((8, dt), lambda p, j, t: (0, gb0 + j)),
                  pl.BlockSpec((1, dt), lambda p, j, t: (0, gb0 + j)),
                  pl.BlockSpec((None, 2, s, dt), lambda p, j, t: (p, 0, 0, j)),
                  pl.BlockSpec((None, 2, N_META, dt), lambda p, j, t: (p, 0, 0, j)),
                  pl.BlockSpec((1, dt), lambda p, j, t: (0, j)),
                  pl.BlockSpec((64, dt), lambda p, j, t: (0, j))],
        out_specs=[pl.BlockSpec((None, 2, s, dt), lambda p, j, t: (p, 0, 0, j)),
                   pl.BlockSpec((None, 2, N_META, dt), lambda p, j, t: (p, 0, 0, j))],
        out_shape=[jax.ShapeDtypeStruct((pairs, 2, s, d), BF16),
                   jax.ShapeDtypeStruct((pairs, 2, N_META, d), BF16)],
        scratch_shapes=[pltpu.VMEM((2, s, dt), F32), pltpu.VMEM((2, 8 * n2, dt), F32),
                        pltpu.VMEM((s + N_META + 16, dt), F32)],
        compiler_params=_cparams(("parallel", "parallel", "arbitrary")),
        name="fft_out",
    )(bre, bim, tab["inv1"], gr4, gm4, conv_w, conv_b, ur4, um4, skip.reshape(1, d), tail)
    return zr.reshape(b * s, d), zm.reshape(b * N_META, d)


def _attn_kernel(own_ref, vprev_ref, vnext_ref, vmeta_ref, ktp_ref, kto_ref, ktn_ref, ktm_ref,
                 bias_ref, shift_ref, o_ref, *, groups):
    blk = pl.program_id(1).astype(F32)
    gw = GROUP * HEAD_DIM
    qd = groups * gw
    lane = lax.broadcasted_iota(jnp.int32, (1, LANE_TILE), 1)
    low = lane < HEAD_DIM
    pad_rows = jnp.zeros((ATT_BLOCK - N_META, 2 * HEAD_DIM), BF16)
    zero = jnp.zeros((ATT_BLOCK, LANE_TILE), BF16)
    for g in range(groups):
        vk_own = own_ref[:, qd + g * LANE_TILE:qd + (g + 1) * LANE_TILE]
        grp = slice(g * LANE_TILE, (g + 1) * LANE_TILE)
        vk = jnp.concatenate([vprev_ref[:, grp], vk_own, vnext_ref[:, grp], vmeta_ref[:, grp], pad_rows],
                             axis=0)
        v_ones = jnp.where(low, vk, jnp.ones_like(vk))
        kt_rows = slice(g * HEAD_DIM, (g + 1) * HEAD_DIM)
        kt = jnp.concatenate([ktp_ref[kt_rows, :], kto_ref[kt_rows, :], ktn_ref[kt_rows, :], ktm_ref[kt_rows, :]],
                             axis=1)
        kt2 = jnp.concatenate([kt, kt], axis=0)
        parts = []
        for pr in range(GROUP // 2):
            qp = own_ref[:, g * gw + pr * LANE_TILE:g * gw + (pr + 1) * LANE_TILE]
            parts += [jnp.where(low, qp, zero), jnp.where(low, zero, qp)]
        q = jnp.concatenate(parts, axis=0)
        sc = jnp.dot(q, kt2, preferred_element_type=F32) + bias_ref[g]
        t = [sc[:, i * LANE_TILE:(i + 1) * LANE_TILE] for i in range(3)]
        t.append(sc[:, 3 * LANE_TILE:] - shift_ref[g] * blk)
        m = jnp.max(jnp.maximum(jnp.maximum(t[0], t[1]), jnp.maximum(t[2], t[3])), axis=1, keepdims=True)
        p = jnp.concatenate([jnp.exp(x - m) for x in t], axis=1).astype(BF16)
        oa = jnp.dot(p, v_ones, preferred_element_type=F32)
        ob = pltpu.roll(oa, HEAD_DIM, axis=1)
        outs = []
        for pr in range(GROUP // 2):
            ev = slice((2 * pr) * ATT_BLOCK, (2 * pr + 1) * ATT_BLOCK)
            od = slice((2 * pr + 1) * ATT_BLOCK, (2 * pr + 2) * ATT_BLOCK)
            outs.append(jnp.where(low, oa[ev] / ob[ev], ob[od] / oa[od]))
        o_ref[:, g * gw:(g + 1) * gw] = jnp.concatenate(outs, axis=1).astype(o_ref.dtype)


def _attention_tables(n_heads, sink):
    groups = n_heads // GROUP
    slopes = jnp.exp2(-8.0 * jnp.arange(1, n_heads + 1, dtype=F32) / n_heads)
    i = jnp.arange(ATT_BLOCK, dtype=jnp.int32)[:, None]
    c = jnp.arange(4 * ATT_BLOCK, dtype=jnp.int32)[None, :]
    dist = jnp.abs(c - ATT_BLOCK - i)
    key_blk = c // ATT_BLOCK
    in_band = jnp.logical_and(c < 3 * ATT_BLOCK, dist <= ATT_BLOCK)
    meta_col = jnp.logical_and(c >= 3 * ATT_BLOCK, c < 3 * ATT_BLOCK + N_META)
    sink_col = c == 3 * ATT_BLOCK + N_META
    meta_dist = N_META + i - (c - 3 * ATT_BLOCK)
    tables = []
    for drop in (None, 0, 2):
        ok = in_band if drop is None else jnp.logical_and(in_band, key_blk != drop)
        d_eff = jnp.where(ok, dist, jnp.where(meta_col, meta_dist, 0)).astype(F32)
        live = jnp.logical_or(ok, meta_col)
        tab = jnp.where(live[None], -slopes[:, None, None] * d_eff[None], MASK_VALUE)
        tables.append(jnp.where(sink_col[None], sink.astype(F32)[:, None, None], tab))
    bias = jnp.stack(tables, axis=0).reshape(3, groups, GROUP * ATT_BLOCK, 4 * ATT_BLOCK)
    lane = jnp.arange(LANE_TILE)[None, :]
    shift = jnp.where(lane < N_META, jnp.repeat(slopes * ATT_BLOCK, ATT_BLOCK)[:, None], 0.0)
    return bias, shift.reshape(groups, GROUP * ATT_BLOCK, LANE_TILE)


def _attention(qvk_r, kt_r, qvk_m, kt_m, sink, *, b, s, n_heads):
    groups = n_heads // GROUP
    nblk = s // ATT_BLOCK
    qd = n_heads * HEAD_DIM
    width = qvk_r.shape[1]
    vkw = groups * 2 * HEAD_DIM
    vkb = qd // vkw
    bias, shift = _attention_tables(n_heads, sink)

    def variant(j):
        return jnp.where(j == 0, 1, jnp.where(j == nblk - 1, 2, 0))

    prev = lambda i, j: i * nblk + jnp.maximum(j - 1, 0)
    nxt = lambda i, j: i * nblk + jnp.minimum(j + 1, nblk - 1)
    return pl.pallas_call(
        functools.partial(_attn_kernel, groups=groups),
        grid=(b, nblk),
        in_specs=[pl.BlockSpec((ATT_BLOCK, width), lambda i, j: (i * nblk + j, 0)),
                  pl.BlockSpec((ATT_BLOCK, vkw), lambda i, j: (prev(i, j), vkb)),
                  pl.BlockSpec((ATT_BLOCK, vkw), lambda i, j: (nxt(i, j), vkb)),
                  pl.BlockSpec((N_META, vkw), lambda i, j: (i, vkb)),
                  pl.BlockSpec((groups * HEAD_DIM, ATT_BLOCK), lambda i, j: (0, prev(i, j))),
                  pl.BlockSpec((groups * HEAD_DIM, ATT_BLOCK), lambda i, j: (0, i * nblk + j)),
                  pl.BlockSpec((groups * HEAD_DIM, ATT_BLOCK), lambda i, j: (0, nxt(i, j))),
                  pl.BlockSpec((None, groups * HEAD_DIM, LANE_TILE), lambda i, j: (i, 0, 0)),
                  pl.BlockSpec((None, groups, GROUP * ATT_BLOCK, 4 * ATT_BLOCK), lambda i, j: (variant(j), 0, 0, 0)),
                  pl.BlockSpec((groups, GROUP * ATT_BLOCK, LANE_TILE), lambda i, j: (0, 0, 0))],
        out_specs=pl.BlockSpec((ATT_BLOCK, qd), lambda i, j: (i * nblk + j, 0)),
        out_shape=jax.ShapeDtypeStruct((b * s, qd), BF16),
        compiler_params=_cparams(("parallel", "arbitrary")),
        name="window_attention",
    )(qvk_r, qvk_r, qvk_r, qvk_m, kt_r, kt_r, kt_r, kt_m, bias, shift)


def _fft_split(s):
    n2 = 128 if s >= 1024 else 32
    return (2 * s) // n2, n2


def _hyena_conv(xr, xm, col_off, conv_w, conv_b, kre, kim, tab, gr, gm, gate_off, skip, tail, *, dims):
    b, s, d, n1, n2 = dims
    kw = dict(b=b, s=s, d=d, n1=n1, n2=n2, dt=LANE_TILE)
    if col_off is not None:
        are, aim, ur4, um4 = _fft_in(xr, xm, col_off, conv_w, conv_b, tab, **kw)
        ur, um = ur4.reshape(b * s, d), um4.reshape(b * N_META, d)
    else:
        are, aim = _fft_in(xr, xm, 0, None, None, tab, **kw)
        ur, um = xr, xm
    bre, bim = _fft_mid(are, aim, kre, kim, tab, dt=min(d, FFT_MID_LANES), slabs=8)
    return _fft_out(bre, bim, tab, gr, gm, gate_off, conv_w, conv_b, ur, um, skip, tail, **kw)


def _hyena_layer(streams, fp, g_mix, w_in, conv_w, conv_b, skip, tm):
    outs = []
    d = w_in.shape[0]
    for st in streams:
        b, s = st["b"], st["s"]
        n1, n2 = _fft_split(s)
        seq_len = s + N_META
        n = 2 * s
        tab = _fft_tables(n1, n2)
        r = jnp.arange(n, dtype=jnp.int32)
        circ = _filter_rows(jnp.where(r <= s, r, n - r), r <= s, seq_len, fp, d)
        kre, kim = _filter_spectrum(circ, tab, n1=n1, n2=n2, dt=2 * LANE_TILE, slabs=8)
        a = jnp.arange(16, dtype=jnp.int32)
        tail_lag = jnp.concatenate([jnp.minimum(s + 1 + a, seq_len - 1), s - 1 - a, s + a, s - a])
        tail_fwd = jnp.concatenate([jnp.ones(16, bool), jnp.zeros(32, bool), jnp.ones(16, bool)])
        tail = _filter_rows(tail_lag, tail_fwd, seq_len, fp, d)
        pr = _norm_matmul(st["hr"], g_mix, w_in, tm)
        pm = _norm_matmul(st["hm"], g_mix, w_in, st["hm"].shape[0])
        dims = (b, s, d, n1, n2)
        z1r, z1m = _hyena_conv(pr, pm, 0, conv_w, conv_b, kre[0], kim[0], tab, pr, pm, d, skip[0], tail[0], dims=dims)
        z2r, z2m = _hyena_conv(z1r, z1m, None, conv_w, conv_b, kre[1], kim[1], tab, pr, pm, 2 * d, skip[1], tail[1],
                               dims=dims)
        outs.append((z2r, z2m))
    return outs


def _attention_weights(w_qkv, n_heads):
    groups = n_heads // GROUP
    qd = n_heads * HEAD_DIM
    kd = groups * HEAD_DIM
    w_q = w_qkv[:, :qd] * (HEAD_DIM ** -0.5)
    w_k = w_qkv[:, qd:qd + kd]
    w_v = w_qkv[:, qd + kd:]
    d = w_qkv.shape[0]
    w_vk = jnp.stack([w_v.reshape(d, groups, HEAD_DIM), w_k.reshape(d, groups, HEAD_DIM)], axis=2)
    w_rows = jnp.concatenate([w_q, w_vk.reshape(d, 2 * kd)], axis=1).astype(BF16)
    return w_rows, w_k.T.astype(BF16)


def _meta_keys_transposed(qvk_m, b, n_heads):
    groups = n_heads // GROUP
    qd = n_heads * HEAD_DIM
    k_m = qvk_m[:, qd:].reshape(b, N_META, groups, 2, HEAD_DIM)[:, :, :, 1, :]
    kt = jnp.transpose(k_m.reshape(b, N_META, groups * HEAD_DIM), (0, 2, 1))
    return jnp.pad(kt, ((0, 0), (0, 0), (0, LANE_TILE - N_META)))


def _encoder_pair(x_prompt, x_sample, meta_tokens, norm_mix, norm_mlp, norm_final,
                  hy_w_in, hy_conv_w, hy_conv_b, fps, hy_skip, hy_w_out, hy_b_out,
                  at_w_qkv, at_sink, at_w_o, mlp_w1, mlp_w2, *, n_heads, tm):
    d = x_prompt.shape[-1]
    streams = []
    for x in (x_prompt, x_sample):
        b, s, _ = x.shape
        streams.append(dict(b=b, s=s, hr=x.reshape(b * s, d),
                            hm=jnp.tile(meta_tokens.astype(F32), (b, 1))))
    zeros_d = jnp.zeros((d,), F32)

    conv_w = jnp.pad(hy_conv_w[0], ((0, 5), (0, 0)))
    conv_b = hy_conv_b[0][None, :]
    zs = _hyena_layer(streams, fps[0], norm_mix[0], hy_w_in[0].astype(BF16), conv_w, conv_b, hy_skip[0], tm)
    w_out = hy_w_out[0].astype(BF16)
    w1 = [w.astype(BF16) for w in mlp_w1]
    w2 = [w.astype(BF16) for w in mlp_w2]
    for st, (zr, zm) in zip(streams, zs):
        st["hr"] = _mixer_out_mlp(st["hr"], zr, w_out, hy_b_out[0], norm_mlp[0], w1[0], w2[0], zeros_d, tm, False)
        st["hm"] = _mixer_out_mlp(st["hm"], zm, w_out, hy_b_out[0], norm_mlp[0], w1[0], w2[0], zeros_d,
                                  st["hm"].shape[0], False)

    w_rows, w_kt = _attention_weights(at_w_qkv[0], n_heads)
    w_o = at_w_o[0].astype(BF16)
    outs = []
    for st in streams:
        qvk_r, kt_r = _norm_matmul(st["hr"], norm_mix[1], w_rows, tm, wt=w_kt)
        qvk_m = _norm_matmul(st["hm"], norm_mix[1], w_rows, st["hm"].shape[0])
        kt_m = _meta_keys_transposed(qvk_m, st["b"], n_heads)
        att = _attention(qvk_r, kt_r, qvk_m, kt_m, at_sink[0], b=st["b"], s=st["s"], n_heads=n_heads)
        y = _mixer_out_mlp(st["hr"], att, w_o, zeros_d, norm_mlp[1], w1[1], w2[1], norm_final, tm, True)
        outs.append(y.reshape(st["b"], st["s"], d))
    return tuple(outs)


def kernel(x_prompt, x_sample, meta_tokens, norm_mix, norm_mlp, norm_final, hy_w_in, hy_conv_w, hy_conv_b,
           hy_f_w1, hy_f_b1, hy_f_w2, hy_f_b2, hy_f_w3, hy_f_b3, hy_f_wout, hy_f_freq, hy_skip, hy_w_out,
           hy_b_out, at_w_qkv, at_sink, at_w_o, mlp_w1, mlp_w2):
    fps = [dict(w1=hy_f_w1[j], b1=hy_f_b1[j], w2=hy_f_w2[j], b2=hy_f_b2[j], w3=hy_f_w3[j], b3=hy_f_b3[j],
                wout=hy_f_wout[j], freq=hy_f_freq[j]) for j in range(hy_f_w1.shape[0])]
    n_heads = at_sink.shape[1]
    return _encoder_pair(x_prompt, x_sample, meta_tokens, norm_mix, norm_mlp, norm_final,
                         hy_w_in, hy_conv_w, hy_conv_b, fps, hy_skip, hy_w_out, hy_b_out,
                         at_w_qkv, at_sink, at_w_o, mlp_w1, mlp_w2, n_heads=n_heads, tm=512)
```

```python
import functools
import math

import jax
import jax.numpy as jnp
from jax import lax
from jax.experimental import pallas as pl
from jax.experimental.pallas import tpu as pltpu

F32 = jnp.float32
BF16 = jnp.bfloat16

N_META = 16
RMS_EPS = 1e-6
HY_BANDS = 16
HY_EMB_PAD = 40
HY_FAST_DECAY = 0.3
HY_SLOW_DECAY = 1.5
HY_DECAY_TARGET = 1e-2
ATT_BLOCK = 128
HEAD_DIM = 64
GROUP = 4
MASK_VALUE = -1e30
FF_CHUNK = 1024
LANE_TILE = 128
N2_CHUNK = 16
FFT_MID_LANES = 512
VMEM_LIMIT = 56 * 1024 * 1024
HIGHEST = lax.Precision.HIGHEST


def _cparams(sem):
    return pltpu.CompilerParams(dimension_semantics=sem, vmem_limit_bytes=VMEM_LIMIT)


def _rms(x, g):
    return x * lax.rsqrt(jnp.mean(x * x, axis=-1, keepdims=True) + RMS_EPS) * g


def _norm_matmul_kernel(x_ref, g_ref, w_ref, *rest):
    u = _rms(x_ref[...], g_ref[...]).astype(BF16)
    if len(rest) == 1:
        (o_ref,) = rest
    else:
        wt_ref, o_ref, ot_ref = rest
        ot_ref[...] = lax.dot_general(wt_ref[...], u, (((1,), (1,)), ((), ())),
                                      preferred_element_type=F32).astype(ot_ref.dtype)
    o_ref[...] = jnp.dot(u, w_ref[...], preferred_element_type=F32).astype(o_ref.dtype)


def _norm_matmul(x, g, w, tm, wt=None):
    rows, d = x.shape
    n = w.shape[1]
    in_specs = [pl.BlockSpec((tm, d), lambda i: (i, 0)),
                pl.BlockSpec((1, d), lambda i: (0, 0)),
                pl.BlockSpec((d, n), lambda i: (0, 0))]
    out_specs = pl.BlockSpec((tm, n), lambda i: (i, 0))
    out_shape = jax.ShapeDtypeStruct((rows, n), BF16)
    args = [x, g.reshape(1, d), w]
    if wt is not None:
        m = wt.shape[0]
        in_specs.append(pl.BlockSpec((m, d), lambda i: (0, 0)))
        out_specs = [out_specs, pl.BlockSpec((m, tm), lambda i: (0, i))]
        out_shape = [out_shape, jax.ShapeDtypeStruct((m, rows), BF16)]
        args.append(wt)
    return pl.pallas_call(
        _norm_matmul_kernel,
        grid=(rows // tm,),
        in_specs=in_specs, out_specs=out_specs, out_shape=out_shape,
        compiler_params=_cparams(("parallel",)),
        name="norm_matmul",
    )(*args)


def _mixer_out_mlp_kernel(h_ref, z_ref, wp_ref, bp_ref, g_ref, w1_ref, w2_ref, gf_ref, o_ref, *, final_norm):
    h = h_ref[...] + jnp.dot(z_ref[...], wp_ref[...], preferred_element_type=F32) + bp_ref[...]
    u = _rms(h, g_ref[...]).astype(BF16)
    acc = h
    d_ff = w1_ref.shape[1]
    for c in range(d_ff // FF_CHUNK):
        a = jnp.dot(u, w1_ref[:, c * FF_CHUNK:(c + 1) * FF_CHUNK], preferred_element_type=F32)
        a = jnp.square(jnp.maximum(a, 0.0)).astype(BF16)
        acc = acc + jnp.dot(a, w2_ref[c * FF_CHUNK:(c + 1) * FF_CHUNK, :], preferred_element_type=F32)
    if final_norm:
        acc = _rms(acc, gf_ref[...])
    o_ref[...] = acc


def _mixer_out_mlp(h, z, wp, bp, g, w1, w2, gf, tm, final_norm):
    rows, d = h.shape
    dz = z.shape[1]
    d_ff = w1.shape[1]
    const = lambda i: (0, 0)
    return pl.pallas_call(
        functools.partial(_mixer_out_mlp_kernel, final_norm=final_norm),
        grid=(rows // tm,),
        in_specs=[pl.BlockSpec((tm, d), lambda i: (i, 0)),
                  pl.BlockSpec((tm, dz), lambda i: (i, 0)),
                  pl.BlockSpec((dz, d), const),
                  pl.BlockSpec((1, d), const),
                  pl.BlockSpec((1, d), const),
                  pl.BlockSpec((d, d_ff), const),
                  pl.BlockSpec((d_ff, d), const),
                  pl.BlockSpec((1, d), const)],
        out_specs=pl.BlockSpec((tm, d), lambda i: (i, 0)),
        out_shape=jax.ShapeDtypeStruct((rows, d), F32),
        compiler_params=_cparams(("parallel",)),
        name="mixer_out_mlp",
    )(h, z, wp, bp.reshape(1, d), g.reshape(1, d), w1, w2, gf.reshape(1, d))


def _split_bf16(x):
    hi = x.astype(BF16)
    return hi, (x - hi.astype(F32)).astype(BF16)


def _dot3(a_hi, a_lo, b_hi, b_lo):
    dot = functools.partial(jnp.dot, preferred_element_type=F32)
    return dot(a_hi, b_hi) + (dot(a_hi, b_lo) + dot(a_lo, b_hi))


def _filter_kernel(z_ref, t_ref, m_ref, w1_ref, b1_ref, w2_ref, b2_ref, w3_ref, b3_ref, fr_ref, woh_ref, wol_ref,
                   ad_ref, o_ref):
    d = ad_ref.shape[1]
    dot = functools.partial(jnp.dot, precision=HIGHEST, preferred_element_type=F32)
    fr = fr_ref[...]
    h = jnp.sin(fr * (dot(z_ref[...], w1_ref[...]) + b1_ref[...]))
    h = jnp.sin(fr * (dot(h, w2_ref[...]) + b2_ref[...]))
    h = jnp.sin(fr * (dot(h, w3_ref[...]) + b3_ref[...]))
    ho = _dot3(*_split_bf16(h), woh_ref[...], wol_ref[...])
    decay = jnp.exp(-t_ref[...] * ad_ref[...])
    fwd = m_ref[...] > 0.5
    for o in range(2):
        o_ref[o] = jnp.where(fwd, ho[:, 2 * o * d:(2 * o + 1) * d], ho[:, (2 * o + 1) * d:(2 * o + 2) * d]) * decay


def _filter_rows(lag, is_fwd, seq_len, fp, d):
    r = lag.shape[0]
    tr = min(r, 512)
    lagf = lag.astype(F32)
    t = lagf / (seq_len - 1)
    w = 2.0 * math.pi * lagf / seq_len
    f = jnp.linspace(1e-4, HY_BANDS - 1, HY_BANDS, dtype=F32)[None, :]
    z = jnp.concatenate([t[:, None], jnp.cos(f * w[:, None]), -jnp.sin(f * w[:, None]),
                         jnp.zeros((r, HY_EMB_PAD - 2 * HY_BANDS - 1), F32)], axis=-1)
    w1 = jnp.pad(fp["w1"], ((0, HY_EMB_PAD - fp["w1"].shape[0]), (0, 0)))
    hid = w1.shape[1]
    max_decay = math.log(HY_DECAY_TARGET) / HY_FAST_DECAY
    min_decay = math.log(HY_DECAY_TARGET) / HY_SLOW_DECAY
    adel = jnp.abs(jnp.linspace(min_decay, max_decay, d, dtype=F32))[None, :]
    const = lambda i: (0, 0)
    row = lambda i: (i, 0)
    wo_hi, wo_lo = _split_bf16(fp["wout"])
    return pl.pallas_call(
        _filter_kernel,
        grid=(r // tr,),
        in_specs=[pl.BlockSpec((tr, HY_EMB_PAD), row), pl.BlockSpec((tr, 1), row), pl.BlockSpec((tr, 1), row),
                  pl.BlockSpec((HY_EMB_PAD, hid), const), pl.BlockSpec((1, hid), const),
                  pl.BlockSpec((hid, hid), const), pl.BlockSpec((1, hid), const),
                  pl.BlockSpec((hid, hid), const), pl.BlockSpec((1, hid), const),
                  pl.BlockSpec((1, hid), const), pl.BlockSpec((hid, 4 * d), const), pl.BlockSpec((hid, 4 * d), const),
                  pl.BlockSpec((1, d), const)],
        out_specs=pl.BlockSpec((2, tr, d), lambda i: (0, i, 0)),
        out_shape=jax.ShapeDtypeStruct((2, r, d), F32),
        compiler_params=_cparams(("parallel",)),
        name="hyena_filter",
    )(z, t[:, None], is_fwd.astype(F32)[:, None], w1, fp["b1"][None], fp["w2"], fp["b2"][None],
      fp["w3"], fp["b3"][None], fp["freq"][None], wo_hi, wo_lo, adel)


def _cplx_block(re, im):
    return jnp.concatenate([jnp.concatenate([re, -im], axis=-1), jnp.concatenate([im, re], axis=-1)], axis=-2)


def _fft_tables(n1, n2):
    n = n1 * n2
    n1h = n1 // 2
    pad = 8 - 1
    k1 = jnp.arange(n1, dtype=jnp.int32)
    c2 = jnp.arange(n2, dtype=jnp.int32)
    cols = jnp.concatenate([jnp.arange(n1h, dtype=jnp.int32), jnp.array([n1 - 1], jnp.int32)])
    pos = n2 * cols[None, None, :] + c2[:, None, None]
    ang = ((k1[None, :, None] * pos) % n).astype(F32) * (-2.0 * math.pi / n)
    wr = jnp.pad(jnp.cos(ang), ((0, 0), (0, 0), (0, pad)))
    wi = jnp.pad(jnp.sin(ang), ((0, 0), (0, 0), (0, pad)))
    fwd1 = _cplx_block(wr, wi)
    inv1 = _cplx_block(jnp.swapaxes(wr, 1, 2), -jnp.swapaxes(wi, 1, 2)) / n
    posf = n2 * k1[None, None, :] + c2[:, None, None]
    angf = ((k1[None, :, None] * posf) % n).astype(F32) * (-2.0 * math.pi / n)
    fil1 = jnp.concatenate([jnp.cos(angf), jnp.sin(angf)], axis=1)
    ang2 = ((c2[:, None] * c2[None, :]) % n2).astype(F32) * (-2.0 * math.pi / n2)
    fr, fi = jnp.cos(ang2), jnp.sin(ang2)
    fwd2 = _cplx_block(fr, fi)
    return dict(fwd1=fwd1.astype(BF16), inv1=inv1.astype(BF16), fil1=_split_bf16(fil1),
                fwd2=fwd2.astype(BF16), inv2=_cplx_block(fr, -fi).astype(BF16), fwd2_split=_split_bf16(fwd2))


def _ordered_short_conv(meta, real, cw_ref, cb_ref, seq_ref):
    s = real.shape[0]
    dt = real.shape[1]
    seq_ref[0:8, :] = jnp.zeros((8, dt), F32)
    seq_ref[8:8 + N_META, :] = meta.astype(F32)
    seq_ref[8 + N_META:8 + N_META + s, :] = real.astype(F32)
    seq_ref[8 + N_META + s:16 + N_META + s, :] = jnp.zeros((8, dt), F32)
    x = seq_ref[...]
    rows = x.shape[0]
    cw = cw_ref[...].astype(F32)
    y = (pltpu.roll(x, 1, axis=0) * cw[0:1] + x * cw[1:2] + pltpu.roll(x, rows - 1, axis=0) * cw[2:3]
         + cb_ref[...].astype(F32))
    return y[8:8 + N_META], y[8 + N_META:8 + N_META + s]


def _fft_in_kernel(*refs, n1, n2, short_conv):
    if short_conv:
        (xr_ref, xm_ref, cw_ref, cb_ref, f_ref, are_ref, aim_ref, vr_ref, vm_ref,
         xs_ref, mp_ref, seq_ref) = refs
    else:
        xr_ref, xm_ref, f_ref, are_ref, aim_ref, xs_ref, mp_ref = refs
    n1h = n1 // 2
    chunk = pl.program_id(2)

    @pl.when(chunk == 0)
    def _prepare():
        for e in range(2):
            if short_conv:
                meta, real = _ordered_short_conv(xm_ref[e], xr_ref[e], cw_ref, cb_ref, seq_ref)
                vr_ref[e] = real.astype(vr_ref.dtype)
                vm_ref[e] = meta.astype(vm_ref.dtype)
            else:
                meta, real = xm_ref[e].astype(F32), xr_ref[e].astype(F32)
            xs_ref[e] = real
            mp_ref[e] = jnp.zeros(mp_ref.shape[1:], F32)
            for j in range(N_META):
                mp_ref[e, pl.ds(8 * (n2 - N_META + j), 1), :] = meta[j:j + 1]

    for i in range(N2_CHUNK):
        c2 = chunk * N2_CHUNK + i
        parts = []
        for e in range(2):
            parts.append(xs_ref[e, pl.ds(c2, n1h, stride=n2), :])
            parts.append(mp_ref[e, pl.ds(pl.multiple_of(c2 * 8, 8), 8), :])
        rhs = jnp.concatenate(parts, axis=0).astype(BF16)
        out = jnp.dot(f_ref[i], rhs, preferred_element_type=F32)
        are_ref[i] = out[:n1]
        aim_ref[i] = out[n1:]


def _fft_in(xr, xm, col_off, conv_w, conv_b, tab, *, b, s, d, n1, n2, dt):
    short_conv = conv_w is not None
    pairs = b // 2
    c = xr.shape[1]
    cb0 = col_off // dt
    xr4 = xr.reshape(pairs, 2, s, c)
    xm4 = xm.reshape(pairs, 2, N_META, c)
    kk = tab["fwd1"].shape[2]
    in_specs = [pl.BlockSpec((None, 2, s, dt), lambda p, j, t: (p, 0, 0, cb0 + j)),
                pl.BlockSpec((None, 2, N_META, dt), lambda p, j, t: (p, 0, 0, cb0 + j))]
    args = [xr4, xm4]
    if short_conv:
        in_specs += [pl.BlockSpec((8, dt), lambda p, j, t: (0, cb0 + j)),
                     pl.BlockSpec((1, dt), lambda p, j, t: (0, cb0 + j))]
        args += [conv_w, conv_b]
    in_specs.append(pl.BlockSpec((N2_CHUNK, 2 * n1, kk), lambda p, j, t: (t, 0, 0)))
    args.append(tab["fwd1"])
    a_spec = pl.BlockSpec((None, N2_CHUNK, n1, dt), lambda p, j, t: (p, t, 0, j))
    a_shape = jax.ShapeDtypeStruct((pairs, n2, n1, d), F32)
    out_specs = [a_spec, a_spec]
    out_shape = [a_shape, a_shape]
    scratch = [pltpu.VMEM((2, s, dt), F32), pltpu.VMEM((2, 8 * n2, dt), F32)]
    if short_conv:
        out_specs += [pl.BlockSpec((None, 2, s, dt), lambda p, j, t: (p, 0, 0, j)),
                      pl.BlockSpec((None, 2, N_META, dt), lambda p, j, t: (p, 0, 0, j))]
        out_shape += [jax.ShapeDtypeStruct((pairs, 2, s, d), BF16),
                      jax.ShapeDtypeStruct((pairs, 2, N_META, d), BF16)]
        scratch.append(pltpu.VMEM((s + N_META + 16, dt), F32))
    return pl.pallas_call(
        functools.partial(_fft_in_kernel, n1=n1, n2=n2, short_conv=short_conv),
        grid=(pairs, d // dt, n2 // N2_CHUNK),
        in_specs=in_specs, out_specs=out_specs, out_shape=out_shape, scratch_shapes=scratch,
        compiler_params=_cparams(("parallel", "parallel", "arbitrary")),
        name="fft_in",
    )(*args)


def _fft_mid_kernel(are_ref, aim_ref, kre_ref, kim_ref, f_ref, g_ref, bre_ref, bim_ref, *, n2, slabs):
    for i in range(slabs):
        x = jnp.concatenate([are_ref[:, i, :], aim_ref[:, i, :]], axis=0).astype(BF16)
        z = jnp.dot(f_ref[...], x, preferred_element_type=F32)
        zr, zi = z[:n2], z[n2:]
        kr, ki = kre_ref[i], kim_ref[i]
        p = jnp.concatenate([zr * kr - zi * ki, zr * ki + zi * kr], axis=0).astype(BF16)
        y = jnp.dot(g_ref[...], p, preferred_element_type=F32)
        bre_ref[:, i, :] = y[:n2]
        bim_ref[:, i, :] = y[n2:]


def _fft_mid(are, aim, kre, kim, k_blk0, tab, *, dt, slabs):
    pairs, n2, n1, d = are.shape
    blk = (None, n2, slabs, dt)
    amap = lambda j, kb, p: (p, 0, kb, j)
    kmap = lambda j, kb, p: (k_blk0 + kb, 0, j)
    const = lambda j, kb, p: (0, 0)
    shp = jax.ShapeDtypeStruct((pairs, n2, n1, d), F32)
    return pl.pallas_call(
        functools.partial(_fft_mid_kernel, n2=n2, slabs=slabs),
        grid=(d // dt, n1 // slabs, pairs),
        in_specs=[pl.BlockSpec(blk, amap), pl.BlockSpec(blk, amap),
                  pl.BlockSpec((slabs, n2, dt), kmap), pl.BlockSpec((slabs, n2, dt), kmap),
                  pl.BlockSpec((2 * n2, 2 * n2), const), pl.BlockSpec((2 * n2, 2 * n2), const)],
        out_specs=[pl.BlockSpec(blk, amap), pl.BlockSpec(blk, amap)],
        out_shape=[shp, shp],
        compiler_params=_cparams(("parallel", "parallel", "arbitrary")),
        name="fft_mid",
    )(are, aim, kre, kim, tab["fwd2"], tab["inv2"])


def _filter_fft_in_kernel(c_ref, fh_ref, fl_ref, are_ref, aim_ref, *, n1, n2):
    chunk = pl.program_id(2)
    for i in range(N2_CHUNK):
        rhs = c_ref[pl.ds(chunk * N2_CHUNK + i, n1, stride=n2), :]
        out = _dot3(fh_ref[i], fl_ref[i], *_split_bf16(rhs))
        are_ref[i] = out[:n1]
        aim_ref[i] = out[n1:]


def _filter_fft_mid_kernel(are_ref, aim_ref, fh_ref, fl_ref, kre_ref, kim_ref, *, n2, slabs):
    for i in range(slabs):
        x = jnp.concatenate([are_ref[:, i, :], aim_ref[:, i, :]], axis=0)
        z = _dot3(fh_ref[...], fl_ref[...], *_split_bf16(x))
        kre_ref[i] = z[:n2]
        kim_ref[i] = z[n2:]


def _filter_spectrum(circ, tab, *, n1, n2, dt, slabs):
    orders, n, d = circ.shape
    shp = [jax.ShapeDtypeStruct((orders, n2, n1, d), F32)] * 2
    tab_spec = pl.BlockSpec((N2_CHUNK, 2 * n1, n1), lambda o, j, t: (t, 0, 0))
    lt = LANE_TILE
    are, aim = pl.pallas_call(
        functools.partial(_filter_fft_in_kernel, n1=n1, n2=n2),
        grid=(orders, d // lt, n2 // N2_CHUNK),
        in_specs=[pl.BlockSpec((None, n, lt), lambda o, j, t: (o, 0, j)), tab_spec, tab_spec],
        out_specs=[pl.BlockSpec((None, N2_CHUNK, n1, lt), lambda o, j, t: (o, t, 0, j))] * 2,
        out_shape=shp,
        compiler_params=_cparams(("parallel", "parallel", "arbitrary")),
        name="filter_fft_in",
    )(circ, *tab["fil1"])
    blk = (None, n2, slabs, dt)
    amap = lambda o, kb, j: (o, 0, kb, j)
    f_spec = pl.BlockSpec((2 * n2, 2 * n2), lambda o, kb, j: (0, 0))
    return pl.pallas_call(
        functools.partial(_filter_fft_mid_kernel, n2=n2, slabs=slabs),
        grid=(orders, n1 // slabs, d // dt),
        in_specs=[pl.BlockSpec(blk, amap), pl.BlockSpec(blk, amap), f_spec, f_spec],
        out_specs=[pl.BlockSpec((None, slabs, n2, dt), lambda o, kb, j: (o, kb, 0, j))] * 2,
        out_shape=[jax.ShapeDtypeStruct((orders, n1, n2, d), F32)] * 2,
        compiler_params=_cparams(("parallel", "parallel", "parallel")),
        name="filter_fft_mid",
    )(are, aim, *tab["fwd2_split"])


def _alias_patch(tail, u_meta, u_last):
    dfw = tail[0:16] - tail[16:32]
    dbw = tail[32:48] - tail[48:64]
    ridx = lax.broadcasted_iota(jnp.int32, dfw.shape, 0)
    real_fix = jnp.zeros_like(dfw)
    meta_fix = jnp.zeros_like(dfw)
    for o in range(15):
        shifted = jnp.where(ridx >= o + 1, pltpu.roll(dfw, o + 1, axis=0), 0.0)
        real_fix = real_fix + shifted * u_meta[o:o + 1]
    for c in range(16):
        src = u_last if c == 0 else pltpu.roll(u_last, 16 - c, axis=0)
        meta_fix = meta_fix + jnp.where(ridx + c <= 15, src, 0.0) * dbw[c:c + 1]
    return real_fix, meta_fix


def _fft_out_kernel(bre_ref, bim_ref, g_ref, gr_ref, gm_ref, cw_ref, cb_ref, ur_ref, um_ref, skip_ref,
                    tail_ref, zr_ref, zm_ref, ys_ref, yp_ref, seq_ref, *, n1, n2):
    n1h = n1 // 2
    kk = n1h + 8
    s = ys_ref.shape[1]
    chunk = pl.program_id(2)

    for i in range(N2_CHUNK):
        c2 = chunk * N2_CHUNK + i
        rhs = jnp.concatenate([bre_ref[i], bim_ref[i]], axis=0).astype(BF16)
        y = jnp.dot(g_ref[i], rhs, preferred_element_type=F32)
        for e in range(2):
            ys_ref[e, pl.ds(c2, n1h, stride=n2), :] = y[e * kk:e * kk + n1h]
            yp_ref[e, pl.ds(pl.multiple_of(c2 * 8, 8), 8), :] = y[e * kk + n1h:(e + 1) * kk]

    @pl.when(chunk == n2 // N2_CHUNK - 1)
    def _gate():
        tail = tail_ref[...]
        skip = skip_ref[...]
        for e in range(2):
            y_meta = jnp.concatenate(
                [yp_ref[e, pl.ds(8 * (n2 - N_META + j), 1), :] for j in range(N_META)], axis=0)
            u_real = ur_ref[e].astype(F32)
            u_meta = um_ref[e].astype(F32)
            real_fix, meta_fix = _alias_patch(tail, u_meta, u_real[s - 16:s])
            ys_ref[e, s - 16:s, :] = ys_ref[e, s - 16:s, :] + real_fix
            g_meta, g_real = _ordered_short_conv(gm_ref[e], gr_ref[e], cw_ref, cb_ref, seq_ref)
            zr_ref[e] = (g_real * (ys_ref[e] + skip * u_real)).astype(zr_ref.dtype)
            zm_ref[e] = (g_meta * (y_meta + meta_fix + skip * u_meta)).astype(zm_ref.dtype)


def _fft_out(bre, bim, tab, gr, gm, gate_off, conv_w, conv_b, ur, um, skip, tail, *, b, s, d, n1, n2, dt):
    pairs = b // 2
    c = gr.shape[1]
    gb0 = gate_off // dt
    kk2 = tab["inv1"].shape[1]
    gr4 = gr.reshape(pairs, 2, s, c)
    gm4 = gm.reshape(pairs, 2, N_META, c)
    ur4 = ur.reshape(pairs, 2, s, d)
    um4 = um.reshape(pairs, 2, N_META, d)
    b_spec = pl.BlockSpec((None, N2_CHUNK, n1, dt), lambda p, j, t: (p, t, 0, j))
    zr, zm = pl.pallas_call(
        functools.partial(_fft_out_kernel, n1=n1, n2=n2),
        grid=(pairs, d // dt, n2 // N2_CHUNK),
        in_specs=[b_spec, b_spec,
                  pl.BlockSpec((N2_CHUNK, kk2, 2 * n1), lambda p, j, t: (t, 0, 0)),
                  pl.BlockSpec((None, 2, s, dt), lambda p, j, t: (p, 0, 0, gb0 + j)),
                  pl.BlockSpec((None, 2, N_META, dt), lambda p, j, t: (p, 0, 0, gb0 + j)),
                  pl.BlockSpec((8, dt), lambda p, j, t: (0, gb0 + j)),
                  pl.BlockSpec((1, dt), lambda p, j, t: (0, gb0 + j)),
                  pl.BlockSpec((None, 2, s, dt), lambda p, j, t: (p, 0, 0, j)),
                  pl.BlockSpec((None, 2, N_META, dt), lambda p, j, t: (p, 0, 0, j)),
                  pl.BlockSpec((1, dt), lambda p, j, t: (0, j)),
                  pl.BlockSpec((64, dt), lambda p, j, t: (0, j))],
        out_specs=[pl.BlockSpec((None, 2, s, dt), lambda p, j, t: (p, 0, 0, j)),
                   pl.BlockSpec((None, 2, N_META, dt), lambda p, j, t: (p, 0, 0, j))],
        out_shape=[jax.ShapeDtypeStruct((pairs, 2, s, d), BF16),
                   jax.ShapeDtypeStruct((pairs, 2, N_META, d), BF16)],
        scratch_shapes=[pltpu.VMEM((2, s, dt), F32), pltpu.VMEM((2, 8 * n2, dt), F32),
                        pltpu.VMEM((s + N_META + 16, dt), F32)],
        compiler_params=_cparams(("parallel", "parallel", "arbitrary")),
        name="fft_out",
    )(bre, bim, tab["inv1"], gr4, gm4, conv_w, conv_b, ur4, um4, skip.reshape(1, d), tail)
    return zr.reshape(b * s, d), zm.reshape(b * N_META, d)


def _fused_conv_kernel(*refs, n1, n2, slabs, short_conv):
    if short_conv:
        (xr_ref, xm_ref, cwx_ref, cbx_ref, f1_ref, kre_ref, kim_ref, f2_ref, g2_ref, g1_ref, gr_ref, gm_ref,
         cwg_ref, cbg_ref, skip_ref, tail_ref, zr_ref, zm_ref, ur_ref, um_ref,
         are_ref, aim_ref, bre_ref, bim_ref, xs_ref, mp_ref, seq_ref) = refs
    else:
        (ur_ref, um_ref, f1_ref, kre_ref, kim_ref, f2_ref, g2_ref, g1_ref, gr_ref, gm_ref,
         cwg_ref, cbg_ref, skip_ref, tail_ref, zr_ref, zm_ref,
         are_ref, aim_ref, bre_ref, bim_ref, xs_ref, mp_ref, seq_ref) = refs
    n1h = n1 // 2
    kk = n1h + 8
    s = xs_ref.shape[1]
    t1 = n2 // N2_CHUNK
    t2 = n1 // slabs
    step = pl.program_id(2)

    @pl.when(step == 0)
    def _prepare():
        for e in range(2):
            if short_conv:
                meta, real = _ordered_short_conv(xm_ref[e], xr_ref[e], cwx_ref, cbx_ref, seq_ref)
                ur_ref[e] = real.astype(ur_ref.dtype)
                um_ref[e] = meta.astype(um_ref.dtype)
            else:
                meta, real = um_ref[e].astype(F32), ur_ref[e].astype(F32)
            xs_ref[e] = real
            mp_ref[e] = jnp.zeros(mp_ref.shape[1:], F32)
            for j in range(N_META):
                mp_ref[e, pl.ds(8 * (n2 - N_META + j), 1), :] = meta[j:j + 1]

    @pl.when(step < t1)
    def _stage1():
        for i in range(N2_CHUNK):
            c2 = step * N2_CHUNK + i
            parts = []
            for e in range(2):
                parts.append(xs_ref[e, pl.ds(c2, n1h, stride=n2), :])
                parts.append(mp_ref[e, pl.ds(pl.multiple_of(c2 * 8, 8), 8), :])
            rhs = jnp.concatenate(parts, axis=0).astype(BF16)
            out = jnp.dot(f1_ref[i], rhs, preferred_element_type=F32)
            are_ref[pl.ds(c2, n1, stride=n2), :] = out[:n1]
            aim_ref[pl.ds(c2, n1, stride=n2), :] = out[n1:]

    @pl.when(jnp.logical_and(step >= t1, step < t1 + t2))
    def _stage2():
        for i in range(slabs):
            k1 = (step - t1) * slabs + i
            rows = pl.ds(pl.multiple_of(k1 * n2, n2), n2)
            x = jnp.concatenate([are_ref[rows, :], aim_ref[rows, :]], axis=0).astype(BF16)
            z = jnp.dot(f2_ref[...], x, preferred_element_type=F32)
            zr, zi = z[:n2], z[n2:]
            kr, ki = kre_ref[i], kim_ref[i]
            p = jnp.concatenate([zr * kr - zi * ki, zr * ki + zi * kr], axis=0).astype(BF16)
            y = jnp.dot(g2_ref[...], p, preferred_element_type=F32)
            bre_ref[pl.ds(k1, n2, stride=n1), :] = y[:n2]
            bim_ref[pl.ds(k1, n2, stride=n1), :] = y[n2:]

    @pl.when(step >= t1 + t2)
    def _stage3():
        for i in range(N2_CHUNK):
            c2 = (step - t1 - t2) * N2_CHUNK + i
            rows = pl.ds(pl.multiple_of(c2 * n1, n1), n1)
            rhs = jnp.concatenate([bre_ref[rows, :], bim_ref[rows, :]], axis=0).astype(BF16)
            y = jnp.dot(g1_ref[i], rhs, preferred_element_type=F32)
            for e in range(2):
                xs_ref[e, pl.ds(c2, n1h, stride=n2), :] = y[e * kk:e * kk + n1h]
                mp_ref[e, pl.ds(pl.multiple_of(c2 * 8, 8), 8), :] = y[e * kk + n1h:(e + 1) * kk]

    @pl.when(step == 2 * t1 + t2 - 1)
    def _gate():
        tail = tail_ref[...]
        skip = skip_ref[...]
        for e in range(2):
            y_meta = jnp.concatenate(
                [mp_ref[e, pl.ds(8 * (n2 - N_META + j), 1), :] for j in range(N_META)], axis=0)
            u_real = ur_ref[e].astype(F32)
            u_meta = um_ref[e].astype(F32)
            real_fix, meta_fix = _alias_patch(tail, u_meta, u_real[s - 16:s])
            xs_ref[e, s - 16:s, :] = xs_ref[e, s - 16:s, :] + real_fix
            g_meta, g_real = _ordered_short_conv(gm_ref[e], gr_ref[e], cwg_ref, cbg_ref, seq_ref)
            zr_ref[e] = (g_real * (xs_ref[e] + skip * u_real)).astype(zr_ref.dtype)
            zm_ref[e] = (g_meta * (y_meta + meta_fix + skip * u_meta)).astype(zm_ref.dtype)


def _fused_conv(xr, xm, col_off, conv_w, conv_b, kre, kim, k_blk0, tab, gr, gm, gate_off, skip, tail, *,
                b, s, d, n1, n2):
    short_conv = col_off is not None
    pairs = b // 2
    dt = LANE_TILE
    slabs = 8
    t1, t2 = n2 // N2_CHUNK, n1 // slabs
    steps = 2 * t1 + t2
    c = xr.shape[1]
    cx0 = (col_off or 0) // dt
    cg0 = gate_off // dt
    kk2 = tab["fwd1"].shape[2]
    seq4 = lambda a, rows: a.reshape(pairs, 2, rows, a.shape[1])
    x_spec = lambda rows, c0: pl.BlockSpec((None, 2, rows, dt), lambda j, p, t: (p, 0, 0, c0 + j))
    row_spec = lambda rows, c0: pl.BlockSpec((rows, dt), lambda j, p, t: (0, c0 + j))
    const2 = pl.BlockSpec((2 * n2, 2 * n2), lambda j, p, t: (0, 0))
    k_spec = pl.BlockSpec((slabs, n2, dt), lambda j, p, t: (k_blk0 + jnp.clip(t - t1, 0, t2 - 1), 0, j))
    in_specs = [x_spec(s, cx0), x_spec(N_META, cx0)]
    args = [seq4(xr, s), seq4(xm, N_META)]
    if short_conv:
        in_specs += [row_spec(8, cx0), row_spec(1, cx0)]
        args += [conv_w, conv_b]
    in_specs += [pl.BlockSpec((N2_CHUNK, 2 * n1, kk2), lambda j, p, t: (jnp.minimum(t, t1 - 1), 0, 0)),
                 k_spec, k_spec, const2, const2,
                 pl.BlockSpec((N2_CHUNK, kk2, 2 * n1), lambda j, p, t: (jnp.clip(t - t1 - t2, 0, t1 - 1), 0, 0)),
                 x_spec(s, cg0), x_spec(N_META, cg0), row_spec(8, cg0), row_spec(1, cg0),
                 row_spec(1, 0), row_spec(64, 0)]
    args += [tab["fwd1"], kre, kim, tab["fwd2"], tab["inv2"], tab["inv1"],
             seq4(gr, s), seq4(gm, N_META), conv_w, conv_b, skip.reshape(1, d), tail]
    seq_out = lambda rows: pl.BlockSpec((None, 2, rows, dt), lambda j, p, t: (p, 0, 0, j))
    out_specs = [seq_out(s), seq_out(N_META)]
    out_shape = [jax.ShapeDtypeStruct((pairs, 2, s, d), BF16), jax.ShapeDtypeStruct((pairs, 2, N_META, d), BF16)]
    if short_conv:
        out_specs = out_specs * 2
        out_shape = out_shape * 2
    spec_rows = n1 * n2
    scratch = [pltpu.VMEM((spec_rows, dt), F32)] * 4 + [pltpu.VMEM((2, s, dt), F32), pltpu.VMEM((2, 8 * n2, dt), F32),
                                                        pltpu.VMEM((s + N_META + 16, dt), F32)]
    outs = pl.pallas_call(
        functools.partial(_fused_conv_kernel, n1=n1, n2=n2, slabs=slabs, short_conv=short_conv),
        grid=(d // dt, pairs, steps),
        in_specs=in_specs, out_specs=out_specs, out_shape=out_shape, scratch_shapes=scratch,
        compiler_params=_cparams(("parallel", "parallel", "arbitrary")),
        name="fused_conv",
    )(*args)
    return [o.reshape(-1, d) for o in outs]


def _attn_kernel(own_ref, vprev_ref, vnext_ref, vmeta_ref, ktp_ref, kto_ref, ktn_ref, ktm_ref,
                 bias_ref, shift_ref, o_ref, *, groups):
    blk = pl.program_id(1).astype(F32)
    gw = GROUP * HEAD_DIM
    qd = groups * gw
    lane = lax.broadcasted_iota(jnp.int32, (1, LANE_TILE), 1)
    low = lane < HEAD_DIM
    pad_rows = jnp.zeros((ATT_BLOCK - N_META, 2 * HEAD_DIM), BF16)
    zero = jnp.zeros((ATT_BLOCK, LANE_TILE), BF16)
    for g in range(groups):
        vk_own = own_ref[:, qd + g * LANE_TILE:qd + (g + 1) * LANE_TILE]
        grp = slice(g * LANE_TILE, (g + 1) * LANE_TILE)
        vk = jnp.concatenate([vprev_ref[:, grp], vk_own, vnext_ref[:, grp], vmeta_ref[:, grp], pad_rows],
                             axis=0)
        v_ones = jnp.where(low, vk, jnp.ones_like(vk))
        kt_rows = slice(g * HEAD_DIM, (g + 1) * HEAD_DIM)
        kt = jnp.concatenate([ktp_ref[kt_rows, :], kto_ref[kt_rows, :], ktn_ref[kt_rows, :], ktm_ref[kt_rows, :]],
                             axis=1)
        kt2 = jnp.concatenate([kt, kt], axis=0)
        parts = []
        for pr in range(GROUP // 2):
            qp = own_ref[:, g * gw + pr * LANE_TILE:g * gw + (pr + 1) * LANE_TILE]
            parts += [jnp.where(low, qp, zero), jnp.where(low, zero, qp)]
        q = jnp.concatenate(parts, axis=0)
        sc = jnp.dot(q, kt2, preferred_element_type=F32) + bias_ref[g]
        t = [sc[:, i * LANE_TILE:(i + 1) * LANE_TILE] for i in range(3)]
        t.append(sc[:, 3 * LANE_TILE:] - shift_ref[g] * blk)
        m = jnp.max(jnp.maximum(jnp.maximum(t[0], t[1]), jnp.maximum(t[2], t[3])), axis=1, keepdims=True)
        p = jnp.concatenate([jnp.exp(x - m) for x in t], axis=1).astype(BF16)
        oa = jnp.dot(p, v_ones, preferred_element_type=F32)
        ob = pltpu.roll(oa, HEAD_DIM, axis=1)
        outs = []
        for pr in range(GROUP // 2):
            ev = slice((2 * pr) * ATT_BLOCK, (2 * pr + 1) * ATT_BLOCK)
            od = slice((2 * pr + 1) * ATT_BLOCK, (2 * pr + 2) * ATT_BLOCK)
            outs.append(jnp.where(low, oa[ev] / ob[ev], ob[od] / oa[od]))
        o_ref[:, g * gw:(g + 1) * gw] = jnp.concatenate(outs, axis=1).astype(o_ref.dtype)


def _attention_tables(n_heads, sink):
    groups = n_heads // GROUP
    slopes = jnp.exp2(-8.0 * jnp.arange(1, n_heads + 1, dtype=F32) / n_heads)
    i = jnp.arange(ATT_BLOCK, dtype=jnp.int32)[:, None]
    c = jnp.arange(4 * ATT_BLOCK, dtype=jnp.int32)[None, :]
    dist = jnp.abs(c - ATT_BLOCK - i)
    key_blk = c // ATT_BLOCK
    in_band = jnp.logical_and(c < 3 * ATT_BLOCK, dist <= ATT_BLOCK)
    meta_col = jnp.logical_and(c >= 3 * ATT_BLOCK, c < 3 * ATT_BLOCK + N_META)
    sink_col = c == 3 * ATT_BLOCK + N_META
    meta_dist = N_META + i - (c - 3 * ATT_BLOCK)
    tables = []
    for drop in (None, 0, 2):
        ok = in_band if drop is None else jnp.logical_and(in_band, key_blk != drop)
        d_eff = jnp.where(ok, dist, jnp.where(meta_col, meta_dist, 0)).astype(F32)
        live = jnp.logical_or(ok, meta_col)
        tab = jnp.where(live[None], -slopes[:, None, None] * d_eff[None], MASK_VALUE)
        tables.append(jnp.where(sink_col[None], sink.astype(F32)[:, None, None], tab))
    bias = jnp.stack(tables, axis=0).reshape(3, groups, GROUP * ATT_BLOCK, 4 * ATT_BLOCK)
    lane = jnp.arange(LANE_TILE)[None, :]
    shift = jnp.where(lane < N_META, jnp.repeat(slopes * ATT_BLOCK, ATT_BLOCK)[:, None], 0.0)
    return bias, shift.reshape(groups, GROUP * ATT_BLOCK, LANE_TILE)


def _attention(qvk_r, kt_r, qvk_m, kt_m, sink, *, b, s, n_heads):
    groups = n_heads // GROUP
    nblk = s // ATT_BLOCK
    qd = n_heads * HEAD_DIM
    width = qvk_r.shape[1]
    vkw = groups * 2 * HEAD_DIM
    vkb = qd // vkw
    bias, shift = _attention_tables(n_heads, sink)

    def variant(j):
        return jnp.where(j == 0, 1, jnp.where(j == nblk - 1, 2, 0))

    prev = lambda i, j: i * nblk + jnp.maximum(j - 1, 0)
    nxt = lambda i, j: i * nblk + jnp.minimum(j + 1, nblk - 1)
    return pl.pallas_call(
        functools.partial(_attn_kernel, groups=groups),
        grid=(b, nblk),
        in_specs=[pl.BlockSpec((ATT_BLOCK, width), lambda i, j: (i * nblk + j, 0)),
                  pl.BlockSpec((ATT_BLOCK, vkw), lambda i, j: (prev(i, j), vkb)),
                  pl.BlockSpec((ATT_BLOCK, vkw), lambda i, j: (nxt(i, j), vkb)),
                  pl.BlockSpec((N_META, vkw), lambda i, j: (i, vkb)),
                  pl.BlockSpec((groups * HEAD_DIM, ATT_BLOCK), lambda i, j: (0, prev(i, j))),
                  pl.BlockSpec((groups * HEAD_DIM, ATT_BLOCK), lambda i, j: (0, i * nblk + j)),
                  pl.BlockSpec((groups * HEAD_DIM, ATT_BLOCK), lambda i, j: (0, nxt(i, j))),
                  pl.BlockSpec((None, groups * HEAD_DIM, LANE_TILE), lambda i, j: (i, 0, 0)),
                  pl.BlockSpec((None, groups, GROUP * ATT_BLOCK, 4 * ATT_BLOCK), lambda i, j: (variant(j), 0, 0, 0)),
                  pl.BlockSpec((groups, GROUP * ATT_BLOCK, LANE_TILE), lambda i, j: (0, 0, 0))],
        out_specs=pl.BlockSpec((ATT_BLOCK, qd), lambda i, j: (i * nblk + j, 0)),
        out_shape=jax.ShapeDtypeStruct((b * s, qd), BF16),
        compiler_params=_cparams(("parallel", "arbitrary")),
        name="window_attention",
    )(qvk_r, qvk_r, qvk_r, qvk_m, kt_r, kt_r, kt_r, kt_m, bias, shift)


def _fft_split(s):
    n2 = 128 if s >= 1024 else 32
    return (2 * s) // n2, n2


def _fused_conv_vmem_bytes(s, n1, n2):
    lane_bytes = LANE_TILE * 4
    spectrum = 4 * n1 * n2 * lane_bytes
    sequence = (2 * s + 2 * 8 * n2 + s + N_META + 16) * lane_bytes
    blocks = 4 * 2 * 2 * (s + N_META) * LANE_TILE * 2
    return spectrum + sequence + blocks


def _hyena_conv(xr, xm, col_off, conv_w, conv_b, kre, kim, order, tab, gr, gm, gate_off, skip, tail, *, dims):
    b, s, d, n1, n2 = dims
    k_blk0 = order * (n1 // 8)
    if _fused_conv_vmem_bytes(s, n1, n2) <= (VMEM_LIMIT * 7) // 8:
        outs = _fused_conv(xr, xm, col_off, conv_w, conv_b, kre, kim, k_blk0, tab, gr, gm, gate_off, skip, tail,
                           b=b, s=s, d=d, n1=n1, n2=n2)
        return outs[0], outs[1]
    kw = dict(b=b, s=s, d=d, n1=n1, n2=n2, dt=LANE_TILE)
    if col_off is not None:
        are, aim, ur4, um4 = _fft_in(xr, xm, col_off, conv_w, conv_b, tab, **kw)
        ur, um = ur4.reshape(b * s, d), um4.reshape(b * N_META, d)
    else:
        are, aim = _fft_in(xr, xm, 0, None, None, tab, **kw)
        ur, um = xr, xm
    bre, bim = _fft_mid(are, aim, kre, kim, k_blk0, tab, dt=min(d, FFT_MID_LANES), slabs=8)
    return _fft_out(bre, bim, tab, gr, gm, gate_off, conv_w, conv_b, ur, um, skip, tail, **kw)


def _hyena_layer(streams, fp, g_mix, w_in, conv_w, conv_b, skip, tm):
    outs = []
    d = w_in.shape[0]
    for st in streams:
        b, s = st["b"], st["s"]
        n1, n2 = _fft_split(s)
        seq_len = s + N_META
        n = 2 * s
        tab = _fft_tables(n1, n2)
        r = jnp.arange(n, dtype=jnp.int32)
        circ = _filter_rows(jnp.where(r <= s, r, n - r), r <= s, seq_len, fp, d)
        kre, kim = [k.reshape(2 * n1, n2, d)
                    for k in _filter_spectrum(circ, tab, n1=n1, n2=n2, dt=2 * LANE_TILE, slabs=8)]
        a = jnp.arange(16, dtype=jnp.int32)
        tail_lag = jnp.concatenate([jnp.minimum(s + 1 + a, seq_len - 1), s - 1 - a, s + a, s - a])
        tail_fwd = jnp.concatenate([jnp.ones(16, bool), jnp.zeros(32, bool), jnp.ones(16, bool)])
        tail = _filter_rows(tail_lag, tail_fwd, seq_len, fp, d)
        pr = _norm_matmul(st["hr"], g_mix, w_in, tm)
        pm = _norm_matmul(st["hm"], g_mix, w_in, st["hm"].shape[0])
        dims = (b, s, d, n1, n2)
        z1r, z1m = _hyena_conv(pr, pm, 0, conv_w, conv_b, kre, kim, 0, tab, pr, pm, d, skip[0], tail[0], dims=dims)
        z2r, z2m = _hyena_conv(z1r, z1m, None, conv_w, conv_b, kre, kim, 1, tab, pr, pm, 2 * d, skip[1], tail[1],
                               dims=dims)
        outs.append((z2r, z2m))
    return outs


def _attention_weights(w_qkv, n_heads):
    groups = n_heads // GROUP
    qd = n_heads * HEAD_DIM
    kd = groups * HEAD_DIM
    w_q = w_qkv[:, :qd] * (HEAD_DIM ** -0.5)
    w_k = w_qkv[:, qd:qd + kd]
    w_v = w_qkv[:, qd + kd:]
    d = w_qkv.shape[0]
    w_vk = jnp.stack([w_v.reshape(d, groups, HEAD_DIM), w_k.reshape(d, groups, HEAD_DIM)], axis=2)
    w_rows = jnp.concatenate([w_q, w_vk.reshape(d, 2 * kd)], axis=1).astype(BF16)
    return w_rows, w_k.T.astype(BF16)


def _meta_keys_transposed(qvk_m, b, n_heads):
    groups = n_heads // GROUP
    qd = n_heads * HEAD_DIM
    k_m = qvk_m[:, qd:].reshape(b, N_META, groups, 2, HEAD_DIM)[:, :, :, 1, :]
    kt = jnp.transpose(k_m.reshape(b, N_META, groups * HEAD_DIM), (0, 2, 1))
    return jnp.pad(kt, ((0, 0), (0, 0), (0, LANE_TILE - N_META)))


def _encoder_pair(x_prompt, x_sample, meta_tokens, norm_mix, norm_mlp, norm_final,
                  hy_w_in, hy_conv_w, hy_conv_b, fps, hy_skip, hy_w_out, hy_b_out,
                  at_w_qkv, at_sink, at_w_o, mlp_w1, mlp_w2, *, n_heads, tm):
    d = x_prompt.shape[-1]
    streams = []
    for x in (x_prompt, x_sample):
        b, s, _ = x.shape
        streams.append(dict(b=b, s=s, hr=x.reshape(b * s, d),
                            hm=jnp.tile(meta_tokens.astype(F32), (b, 1))))
    zeros_d = jnp.zeros((d,), F32)

    conv_w = jnp.pad(hy_conv_w[0], ((0, 5), (0, 0)))
    conv_b = hy_conv_b[0][None, :]
    zs = _hyena_layer(streams, fps[0], norm_mix[0], hy_w_in[0].astype(BF16), conv_w, conv_b, hy_skip[0], tm)
    w_out = hy_w_out[0].astype(BF16)
    w1 = [w.astype(BF16) for w in mlp_w1]
    w2 = [w.astype(BF16) for w in mlp_w2]
    for st, (zr, zm) in zip(streams, zs):
        st["hr"] = _mixer_out_mlp(st["hr"], zr, w_out, hy_b_out[0], norm_mlp[0], w1[0], w2[0], zeros_d, tm, False)
        st["hm"] = _mixer_out_mlp(st["hm"], zm, w_out, hy_b_out[0], norm_mlp[0], w1[0], w2[0], zeros_d,
                                  st["hm"].shape[0], False)

    w_rows, w_kt = _attention_weights(at_w_qkv[0], n_heads)
    w_o = at_w_o[0].astype(BF16)
    outs = []
    for st in streams:
        qvk_r, kt_r = _norm_matmul(st["hr"], norm_mix[1], w_rows, tm, wt=w_kt)
        qvk_m = _norm_matmul(st["hm"], norm_mix[1], w_rows, st["hm"].shape[0])
        kt_m = _meta_keys_transposed(qvk_m, st["b"], n_heads)
        att = _attention(qvk_r, kt_r, qvk_m, kt_m, at_sink[0], b=st["b"], s=st["s"], n_heads=n_heads)
        y = _mixer_out_mlp(st["hr"], att, w_o, zeros_d, norm_mlp[1], w1[1], w2[1], norm_final, tm, True)
        outs.append(y.reshape(st["b"], st["s"], d))
    return tuple(outs)


def kernel(x_prompt, x_sample, meta_tokens, norm_mix, norm_mlp, norm_final, hy_w_in, hy_conv_w, hy_conv_b,
           hy_f_w1, hy_f_b1, hy_f_w2, hy_f_b2, hy_f_w3, hy_f_b3, hy_f_wout, hy_f_freq, hy_skip, hy_w_out,
           hy_b_out, at_w_qkv, at_sink, at_w_o, mlp_w1, mlp_w2):
    fps = [dict(w1=hy_f_w1[j], b1=hy_f_b1[j], w2=hy_f_w2[j], b2=hy_f_b2[j], w3=hy_f_w3[j], b3=hy_f_b3[j],
                wout=hy_f_wout[j], freq=hy_f_freq[j]) for j in range(hy_f_w1.shape[0])]
    n_heads = at_sink.shape[1]
    return _encoder_pair(x_prompt, x_sample, meta_tokens, norm_mix, norm_mlp, norm_final,
                         hy_w_in, hy_conv_w, hy_conv_b, fps, hy_skip, hy_w_out, hy_b_out,
                         at_w_qkv, at_sink, at_w_o, mlp_w1, mlp_w2, n_heads=n_heads, tm=512)
```

```python
import functools
import math

import jax
import jax.numpy as jnp
from jax import lax
from jax.experimental import pallas as pl
from jax.experimental.pallas import tpu as pltpu

F32 = jnp.float32
BF16 = jnp.bfloat16

N_META = 16
RMS_EPS = 1e-6
HY_BANDS = 16
HY_EMB_PAD = 40
HY_FAST_DECAY = 0.3
HY_SLOW_DECAY = 1.5
HY_DECAY_TARGET = 1e-2
ATT_BLOCK = 128
HEAD_DIM = 64
GROUP = 4
MASK_VALUE = -1e30
FF_CHUNK = 1024
LANE_TILE = 128
N2_CHUNK = 32
FUSED_SLABS = 16
FFT_MID_LANES = 512
VMEM_LIMIT = 56 * 1024 * 1024
HIGHEST = lax.Precision.HIGHEST


def _cparams(sem):
    return pltpu.CompilerParams(dimension_semantics=sem, vmem_limit_bytes=VMEM_LIMIT)


def _rms(x, g):
    return x * lax.rsqrt(jnp.mean(x * x, axis=-1, keepdims=True) + RMS_EPS) * g


def _norm_matmul_kernel(x_ref, g_ref, w_ref, *rest):
    u = _rms(x_ref[...], g_ref[...]).astype(BF16)
    if len(rest) == 1:
        (o_ref,) = rest
    else:
        wt_ref, o_ref, ot_ref = rest
        ot_ref[...] = lax.dot_general(wt_ref[...], u, (((1,), (1,)), ((), ())),
                                      preferred_element_type=F32).astype(ot_ref.dtype)
    o_ref[...] = jnp.dot(u, w_ref[...], preferred_element_type=F32).astype(o_ref.dtype)


def _norm_matmul(x, g, w, tm, wt=None):
    rows, d = x.shape
    n = w.shape[1]
    in_specs = [pl.BlockSpec((tm, d), lambda i: (i, 0)),
                pl.BlockSpec((1, d), lambda i: (0, 0)),
                pl.BlockSpec((d, n), lambda i: (0, 0))]
    out_specs = pl.BlockSpec((tm, n), lambda i: (i, 0))
    out_shape = jax.ShapeDtypeStruct((rows, n), BF16)
    args = [x, g.reshape(1, d), w]
    if wt is not None:
        m = wt.shape[0]
        in_specs.append(pl.BlockSpec((m, d), lambda i: (0, 0)))
        out_specs = [out_specs, pl.BlockSpec((m, tm), lambda i: (0, i))]
        out_shape = [out_shape, jax.ShapeDtypeStruct((m, rows), BF16)]
        args.append(wt)
    return pl.pallas_call(
        _norm_matmul_kernel,
        grid=(rows // tm,),
        in_specs=in_specs, out_specs=out_specs, out_shape=out_shape,
        compiler_params=_cparams(("parallel",)),
        name="norm_matmul",
    )(*args)


def _mixer_out_mlp_kernel(h_ref, z_ref, wp_ref, bp_ref, g_ref, w1_ref, w2_ref, gf_ref, o_ref, *, final_norm):
    h = h_ref[...] + jnp.dot(z_ref[...], wp_ref[...], preferred_element_type=F32) + bp_ref[...]
    u = _rms(h, g_ref[...]).astype(BF16)
    acc = h
    d_ff = w1_ref.shape[1]
    for c in range(d_ff // FF_CHUNK):
        a = jnp.dot(u, w1_ref[:, c * FF_CHUNK:(c + 1) * FF_CHUNK], preferred_element_type=F32)
        a = jnp.square(jnp.maximum(a, 0.0)).astype(BF16)
        acc = acc + jnp.dot(a, w2_ref[c * FF_CHUNK:(c + 1) * FF_CHUNK, :], preferred_element_type=F32)
    if final_norm:
        acc = _rms(acc, gf_ref[...])
    o_ref[...] = acc


def _mixer_out_mlp(h, z, wp, bp, g, w1, w2, gf, tm, final_norm):
    rows, d = h.shape
    dz = z.shape[1]
    d_ff = w1.shape[1]
    const = lambda i: (0, 0)
    return pl.pallas_call(
        functools.partial(_mixer_out_mlp_kernel, final_norm=final_norm),
        grid=(rows // tm,),
        in_specs=[pl.BlockSpec((tm, d), lambda i: (i, 0)),
                  pl.BlockSpec((tm, dz), lambda i: (i, 0)),
                  pl.BlockSpec((dz, d), const),
                  pl.BlockSpec((1, d), const),
                  pl.BlockSpec((1, d), const),
                  pl.BlockSpec((d, d_ff), const),
                  pl.BlockSpec((d_ff, d), const),
                  pl.BlockSpec((1, d), const)],
        out_specs=pl.BlockSpec((tm, d), lambda i: (i, 0)),
        out_shape=jax.ShapeDtypeStruct((rows, d), F32),
        compiler_params=_cparams(("parallel",)),
        name="mixer_out_mlp",
    )(h, z, wp, bp.reshape(1, d), g.reshape(1, d), w1, w2, gf.reshape(1, d))


def _split_bf16(x):
    hi = x.astype(BF16)
    return hi, (x - hi.astype(F32)).astype(BF16)


def _dot3(a_hi, a_lo, b_hi, b_lo):
    dot = functools.partial(jnp.dot, preferred_element_type=F32)
    return dot(a_hi, b_hi) + (dot(a_hi, b_lo) + dot(a_lo, b_hi))


def _filter_kernel(z_ref, t_ref, m_ref, w1_ref, b1_ref, w2_ref, b2_ref, w3_ref, b3_ref, fr_ref, woh_ref, wol_ref,
                   ad_ref, o_ref):
    d = ad_ref.shape[1]
    dot = functools.partial(jnp.dot, precision=HIGHEST, preferred_element_type=F32)
    fr = fr_ref[...]
    h = jnp.sin(fr * (dot(z_ref[...], w1_ref[...]) + b1_ref[...]))
    h = jnp.sin(fr * (dot(h, w2_ref[...]) + b2_ref[...]))
    h = jnp.sin(fr * (dot(h, w3_ref[...]) + b3_ref[...]))
    ho = _dot3(*_split_bf16(h), woh_ref[...], wol_ref[...])
    decay = jnp.exp(-t_ref[...] * ad_ref[...])
    fwd = m_ref[...] > 0.5
    for o in range(2):
        o_ref[o] = jnp.where(fwd, ho[:, 2 * o * d:(2 * o + 1) * d], ho[:, (2 * o + 1) * d:(2 * o + 2) * d]) * decay


def _filter_rows(lag, is_fwd, seq_len, fp, d):
    r = lag.shape[0]
    tr = min(r, 512)
    lagf = lag.astype(F32)
    t = lagf / (seq_len - 1)
    w = 2.0 * math.pi * lagf / seq_len
    f = jnp.linspace(1e-4, HY_BANDS - 1, HY_BANDS, dtype=F32)[None, :]
    z = jnp.concatenate([t[:, None], jnp.cos(f * w[:, None]), -jnp.sin(f * w[:, None]),
                         jnp.zeros((r, HY_EMB_PAD - 2 * HY_BANDS - 1), F32)], axis=-1)
    w1 = jnp.pad(fp["w1"], ((0, HY_EMB_PAD - fp["w1"].shape[0]), (0, 0)))
    hid = w1.shape[1]
    max_decay = math.log(HY_DECAY_TARGET) / HY_FAST_DECAY
    min_decay = math.log(HY_DECAY_TARGET) / HY_SLOW_DECAY
    adel = jnp.abs(jnp.linspace(min_decay, max_decay, d, dtype=F32))[None, :]
    const = lambda i: (0, 0)
    row = lambda i: (i, 0)
    wo_hi, wo_lo = _split_bf16(fp["wout"])
    return pl.pallas_call(
        _filter_kernel,
        grid=(r // tr,),
        in_specs=[pl.BlockSpec((tr, HY_EMB_PAD), row), pl.BlockSpec((tr, 1), row), pl.BlockSpec((tr, 1), row),
                  pl.BlockSpec((HY_EMB_PAD, hid), const), pl.BlockSpec((1, hid), const),
                  pl.BlockSpec((hid, hid), const), pl.BlockSpec((1, hid), const),
                  pl.BlockSpec((hid, hid), const), pl.BlockSpec((1, hid), const),
                  pl.BlockSpec((1, hid), const), pl.BlockSpec((hid, 4 * d), const), pl.BlockSpec((hid, 4 * d), const),
                  pl.BlockSpec((1, d), const)],
        out_specs=pl.BlockSpec((2, tr, d), lambda i: (0, i, 0)),
        out_shape=jax.ShapeDtypeStruct((2, r, d), F32),
        compiler_params=_cparams(("parallel",)),
        name="hyena_filter",
    )(z, t[:, None], is_fwd.astype(F32)[:, None], w1, fp["b1"][None], fp["w2"], fp["b2"][None],
      fp["w3"], fp["b3"][None], fp["freq"][None], wo_hi, wo_lo, adel)


def _cplx_block(re, im):
    return jnp.concatenate([jnp.concatenate([re, -im], axis=-1), jnp.concatenate([im, re], axis=-1)], axis=-2)


def _fft_tables(n1, n2):
    n = n1 * n2
    n1h = n1 // 2
    pad = 8 - 1
    k1 = jnp.arange(n1, dtype=jnp.int32)
    c2 = jnp.arange(n2, dtype=jnp.int32)
    cols = jnp.concatenate([jnp.arange(n1h, dtype=jnp.int32), jnp.array([n1 - 1], jnp.int32)])
    pos = n2 * cols[None, None, :] + c2[:, None, None]
    ang = ((k1[None, :, None] * pos) % n).astype(F32) * (-2.0 * math.pi / n)
    wr = jnp.pad(jnp.cos(ang), ((0, 0), (0, 0), (0, pad)))
    wi = jnp.pad(jnp.sin(ang), ((0, 0), (0, 0), (0, pad)))
    fwd1 = _cplx_block(wr, wi)
    inv1 = _cplx_block(jnp.swapaxes(wr, 1, 2), -jnp.swapaxes(wi, 1, 2)) / n
    posf = n2 * k1[None, None, :] + c2[:, None, None]
    angf = ((k1[None, :, None] * posf) % n).astype(F32) * (-2.0 * math.pi / n)
    fil1 = jnp.concatenate([jnp.cos(angf), jnp.sin(angf)], axis=1)
    ang2 = ((c2[:, None] * c2[None, :]) % n2).astype(F32) * (-2.0 * math.pi / n2)
    fr, fi = jnp.cos(ang2), jnp.sin(ang2)
    fwd2 = _cplx_block(fr, fi)
    return dict(fwd1=fwd1.astype(BF16), inv1=inv1.astype(BF16), fil1=_split_bf16(fil1),
                fwd2=fwd2.astype(BF16), inv2=_cplx_block(fr, -fi).astype(BF16), fwd2_split=_split_bf16(fwd2))


def _ordered_short_conv(meta, real, cw_ref, cb_ref, seq_ref):
    s = real.shape[0]
    dt = real.shape[1]
    seq_ref[0:8, :] = jnp.zeros((8, dt), F32)
    seq_ref[8:8 + N_META, :] = meta.astype(F32)
    seq_ref[8 + N_META:8 + N_META + s, :] = real.astype(F32)
    seq_ref[8 + N_META + s:16 + N_META + s, :] = jnp.zeros((8, dt), F32)
    x = seq_ref[...]
    rows = x.shape[0]
    cw = cw_ref[...].astype(F32)
    y = (pltpu.roll(x, 1, axis=0) * cw[0:1] + x * cw[1:2] + pltpu.roll(x, rows - 1, axis=0) * cw[2:3]
         + cb_ref[...].astype(F32))
    return y[8:8 + N_META], y[8 + N_META:8 + N_META + s]


def _fft_in_kernel(*refs, n1, n2, short_conv):
    if short_conv:
        (xr_ref, xm_ref, cw_ref, cb_ref, f_ref, are_ref, aim_ref, vr_ref, vm_ref,
         xs_ref, mp_ref, seq_ref) = refs
    else:
        xr_ref, xm_ref, f_ref, are_ref, aim_ref, xs_ref, mp_ref = refs
    n1h = n1 // 2
    chunk = pl.program_id(2)

    @pl.when(chunk == 0)
    def _prepare():
        for e in range(2):
            if short_conv:
                meta, real = _ordered_short_conv(xm_ref[e], xr_ref[e], cw_ref, cb_ref, seq_ref)
                vr_ref[e] = real.astype(vr_ref.dtype)
                vm_ref[e] = meta.astype(vm_ref.dtype)
            else:
                meta, real = xm_ref[e].astype(F32), xr_ref[e].astype(F32)
            xs_ref[e] = real
            mp_ref[e] = jnp.zeros(mp_ref.shape[1:], F32)
            for j in range(N_META):
                mp_ref[e, pl.ds(8 * (n2 - N_META + j), 1), :] = meta[j:j + 1]

    for i in range(N2_CHUNK):
        c2 = chunk * N2_CHUNK + i
        parts = []
        for e in range(2):
            parts.append(xs_ref[e, pl.ds(c2, n1h, stride=n2), :])
            parts.append(mp_ref[e, pl.ds(pl.multiple_of(c2 * 8, 8), 8), :])
        rhs = jnp.concatenate(parts, axis=0).astype(BF16)
        out = jnp.dot(f_ref[i], rhs, preferred_element_type=F32)
        are_ref[i] = out[:n1]
        aim_ref[i] = out[n1:]


def _fft_in(xr, xm, col_off, conv_w, conv_b, tab, *, b, s, d, n1, n2, dt):
    short_conv = conv_w is not None
    pairs = b // 2
    c = xr.shape[1]
    cb0 = col_off // dt
    xr4 = xr.reshape(pairs, 2, s, c)
    xm4 = xm.reshape(pairs, 2, N_META, c)
    kk = tab["fwd1"].shape[2]
    in_specs = [pl.BlockSpec((None, 2, s, dt), lambda p, j, t: (p, 0, 0, cb0 + j)),
                pl.BlockSpec((None, 2, N_META, dt), lambda p, j, t: (p, 0, 0, cb0 + j))]
    args = [xr4, xm4]
    if short_conv:
        in_specs += [pl.BlockSpec((8, dt), lambda p, j, t: (0, cb0 + j)),
                     pl.BlockSpec((1, dt), lambda p, j, t: (0, cb0 + j))]
        args += [conv_w, conv_b]
    in_specs.append(pl.BlockSpec((N2_CHUNK, 2 * n1, kk), lambda p, j, t: (t, 0, 0)))
    args.append(tab["fwd1"])
    a_spec = pl.BlockSpec((None, N2_CHUNK, n1, dt), lambda p, j, t: (p, t, 0, j))
    a_shape = jax.ShapeDtypeStruct((pairs, n2, n1, d), F32)
    out_specs = [a_spec, a_spec]
    out_shape = [a_shape, a_shape]
    scratch = [pltpu.VMEM((2, s, dt), F32), pltpu.VMEM((2, 8 * n2, dt), F32)]
    if short_conv:
        out_specs += [pl.BlockSpec((None, 2, s, dt), lambda p, j, t: (p, 0, 0, j)),
                      pl.BlockSpec((None, 2, N_META, dt), lambda p, j, t: (p, 0, 0, j))]
        out_shape += [jax.ShapeDtypeStruct((pairs, 2, s, d), BF16),
                      jax.ShapeDtypeStruct((pairs, 2, N_META, d), BF16)]
        scratch.append(pltpu.VMEM((s + N_META + 16, dt), F32))
    return pl.pallas_call(
        functools.partial(_fft_in_kernel, n1=n1, n2=n2, short_conv=short_conv),
        grid=(pairs, d // dt, n2 // N2_CHUNK),
        in_specs=in_specs, out_specs=out_specs, out_shape=out_shape, scratch_shapes=scratch,
        compiler_params=_cparams(("parallel", "parallel", "arbitrary")),
        name="fft_in",
    )(*args)


def _fft_mid_kernel(are_ref, aim_ref, kre_ref, kim_ref, f_ref, g_ref, bre_ref, bim_ref, *, n2, slabs):
    for i in range(slabs):
        x = jnp.concatenate([are_ref[:, i, :], aim_ref[:, i, :]], axis=0).astype(BF16)
        z = jnp.dot(f_ref[...], x, preferred_element_type=F32)
        zr, zi = z[:n2], z[n2:]
        kr, ki = kre_ref[i], kim_ref[i]
        p = jnp.concatenate([zr * kr - zi * ki, zr * ki + zi * kr], axis=0).astype(BF16)
        y = jnp.dot(g_ref[...], p, preferred_element_type=F32)
        bre_ref[:, i, :] = y[:n2]
        bim_ref[:, i, :] = y[n2:]


def _fft_mid(are, aim, kre, kim, order, tab, *, dt, slabs):
    pairs, n2, n1, d = are.shape
    k_blk0 = order * (n1 // slabs)
    blk = (None, n2, slabs, dt)
    amap = lambda j, kb, p: (p, 0, kb, j)
    kmap = lambda j, kb, p: (k_blk0 + kb, 0, j)
    const = lambda j, kb, p: (0, 0)
    shp = jax.ShapeDtypeStruct((pairs, n2, n1, d), F32)
    return pl.pallas_call(
        functools.partial(_fft_mid_kernel, n2=n2, slabs=slabs),
        grid=(d // dt, n1 // slabs, pairs),
        in_specs=[pl.BlockSpec(blk, amap), pl.BlockSpec(blk, amap),
                  pl.BlockSpec((slabs, n2, dt), kmap), pl.BlockSpec((slabs, n2, dt), kmap),
                  pl.BlockSpec((2 * n2, 2 * n2), const), pl.BlockSpec((2 * n2, 2 * n2), const)],
        out_specs=[pl.BlockSpec(blk, amap), pl.BlockSpec(blk, amap)],
        out_shape=[shp, shp],
        compiler_params=_cparams(("parallel", "parallel", "arbitrary")),
        name="fft_mid",
    )(are, aim, kre, kim, tab["fwd2"], tab["inv2"])


def _filter_fft_in_kernel(c_ref, fh_ref, fl_ref, are_ref, aim_ref, *, n1, n2):
    chunk = pl.program_id(2)
    for i in range(N2_CHUNK):
        rhs = c_ref[pl.ds(chunk * N2_CHUNK + i, n1, stride=n2), :]
        out = _dot3(fh_ref[i], fl_ref[i], *_split_bf16(rhs))
        are_ref[i] = out[:n1]
        aim_ref[i] = out[n1:]


def _filter_fft_mid_kernel(are_ref, aim_ref, fh_ref, fl_ref, kre_ref, kim_ref, *, n2, slabs):
    for i in range(slabs):
        x = jnp.concatenate([are_ref[:, i, :], aim_ref[:, i, :]], axis=0)
        z = _dot3(fh_ref[...], fl_ref[...], *_split_bf16(x))
        kre_ref[i] = z[:n2]
        kim_ref[i] = z[n2:]


def _filter_spectrum(circ, tab, *, n1, n2, dt, slabs):
    orders, n, d = circ.shape
    shp = [jax.ShapeDtypeStruct((orders, n2, n1, d), F32)] * 2
    tab_spec = pl.BlockSpec((N2_CHUNK, 2 * n1, n1), lambda o, j, t: (t, 0, 0))
    lt = LANE_TILE
    are, aim = pl.pallas_call(
        functools.partial(_filter_fft_in_kernel, n1=n1, n2=n2),
        grid=(orders, d // lt, n2 // N2_CHUNK),
        in_specs=[pl.BlockSpec((None, n, lt), lambda o, j, t: (o, 0, j)), tab_spec, tab_spec],
        out_specs=[pl.BlockSpec((None, N2_CHUNK, n1, lt), lambda o, j, t: (o, t, 0, j))] * 2,
        out_shape=shp,
        compiler_params=_cparams(("parallel", "parallel", "arbitrary")),
        name="filter_fft_in",
    )(circ, *tab["fil1"])
    blk = (None, n2, slabs, dt)
    amap = lambda o, kb, j: (o, 0, kb, j)
    f_spec = pl.BlockSpec((2 * n2, 2 * n2), lambda o, kb, j: (0, 0))
    return pl.pallas_call(
        functools.partial(_filter_fft_mid_kernel, n2=n2, slabs=slabs),
        grid=(orders, n1 // slabs, d // dt),
        in_specs=[pl.BlockSpec(blk, amap), pl.BlockSpec(blk, amap), f_spec, f_spec],
        out_specs=[pl.BlockSpec((None, slabs, n2, dt), lambda o, kb, j: (o, kb, 0, j))] * 2,
        out_shape=[jax.ShapeDtypeStruct((orders, n1, n2, d), F32)] * 2,
        compiler_params=_cparams(("parallel", "parallel", "parallel")),
        name="filter_fft_mid",
    )(are, aim, *tab["fwd2_split"])


def _alias_patch(tail, u_meta, u_last):
    dfw = tail[0:16] - tail[16:32]
    dbw = tail[32:48] - tail[48:64]
    ridx = lax.broadcasted_iota(jnp.int32, dfw.shape, 0)
    real_fix = jnp.zeros_like(dfw)
    meta_fix = jnp.zeros_like(dfw)
    for o in range(15):
        shifted = jnp.where(ridx >= o + 1, pltpu.roll(dfw, o + 1, axis=0), 0.0)
        real_fix = real_fix + shifted * u_meta[o:o + 1]
    for c in range(16):
        src = u_last if c == 0 else pltpu.roll(u_last, 16 - c, axis=0)
        meta_fix = meta_fix + jnp.where(ridx + c <= 15, src, 0.0) * dbw[c:c + 1]
    return real_fix, meta_fix


def _fft_out_kernel(bre_ref, bim_ref, g_ref, gr_ref, gm_ref, cw_ref, cb_ref, ur_ref, um_ref, skip_ref,
                    tail_ref, zr_ref, zm_ref, ys_ref, yp_ref, seq_ref, *, n1, n2):
    n1h = n1 // 2
    kk = n1h + 8
    s = ys_ref.shape[1]
    chunk = pl.program_id(2)

    for i in range(N2_CHUNK):
        c2 = chunk * N2_CHUNK + i
        rhs = jnp.concatenate([bre_ref[i], bim_ref[i]], axis=0).astype(BF16)
        y = jnp.dot(g_ref[i], rhs, preferred_element_type=F32)
        for e in range(2):
            ys_ref[e, pl.ds(c2, n1h, stride=n2), :] = y[e * kk:e * kk + n1h]
            yp_ref[e, pl.ds(pl.multiple_of(c2 * 8, 8), 8), :] = y[e * kk + n1h:(e + 1) * kk]

    @pl.when(chunk == n2 // N2_CHUNK - 1)
    def _gate():
        tail = tail_ref[...]
        skip = skip_ref[...]
        for e in range(2):
            y_meta = jnp.concatenate(
                [yp_ref[e, pl.ds(8 * (n2 - N_META + j), 1), :] for j in range(N_META)], axis=0)
            u_real = ur_ref[e].astype(F32)
            u_meta = um_ref[e].astype(F32)
            real_fix, meta_fix = _alias_patch(tail, u_meta, u_real[s - 16:s])
            ys_ref[e, s - 16:s, :] = ys_ref[e, s - 16:s, :] + real_fix
            g_meta, g_real = _ordered_short_conv(gm_ref[e], gr_ref[e], cw_ref, cb_ref, seq_ref)
            zr_ref[e] = (g_real * (ys_ref[e] + skip * u_real)).astype(zr_ref.dtype)
            zm_ref[e] = (g_meta * (y_meta + meta_fix + skip * u_meta)).astype(zm_ref.dtype)


def _fft_out(bre, bim, tab, gr, gm, gate_off, conv_w, conv_b, ur, um, skip, tail, *, b, s, d, n1, n2, dt):
    pairs = b // 2
    c = gr.shape[1]
    gb0 = gate_off // dt
    kk2 = tab["inv1"].shape[1]
    gr4 = gr.reshape(pairs, 2, s, c)
    gm4 = gm.reshape(pairs, 2, N_META, c)
    ur4 = ur.reshape(pairs, 2, s, d)
    um4 = um.reshape(pairs, 2, N_META, d)
    b_spec = pl.BlockSpec((None, N2_CHUNK, n1, dt), lambda p, j, t: (p, t, 0, j))
    zr, zm = pl.pallas_call(
        functools.partial(_fft_out_kernel, n1=n1, n2=n2),
        grid=(pairs, d // dt, n2 // N2_CHUNK),
        in_specs=[b_spec, b_spec,
                  pl.BlockSpec((N2_CHUNK, kk2, 2 * n1), lambda p, j, t: (t, 0, 0)),
                  pl.BlockSpec((None, 2, s, dt), lambda p, j, t: (p, 0, 0, gb0 + j)),
                  pl.BlockSpec((None, 2, N_META, dt), lambda p, j, t: (p, 0, 0, gb0 + j)),
                  pl.BlockSpec((8, dt), lambda p, j, t: (0, gb0 + j)),
                  pl.BlockSpec((1, dt), lambda p, j, t: (0, gb0 + j)),
                  pl.BlockSpec((None, 2, s, dt), lambda p, j, t: (p, 0, 0, j)),
                  pl.BlockSpec((None, 2, N_META, dt), lambda p, j, t: (p, 0, 0, j)),
                  pl.BlockSpec((1, dt), lambda p, j, t: (0, j)),
                  pl.BlockSpec((64, dt), lambda p, j, t: (0, j))],
        out_specs=[pl.BlockSpec((None, 2, s, dt), lambda p, j, t: (p, 0, 0, j)),
                   pl.BlockSpec((None, 2, N_META, dt), lambda p, j, t: (p, 0, 0, j))],
        out_shape=[jax.ShapeDtypeStruct((pairs, 2, s, d), BF16),
                   jax.ShapeDtypeStruct((pairs, 2, N_META, d), BF16)],
        scratch_shapes=[pltpu.VMEM((2, s, dt), F32), pltpu.VMEM((2, 8 * n2, dt), F32),
                        pltpu.VMEM((s + N_META + 16, dt), F32)],
        compiler_params=_cparams(("parallel", "parallel", "arbitrary")),
        name="fft_out",
    )(bre, bim, tab["inv1"], gr4, gm4, conv_w, conv_b, ur4, um4, skip.reshape(1, d), tail)
    return zr.reshape(b * s, d), zm.reshape(b * N_META, d)


def _fused_conv_kernel(*refs, n1, n2, slabs, short_conv):
    if short_conv:
        (xr_ref, xm_ref, cwx_ref, cbx_ref, f1_ref, kre_ref, kim_ref, f2_ref, g2_ref, g1_ref, gr_ref, gm_ref,
         cwg_ref, cbg_ref, skip_ref, tail_ref, zr_ref, zm_ref, ur_ref, um_ref,
         are_ref, aim_ref, bre_ref, bim_ref, xs_ref, mp_ref, seq_ref) = refs
    else:
        (ur_ref, um_ref, f1_ref, kre_ref, kim_ref, f2_ref, g2_ref, g1_ref, gr_ref, gm_ref,
         cwg_ref, cbg_ref, skip_ref, tail_ref, zr_ref, zm_ref,
         are_ref, aim_ref, bre_ref, bim_ref, xs_ref, mp_ref, seq_ref) = refs
    n1h = n1 // 2
    kk = n1h + 8
    s = n1h * n2
    t1 = n2 // N2_CHUNK
    t2 = n1 // slabs
    step = pl.program_id(2)

    @pl.when(step == 0)
    def _prepare():
        for e in range(2):
            if short_conv:
                meta, real = _ordered_short_conv(xm_ref[e], xr_ref[e], cwx_ref, cbx_ref, seq_ref)
                ur_ref[e] = real.astype(ur_ref.dtype)
                um_ref[e] = meta.astype(um_ref.dtype)
            else:
                meta, real = um_ref[e].astype(F32), ur_ref[e].astype(F32)
            xs_ref[e] = real.reshape(n1h, n2, real.shape[1])
            mp_ref[e] = jnp.zeros(mp_ref.shape[1:], F32)
            for j in range(N_META):
                mp_ref[e, pl.ds(8 * (n2 - N_META + j), 1), :] = meta[j:j + 1]

    @pl.when(step < t1)
    def _stage1():
        base = pl.multiple_of(step * N2_CHUNK, N2_CHUNK)
        xt = [jnp.swapaxes(xs_ref[e, :, pl.ds(base, N2_CHUNK), :], 0, 1) for e in range(2)]
        outs = []
        for i in range(N2_CHUNK):
            parts = []
            for e in range(2):
                parts += [xt[e][i], mp_ref[e, pl.ds(pl.multiple_of((base + i) * 8, 8), 8), :]]
            rhs = jnp.concatenate(parts, axis=0).astype(BF16)
            outs.append(jnp.dot(f1_ref[i], rhs, preferred_element_type=F32))
        o = jnp.swapaxes(jnp.stack(outs, axis=0), 0, 1)
        are_ref[:, pl.ds(base, N2_CHUNK), :] = o[:n1]
        aim_ref[:, pl.ds(base, N2_CHUNK), :] = o[n1:]

    @pl.when(jnp.logical_and(step >= t1, step < t1 + t2))
    def _stage2():
        k0 = pl.multiple_of((step - t1) * slabs, slabs)
        dt = are_ref.shape[2]
        ys = []
        for i in range(0, slabs, 2):
            x = jnp.concatenate(
                [jnp.concatenate([are_ref[k0 + i + h], aim_ref[k0 + i + h]], axis=0) for h in range(2)],
                axis=1).astype(BF16)
            z = jnp.dot(f2_ref[...], x, preferred_element_type=F32)
            zr, zi = z[:n2], z[n2:]
            kr = jnp.concatenate([kre_ref[i], kre_ref[i + 1]], axis=1)
            ki = jnp.concatenate([kim_ref[i], kim_ref[i + 1]], axis=1)
            p = jnp.concatenate([zr * kr - zi * ki, zr * ki + zi * kr], axis=0).astype(BF16)
            y = jnp.dot(g2_ref[...], p, preferred_element_type=F32)
            ys += [y[:, :dt], y[:, dt:]]
        yt = jnp.swapaxes(jnp.stack(ys, axis=0), 0, 1)
        bre_ref[:, pl.ds(k0, slabs), :] = yt[:n2]
        bim_ref[:, pl.ds(k0, slabs), :] = yt[n2:]

    @pl.when(step >= t1 + t2)
    def _stage3():
        base = pl.multiple_of((step - t1 - t2) * N2_CHUNK, N2_CHUNK)
        ys = []
        for i in range(N2_CHUNK):
            rhs = jnp.concatenate([bre_ref[base + i], bim_ref[base + i]], axis=0).astype(BF16)
            y = jnp.dot(g1_ref[i], rhs, preferred_element_type=F32)
            ys.append(y)
            for e in range(2):
                mp_ref[e, pl.ds(pl.multiple_of((base + i) * 8, 8), 8), :] = y[e * kk + n1h:(e + 1) * kk]
        yt = jnp.swapaxes(jnp.stack(ys, axis=0), 0, 1)
        for e in range(2):
            xs_ref[e, :, pl.ds(base, N2_CHUNK), :] = yt[e * kk:e * kk + n1h]

    @pl.when(step == 2 * t1 + t2 - 1)
    def _gate():
        tail = tail_ref[...]
        skip = skip_ref[...]
        for e in range(2):
            y_meta = jnp.concatenate(
                [mp_ref[e, pl.ds(8 * (n2 - N_META + j), 1), :] for j in range(N_META)], axis=0)
            u_real = ur_ref[e].astype(F32)
            u_meta = um_ref[e].astype(F32)
            real_fix, meta_fix = _alias_patch(tail, u_meta, u_real[s - 16:s])
            xs_ref[e, n1h - 1, n2 - 16:n2, :] = xs_ref[e, n1h - 1, n2 - 16:n2, :] + real_fix
            y_real = xs_ref[e].reshape(s, u_real.shape[1])
            g_meta, g_real = _ordered_short_conv(gm_ref[e], gr_ref[e], cwg_ref, cbg_ref, seq_ref)
            zr_ref[e] = (g_real * (y_real + skip * u_real)).astype(zr_ref.dtype)
            zm_ref[e] = (g_meta * (y_meta + meta_fix + skip * u_meta)).astype(zm_ref.dtype)


def _fused_conv(xr, xm, col_off, conv_w, conv_b, kre, kim, order, tab, gr, gm, gate_off, skip, tail, *,
                b, s, d, n1, n2):
    short_conv = col_off is not None
    pairs = b // 2
    dt = LANE_TILE
    slabs = FUSED_SLABS
    k_blk0 = order * (n1 // slabs)
    t1, t2 = n2 // N2_CHUNK, n1 // slabs
    steps = 2 * t1 + t2
    c = xr.shape[1]
    cx0 = (col_off or 0) // dt
    cg0 = gate_off // dt
    kk2 = tab["fwd1"].shape[2]
    seq4 = lambda a, rows: a.reshape(pairs, 2, rows, a.shape[1])
    x_spec = lambda rows, c0: pl.BlockSpec((None, 2, rows, dt), lambda j, p, t: (p, 0, 0, c0 + j))
    row_spec = lambda rows, c0: pl.BlockSpec((rows, dt), lambda j, p, t: (0, c0 + j))
    const2 = pl.BlockSpec((2 * n2, 2 * n2), lambda j, p, t: (0, 0))
    k_spec = pl.BlockSpec((slabs, n2, dt), lambda j, p, t: (k_blk0 + jnp.clip(t - t1, 0, t2 - 1), 0, j))
    in_specs = [x_spec(s, cx0), x_spec(N_META, cx0)]
    args = [seq4(xr, s), seq4(xm, N_META)]
    if short_conv:
        in_specs += [row_spec(8, cx0), row_spec(1, cx0)]
        args += [conv_w, conv_b]
    in_specs += [pl.BlockSpec((N2_CHUNK, 2 * n1, kk2), lambda j, p, t: (jnp.minimum(t, t1 - 1), 0, 0)),
                 k_spec, k_spec, const2, const2,
                 pl.BlockSpec((N2_CHUNK, kk2, 2 * n1), lambda j, p, t: (jnp.clip(t - t1 - t2, 0, t1 - 1), 0, 0)),
                 x_spec(s, cg0), x_spec(N_META, cg0), row_spec(8, cg0), row_spec(1, cg0),
                 row_spec(1, 0), row_spec(64, 0)]
    args += [tab["fwd1"], kre, kim, tab["fwd2"], tab["inv2"], tab["inv1"],
             seq4(gr, s), seq4(gm, N_META), conv_w, conv_b, skip.reshape(1, d), tail]
    seq_out = lambda rows: pl.BlockSpec((None, 2, rows, dt), lambda j, p, t: (p, 0, 0, j))
    out_specs = [seq_out(s), seq_out(N_META)]
    out_shape = [jax.ShapeDtypeStruct((pairs, 2, s, d), BF16), jax.ShapeDtypeStruct((pairs, 2, N_META, d), BF16)]
    if short_conv:
        out_specs = out_specs * 2
        out_shape = out_shape * 2
    scratch = ([pltpu.VMEM((n1, n2, dt), F32)] * 2 + [pltpu.VMEM((n2, n1, dt), F32)] * 2
               + [pltpu.VMEM((2, n1 // 2, n2, dt), F32), pltpu.VMEM((2, 8 * n2, dt), F32),
                  pltpu.VMEM((s + N_META + 16, dt), F32)])
    outs = pl.pallas_call(
        functools.partial(_fused_conv_kernel, n1=n1, n2=n2, slabs=slabs, short_conv=short_conv),
        grid=(d // dt, pairs, steps),
        in_specs=in_specs, out_specs=out_specs, out_shape=out_shape, scratch_shapes=scratch,
        compiler_params=_cparams(("parallel", "parallel", "arbitrary")),
        name="fused_conv",
    )(*args)
    return [o.reshape(-1, d) for o in outs]


def _attn_kernel(own_ref, vprev_ref, vnext_ref, vmeta_ref, ktp_ref, kto_ref, ktn_ref, ktm_ref,
                 bias_ref, shift_ref, o_ref, *, groups):
    blk = pl.program_id(1).astype(F32)
    gw = GROUP * HEAD_DIM
    qd = groups * gw
    lane = lax.broadcasted_iota(jnp.int32, (1, LANE_TILE), 1)
    low = lane < HEAD_DIM
    pad_rows = jnp.zeros((ATT_BLOCK - N_META, 2 * HEAD_DIM), BF16)
    zero = jnp.zeros((ATT_BLOCK, LANE_TILE), BF16)
    for g in range(groups):
        vk_own = own_ref[:, qd + g * LANE_TILE:qd + (g + 1) * LANE_TILE]
        grp = slice(g * LANE_TILE, (g + 1) * LANE_TILE)
        vk = jnp.concatenate([vprev_ref[:, grp], vk_own, vnext_ref[:, grp], vmeta_ref[:, grp], pad_rows],
                             axis=0)
        v_ones = jnp.where(low, vk, jnp.ones_like(vk))
        kt_rows = slice(g * HEAD_DIM, (g + 1) * HEAD_DIM)
        kt = jnp.concatenate([ktp_ref[kt_rows, :], kto_ref[kt_rows, :], ktn_ref[kt_rows, :], ktm_ref[kt_rows, :]],
                             axis=1)
        kt2 = jnp.concatenate([kt, kt], axis=0)
        parts = []
        for pr in range(GROUP // 2):
            qp = own_ref[:, g * gw + pr * LANE_TILE:g * gw + (pr + 1) * LANE_TILE]
            parts += [jnp.where(low, qp, zero), jnp.where(low, zero, qp)]
        q = jnp.concatenate(parts, axis=0)
        sc = jnp.dot(q, kt2, preferred_element_type=F32) + bias_ref[g]
        t = [sc[:, i * LANE_TILE:(i + 1) * LANE_TILE] for i in range(3)]
        t.append(sc[:, 3 * LANE_TILE:] - shift_ref[g] * blk)
        m = jnp.max(jnp.maximum(jnp.maximum(t[0], t[1]), jnp.maximum(t[2], t[3])), axis=1, keepdims=True)
        p = jnp.concatenate([jnp.exp(x - m) for x in t], axis=1).astype(BF16)
        oa = jnp.dot(p, v_ones, preferred_element_type=F32)
        ob = pltpu.roll(oa, HEAD_DIM, axis=1)
        outs = []
        for pr in range(GROUP // 2):
            ev = slice((2 * pr) * ATT_BLOCK, (2 * pr + 1) * ATT_BLOCK)
            od = slice((2 * pr + 1) * ATT_BLOCK, (2 * pr + 2) * ATT_BLOCK)
            outs.append(jnp.where(low, oa[ev] / ob[ev], ob[od] / oa[od]))
        o_ref[:, g * gw:(g + 1) * gw] = jnp.concatenate(outs, axis=1).astype(o_ref.dtype)


def _attention_tables(n_heads, sink):
    groups = n_heads // GROUP
    slopes = jnp.exp2(-8.0 * jnp.arange(1, n_heads + 1, dtype=F32) / n_heads)
    i = jnp.arange(ATT_BLOCK, dtype=jnp.int32)[:, None]
    c = jnp.arange(4 * ATT_BLOCK, dtype=jnp.int32)[None, :]
    dist = jnp.abs(c - ATT_BLOCK - i)
    key_blk = c // ATT_BLOCK
    in_band = jnp.logical_and(c < 3 * ATT_BLOCK, dist <= ATT_BLOCK)
    meta_col = jnp.logical_and(c >= 3 * ATT_BLOCK, c < 3 * ATT_BLOCK + N_META)
    sink_col = c == 3 * ATT_BLOCK + N_META
    meta_dist = N_META + i - (c - 3 * ATT_BLOCK)
    tables = []
    for drop in (None, 0, 2):
        ok = in_band if drop is None else jnp.logical_and(in_band, key_blk != drop)
        d_eff = jnp.where(ok, dist, jnp.where(meta_col, meta_dist, 0)).astype(F32)
        live = jnp.logical_or(ok, meta_col)
        tab = jnp.where(live[None], -slopes[:, None, None] * d_eff[None], MASK_VALUE)
        tables.append(jnp.where(sink_col[None], sink.astype(F32)[:, None, None], tab))
    bias = jnp.stack(tables, axis=0).reshape(3, groups, GROUP * ATT_BLOCK, 4 * ATT_BLOCK)
    lane = jnp.arange(LANE_TILE)[None, :]
    shift = jnp.where(lane < N_META, jnp.repeat(slopes * ATT_BLOCK, ATT_BLOCK)[:, None], 0.0)
    return bias, shift.reshape(groups, GROUP * ATT_BLOCK, LANE_TILE)


def _attention(qvk_r, kt_r, qvk_m, kt_m, sink, *, b, s, n_heads):
    groups = n_heads // GROUP
    nblk = s // ATT_BLOCK
    qd = n_heads * HEAD_DIM
    width = qvk_r.shape[1]
    vkw = groups * 2 * HEAD_DIM
    vkb = qd // vkw
    bias, shift = _attention_tables(n_heads, sink)

    def variant(j):
        return jnp.where(j == 0, 1, jnp.where(j == nblk - 1, 2, 0))

    prev = lambda i, j: i * nblk + jnp.maximum(j - 1, 0)
    nxt = lambda i, j: i * nblk + jnp.minimum(j + 1, nblk - 1)
    return pl.pallas_call(
        functools.partial(_attn_kernel, groups=groups),
        grid=(b, nblk),
        in_specs=[pl.BlockSpec((ATT_BLOCK, width), lambda i, j: (i * nblk + j, 0)),
                  pl.BlockSpec((ATT_BLOCK, vkw), lambda i, j: (prev(i, j), vkb)),
                  pl.BlockSpec((ATT_BLOCK, vkw), lambda i, j: (nxt(i, j), vkb)),
                  pl.BlockSpec((N_META, vkw), lambda i, j: (i, vkb)),
                  pl.BlockSpec((groups * HEAD_DIM, ATT_BLOCK), lambda i, j: (0, prev(i, j))),
                  pl.BlockSpec((groups * HEAD_DIM, ATT_BLOCK), lambda i, j: (0, i * nblk + j)),
                  pl.BlockSpec((groups * HEAD_DIM, ATT_BLOCK), lambda i, j: (0, nxt(i, j))),
                  pl.BlockSpec((None, groups * HEAD_DIM, LANE_TILE), lambda i, j: (i, 0, 0)),
                  pl.BlockSpec((None, groups, GROUP * ATT_BLOCK, 4 * ATT_BLOCK), lambda i, j: (variant(j), 0, 0, 0)),
                  pl.BlockSpec((groups, GROUP * ATT_BLOCK, LANE_TILE), lambda i, j: (0, 0, 0))],
        out_specs=pl.BlockSpec((ATT_BLOCK, qd), lambda i, j: (i * nblk + j, 0)),
        out_shape=jax.ShapeDtypeStruct((b * s, qd), BF16),
        compiler_params=_cparams(("parallel", "arbitrary")),
        name="window_attention",
    )(qvk_r, qvk_r, qvk_r, qvk_m, kt_r, kt_r, kt_r, kt_m, bias, shift)


def _fft_split(s):
    n2 = 128 if s >= 1024 else 32
    return (2 * s) // n2, n2


def _fused_conv_vmem_bytes(s, n1, n2):
    lane_bytes = LANE_TILE * 4
    spectrum = 4 * n1 * n2 * lane_bytes
    sequence = (2 * s + 2 * 8 * n2 + s + N_META + 16) * lane_bytes
    blocks = 4 * 2 * 2 * (s + N_META) * LANE_TILE * 2
    return spectrum + sequence + blocks


def _hyena_conv(xr, xm, col_off, conv_w, conv_b, kre, kim, order, tab, gr, gm, gate_off, skip, tail, *, dims):
    b, s, d, n1, n2 = dims
    if _fused_conv_vmem_bytes(s, n1, n2) <= (VMEM_LIMIT * 7) // 8:
        outs = _fused_conv(xr, xm, col_off, conv_w, conv_b, kre, kim, order, tab, gr, gm, gate_off, skip, tail,
                           b=b, s=s, d=d, n1=n1, n2=n2)
        return outs[0], outs[1]
    kw = dict(b=b, s=s, d=d, n1=n1, n2=n2, dt=LANE_TILE)
    if col_off is not None:
        are, aim, ur4, um4 = _fft_in(xr, xm, col_off, conv_w, conv_b, tab, **kw)
        ur, um = ur4.reshape(b * s, d), um4.reshape(b * N_META, d)
    else:
        are, aim = _fft_in(xr, xm, 0, None, None, tab, **kw)
        ur, um = xr, xm
    bre, bim = _fft_mid(are, aim, kre, kim, order, tab, dt=min(d, FFT_MID_LANES), slabs=8)
    return _fft_out(bre, bim, tab, gr, gm, gate_off, conv_w, conv_b, ur, um, skip, tail, **kw)


def _hyena_layer(streams, fp, g_mix, w_in, conv_w, conv_b, skip, tm):
    outs = []
    d = w_in.shape[0]
    for st in streams:
        b, s = st["b"], st["s"]
        n1, n2 = _fft_split(s)
        seq_len = s + N_META
        n = 2 * s
        tab = _fft_tables(n1, n2)
        r = jnp.arange(n, dtype=jnp.int32)
        circ = _filter_rows(jnp.where(r <= s, r, n - r), r <= s, seq_len, fp, d)
        kre, kim = [k.reshape(2 * n1, n2, d)
                    for k in _filter_spectrum(circ, tab, n1=n1, n2=n2, dt=2 * LANE_TILE, slabs=8)]
        a = jnp.arange(16, dtype=jnp.int32)
        tail_lag = jnp.concatenate([jnp.minimum(s + 1 + a, seq_len - 1), s - 1 - a, s + a, s - a])
        tail_fwd = jnp.concatenate([jnp.ones(16, bool), jnp.zeros(32, bool), jnp.ones(16, bool)])
        tail = _filter_rows(tail_lag, tail_fwd, seq_len, fp, d)
        pr = _norm_matmul(st["hr"], g_mix, w_in, tm)
        pm = _norm_matmul(st["hm"], g_mix, w_in, st["hm"].shape[0])
        dims = (b, s, d, n1, n2)
        z1r, z1m = _hyena_conv(pr, pm, 0, conv_w, conv_b, kre, kim, 0, tab, pr, pm, d, skip[0], tail[0], dims=dims)
        z2r, z2m = _hyena_conv(z1r, z1m, None, conv_w, conv_b, kre, kim, 1, tab, pr, pm, 2 * d, skip[1], tail[1],
                               dims=dims)
        outs.append((z2r, z2m))
    return outs


def _attention_weights(w_qkv, n_heads):
    groups = n_heads // GROUP
    qd = n_heads * HEAD_DIM
    kd = groups * HEAD_DIM
    w_q = w_qkv[:, :qd] * (HEAD_DIM ** -0.5)
    w_k = w_qkv[:, qd:qd + kd]
    w_v = w_qkv[:, qd + kd:]
    d = w_qkv.shape[0]
    w_vk = jnp.stack([w_v.reshape(d, groups, HEAD_DIM), w_k.reshape(d, groups, HEAD_DIM)], axis=2)
    w_rows = jnp.concatenate([w_q, w_vk.reshape(d, 2 * kd)], axis=1).astype(BF16)
    return w_rows, w_k.T.astype(BF16)


def _meta_keys_transposed(qvk_m, b, n_heads):
    groups = n_heads // GROUP
    qd = n_heads * HEAD_DIM
    k_m = qvk_m[:, qd:].reshape(b, N_META, groups, 2, HEAD_DIM)[:, :, :, 1, :]
    kt = jnp.transpose(k_m.reshape(b, N_META, groups * HEAD_DIM), (0, 2, 1))
    return jnp.pad(kt, ((0, 0), (0, 0), (0, LANE_TILE - N_META)))


def _encoder_pair(x_prompt, x_sample, meta_tokens, norm_mix, norm_mlp, norm_final,
                  hy_w_in, hy_conv_w, hy_conv_b, fps, hy_skip, hy_w_out, hy_b_out,
                  at_w_qkv, at_sink, at_w_o, mlp_w1, mlp_w2, *, n_heads, tm):
    d = x_prompt.shape[-1]
    streams = []
    for x in (x_prompt, x_sample):
        b, s, _ = x.shape
        streams.append(dict(b=b, s=s, hr=x.reshape(b * s, d),
                            hm=jnp.tile(meta_tokens.astype(F32), (b, 1))))
    zeros_d = jnp.zeros((d,), F32)

    conv_w = jnp.pad(hy_conv_w[0], ((0, 5), (0, 0)))
    conv_b = hy_conv_b[0][None, :]
    zs = _hyena_layer(streams, fps[0], norm_mix[0], hy_w_in[0].astype(BF16), conv_w, conv_b, hy_skip[0], tm)
    w_out = hy_w_out[0].astype(BF16)
    w1 = [w.astype(BF16) for w in mlp_w1]
    w2 = [w.astype(BF16) for w in mlp_w2]
    for st, (zr, zm) in zip(streams, zs):
        st["hr"] = _mixer_out_mlp(st["hr"], zr, w_out, hy_b_out[0], norm_mlp[0], w1[0], w2[0], zeros_d, tm, False)
        st["hm"] = _mixer_out_mlp(st["hm"], zm, w_out, hy_b_out[0], norm_mlp[0], w1[0], w2[0], zeros_d,
                                  st["hm"].shape[0], False)

    w_rows, w_kt = _attention_weights(at_w_qkv[0], n_heads)
    w_o = at_w_o[0].astype(BF16)
    outs = []
    for st in streams:
        qvk_r, kt_r = _norm_matmul(st["hr"], norm_mix[1], w_rows, tm, wt=w_kt)
        qvk_m = _norm_matmul(st["hm"], norm_mix[1], w_rows, st["hm"].shape[0])
        kt_m = _meta_keys_transposed(qvk_m, st["b"], n_heads)
        att = _attention(qvk_r, kt_r, qvk_m, kt_m, at_sink[0], b=st["b"], s=st["s"], n_heads=n_heads)
        y = _mixer_out_mlp(st["hr"], att, w_o, zeros_d, norm_mlp[1], w1[1], w2[1], norm_final, tm, True)
        outs.append(y.reshape(st["b"], st["s"], d))
    return tuple(outs)


def kernel(x_prompt, x_sample, meta_tokens, norm_mix, norm_mlp, norm_final, hy_w_in, hy_conv_w, hy_conv_b,
           hy_f_w1, hy_f_b1, hy_f_w2, hy_f_b2, hy_f_w3, hy_f_b3, hy_f_wout, hy_f_freq, hy_skip, hy_w_out,
           hy_b_out, at_w_qkv, at_sink, at_w_o, mlp_w1, mlp_w2):
    fps = [dict(w1=hy_f_w1[j], b1=hy_f_b1[j], w2=hy_f_w2[j], b2=hy_f_b2[j], w3=hy_f_w3[j], b3=hy_f_b3[j],
                wout=hy_f_wout[j], freq=hy_f_freq[j]) for j in range(hy_f_w1.shape[0])]
    n_heads = at_sink.shape[1]
    return _encoder_pair(x_prompt, x_sample, meta_tokens, norm_mix, norm_mlp, norm_final,
                         hy_w_in, hy_conv_w, hy_conv_b, fps, hy_skip, hy_w_out, hy_b_out,
                         at_w_qkv, at_sink, at_w_o, mlp_w1, mlp_w2, n_heads=n_heads, tm=512)
```

```python
import functools
import math

import jax
import jax.numpy as jnp
from jax import lax
from jax.experimental import pallas as pl
from jax.experimental.pallas import tpu as pltpu

F32 = jnp.float32
BF16 = jnp.bfloat16

N_META = 16
RMS_EPS = 1e-6
HY_BANDS = 16
HY_EMB_PAD = 40
HY_FAST_DECAY = 0.3
HY_SLOW_DECAY = 1.5
HY_DECAY_TARGET = 1e-2
ATT_BLOCK = 128
HEAD_DIM = 64
GROUP = 4
MASK_VALUE = -1e30
FF_CHUNK = 1024
LANE_TILE = 128
N2_CHUNK = 32
FUSED_SLABS = 16
FFT_MID_LANES = 512
VMEM_LIMIT = 56 * 1024 * 1024
HIGHEST = lax.Precision.HIGHEST


def _cparams(sem):
    return pltpu.CompilerParams(dimension_semantics=sem, vmem_limit_bytes=VMEM_LIMIT)


def _rms(x, g):
    return x * lax.rsqrt(jnp.mean(x * x, axis=-1, keepdims=True) + RMS_EPS) * g


def _norm_matmul_kernel(x_ref, g_ref, w_ref, *rest):
    u = _rms(x_ref[...], g_ref[...]).astype(BF16)
    if len(rest) == 1:
        (o_ref,) = rest
    else:
        wt_ref, o_ref, ot_ref = rest
        ot_ref[...] = lax.dot_general(wt_ref[...], u, (((1,), (1,)), ((), ())),
                                      preferred_element_type=F32).astype(ot_ref.dtype)
    o_ref[...] = jnp.dot(u, w_ref[...], preferred_element_type=F32).astype(o_ref.dtype)


def _norm_matmul(x, g, w, tm, wt=None):
    rows, d = x.shape
    n = w.shape[1]
    in_specs = [pl.BlockSpec((tm, d), lambda i: (i, 0)),
                pl.BlockSpec((1, d), lambda i: (0, 0)),
                pl.BlockSpec((d, n), lambda i: (0, 0))]
    out_specs = pl.BlockSpec((tm, n), lambda i: (i, 0))
    out_shape = jax.ShapeDtypeStruct((rows, n), BF16)
    args = [x, g.reshape(1, d), w]
    if wt is not None:
        m = wt.shape[0]
        in_specs.append(pl.BlockSpec((m, d), lambda i: (0, 0)))
        out_specs = [out_specs, pl.BlockSpec((m, tm), lambda i: (0, i))]
        out_shape = [out_shape, jax.ShapeDtypeStruct((m, rows), BF16)]
        args.append(wt)
    return pl.pallas_call(
        _norm_matmul_kernel,
        grid=(rows // tm,),
        in_specs=in_specs, out_specs=out_specs, out_shape=out_shape,
        compiler_params=_cparams(("parallel",)),
        name="norm_matmul",
    )(*args)


def _mixer_out_mlp_kernel(h_ref, z_ref, wp_ref, bp_ref, g_ref, w1_ref, w2_ref, gf_ref, o_ref, *, final_norm):
    h = h_ref[...] + jnp.dot(z_ref[...], wp_ref[...], preferred_element_type=F32) + bp_ref[...]
    u = _rms(h, g_ref[...]).astype(BF16)
    acc = h
    d_ff = w1_ref.shape[1]
    for c in range(d_ff // FF_CHUNK):
        a = jnp.dot(u, w1_ref[:, c * FF_CHUNK:(c + 1) * FF_CHUNK], preferred_element_type=F32)
        a = jnp.square(jnp.maximum(a, 0.0)).astype(BF16)
        acc = acc + jnp.dot(a, w2_ref[c * FF_CHUNK:(c + 1) * FF_CHUNK, :], preferred_element_type=F32)
    if final_norm:
        acc = _rms(acc, gf_ref[...])
    o_ref[...] = acc


def _mixer_out_mlp(h, z, wp, bp, g, w1, w2, gf, tm, final_norm):
    rows, d = h.shape
    dz = z.shape[1]
    d_ff = w1.shape[1]
    const = lambda i: (0, 0)
    return pl.pallas_call(
        functools.partial(_mixer_out_mlp_kernel, final_norm=final_norm),
        grid=(rows // tm,),
        in_specs=[pl.BlockSpec((tm, d), lambda i: (i, 0)),
                  pl.BlockSpec((tm, dz), lambda i: (i, 0)),
                  pl.BlockSpec((dz, d), const),
                  pl.BlockSpec((1, d), const),
                  pl.BlockSpec((1, d), const),
                  pl.BlockSpec((d, d_ff), const),
                  pl.BlockSpec((d_ff, d), const),
                  pl.BlockSpec((1, d), const)],
        out_specs=pl.BlockSpec((tm, d), lambda i: (i, 0)),
        out_shape=jax.ShapeDtypeStruct((rows, d), F32),
        compiler_params=_cparams(("parallel",)),
        name="mixer_out_mlp",
    )(h, z, wp, bp.reshape(1, d), g.reshape(1, d), w1, w2, gf.reshape(1, d))


def _split_bf16(x):
    hi = x.astype(BF16)
    return hi, (x - hi.astype(F32)).astype(BF16)


def _dot3(a_hi, a_lo, b_hi, b_lo):
    dot = functools.partial(jnp.dot, preferred_element_type=F32)
    return dot(a_hi, b_hi) + (dot(a_hi, b_lo) + dot(a_lo, b_hi))


def _filter_kernel(z_ref, t_ref, w1_ref, b1_ref, w2_ref, b2_ref, w3_ref, b3_ref, fr_ref, woh_ref, wol_ref,
                   ad_ref, o_ref):
    d = ad_ref.shape[1]
    dot = functools.partial(jnp.dot, precision=HIGHEST, preferred_element_type=F32)
    fr = fr_ref[...]
    h = jnp.sin(fr * (dot(z_ref[...], w1_ref[...]) + b1_ref[...]))
    h = jnp.sin(fr * (dot(h, w2_ref[...]) + b2_ref[...]))
    h = jnp.sin(fr * (dot(h, w3_ref[...]) + b3_ref[...]))
    ho = _dot3(*_split_bf16(h), woh_ref[...], wol_ref[...])
    decay = jnp.exp(-t_ref[...] * ad_ref[...])
    for o in range(2):
        for dr in range(2):
            o_ref[o, dr] = ho[:, (2 * o + dr) * d:(2 * o + dr + 1) * d] * decay


def _filter_taps(rows, seq_len, fp, d):
    tr = min(rows, 512)
    lagf = jnp.arange(rows, dtype=F32)
    t = lagf / (seq_len - 1)
    w = 2.0 * math.pi * lagf / seq_len
    f = jnp.linspace(1e-4, HY_BANDS - 1, HY_BANDS, dtype=F32)[None, :]
    z = jnp.concatenate([t[:, None], jnp.cos(f * w[:, None]), -jnp.sin(f * w[:, None]),
                         jnp.zeros((rows, HY_EMB_PAD - 2 * HY_BANDS - 1), F32)], axis=-1)
    w1 = jnp.pad(fp["w1"], ((0, HY_EMB_PAD - fp["w1"].shape[0]), (0, 0)))
    hid = w1.shape[1]
    max_decay = math.log(HY_DECAY_TARGET) / HY_FAST_DECAY
    min_decay = math.log(HY_DECAY_TARGET) / HY_SLOW_DECAY
    adel = jnp.abs(jnp.linspace(min_decay, max_decay, d, dtype=F32))[None, :]
    const = lambda i: (0, 0)
    row = lambda i: (i, 0)
    wo_hi, wo_lo = _split_bf16(fp["wout"])
    return pl.pallas_call(
        _filter_kernel,
        grid=(rows // tr,),
        in_specs=[pl.BlockSpec((tr, HY_EMB_PAD), row), pl.BlockSpec((tr, 1), row),
                  pl.BlockSpec((HY_EMB_PAD, hid), const), pl.BlockSpec((1, hid), const),
                  pl.BlockSpec((hid, hid), const), pl.BlockSpec((1, hid), const),
                  pl.BlockSpec((hid, hid), const), pl.BlockSpec((1, hid), const),
                  pl.BlockSpec((1, hid), const), pl.BlockSpec((hid, 4 * d), const), pl.BlockSpec((hid, 4 * d), const),
                  pl.BlockSpec((1, d), const)],
        out_specs=pl.BlockSpec((2, 2, tr, d), lambda i: (0, 0, i, 0)),
        out_shape=jax.ShapeDtypeStruct((2, 2, rows, d), F32),
        compiler_params=_cparams(("parallel",)),
        name="hyena_filter",
    )(z, t[:, None], w1, fp["b1"][None], fp["w2"], fp["b2"][None],
      fp["w3"], fp["b3"][None], fp["freq"][None], wo_hi, wo_lo, adel)


def _cplx_block(re, im):
    return jnp.concatenate([jnp.concatenate([re, -im], axis=-1), jnp.concatenate([im, re], axis=-1)], axis=-2)


def _fft_tables(n1, n2):
    n = n1 * n2
    n1h = n1 // 2
    pad = 8 - 1
    k1 = jnp.arange(n1, dtype=jnp.int32)
    c2 = jnp.arange(n2, dtype=jnp.int32)
    cols = jnp.concatenate([jnp.arange(n1h, dtype=jnp.int32), jnp.array([n1 - 1], jnp.int32)])
    pos = n2 * cols[None, None, :] + c2[:, None, None]
    ang = ((k1[None, :, None] * pos) % n).astype(F32) * (-2.0 * math.pi / n)
    wr = jnp.pad(jnp.cos(ang), ((0, 0), (0, 0), (0, pad)))
    wi = jnp.pad(jnp.sin(ang), ((0, 0), (0, 0), (0, pad)))
    fwd1 = _cplx_block(wr, wi)
    inv1 = _cplx_block(jnp.swapaxes(wr, 1, 2), -jnp.swapaxes(wi, 1, 2)) / n
    posf = n2 * k1[None, None, :] + c2[:, None, None]
    angf = ((k1[None, :, None] * posf) % n).astype(F32) * (-2.0 * math.pi / n)
    fil1 = jnp.concatenate([jnp.cos(angf), jnp.sin(angf)], axis=1)
    ang2 = ((c2[:, None] * c2[None, :]) % n2).astype(F32) * (-2.0 * math.pi / n2)
    fr, fi = jnp.cos(ang2), jnp.sin(ang2)
    fwd2 = _cplx_block(fr, fi)
    return dict(fwd1=fwd1.astype(BF16), inv1=inv1.astype(BF16), fil1=fil1.astype(BF16),
                fwd2=fwd2.astype(BF16), inv2=_cplx_block(fr, -fi).astype(BF16))


def _ordered_short_conv(meta, real, cw_ref, cb_ref, seq_ref):
    s = real.shape[0]
    dt = real.shape[1]
    seq_ref[0:8, :] = jnp.zeros((8, dt), F32)
    seq_ref[8:8 + N_META, :] = meta.astype(F32)
    seq_ref[8 + N_META:8 + N_META + s, :] = real.astype(F32)
    seq_ref[8 + N_META + s:16 + N_META + s, :] = jnp.zeros((8, dt), F32)
    x = seq_ref[...]
    rows = x.shape[0]
    cw = cw_ref[...].astype(F32)
    y = (pltpu.roll(x, 1, axis=0) * cw[0:1] + x * cw[1:2] + pltpu.roll(x, rows - 1, axis=0) * cw[2:3]
         + cb_ref[...].astype(F32))
    return y[8:8 + N_META], y[8 + N_META:8 + N_META + s]


def _fft_in_kernel(*refs, n1, n2, short_conv):
    if short_conv:
        (xr_ref, xm_ref, cw_ref, cb_ref, f_ref, are_ref, aim_ref, vr_ref, vm_ref,
         xs_ref, mp_ref, seq_ref) = refs
    else:
        xr_ref, xm_ref, f_ref, are_ref, aim_ref, xs_ref, mp_ref = refs
    n1h = n1 // 2
    chunk = pl.program_id(2)

    @pl.when(chunk == 0)
    def _prepare():
        for e in range(2):
            if short_conv:
                meta, real = _ordered_short_conv(xm_ref[e], xr_ref[e], cw_ref, cb_ref, seq_ref)
                vr_ref[e] = real.astype(vr_ref.dtype)
                vm_ref[e] = meta.astype(vm_ref.dtype)
            else:
                meta, real = xm_ref[e].astype(F32), xr_ref[e].astype(F32)
            xs_ref[e] = real
            mp_ref[e] = jnp.zeros(mp_ref.shape[1:], F32)
            for j in range(N_META):
                mp_ref[e, pl.ds(8 * (n2 - N_META + j), 1), :] = meta[j:j + 1]

    for i in range(N2_CHUNK):
        c2 = chunk * N2_CHUNK + i
        parts = []
        for e in range(2):
            parts.append(xs_ref[e, pl.ds(c2, n1h, stride=n2), :])
            parts.append(mp_ref[e, pl.ds(pl.multiple_of(c2 * 8, 8), 8), :])
        rhs = jnp.concatenate(parts, axis=0).astype(BF16)
        out = jnp.dot(f_ref[i], rhs, preferred_element_type=F32)
        are_ref[i] = out[:n1]
        aim_ref[i] = out[n1:]


def _fft_in(xr, xm, col_off, conv_w, conv_b, tab, *, b, s, d, n1, n2, dt):
    short_conv = conv_w is not None
    pairs = b // 2
    c = xr.shape[1]
    cb0 = col_off // dt
    xr4 = xr.reshape(pairs, 2, s, c)
    xm4 = xm.reshape(pairs, 2, N_META, c)
    kk = tab["fwd1"].shape[2]
    in_specs = [pl.BlockSpec((None, 2, s, dt), lambda p, j, t: (p, 0, 0, cb0 + j)),
                pl.BlockSpec((None, 2, N_META, dt), lambda p, j, t: (p, 0, 0, cb0 + j))]
    args = [xr4, xm4]
    if short_conv:
        in_specs += [pl.BlockSpec((8, dt), lambda p, j, t: (0, cb0 + j)),
                     pl.BlockSpec((1, dt), lambda p, j, t: (0, cb0 + j))]
        args += [conv_w, conv_b]
    in_specs.append(pl.BlockSpec((N2_CHUNK, 2 * n1, kk), lambda p, j, t: (t, 0, 0)))
    args.append(tab["fwd1"])
    a_spec = pl.BlockSpec((None, N2_CHUNK, n1, dt), lambda p, j, t: (p, t, 0, j))
    a_shape = jax.ShapeDtypeStruct((pairs, n2, n1, d), F32)
    out_specs = [a_spec, a_spec]
    out_shape = [a_shape, a_shape]
    scratch = [pltpu.VMEM((2, s, dt), F32), pltpu.VMEM((2, 8 * n2, dt), F32)]
    if short_conv:
        out_specs += [pl.BlockSpec((None, 2, s, dt), lambda p, j, t: (p, 0, 0, j)),
                      pl.BlockSpec((None, 2, N_META, dt), lambda p, j, t: (p, 0, 0, j))]
        out_shape += [jax.ShapeDtypeStruct((pairs, 2, s, d), BF16),
                      jax.ShapeDtypeStruct((pairs, 2, N_META, d), BF16)]
        scratch.append(pltpu.VMEM((s + N_META + 16, dt), F32))
    return pl.pallas_call(
        functools.partial(_fft_in_kernel, n1=n1, n2=n2, short_conv=short_conv),
        grid=(pairs, d // dt, n2 // N2_CHUNK),
        in_specs=in_specs, out_specs=out_specs, out_shape=out_shape, scratch_shapes=scratch,
        compiler_params=_cparams(("parallel", "parallel", "arbitrary")),
        name="fft_in",
    )(*args)


def _fft_mid_kernel(are_ref, aim_ref, kre_ref, kim_ref, f_ref, g_ref, bre_ref, bim_ref, *, n2, slabs):
    for i in range(slabs):
        x = jnp.concatenate([are_ref[:, i, :], aim_ref[:, i, :]], axis=0).astype(BF16)
        z = jnp.dot(f_ref[...], x, preferred_element_type=F32)
        zr, zi = z[:n2], z[n2:]
        kr, ki = kre_ref[i], kim_ref[i]
        p = jnp.concatenate([zr * kr - zi * ki, zr * ki + zi * kr], axis=0).astype(BF16)
        y = jnp.dot(g_ref[...], p, preferred_element_type=F32)
        bre_ref[:, i, :] = y[:n2]
        bim_ref[:, i, :] = y[n2:]


def _fft_mid(are, aim, kre, kim, order, tab, *, dt, slabs):
    pairs, n2, n1, d = are.shape
    k_blk0 = order * (n1 // slabs)
    blk = (None, n2, slabs, dt)
    amap = lambda j, kb, p: (p, 0, kb, j)
    kmap = lambda j, kb, p: (k_blk0 + kb, 0, j)
    const = lambda j, kb, p: (0, 0)
    shp = jax.ShapeDtypeStruct((pairs, n2, n1, d), F32)
    return pl.pallas_call(
        functools.partial(_fft_mid_kernel, n2=n2, slabs=slabs),
        grid=(d // dt, n1 // slabs, pairs),
        in_specs=[pl.BlockSpec(blk, amap), pl.BlockSpec(blk, amap),
                  pl.BlockSpec((slabs, n2, dt), kmap), pl.BlockSpec((slabs, n2, dt), kmap),
                  pl.BlockSpec((2 * n2, 2 * n2), const), pl.BlockSpec((2 * n2, 2 * n2), const)],
        out_specs=[pl.BlockSpec(blk, amap), pl.BlockSpec(blk, amap)],
        out_shape=[shp, shp],
        compiler_params=_cparams(("parallel", "parallel", "arbitrary")),
        name="fft_mid",
    )(are, aim, kre, kim, tab["fwd2"], tab["inv2"])


def _filter_fft_in_kernel(c_ref, f_ref, are_ref, aim_ref, *, n1, n2):
    chunk = pl.program_id(2)
    for i in range(N2_CHUNK):
        rhs = c_ref[pl.ds(chunk * N2_CHUNK + i, n1, stride=n2), :]
        out = jnp.dot(f_ref[i], rhs.astype(BF16), preferred_element_type=F32)
        are_ref[i] = out[:n1]
        aim_ref[i] = out[n1:]


def _filter_fft_mid_kernel(are_ref, aim_ref, f_ref, kre_ref, kim_ref, *, n2, slabs):
    for i in range(slabs):
        x = jnp.concatenate([are_ref[:, i, :], aim_ref[:, i, :]], axis=0)
        z = jnp.dot(f_ref[...], x.astype(BF16), preferred_element_type=F32)
        kre_ref[i] = z[:n2]
        kim_ref[i] = z[n2:]


def _filter_spectrum(circ, tab, *, n1, n2, dt, slabs):
    orders, n, d = circ.shape
    shp = [jax.ShapeDtypeStruct((orders, n2, n1, d), F32)] * 2
    tab_spec = pl.BlockSpec((N2_CHUNK, 2 * n1, n1), lambda o, j, t: (t, 0, 0))
    lt = LANE_TILE
    are, aim = pl.pallas_call(
        functools.partial(_filter_fft_in_kernel, n1=n1, n2=n2),
        grid=(orders, d // lt, n2 // N2_CHUNK),
        in_specs=[pl.BlockSpec((None, n, lt), lambda o, j, t: (o, 0, j)), tab_spec],
        out_specs=[pl.BlockSpec((None, N2_CHUNK, n1, lt), lambda o, j, t: (o, t, 0, j))] * 2,
        out_shape=shp,
        compiler_params=_cparams(("parallel", "parallel", "arbitrary")),
        name="filter_fft_in",
    )(circ, tab["fil1"])
    blk = (None, n2, slabs, dt)
    amap = lambda o, kb, j: (o, 0, kb, j)
    f_spec = pl.BlockSpec((2 * n2, 2 * n2), lambda o, kb, j: (0, 0))
    return pl.pallas_call(
        functools.partial(_filter_fft_mid_kernel, n2=n2, slabs=slabs),
        grid=(orders, n1 // slabs, d // dt),
        in_specs=[pl.BlockSpec(blk, amap), pl.BlockSpec(blk, amap), f_spec],
        out_specs=[pl.BlockSpec((None, slabs, n2, dt), lambda o, kb, j: (o, kb, 0, j))] * 2,
        out_shape=[jax.ShapeDtypeStruct((orders, n1, n2, d), F32)] * 2,
        compiler_params=_cparams(("parallel", "parallel", "parallel")),
        name="filter_fft_mid",
    )(are, aim, tab["fwd2"])


def _alias_patch(tail, u_meta, u_last):
    dfw = tail[0:16] - tail[16:32]
    dbw = tail[32:48] - tail[48:64]
    ridx = lax.broadcasted_iota(jnp.int32, dfw.shape, 0)
    real_fix = jnp.zeros_like(dfw)
    meta_fix = jnp.zeros_like(dfw)
    for o in range(15):
        shifted = jnp.where(ridx >= o + 1, pltpu.roll(dfw, o + 1, axis=0), 0.0)
        real_fix = real_fix + shifted * u_meta[o:o + 1]
    for c in range(16):
        src = u_last if c == 0 else pltpu.roll(u_last, 16 - c, axis=0)
        meta_fix = meta_fix + jnp.where(ridx + c <= 15, src, 0.0) * dbw[c:c + 1]
    return real_fix, meta_fix


def _fft_out_kernel(bre_ref, bim_ref, g_ref, gr_ref, gm_ref, cw_ref, cb_ref, ur_ref, um_ref, skip_ref,
                    tail_ref, zr_ref, zm_ref, ys_ref, yp_ref, seq_ref, *, n1, n2):
    n1h = n1 // 2
    kk = n1h + 8
    s = ys_ref.shape[1]
    chunk = pl.program_id(2)

    for i in range(N2_CHUNK):
        c2 = chunk * N2_CHUNK + i
        rhs = jnp.concatenate([bre_ref[i], bim_ref[i]], axis=0).astype(BF16)
        y = jnp.dot(g_ref[i], rhs, preferred_element_type=F32)
        for e in range(2):
            ys_ref[e, pl.ds(c2, n1h, stride=n2), :] = y[e * kk:e * kk + n1h]
            yp_ref[e, pl.ds(pl.multiple_of(c2 * 8, 8), 8), :] = y[e * kk + n1h:(e + 1) * kk]

    @pl.when(chunk == n2 // N2_CHUNK - 1)
    def _gate():
        tail = tail_ref[...]
        skip = skip_ref[...]
        for e in range(2):
            y_meta = jnp.concatenate(
                [yp_ref[e, pl.ds(8 * (n2 - N_META + j), 1), :] for j in range(N_META)], axis=0)
            u_real = ur_ref[e].astype(F32)
            u_meta = um_ref[e].astype(F32)
            real_fix, meta_fix = _alias_patch(tail, u_meta, u_real[s - 16:s])
            ys_ref[e, s - 16:s, :] = ys_ref[e, s - 16:s, :] + real_fix
            g_meta, g_real = _ordered_short_conv(gm_ref[e], gr_ref[e], cw_ref, cb_ref, seq_ref)
            zr_ref[e] = (g_real * (ys_ref[e] + skip * u_real)).astype(zr_ref.dtype)
            zm_ref[e] = (g_meta * (y_meta + meta_fix + skip * u_meta)).astype(zm_ref.dtype)


def _fft_out(bre, bim, tab, gr, gm, gate_off, conv_w, conv_b, ur, um, skip, tail, *, b, s, d, n1, n2, dt):
    pairs = b // 2
    c = gr.shape[1]
    gb0 = gate_off // dt
    kk2 = tab["inv1"].shape[1]
    gr4 = gr.reshape(pairs, 2, s, c)
    gm4 = gm.reshape(pairs, 2, N_META, c)
    ur4 = ur.reshape(pairs, 2, s, d)
    um4 = um.reshape(pairs, 2, N_META, d)
    b_spec = pl.BlockSpec((None, N2_CHUNK, n1, dt), lambda p, j, t: (p, t, 0, j))
    zr, zm = pl.pallas_call(
        functools.partial(_fft_out_kernel, n1=n1, n2=n2),
        grid=(pairs, d // dt, n2 // N2_CHUNK),
        in_specs=[b_spec, b_spec,
                  pl.BlockSpec((N2_CHUNK, kk2, 2 * n1), lambda p, j, t: (t, 0, 0)),
                  pl.BlockSpec((None, 2, s, dt), lambda p, j, t: (p, 0, 0, gb0 + j)),
                  pl.BlockSpec((None, 2, N_META, dt), lambda p, j, t: (p, 0, 0, gb0 + j)),
                  pl.BlockSpec((8, dt), lambda p, j, t: (0, gb0 + j)),
                  pl.BlockSpec((1, dt), lambda p, j, t: (0, gb0 + j)),
                  pl.BlockSpec((None, 2, s, dt), lambda p, j, t: (p, 0, 0, j)),
                  pl.BlockSpec((None, 2, N_META, dt), lambda p, j, t: (p, 0, 0, j)),
                  pl.BlockSpec((1, dt), lambda p, j, t: (0, j)),
                  pl.BlockSpec((64, dt), lambda p, j, t: (0, j))],
        out_specs=[pl.BlockSpec((None, 2, s, dt), lambda p, j, t: (p, 0, 0, j)),
                   pl.BlockSpec((None, 2, N_META, dt), lambda p, j, t: (p, 0, 0, j))],
        out_shape=[jax.ShapeDtypeStruct((pairs, 2, s, d), BF16),
                   jax.ShapeDtypeStruct((pairs, 2, N_META, d), BF16)],
        scratch_shapes=[pltpu.VMEM((2, s, dt), F32), pltpu.VMEM((2, 8 * n2, dt), F32),
                        pltpu.VMEM((s + N_META + 16, dt), F32)],
        compiler_params=_cparams(("parallel", "parallel", "arbitrary")),
        name="fft_out",
    )(bre, bim, tab["inv1"], gr4, gm4, conv_w, conv_b, ur4, um4, skip.reshape(1, d), tail)
    return zr.reshape(b * s, d), zm.reshape(b * N_META, d)


def _fused_conv_kernel(*refs, n1, n2, short_conv):
    if short_conv:
        (xr_ref, xm_ref, cwx_ref, cbx_ref, f1_ref, kre_ref, kim_ref, f2_ref, g2_ref, g1_ref, gr_ref, gm_ref,
         cwg_ref, cbg_ref, skip_ref, tail_ref, zr_ref, zm_ref, ur_ref, um_ref,
         are_ref, aim_ref, xs_ref, mp_ref, seq_ref) = refs
    else:
        (ur_ref, um_ref, f1_ref, kre_ref, kim_ref, f2_ref, g2_ref, g1_ref, gr_ref, gm_ref,
         cwg_ref, cbg_ref, skip_ref, tail_ref, zr_ref, zm_ref,
         are_ref, aim_ref, xs_ref, mp_ref, seq_ref) = refs
    n1h = n1 // 2
    kk = n1h + 8
    s = n1h * n2
    nkb = n1 // FUSED_SLABS
    dt = xs_ref.shape[3]

    for e in range(2):
        if short_conv:
            meta, real = _ordered_short_conv(xm_ref[e], xr_ref[e], cwx_ref, cbx_ref, seq_ref)
            ur_ref[e] = real.astype(ur_ref.dtype)
            um_ref[e] = meta.astype(um_ref.dtype)
        else:
            meta, real = um_ref[e].astype(F32), ur_ref[e].astype(F32)
        xs_ref[e] = real.reshape(n1h, n2, dt)
        mp_ref[e] = jnp.zeros(mp_ref.shape[1:], F32)
        for j in range(N_META):
            mp_ref[e, pl.ds(8 * (n2 - N_META + j), 1), :] = meta[j:j + 1]

    def stage1(t, carry):
        base = pl.multiple_of(t * N2_CHUNK, N2_CHUNK)
        xt = [jnp.swapaxes(xs_ref[e, :, pl.ds(base, N2_CHUNK), :], 0, 1) for e in range(2)]
        for i in range(N2_CHUNK):
            parts = []
            for e in range(2):
                parts += [xt[e][i], mp_ref[e, pl.ds(pl.multiple_of((base + i) * 8, 8), 8), :]]
            rhs = jnp.concatenate(parts, axis=0).astype(BF16)
            out = jnp.dot(f1_ref[base + i], rhs, preferred_element_type=F32)
            are_ref[:, base + i] = out[:n1].reshape(nkb, FUSED_SLABS, dt)
            aim_ref[:, base + i] = out[n1:].reshape(nkb, FUSED_SLABS, dt)
        return carry

    def stage2(kb, carry):
        k0 = pl.multiple_of(kb * FUSED_SLABS, FUSED_SLABS)
        xr_blk = jnp.swapaxes(are_ref[kb], 0, 1)
        xi_blk = jnp.swapaxes(aim_ref[kb], 0, 1)
        ys = []
        for i in range(0, FUSED_SLABS, 2):
            x = jnp.concatenate([jnp.concatenate([xr_blk[i + h], xi_blk[i + h]], axis=0) for h in range(2)],
                                axis=1).astype(BF16)
            z = jnp.dot(f2_ref[...], x, preferred_element_type=F32)
            zr, zi = z[:n2], z[n2:]
            kr = jnp.concatenate([kre_ref[k0 + i], kre_ref[k0 + i + 1]], axis=1)
            ki = jnp.concatenate([kim_ref[k0 + i], kim_ref[k0 + i + 1]], axis=1)
            p = jnp.concatenate([zr * kr - zi * ki, zr * ki + zi * kr], axis=0).astype(BF16)
            y = jnp.dot(g2_ref[...], p, preferred_element_type=F32)
            ys += [y[:, :dt], y[:, dt:]]
        yt = jnp.swapaxes(jnp.stack(ys, axis=0), 0, 1)
        are_ref[kb] = yt[:n2]
        aim_ref[kb] = yt[n2:]
        return carry

    def stage3(t, carry):
        base = pl.multiple_of(t * N2_CHUNK, N2_CHUNK)
        ys = []
        for i in range(N2_CHUNK):
            rhs = jnp.concatenate([are_ref[:, base + i].reshape(n1, dt), aim_ref[:, base + i].reshape(n1, dt)],
                                  axis=0).astype(BF16)
            y = jnp.dot(g1_ref[base + i], rhs, preferred_element_type=F32)
            ys.append(y)
            for e in range(2):
                mp_ref[e, pl.ds(pl.multiple_of((base + i) * 8, 8), 8), :] = y[e * kk + n1h:(e + 1) * kk]
        yt = jnp.swapaxes(jnp.stack(ys, axis=0), 0, 1)
        for e in range(2):
            xs_ref[e, :, pl.ds(base, N2_CHUNK), :] = yt[e * kk:e * kk + n1h]
        return carry

    lax.fori_loop(0, n2 // N2_CHUNK, stage1, 0)
    lax.fori_loop(0, nkb, stage2, 0)
    lax.fori_loop(0, n2 // N2_CHUNK, stage3, 0)

    tail = tail_ref[...]
    skip = skip_ref[...]
    for e in range(2):
        y_meta = jnp.concatenate(
            [mp_ref[e, pl.ds(8 * (n2 - N_META + j), 1), :] for j in range(N_META)], axis=0)
        u_real = ur_ref[e].astype(F32)
        u_meta = um_ref[e].astype(F32)
        real_fix, meta_fix = _alias_patch(tail, u_meta, u_real[s - 16:s])
        xs_ref[e, n1h - 1, n2 - 16:n2, :] = xs_ref[e, n1h - 1, n2 - 16:n2, :] + real_fix
        y_real = xs_ref[e].reshape(s, dt)
        g_meta, g_real = _ordered_short_conv(gm_ref[e], gr_ref[e], cwg_ref, cbg_ref, seq_ref)
        zr_ref[e] = (g_real * (y_real + skip * u_real)).astype(zr_ref.dtype)
        zm_ref[e] = (g_meta * (y_meta + meta_fix + skip * u_meta)).astype(zm_ref.dtype)


def _fused_conv(xr, xm, col_off, conv_w, conv_b, kre, kim, order, tab, gr, gm, gate_off, skip, tail, *,
                b, s, d, n1, n2):
    short_conv = col_off is not None
    pairs = b // 2
    dt = LANE_TILE
    cx0 = (col_off or 0) // dt
    cg0 = gate_off // dt
    kk2 = tab["fwd1"].shape[2]
    once = dict(pipeline_mode=pl.Buffered(1))
    seq4 = lambda a, rows: a.reshape(pairs, 2, rows, a.shape[1])
    x_spec = lambda rows, c0: pl.BlockSpec((None, 2, rows, dt), lambda j, p: (p, 0, 0, c0 + j))
    row_spec = lambda rows, c0: pl.BlockSpec((rows, dt), lambda j, p: (0, c0 + j))
    const2 = pl.BlockSpec((2 * n2, 2 * n2), lambda j, p: (0, 0), **once)
    k_spec = pl.BlockSpec((n1, n2, dt), lambda j, p: (order, 0, j), **once)
    in_specs = [x_spec(s, cx0), x_spec(N_META, cx0)]
    args = [seq4(xr, s), seq4(xm, N_META)]
    if short_conv:
        in_specs += [row_spec(8, cx0), row_spec(1, cx0)]
        args += [conv_w, conv_b]
    in_specs += [pl.BlockSpec((n2, 2 * n1, kk2), lambda j, p: (0, 0, 0), **once),
                 k_spec, k_spec, const2, const2,
                 pl.BlockSpec((n2, kk2, 2 * n1), lambda j, p: (0, 0, 0), **once),
                 x_spec(s, cg0), x_spec(N_META, cg0), row_spec(8, cg0), row_spec(1, cg0),
                 row_spec(1, 0), row_spec(64, 0)]
    args += [tab["fwd1"], kre, kim, tab["fwd2"], tab["inv2"], tab["inv1"],
             seq4(gr, s), seq4(gm, N_META), conv_w, conv_b, skip.reshape(1, d), tail]
    seq_out = lambda rows: pl.BlockSpec((None, 2, rows, dt), lambda j, p: (p, 0, 0, j))
    out_specs = [seq_out(s), seq_out(N_META)]
    out_shape = [jax.ShapeDtypeStruct((pairs, 2, s, d), BF16), jax.ShapeDtypeStruct((pairs, 2, N_META, d), BF16)]
    if short_conv:
        out_specs = out_specs * 2
        out_shape = out_shape * 2
    scratch = ([pltpu.VMEM((n1 // FUSED_SLABS, n2, FUSED_SLABS, dt), F32)] * 2
               + [pltpu.VMEM((2, n1 // 2, n2, dt), F32), pltpu.VMEM((2, 8 * n2, dt), F32),
                  pltpu.VMEM((s + N_META + 16, dt), F32)])
    outs = pl.pallas_call(
        functools.partial(_fused_conv_kernel, n1=n1, n2=n2, short_conv=short_conv),
        grid=(d // dt, pairs),
        in_specs=in_specs, out_specs=out_specs, out_shape=out_shape, scratch_shapes=scratch,
        compiler_params=_cparams(("parallel", "parallel")),
        name="fused_conv",
    )(*args)
    return [o.reshape(-1, d) for o in outs]


def _attn_kernel(own_ref, vprev_ref, vnext_ref, vmeta_ref, ktp_ref, kto_ref, ktn_ref, ktm_ref,
                 bias_ref, shift_ref, o_ref, *, groups):
    blk = pl.program_id(1).astype(F32)
    gw = GROUP * HEAD_DIM
    qd = groups * gw
    lane = lax.broadcasted_iota(jnp.int32, (1, LANE_TILE), 1)
    low = lane < HEAD_DIM
    pad_rows = jnp.zeros((ATT_BLOCK - N_META, 2 * HEAD_DIM), BF16)
    zero = jnp.zeros((ATT_BLOCK, LANE_TILE), BF16)
    for g in range(groups):
        vk_own = own_ref[:, qd + g * LANE_TILE:qd + (g + 1) * LANE_TILE]
        grp = slice(g * LANE_TILE, (g + 1) * LANE_TILE)
        vk = jnp.concatenate([vprev_ref[:, grp], vk_own, vnext_ref[:, grp], vmeta_ref[:, grp], pad_rows],
                             axis=0)
        v_ones = jnp.where(low, vk, jnp.ones_like(vk))
        kt_rows = slice(g * HEAD_DIM, (g + 1) * HEAD_DIM)
        kt = jnp.concatenate([ktp_ref[kt_rows, :], kto_ref[kt_rows, :], ktn_ref[kt_rows, :], ktm_ref[kt_rows, :]],
                             axis=1)
        kt2 = jnp.concatenate([kt, kt], axis=0)
        parts = []
        for pr in range(GROUP // 2):
            qp = own_ref[:, g * gw + pr * LANE_TILE:g * gw + (pr + 1) * LANE_TILE]
            parts += [jnp.where(low, qp, zero), jnp.where(low, zero, qp)]
        q = jnp.concatenate(parts, axis=0)
        sc = jnp.dot(q, kt2, preferred_element_type=F32) + bias_ref[g]
        t = [sc[:, i * LANE_TILE:(i + 1) * LANE_TILE] for i in range(3)]
        t.append(sc[:, 3 * LANE_TILE:] - shift_ref[g] * blk)
        m = jnp.max(jnp.maximum(jnp.maximum(t[0], t[1]), jnp.maximum(t[2], t[3])), axis=1, keepdims=True)
        p = jnp.concatenate([jnp.exp(x - m) for x in t], axis=1).astype(BF16)
        oa = jnp.dot(p, v_ones, preferred_element_type=F32)
        ob = pltpu.roll(oa, HEAD_DIM, axis=1)
        outs = []
        for pr in range(GROUP // 2):
            ev = slice((2 * pr) * ATT_BLOCK, (2 * pr + 1) * ATT_BLOCK)
            od = slice((2 * pr + 1) * ATT_BLOCK, (2 * pr + 2) * ATT_BLOCK)
            outs.append(jnp.where(low, oa[ev] / ob[ev], ob[od] / oa[od]))
        o_ref[:, g * gw:(g + 1) * gw] = jnp.concatenate(outs, axis=1).astype(o_ref.dtype)


def _attention_tables(n_heads, sink):
    groups = n_heads // GROUP
    slopes = jnp.exp2(-8.0 * jnp.arange(1, n_heads + 1, dtype=F32) / n_heads)
    i = jnp.arange(ATT_BLOCK, dtype=jnp.int32)[:, None]
    c = jnp.arange(4 * ATT_BLOCK, dtype=jnp.int32)[None, :]
    dist = jnp.abs(c - ATT_BLOCK - i)
    key_blk = c // ATT_BLOCK
    in_band = jnp.logical_and(c < 3 * ATT_BLOCK, dist <= ATT_BLOCK)
    meta_col = jnp.logical_and(c >= 3 * ATT_BLOCK, c < 3 * ATT_BLOCK + N_META)
    sink_col = c == 3 * ATT_BLOCK + N_META
    meta_dist = N_META + i - (c - 3 * ATT_BLOCK)
    tables = []
    for drop in (None, 0, 2):
        ok = in_band if drop is None else jnp.logical_and(in_band, key_blk != drop)
        d_eff = jnp.where(ok, dist, jnp.where(meta_col, meta_dist, 0)).astype(F32)
        live = jnp.logical_or(ok, meta_col)
        tab = jnp.where(live[None], -slopes[:, None, None] * d_eff[None], MASK_VALUE)
        tables.append(jnp.where(sink_col[None], sink.astype(F32)[:, None, None], tab))
    bias = jnp.stack(tables, axis=0).reshape(3, groups, GROUP * ATT_BLOCK, 4 * ATT_BLOCK)
    lane = jnp.arange(LANE_TILE)[None, :]
    shift = jnp.where(lane < N_META, jnp.repeat(slopes * ATT_BLOCK, ATT_BLOCK)[:, None], 0.0)
    return bias, shift.reshape(groups, GROUP * ATT_BLOCK, LANE_TILE)


def _attention(qvk_r, kt_r, qvk_m, kt_m, sink, *, b, s, n_heads):
    groups = n_heads // GROUP
    nblk = s // ATT_BLOCK
    qd = n_heads * HEAD_DIM
    width = qvk_r.shape[1]
    vkw = groups * 2 * HEAD_DIM
    vkb = qd // vkw
    bias, shift = _attention_tables(n_heads, sink)

    def variant(j):
        return jnp.where(j == 0, 1, jnp.where(j == nblk - 1, 2, 0))

    prev = lambda i, j: i * nblk + jnp.maximum(j - 1, 0)
    nxt = lambda i, j: i * nblk + jnp.minimum(j + 1, nblk - 1)
    return pl.pallas_call(
        functools.partial(_attn_kernel, groups=groups),
        grid=(b, nblk),
        in_specs=[pl.BlockSpec((ATT_BLOCK, width), lambda i, j: (i * nblk + j, 0)),
                  pl.BlockSpec((ATT_BLOCK, vkw), lambda i, j: (prev(i, j), vkb)),
                  pl.BlockSpec((ATT_BLOCK, vkw), lambda i, j: (nxt(i, j), vkb)),
                  pl.BlockSpec((N_META, vkw), lambda i, j: (i, vkb)),
                  pl.BlockSpec((groups * HEAD_DIM, ATT_BLOCK), lambda i, j: (0, prev(i, j))),
                  pl.BlockSpec((groups * HEAD_DIM, ATT_BLOCK), lambda i, j: (0, i * nblk + j)),
                  pl.BlockSpec((groups * HEAD_DIM, ATT_BLOCK), lambda i, j: (0, nxt(i, j))),
                  pl.BlockSpec((None, groups * HEAD_DIM, LANE_TILE), lambda i, j: (i, 0, 0)),
                  pl.BlockSpec((None, groups, GROUP * ATT_BLOCK, 4 * ATT_BLOCK), lambda i, j: (variant(j), 0, 0, 0)),
                  pl.BlockSpec((groups, GROUP * ATT_BLOCK, LANE_TILE), lambda i, j: (0, 0, 0))],
        out_specs=pl.BlockSpec((ATT_BLOCK, qd), lambda i, j: (i * nblk + j, 0)),
        out_shape=jax.ShapeDtypeStruct((b * s, qd), BF16),
        compiler_params=_cparams(("parallel", "arbitrary")),
        name="window_attention",
    )(qvk_r, qvk_r, qvk_r, qvk_m, kt_r, kt_r, kt_r, kt_m, bias, shift)


def _fft_split(s):
    n2 = 128 if s >= 1024 else 32
    return (2 * s) // n2, n2


def _fused_conv_vmem_bytes(s, n1, n2):
    lane_bytes = LANE_TILE * 4
    spectrum = 4 * n1 * n2 * lane_bytes
    tables = 2 * n2 * 2 * n1 * 2 * (n1 // 2 + 8) * 2
    sequence = (2 * s + 2 * 8 * n2 + s + N_META + 16) * lane_bytes
    blocks = 4 * 2 * 2 * (s + N_META) * LANE_TILE * 2
    return spectrum + tables + sequence + blocks


def _hyena_conv(xr, xm, col_off, conv_w, conv_b, kre, kim, order, tab, gr, gm, gate_off, skip, tail, *, dims):
    b, s, d, n1, n2 = dims
    if _fused_conv_vmem_bytes(s, n1, n2) <= (VMEM_LIMIT * 7) // 8:
        outs = _fused_conv(xr, xm, col_off, conv_w, conv_b, kre, kim, order, tab, gr, gm, gate_off, skip, tail,
                           b=b, s=s, d=d, n1=n1, n2=n2)
        return outs[0], outs[1]
    kw = dict(b=b, s=s, d=d, n1=n1, n2=n2, dt=LANE_TILE)
    if col_off is not None:
        are, aim, ur4, um4 = _fft_in(xr, xm, col_off, conv_w, conv_b, tab, **kw)
        ur, um = ur4.reshape(b * s, d), um4.reshape(b * N_META, d)
    else:
        are, aim = _fft_in(xr, xm, 0, None, None, tab, **kw)
        ur, um = xr, xm
    bre, bim = _fft_mid(are, aim, kre, kim, order, tab, dt=min(d, FFT_MID_LANES), slabs=8)
    return _fft_out(bre, bim, tab, gr, gm, gate_off, conv_w, conv_b, ur, um, skip, tail, **kw)


def _hyena_layer(streams, fp, g_mix, w_in, conv_w, conv_b, skip, tm):
    outs = []
    d = w_in.shape[0]
    for st in streams:
        b, s = st["b"], st["s"]
        n1, n2 = _fft_split(s)
        seq_len = s + N_META
        n = 2 * s
        tab = _fft_tables(n1, n2)
        taps = _filter_taps(-(-seq_len // 512) * 512, seq_len, fp, d)
        circ = jnp.concatenate([taps[:, 0, :s + 1], jnp.flip(taps[:, 1, 1:s], axis=1)], axis=1)
        kre, kim = [k.reshape(2 * n1, n2, d)
                    for k in _filter_spectrum(circ, tab, n1=n1, n2=n2, dt=2 * LANE_TILE, slabs=8)]
        a = jnp.arange(16, dtype=jnp.int32)
        tail = jnp.concatenate([jnp.take(taps[:, 0], jnp.minimum(s + 1 + a, seq_len - 1), axis=1),
                                jnp.take(taps[:, 1], s - 1 - a, axis=1),
                                jnp.take(taps[:, 1], s + a, axis=1),
                                jnp.take(taps[:, 0], s - a, axis=1)], axis=1)
        pr = _norm_matmul(st["hr"], g_mix, w_in, tm)
        pm = _norm_matmul(st["hm"], g_mix, w_in, st["hm"].shape[0])
        dims = (b, s, d, n1, n2)
        z1r, z1m = _hyena_conv(pr, pm, 0, conv_w, conv_b, kre, kim, 0, tab, pr, pm, d, skip[0], tail[0], dims=dims)
        z2r, z2m = _hyena_conv(z1r, z1m, None, conv_w, conv_b, kre, kim, 1, tab, pr, pm, 2 * d, skip[1], tail[1],
                               dims=dims)
        outs.append((z2r, z2m))
    return outs


def _attention_weights(w_qkv, n_heads):
    groups = n_heads // GROUP
    qd = n_heads * HEAD_DIM
    kd = groups * HEAD_DIM
    w_q = w_qkv[:, :qd] * (HEAD_DIM ** -0.5)
    w_k = w_qkv[:, qd:qd + kd]
    w_v = w_qkv[:, qd + kd:]
    d = w_qkv.shape[0]
    w_vk = jnp.stack([w_v.reshape(d, groups, HEAD_DIM), w_k.reshape(d, groups, HEAD_DIM)], axis=2)
    w_rows = jnp.concatenate([w_q, w_vk.reshape(d, 2 * kd)], axis=1).astype(BF16)
    return w_rows, w_k.T.astype(BF16)


def _meta_keys_transposed(qvk_m, b, n_heads):
    groups = n_heads // GROUP
    qd = n_heads * HEAD_DIM
    k_m = qvk_m[:, qd:].reshape(b, N_META, groups, 2, HEAD_DIM)[:, :, :, 1, :]
    kt = jnp.transpose(k_m.reshape(b, N_META, groups * HEAD_DIM), (0, 2, 1))
    return jnp.pad(kt, ((0, 0), (0, 0), (0, LANE_TILE - N_META)))


def _encoder_pair(x_prompt, x_sample, meta_tokens, norm_mix, norm_mlp, norm_final,
                  hy_w_in, hy_conv_w, hy_conv_b, fps, hy_skip, hy_w_out, hy_b_out,
                  at_w_qkv, at_sink, at_w_o, mlp_w1, mlp_w2, *, n_heads, tm):
    d = x_prompt.shape[-1]
    streams = []
    for x in (x_prompt, x_sample):
        b, s, _ = x.shape
        streams.append(dict(b=b, s=s, hr=x.reshape(b * s, d),
                            hm=jnp.tile(meta_tokens.astype(F32), (b, 1))))
    zeros_d = jnp.zeros((d,), F32)

    conv_w = jnp.pad(hy_conv_w[0], ((0, 5), (0, 0)))
    conv_b = hy_conv_b[0][None, :]
    zs = _hyena_layer(streams, fps[0], norm_mix[0], hy_w_in[0].astype(BF16), conv_w, conv_b, hy_skip[0], tm)
    w_out = hy_w_out[0].astype(BF16)
    w1 = [w.astype(BF16) for w in mlp_w1]
    w2 = [w.astype(BF16) for w in mlp_w2]
    for st, (zr, zm) in zip(streams, zs):
        st["hr"] = _mixer_out_mlp(st["hr"], zr, w_out, hy_b_out[0], norm_mlp[0], w1[0], w2[0], zeros_d, tm, False)
        st["hm"] = _mixer_out_mlp(st["hm"], zm, w_out, hy_b_out[0], norm_mlp[0], w1[0], w2[0], zeros_d,
                                  st["hm"].shape[0], False)

    w_rows, w_kt = _attention_weights(at_w_qkv[0], n_heads)
    w_o = at_w_o[0].astype(BF16)
    outs = []
    for st in streams:
        qvk_r, kt_r = _norm_matmul(st["hr"], norm_mix[1], w_rows, tm, wt=w_kt)
        qvk_m = _norm_matmul(st["hm"], norm_mix[1], w_rows, st["hm"].shape[0])
        kt_m = _meta_keys_transposed(qvk_m, st["b"], n_heads)
        att = _attention(qvk_r, kt_r, qvk_m, kt_m, at_sink[0], b=st["b"], s=st["s"], n_heads=n_heads)
        y = _mixer_out_mlp(st["hr"], att, w_o, zeros_d, norm_mlp[1], w1[1], w2[1], norm_final, tm, True)
        outs.append(y.reshape(st["b"], st["s"], d))
    return tuple(outs)


def kernel(x_prompt, x_sample, meta_tokens, norm_mix, norm_mlp, norm_final, hy_w_in, hy_conv_w, hy_conv_b,
           hy_f_w1, hy_f_b1, hy_f_w2, hy_f_b2, hy_f_w3, hy_f_b3, hy_f_wout, hy_f_freq, hy_skip, hy_w_out,
           hy_b_out, at_w_qkv, at_sink, at_w_o, mlp_w1, mlp_w2):
    fps = [dict(w1=hy_f_w1[j], b1=hy_f_b1[j], w2=hy_f_w2[j], b2=hy_f_b2[j], w3=hy_f_w3[j], b3=hy_f_b3[j],
                wout=hy_f_wout[j], freq=hy_f_freq[j]) for j in range(hy_f_w1.shape[0])]
    n_heads = at_sink.shape[1]
    return _encoder_pair(x_prompt, x_sample, meta_tokens, norm_mix, norm_mlp, norm_final,
                         hy_w_in, hy_conv_w, hy_conv_b, fps, hy_skip, hy_w_out, hy_b_out,
                         at_w_qkv, at_sink, at_w_o, mlp_w1, mlp_w2, n_heads=n_heads, tm=512)
```

```python
import functools
import math

import jax
import jax.numpy as jnp
from jax import lax
from jax.experimental import pallas as pl
from jax.experimental.pallas import tpu as pltpu

F32 = jnp.float32
BF16 = jnp.bfloat16

N_META = 16
RMS_EPS = 1e-6
HY_BANDS = 16
HY_EMB_PAD = 40
HY_FAST_DECAY = 0.3
HY_SLOW_DECAY = 1.5
HY_DECAY_TARGET = 1e-2
ATT_BLOCK = 128
HEAD_DIM = 64
GROUP = 4
MASK_VALUE = -1e30
FF_CHUNK = 1024
LANE_TILE = 128
N2_CHUNK = 32
FUSED_SLABS = 16
FFT_MID_LANES = 512
VMEM_LIMIT = 56 * 1024 * 1024
HIGHEST = lax.Precision.HIGHEST


def _cparams(sem):
    return pltpu.CompilerParams(dimension_semantics=sem, vmem_limit_bytes=VMEM_LIMIT)


def _rms(x, g):
    return x * lax.rsqrt(jnp.mean(x * x, axis=-1, keepdims=True) + RMS_EPS) * g


def _norm_matmul_kernel(x_ref, g_ref, w_ref, *rest):
    u = _rms(x_ref[...], g_ref[...]).astype(BF16)
    if len(rest) == 1:
        (o_ref,) = rest
    else:
        wt_ref, o_ref, ot_ref = rest
        ot_ref[...] = lax.dot_general(wt_ref[...], u, (((1,), (1,)), ((), ())),
                                      preferred_element_type=F32).astype(ot_ref.dtype)
    o_ref[...] = jnp.dot(u, w_ref[...], preferred_element_type=F32).astype(o_ref.dtype)


def _norm_matmul(x, g, w, tm, wt=None):
    rows, d = x.shape
    n = w.shape[1]
    in_specs = [pl.BlockSpec((tm, d), lambda i: (i, 0)),
                pl.BlockSpec((1, d), lambda i: (0, 0)),
                pl.BlockSpec((d, n), lambda i: (0, 0))]
    out_specs = pl.BlockSpec((tm, n), lambda i: (i, 0))
    out_shape = jax.ShapeDtypeStruct((rows, n), BF16)
    args = [x, g.reshape(1, d), w]
    if wt is not None:
        m = wt.shape[0]
        in_specs.append(pl.BlockSpec((m, d), lambda i: (0, 0)))
        out_specs = [out_specs, pl.BlockSpec((m, tm), lambda i: (0, i))]
        out_shape = [out_shape, jax.ShapeDtypeStruct((m, rows), BF16)]
        args.append(wt)
    return pl.pallas_call(
        _norm_matmul_kernel,
        grid=(rows // tm,),
        in_specs=in_specs, out_specs=out_specs, out_shape=out_shape,
        compiler_params=_cparams(("parallel",)),
        name="norm_matmul",
    )(*args)


def _mixer_out_mlp_kernel(h_ref, z_ref, wp_ref, bp_ref, g_ref, w1_ref, w2_ref, gf_ref, o_ref, *, final_norm):
    h = h_ref[...] + jnp.dot(z_ref[...], wp_ref[...], preferred_element_type=F32) + bp_ref[...]
    u = _rms(h, g_ref[...]).astype(BF16)
    acc = h
    d_ff = w1_ref.shape[1]
    for c in range(d_ff // FF_CHUNK):
        a = jnp.dot(u, w1_ref[:, c * FF_CHUNK:(c + 1) * FF_CHUNK], preferred_element_type=F32)
        a = jnp.square(jnp.maximum(a, 0.0)).astype(BF16)
        acc = acc + jnp.dot(a, w2_ref[c * FF_CHUNK:(c + 1) * FF_CHUNK, :], preferred_element_type=F32)
    if final_norm:
        acc = _rms(acc, gf_ref[...])
    o_ref[...] = acc


def _mixer_out_mlp(h, z, wp, bp, g, w1, w2, gf, tm, final_norm):
    rows, d = h.shape
    dz = z.shape[1]
    d_ff = w1.shape[1]
    const = lambda i: (0, 0)
    return pl.pallas_call(
        functools.partial(_mixer_out_mlp_kernel, final_norm=final_norm),
        grid=(rows // tm,),
        in_specs=[pl.BlockSpec((tm, d), lambda i: (i, 0)),
                  pl.BlockSpec((tm, dz), lambda i: (i, 0)),
                  pl.BlockSpec((dz, d), const),
                  pl.BlockSpec((1, d), const),
                  pl.BlockSpec((1, d), const),
                  pl.BlockSpec((d, d_ff), const),
                  pl.BlockSpec((d_ff, d), const),
                  pl.BlockSpec((1, d), const)],
        out_specs=pl.BlockSpec((tm, d), lambda i: (i, 0)),
        out_shape=jax.ShapeDtypeStruct((rows, d), F32),
        compiler_params=_cparams(("parallel",)),
        name="mixer_out_mlp",
    )(h, z, wp, bp.reshape(1, d), g.reshape(1, d), w1, w2, gf.reshape(1, d))


def _split_bf16(x):
    hi = x.astype(BF16)
    return hi, (x - hi.astype(F32)).astype(BF16)


def _dot3(a_hi, a_lo, b_hi, b_lo):
    dot = functools.partial(jnp.dot, preferred_element_type=F32)
    return dot(a_hi, b_hi) + (dot(a_hi, b_lo) + dot(a_lo, b_hi))


def _filter_kernel(z_ref, t_ref, w1_ref, b1_ref, w2_ref, b2_ref, w3_ref, b3_ref, fr_ref, woh_ref, wol_ref,
                   ad_ref, o_ref):
    d = ad_ref.shape[1]
    dot = functools.partial(jnp.dot, precision=HIGHEST, preferred_element_type=F32)
    fr = fr_ref[...]
    h = jnp.sin(fr * (dot(z_ref[...], w1_ref[...]) + b1_ref[...]))
    h = jnp.sin(fr * (dot(h, w2_ref[...]) + b2_ref[...]))
    h = jnp.sin(fr * (dot(h, w3_ref[...]) + b3_ref[...]))
    ho = _dot3(*_split_bf16(h), woh_ref[...], wol_ref[...])
    decay = jnp.exp(-t_ref[...] * ad_ref[...])
    for o in range(2):
        o_ref[o] = ho[:, o * d:(o + 1) * d] * decay


def _filter_rows(lag, first_bwd_tile, tr, seq_len, fp, d):
    rows = lag.shape[0]
    lagf = lag.astype(F32)
    t = lagf / (seq_len - 1)
    w = 2.0 * math.pi * lagf / seq_len
    f = jnp.linspace(1e-4, HY_BANDS - 1, HY_BANDS, dtype=F32)[None, :]
    z = jnp.concatenate([t[:, None], jnp.cos(f * w[:, None]), -jnp.sin(f * w[:, None]),
                         jnp.zeros((rows, HY_EMB_PAD - 2 * HY_BANDS - 1), F32)], axis=-1)
    w1 = jnp.pad(fp["w1"], ((0, HY_EMB_PAD - fp["w1"].shape[0]), (0, 0)))
    hid = w1.shape[1]
    max_decay = math.log(HY_DECAY_TARGET) / HY_FAST_DECAY
    min_decay = math.log(HY_DECAY_TARGET) / HY_SLOW_DECAY
    adel = jnp.abs(jnp.linspace(min_decay, max_decay, d, dtype=F32))[None, :]
    const = lambda i: (0, 0)
    row = lambda i: (i, 0)
    wo = jnp.transpose(fp["wout"].reshape(hid, 2, 2, d), (2, 0, 1, 3)).reshape(2, hid, 2 * d)
    wo_hi, wo_lo = _split_bf16(wo)
    wo_spec = pl.BlockSpec((None, hid, 2 * d), lambda i: ((i >= first_bwd_tile).astype(jnp.int32), 0, 0))
    return pl.pallas_call(
        _filter_kernel,
        grid=(rows // tr,),
        in_specs=[pl.BlockSpec((tr, HY_EMB_PAD), row), pl.BlockSpec((tr, 1), row),
                  pl.BlockSpec((HY_EMB_PAD, hid), const), pl.BlockSpec((1, hid), const),
                  pl.BlockSpec((hid, hid), const), pl.BlockSpec((1, hid), const),
                  pl.BlockSpec((hid, hid), const), pl.BlockSpec((1, hid), const),
                  pl.BlockSpec((1, hid), const), wo_spec, wo_spec,
                  pl.BlockSpec((1, d), const)],
        out_specs=pl.BlockSpec((2, tr, d), lambda i: (0, i, 0)),
        out_shape=jax.ShapeDtypeStruct((2, rows, d), F32),
        compiler_params=_cparams(("parallel",)),
        name="hyena_filter",
    )(z, t[:, None], w1, fp["b1"][None], fp["w2"], fp["b2"][None],
      fp["w3"], fp["b3"][None], fp["freq"][None], wo_hi, wo_lo, adel)


def _cplx_block(re, im):
    return jnp.concatenate([jnp.concatenate([re, -im], axis=-1), jnp.concatenate([im, re], axis=-1)], axis=-2)


def _fft_tables(n1, n2):
    n = n1 * n2
    n1h = n1 // 2
    pad = 8 - 1
    k1 = jnp.arange(n1, dtype=jnp.int32)
    c2 = jnp.arange(n2, dtype=jnp.int32)
    cols = jnp.concatenate([jnp.arange(n1h, dtype=jnp.int32), jnp.array([n1 - 1], jnp.int32)])
    pos = n2 * cols[None, None, :] + c2[:, None, None]
    ang = ((k1[None, :, None] * pos) % n).astype(F32) * (-2.0 * math.pi / n)
    wr = jnp.pad(jnp.cos(ang), ((0, 0), (0, 0), (0, pad)))
    wi = jnp.pad(jnp.sin(ang), ((0, 0), (0, 0), (0, pad)))
    fwd1 = _cplx_block(wr, wi)
    inv1 = _cplx_block(jnp.swapaxes(wr, 1, 2), -jnp.swapaxes(wi, 1, 2)) / n
    posf = n2 * k1[None, None, :] + c2[:, None, None]
    angf = ((k1[None, :, None] * posf) % n).astype(F32) * (-2.0 * math.pi / n)
    fil1 = jnp.concatenate([jnp.cos(angf), jnp.sin(angf)], axis=1)
    ang2 = ((c2[:, None] * c2[None, :]) % n2).astype(F32) * (-2.0 * math.pi / n2)
    fr, fi = jnp.cos(ang2), jnp.sin(ang2)
    fwd2 = _cplx_block(fr, fi)
    return dict(fwd1=fwd1.astype(BF16), inv1=inv1.astype(BF16), fil1=fil1.astype(BF16),
                fwd2=fwd2.astype(BF16), inv2=_cplx_block(fr, -fi).astype(BF16))


def _ordered_short_conv(meta, real, cw_ref, cb_ref, seq_ref):
    s = real.shape[0]
    dt = real.shape[1]
    seq_ref[0:8, :] = jnp.zeros((8, dt), F32)
    seq_ref[8:8 + N_META, :] = meta.astype(F32)
    seq_ref[8 + N_META:8 + N_META + s, :] = real.astype(F32)
    seq_ref[8 + N_META + s:16 + N_META + s, :] = jnp.zeros((8, dt), F32)
    x = seq_ref[...]
    rows = x.shape[0]
    cw = cw_ref[...].astype(F32)
    y = (pltpu.roll(x, 1, axis=0) * cw[0:1] + x * cw[1:2] + pltpu.roll(x, rows - 1, axis=0) * cw[2:3]
         + cb_ref[...].astype(F32))
    return y[8:8 + N_META], y[8 + N_META:8 + N_META + s]


def _fft_in_kernel(*refs, n1, n2, short_conv):
    if short_conv:
        (xr_ref, xm_ref, cw_ref, cb_ref, f_ref, are_ref, aim_ref, vr_ref, vm_ref,
         xs_ref, mp_ref, seq_ref) = refs
    else:
        xr_ref, xm_ref, f_ref, are_ref, aim_ref, xs_ref, mp_ref = refs
    n1h = n1 // 2
    chunk = pl.program_id(2)

    @pl.when(chunk == 0)
    def _prepare():
        for e in range(2):
            if short_conv:
                meta, real = _ordered_short_conv(xm_ref[e], xr_ref[e], cw_ref, cb_ref, seq_ref)
                vr_ref[e] = real.astype(vr_ref.dtype)
                vm_ref[e] = meta.astype(vm_ref.dtype)
            else:
                meta, real = xm_ref[e].astype(F32), xr_ref[e].astype(F32)
            xs_ref[e] = real
            mp_ref[e] = jnp.zeros(mp_ref.shape[1:], F32)
            for j in range(N_META):
                mp_ref[e, pl.ds(8 * (n2 - N_META + j), 1), :] = meta[j:j + 1]

    for i in range(N2_CHUNK):
        c2 = chunk * N2_CHUNK + i
        parts = []
        for e in range(2):
            parts.append(xs_ref[e, pl.ds(c2, n1h, stride=n2), :])
            parts.append(mp_ref[e, pl.ds(pl.multiple_of(c2 * 8, 8), 8), :])
        rhs = jnp.concatenate(parts, axis=0).astype(BF16)
        out = jnp.dot(f_ref[i], rhs, preferred_element_type=F32)
        are_ref[i] = out[:n1]
        aim_ref[i] = out[n1:]


def _fft_in(xr, xm, col_off, conv_w, conv_b, tab, *, b, s, d, n1, n2, dt):
    short_conv = conv_w is not None
    pairs = b // 2
    c = xr.shape[1]
    cb0 = col_off // dt
    xr4 = xr.reshape(pairs, 2, s, c)
    xm4 = xm.reshape(pairs, 2, N_META, c)
    kk = tab["fwd1"].shape[2]
    in_specs = [pl.BlockSpec((None, 2, s, dt), lambda p, j, t: (p, 0, 0, cb0 + j)),
                pl.BlockSpec((None, 2, N_META, dt), lambda p, j, t: (p, 0, 0, cb0 + j))]
    args = [xr4, xm4]
    if short_conv:
        in_specs += [pl.BlockSpec((8, dt), lambda p, j, t: (0, cb0 + j)),
                     pl.BlockSpec((1, dt), lambda p, j, t: (0, cb0 + j))]
        args += [conv_w, conv_b]
    in_specs.append(pl.BlockSpec((N2_CHUNK, 2 * n1, kk), lambda p, j, t: (t, 0, 0)))
    args.append(tab["fwd1"])
    a_spec = pl.BlockSpec((None, N2_CHUNK, n1, dt), lambda p, j, t: (p, t, 0, j))
    a_shape = jax.ShapeDtypeStruct((pairs, n2, n1, d), F32)
    out_specs = [a_spec, a_spec]
    out_shape = [a_shape, a_shape]
    scratch = [pltpu.VMEM((2, s, dt), F32), pltpu.VMEM((2, 8 * n2, dt), F32)]
    if short_conv:
        out_specs += [pl.BlockSpec((None, 2, s, dt), lambda p, j, t: (p, 0, 0, j)),
                      pl.BlockSpec((None, 2, N_META, dt), lambda p, j, t: (p, 0, 0, j))]
        out_shape += [jax.ShapeDtypeStruct((pairs, 2, s, d), BF16),
                      jax.ShapeDtypeStruct((pairs, 2, N_META, d), BF16)]
        scratch.append(pltpu.VMEM((s + N_META + 16, dt), F32))
    return pl.pallas_call(
        functools.partial(_fft_in_kernel, n1=n1, n2=n2, short_conv=short_conv),
        grid=(pairs, d // dt, n2 // N2_CHUNK),
        in_specs=in_specs, out_specs=out_specs, out_shape=out_shape, scratch_shapes=scratch,
        compiler_params=_cparams(("parallel", "parallel", "arbitrary")),
        name="fft_in",
    )(*args)


def _fft_mid_kernel(are_ref, aim_ref, kre_ref, kim_ref, f_ref, g_ref, bre_ref, bim_ref, *, n2, slabs):
    for i in range(slabs):
        x = jnp.concatenate([are_ref[:, i, :], aim_ref[:, i, :]], axis=0).astype(BF16)
        z = jnp.dot(f_ref[...], x, preferred_element_type=F32)
        zr, zi = z[:n2], z[n2:]
        kr, ki = kre_ref[i], kim_ref[i]
        p = jnp.concatenate([zr * kr - zi * ki, zr * ki + zi * kr], axis=0).astype(BF16)
        y = jnp.dot(g_ref[...], p, preferred_element_type=F32)
        bre_ref[:, i, :] = y[:n2]
        bim_ref[:, i, :] = y[n2:]


def _fft_mid(are, aim, kre, kim, order, tab, *, dt, slabs):
    pairs, n2, n1, d = are.shape
    k_blk0 = order * (n1 // slabs)
    blk = (None, n2, slabs, dt)
    amap = lambda j, kb, p: (p, 0, kb, j)
    kmap = lambda j, kb, p: (k_blk0 + kb, 0, j)
    const = lambda j, kb, p: (0, 0)
    shp = jax.ShapeDtypeStruct((pairs, n2, n1, d), F32)
    return pl.pallas_call(
        functools.partial(_fft_mid_kernel, n2=n2, slabs=slabs),
        grid=(d // dt, n1 // slabs, pairs),
        in_specs=[pl.BlockSpec(blk, amap), pl.BlockSpec(blk, amap),
                  pl.BlockSpec((slabs, n2, dt), kmap), pl.BlockSpec((slabs, n2, dt), kmap),
                  pl.BlockSpec((2 * n2, 2 * n2), const), pl.BlockSpec((2 * n2, 2 * n2), const)],
        out_specs=[pl.BlockSpec(blk, amap), pl.BlockSpec(blk, amap)],
        out_shape=[shp, shp],
        compiler_params=_cparams(("parallel", "parallel", "arbitrary")),
        name="fft_mid",
    )(are, aim, kre, kim, tab["fwd2"], tab["inv2"])


def _filter_fft_in_kernel(c_ref, f_ref, are_ref, aim_ref, *, n1, n2):
    chunk = pl.program_id(2)
    for i in range(N2_CHUNK):
        rhs = c_ref[pl.ds(chunk * N2_CHUNK + i, n1, stride=n2), :]
        out = jnp.dot(f_ref[i], rhs.astype(BF16), preferred_element_type=F32)
        are_ref[i] = out[:n1]
        aim_ref[i] = out[n1:]


def _filter_fft_mid_kernel(are_ref, aim_ref, f_ref, kre_ref, kim_ref, *, n2, slabs):
    for i in range(slabs):
        x = jnp.concatenate([are_ref[:, i, :], aim_ref[:, i, :]], axis=0)
        z = jnp.dot(f_ref[...], x.astype(BF16), preferred_element_type=F32)
        kre_ref[i] = z[:n2]
        kim_ref[i] = z[n2:]


def _filter_spectrum(circ, tab, *, n1, n2, dt, slabs):
    orders, n, d = circ.shape
    shp = [jax.ShapeDtypeStruct((orders, n2, n1, d), F32)] * 2
    tab_spec = pl.BlockSpec((N2_CHUNK, 2 * n1, n1), lambda o, j, t: (t, 0, 0))
    lt = LANE_TILE
    are, aim = pl.pallas_call(
        functools.partial(_filter_fft_in_kernel, n1=n1, n2=n2),
        grid=(orders, d // lt, n2 // N2_CHUNK),
        in_specs=[pl.BlockSpec((None, n, lt), lambda o, j, t: (o, 0, j)), tab_spec],
        out_specs=[pl.BlockSpec((None, N2_CHUNK, n1, lt), lambda o, j, t: (o, t, 0, j))] * 2,
        out_shape=shp,
        compiler_params=_cparams(("parallel", "parallel", "arbitrary")),
        name="filter_fft_in",
    )(circ, tab["fil1"])
    blk = (None, n2, slabs, dt)
    amap = lambda o, kb, j: (o, 0, kb, j)
    f_spec = pl.BlockSpec((2 * n2, 2 * n2), lambda o, kb, j: (0, 0))
    return pl.pallas_call(
        functools.partial(_filter_fft_mid_kernel, n2=n2, slabs=slabs),
        grid=(orders, n1 // slabs, d // dt),
        in_specs=[pl.BlockSpec(blk, amap), pl.BlockSpec(blk, amap), f_spec],
        out_specs=[pl.BlockSpec((None, slabs, n2, dt), lambda o, kb, j: (o, kb, 0, j))] * 2,
        out_shape=[jax.ShapeDtypeStruct((orders, n1, n2, d), F32)] * 2,
        compiler_params=_cparams(("parallel", "parallel", "parallel")),
        name="filter_fft_mid",
    )(are, aim, tab["fwd2"])


def _alias_patch(tail, u_meta, u_last):
    dfw = tail[0:16] - tail[16:32]
    dbw = tail[32:48] - tail[48:64]
    ridx = lax.broadcasted_iota(jnp.int32, dfw.shape, 0)
    real_fix = jnp.zeros_like(dfw)
    meta_fix = jnp.zeros_like(dfw)
    for o in range(16):
        src = dfw if o == 0 else pltpu.roll(dfw, o, axis=0)
        real_fix = real_fix + jnp.where(ridx >= o, src, 0.0) * u_meta[o:o + 1]
    for c in range(1, 16):
        meta_fix = meta_fix + jnp.where(ridx + c <= 15, pltpu.roll(u_last, 16 - c, axis=0), 0.0) * dbw[c:c + 1]
    return real_fix, meta_fix


def _fft_out_kernel(bre_ref, bim_ref, g_ref, gr_ref, gm_ref, cw_ref, cb_ref, ur_ref, um_ref, skip_ref,
                    tail_ref, zr_ref, zm_ref, ys_ref, yp_ref, seq_ref, *, n1, n2):
    n1h = n1 // 2
    kk = n1h + 8
    s = ys_ref.shape[1]
    chunk = pl.program_id(2)

    for i in range(N2_CHUNK):
        c2 = chunk * N2_CHUNK + i
        rhs = jnp.concatenate([bre_ref[i], bim_ref[i]], axis=0).astype(BF16)
        y = jnp.dot(g_ref[i], rhs, preferred_element_type=F32)
        for e in range(2):
            ys_ref[e, pl.ds(c2, n1h, stride=n2), :] = y[e * kk:e * kk + n1h]
            yp_ref[e, pl.ds(pl.multiple_of(c2 * 8, 8), 8), :] = y[e * kk + n1h:(e + 1) * kk]

    @pl.when(chunk == n2 // N2_CHUNK - 1)
    def _gate():
        tail = tail_ref[...]
        skip = skip_ref[...]
        for e in range(2):
            y_meta = jnp.concatenate(
                [yp_ref[e, pl.ds(8 * (n2 - N_META + j), 1), :] for j in range(N_META)], axis=0)
            u_real = ur_ref[e].astype(F32)
            u_meta = um_ref[e].astype(F32)
            real_fix, meta_fix = _alias_patch(tail, u_meta, u_real[s - 16:s])
            ys_ref[e, s - 16:s, :] = ys_ref[e, s - 16:s, :] + real_fix
            g_meta, g_real = _ordered_short_conv(gm_ref[e], gr_ref[e], cw_ref, cb_ref, seq_ref)
            zr_ref[e] = (g_real * (ys_ref[e] + skip * u_real)).astype(zr_ref.dtype)
            zm_ref[e] = (g_meta * (y_meta + meta_fix + skip * u_meta)).astype(zm_ref.dtype)


def _fft_out(bre, bim, tab, gr, gm, gate_off, conv_w, conv_b, ur, um, skip, tail, *, b, s, d, n1, n2, dt):
    pairs = b // 2
    c = gr.shape[1]
    gb0 = gate_off // dt
    kk2 = tab["inv1"].shape[1]
    gr4 = gr.reshape(pairs, 2, s, c)
    gm4 = gm.reshape(pairs, 2, N_META, c)
    ur4 = ur.reshape(pairs, 2, s, d)
    um4 = um.reshape(pairs, 2, N_META, d)
    b_spec = pl.BlockSpec((None, N2_CHUNK, n1, dt), lambda p, j, t: (p, t, 0, j))
    zr, zm = pl.pallas_call(
        functools.partial(_fft_out_kernel, n1=n1, n2=n2),
        grid=(pairs, d // dt, n2 // N2_CHUNK),
        in_specs=[b_spec, b_spec,
                  pl.BlockSpec((N2_CHUNK, kk2, 2 * n1), lambda p, j, t: (t, 0, 0)),
                  pl.BlockSpec((None, 2, s, dt), lambda p, j, t: (p, 0, 0, gb0 + j)),
                  pl.BlockSpec((None, 2, N_META, dt), lambda p, j, t: (p, 0, 0, gb0 + j)),
                  pl.BlockSpec((8, dt), lambda p, j, t: (0, gb0 + j)),
                  pl.BlockSpec((1, dt), lambda p, j, t: (0, gb0 + j)),
                  pl.BlockSpec((None, 2, s, dt), lambda p, j, t: (p, 0, 0, j)),
                  pl.BlockSpec((None, 2, N_META, dt), lambda p, j, t: (p, 0, 0, j)),
                  pl.BlockSpec((1, dt), lambda p, j, t: (0, j)),
                  pl.BlockSpec((64, dt), lambda p, j, t: (0, j))],
        out_specs=[pl.BlockSpec((None, 2, s, dt), lambda p, j, t: (p, 0, 0, j)),
                   pl.BlockSpec((None, 2, N_META, dt), lambda p, j, t: (p, 0, 0, j))],
        out_shape=[jax.ShapeDtypeStruct((pairs, 2, s, d), BF16),
                   jax.ShapeDtypeStruct((pairs, 2, N_META, d), BF16)],
        scratch_shapes=[pltpu.VMEM((2, s, dt), F32), pltpu.VMEM((2, 8 * n2, dt), F32),
                        pltpu.VMEM((s + N_META + 16, dt), F32)],
        compiler_params=_cparams(("parallel", "parallel", "arbitrary")),
        name="fft_out",
    )(bre, bim, tab["inv1"], gr4, gm4, conv_w, conv_b, ur4, um4, skip.reshape(1, d), tail)
    return zr.reshape(b * s, d), zm.reshape(b * N_META, d)


def _fused_conv_kernel(*refs, n1, n2, short_conv):
    if short_conv:
        (xr_ref, xm_ref, cwx_ref, cbx_ref, f1_ref, kre_ref, kim_ref, f2_ref, g2_ref, g1_ref, gr_ref, gm_ref,
         cwg_ref, cbg_ref, skip_ref, tail_ref, zr_ref, zm_ref, ur_ref, um_ref,
         are_ref, aim_ref, xs_ref, mp_ref, seq_ref) = refs
    else:
        (ur_ref, um_ref, f1_ref, kre_ref, kim_ref, f2_ref, g2_ref, g1_ref, gr_ref, gm_ref,
         cwg_ref, cbg_ref, skip_ref, tail_ref, zr_ref, zm_ref,
         are_ref, aim_ref, xs_ref, mp_ref, seq_ref) = refs
    n1h = n1 // 2
    kk = n1h + 8
    s = n1h * n2
    nkb = n1 // FUSED_SLABS
    dt = xs_ref.shape[3]

    for e in range(2):
        if short_conv:
            meta, real = _ordered_short_conv(xm_ref[e], xr_ref[e], cwx_ref, cbx_ref, seq_ref)
            ur_ref[e] = real.astype(ur_ref.dtype)
            um_ref[e] = meta.astype(um_ref.dtype)
        else:
            meta, real = um_ref[e].astype(F32), ur_ref[e].astype(F32)
        xs_ref[e] = real.reshape(n1h, n2, dt)
        mp_ref[e] = jnp.zeros(mp_ref.shape[1:], F32)
        for j in range(N_META):
            mp_ref[e, pl.ds(8 * (n2 - N_META + j), 1), :] = meta[j:j + 1]

    def stage1(t, carry):
        base = pl.multiple_of(t * N2_CHUNK, N2_CHUNK)
        xt = [jnp.swapaxes(xs_ref[e, :, pl.ds(base, N2_CHUNK), :], 0, 1) for e in range(2)]
        for i in range(N2_CHUNK):
            parts = []
            for e in range(2):
                parts += [xt[e][i], mp_ref[e, pl.ds(pl.multiple_of((base + i) * 8, 8), 8), :]]
            rhs = jnp.concatenate(parts, axis=0).astype(BF16)
            out = jnp.dot(f1_ref[base + i], rhs, preferred_element_type=F32)
            are_ref[:, base + i] = out[:n1].reshape(nkb, FUSED_SLABS, dt)
            aim_ref[:, base + i] = out[n1:].reshape(nkb, FUSED_SLABS, dt)
        return carry

    def stage2(kb, carry):
        k0 = pl.multiple_of(kb * FUSED_SLABS, FUSED_SLABS)
        xr_blk = jnp.swapaxes(are_ref[kb], 0, 1)
        xi_blk = jnp.swapaxes(aim_ref[kb], 0, 1)
        ys = []
        for i in range(0, FUSED_SLABS, 2):
            x = jnp.concatenate([jnp.concatenate([xr_blk[i + h], xi_blk[i + h]], axis=0) for h in range(2)],
                                axis=1).astype(BF16)
            z = jnp.dot(f2_ref[...], x, preferred_element_type=F32)
            zr, zi = z[:n2], z[n2:]
            kr = jnp.concatenate([kre_ref[k0 + i], kre_ref[k0 + i + 1]], axis=1)
            ki = jnp.concatenate([kim_ref[k0 + i], kim_ref[k0 + i + 1]], axis=1)
            p = jnp.concatenate([zr * kr - zi * ki, zr * ki + zi * kr], axis=0).astype(BF16)
            y = jnp.dot(g2_ref[...], p, preferred_element_type=F32)
            ys += [y[:, :dt], y[:, dt:]]
        yt = jnp.swapaxes(jnp.stack(ys, axis=0), 0, 1)
        are_ref[kb] = yt[:n2]
        aim_ref[kb] = yt[n2:]
        return carry

    def stage3(t, carry):
        base = pl.multiple_of(t * N2_CHUNK, N2_CHUNK)
        ys = []
        for i in range(N2_CHUNK):
            rhs = jnp.concatenate([are_ref[:, base + i].reshape(n1, dt), aim_ref[:, base + i].reshape(n1, dt)],
                                  axis=0).astype(BF16)
            y = jnp.dot(g1_ref[base + i], rhs, preferred_element_type=F32)
            ys.append(y)
            for e in range(2):
                mp_ref[e, pl.ds(pl.multiple_of((base + i) * 8, 8), 8), :] = y[e * kk + n1h:(e + 1) * kk]
        yt = jnp.swapaxes(jnp.stack(ys, axis=0), 0, 1)
        for e in range(2):
            xs_ref[e, :, pl.ds(base, N2_CHUNK), :] = yt[e * kk:e * kk + n1h]
        return carry

    lax.fori_loop(0, n2 // N2_CHUNK, stage1, 0)
    lax.fori_loop(0, nkb, stage2, 0)
    lax.fori_loop(0, n2 // N2_CHUNK, stage3, 0)

    tail = tail_ref[...]
    skip = skip_ref[...]
    for e in range(2):
        y_meta = jnp.concatenate(
            [mp_ref[e, pl.ds(8 * (n2 - N_META + j), 1), :] for j in range(N_META)], axis=0)
        u_real = ur_ref[e].astype(F32)
        u_meta = um_ref[e].astype(F32)
        real_fix, meta_fix = _alias_patch(tail, u_meta, u_real[s - 16:s])
        xs_ref[e, n1h - 1, n2 - 16:n2, :] = xs_ref[e, n1h - 1, n2 - 16:n2, :] + real_fix
        y_real = xs_ref[e].reshape(s, dt)
        g_meta, g_real = _ordered_short_conv(gm_ref[e], gr_ref[e], cwg_ref, cbg_ref, seq_ref)
        zr_ref[e] = (g_real * (y_real + skip * u_real)).astype(zr_ref.dtype)
        zm_ref[e] = (g_meta * (y_meta + meta_fix + skip * u_meta)).astype(zm_ref.dtype)


def _fused_conv(xr, xm, col_off, conv_w, conv_b, kre, kim, order, tab, gr, gm, gate_off, skip, tail, *,
                b, s, d, n1, n2):
    short_conv = col_off is not None
    pairs = b // 2
    dt = LANE_TILE
    cx0 = (col_off or 0) // dt
    cg0 = gate_off // dt
    kk2 = tab["fwd1"].shape[2]
    once = dict(pipeline_mode=pl.Buffered(1))
    seq4 = lambda a, rows: a.reshape(pairs, 2, rows, a.shape[1])
    x_spec = lambda rows, c0: pl.BlockSpec((None, 2, rows, dt), lambda j, p: (p, 0, 0, c0 + j))
    row_spec = lambda rows, c0: pl.BlockSpec((rows, dt), lambda j, p: (0, c0 + j))
    const2 = pl.BlockSpec((2 * n2, 2 * n2), lambda j, p: (0, 0), **once)
    k_spec = pl.BlockSpec((n1, n2, dt), lambda j, p: (order, 0, j), **once)
    in_specs = [x_spec(s, cx0), x_spec(N_META, cx0)]
    args = [seq4(xr, s), seq4(xm, N_META)]
    if short_conv:
        in_specs += [row_spec(8, cx0), row_spec(1, cx0)]
        args += [conv_w, conv_b]
    in_specs += [pl.BlockSpec((n2, 2 * n1, kk2), lambda j, p: (0, 0, 0), **once),
                 k_spec, k_spec, const2, const2,
                 pl.BlockSpec((n2, kk2, 2 * n1), lambda j, p: (0, 0, 0), **once),
                 x_spec(s, cg0), x_spec(N_META, cg0), row_spec(8, cg0), row_spec(1, cg0),
                 row_spec(1, 0), row_spec(64, 0)]
    args += [tab["fwd1"], kre, kim, tab["fwd2"], tab["inv2"], tab["inv1"],
             seq4(gr, s), seq4(gm, N_META), conv_w, conv_b, skip.reshape(1, d), tail]
    seq_out = lambda rows: pl.BlockSpec((None, 2, rows, dt), lambda j, p: (p, 0, 0, j))
    out_specs = [seq_out(s), seq_out(N_META)]
    out_shape = [jax.ShapeDtypeStruct((pairs, 2, s, d), BF16), jax.ShapeDtypeStruct((pairs, 2, N_META, d), BF16)]
    if short_conv:
        out_specs = out_specs * 2
        out_shape = out_shape * 2
    scratch = ([pltpu.VMEM((n1 // FUSED_SLABS, n2, FUSED_SLABS, dt), F32)] * 2
               + [pltpu.VMEM((2, n1 // 2, n2, dt), F32), pltpu.VMEM((2, 8 * n2, dt), F32),
                  pltpu.VMEM((s + N_META + 16, dt), F32)])
    outs = pl.pallas_call(
        functools.partial(_fused_conv_kernel, n1=n1, n2=n2, short_conv=short_conv),
        grid=(d // dt, pairs),
        in_specs=in_specs, out_specs=out_specs, out_shape=out_shape, scratch_shapes=scratch,
        compiler_params=_cparams(("parallel", "parallel")),
        name="fused_conv",
    )(*args)
    return [o.reshape(-1, d) for o in outs]


def _attn_kernel(own_ref, vprev_ref, vnext_ref, vmeta_ref, ktp_ref, kto_ref, ktn_ref, ktm_ref,
                 bias_ref, shift_ref, o_ref, *, groups):
    blk = pl.program_id(1).astype(F32)
    gw = GROUP * HEAD_DIM
    qd = groups * gw
    lane = lax.broadcasted_iota(jnp.int32, (1, LANE_TILE), 1)
    low = lane < HEAD_DIM
    pad_rows = jnp.zeros((ATT_BLOCK - N_META, 2 * HEAD_DIM), BF16)
    zero = jnp.zeros((ATT_BLOCK, LANE_TILE), BF16)
    for g in range(groups):
        vk_own = own_ref[:, qd + g * LANE_TILE:qd + (g + 1) * LANE_TILE]
        grp = slice(g * LANE_TILE, (g + 1) * LANE_TILE)
        vk = jnp.concatenate([vprev_ref[:, grp], vk_own, vnext_ref[:, grp], vmeta_ref[:, grp], pad_rows],
                             axis=0)
        v_ones = jnp.where(low, vk, jnp.ones_like(vk))
        kt_rows = slice(g * HEAD_DIM, (g + 1) * HEAD_DIM)
        kt = jnp.concatenate([ktp_ref[kt_rows, :], kto_ref[kt_rows, :], ktn_ref[kt_rows, :], ktm_ref[kt_rows, :]],
                             axis=1)
        kt2 = jnp.concatenate([kt, kt], axis=0)
        parts = []
        for pr in range(GROUP // 2):
            qp = own_ref[:, g * gw + pr * LANE_TILE:g * gw + (pr + 1) * LANE_TILE]
            parts += [jnp.where(low, qp, zero), jnp.where(low, zero, qp)]
        q = jnp.concatenate(parts, axis=0)
        sc = jnp.dot(q, kt2, preferred_element_type=F32) + bias_ref[g]
        t = [sc[:, i * LANE_TILE:(i + 1) * LANE_TILE] for i in range(3)]
        t.append(sc[:, 3 * LANE_TILE:] - shift_ref[g] * blk)
        m = jnp.max(jnp.maximum(jnp.maximum(t[0], t[1]), jnp.maximum(t[2], t[3])), axis=1, keepdims=True)
        p = jnp.concatenate([jnp.exp(x - m) for x in t], axis=1).astype(BF16)
        oa = jnp.dot(p, v_ones, preferred_element_type=F32)
        ob = pltpu.roll(oa, HEAD_DIM, axis=1)
        outs = []
        for pr in range(GROUP // 2):
            ev = slice((2 * pr) * ATT_BLOCK, (2 * pr + 1) * ATT_BLOCK)
            od = slice((2 * pr + 1) * ATT_BLOCK, (2 * pr + 2) * ATT_BLOCK)
            outs.append(jnp.where(low, oa[ev] / ob[ev], ob[od] / oa[od]))
        o_ref[:, g * gw:(g + 1) * gw] = jnp.concatenate(outs, axis=1).astype(o_ref.dtype)


def _attention_tables(n_heads, sink):
    groups = n_heads // GROUP
    slopes = jnp.exp2(-8.0 * jnp.arange(1, n_heads + 1, dtype=F32) / n_heads)
    i = jnp.arange(ATT_BLOCK, dtype=jnp.int32)[:, None]
    c = jnp.arange(4 * ATT_BLOCK, dtype=jnp.int32)[None, :]
    dist = jnp.abs(c - ATT_BLOCK - i)
    key_blk = c // ATT_BLOCK
    in_band = jnp.logical_and(c < 3 * ATT_BLOCK, dist <= ATT_BLOCK)
    meta_col = jnp.logical_and(c >= 3 * ATT_BLOCK, c < 3 * ATT_BLOCK + N_META)
    sink_col = c == 3 * ATT_BLOCK + N_META
    meta_dist = N_META + i - (c - 3 * ATT_BLOCK)
    tables = []
    for drop in (None, 0, 2):
        ok = in_band if drop is None else jnp.logical_and(in_band, key_blk != drop)
        d_eff = jnp.where(ok, dist, jnp.where(meta_col, meta_dist, 0)).astype(F32)
        live = jnp.logical_or(ok, meta_col)
        tab = jnp.where(live[None], -slopes[:, None, None] * d_eff[None], MASK_VALUE)
        tables.append(jnp.where(sink_col[None], sink.astype(F32)[:, None, None], tab))
    bias = jnp.stack(tables, axis=0).reshape(3, groups, GROUP * ATT_BLOCK, 4 * ATT_BLOCK)
    lane = jnp.arange(LANE_TILE)[None, :]
    shift = jnp.where(lane < N_META, jnp.repeat(slopes * ATT_BLOCK, ATT_BLOCK)[:, None], 0.0)
    return bias, shift.reshape(groups, GROUP * ATT_BLOCK, LANE_TILE)


def _attention(qvk_r, kt_r, qvk_m, kt_m, sink, *, b, s, n_heads):
    groups = n_heads // GROUP
    nblk = s // ATT_BLOCK
    qd = n_heads * HEAD_DIM
    width = qvk_r.shape[1]
    vkw = groups * 2 * HEAD_DIM
    vkb = qd // vkw
    bias, shift = _attention_tables(n_heads, sink)

    def variant(j):
        return jnp.where(j == 0, 1, jnp.where(j == nblk - 1, 2, 0))

    prev = lambda i, j: i * nblk + jnp.maximum(j - 1, 0)
    nxt = lambda i, j: i * nblk + jnp.minimum(j + 1, nblk - 1)
    return pl.pallas_call(
        functools.partial(_attn_kernel, groups=groups),
        grid=(b, nblk),
        in_specs=[pl.BlockSpec((ATT_BLOCK, width), lambda i, j: (i * nblk + j, 0)),
                  pl.BlockSpec((ATT_BLOCK, vkw), lambda i, j: (prev(i, j), vkb)),
                  pl.BlockSpec((ATT_BLOCK, vkw), lambda i, j: (nxt(i, j), vkb)),
                  pl.BlockSpec((N_META, vkw), lambda i, j: (i, vkb)),
                  pl.BlockSpec((groups * HEAD_DIM, ATT_BLOCK), lambda i, j: (0, prev(i, j))),
                  pl.BlockSpec((groups * HEAD_DIM, ATT_BLOCK), lambda i, j: (0, i * nblk + j)),
                  pl.BlockSpec((groups * HEAD_DIM, ATT_BLOCK), lambda i, j: (0, nxt(i, j))),
                  pl.BlockSpec((None, groups * HEAD_DIM, LANE_TILE), lambda i, j: (i, 0, 0)),
                  pl.BlockSpec((None, groups, GROUP * ATT_BLOCK, 4 * ATT_BLOCK), lambda i, j: (variant(j), 0, 0, 0)),
                  pl.BlockSpec((groups, GROUP * ATT_BLOCK, LANE_TILE), lambda i, j: (0, 0, 0))],
        out_specs=pl.BlockSpec((ATT_BLOCK, qd), lambda i, j: (i * nblk + j, 0)),
        out_shape=jax.ShapeDtypeStruct((b * s, qd), BF16),
        compiler_params=_cparams(("parallel", "arbitrary")),
        name="window_attention",
    )(qvk_r, qvk_r, qvk_r, qvk_m, kt_r, kt_r, kt_r, kt_m, bias, shift)


def _fft_split(s):
    n2 = 128 if s >= 1024 else 32
    return (2 * s) // n2, n2


def _fused_conv_vmem_bytes(s, n1, n2):
    lane_bytes = LANE_TILE * 4
    spectrum = 4 * n1 * n2 * lane_bytes
    tables = 2 * n2 * 2 * n1 * 2 * (n1 // 2 + 8) * 2
    sequence = (2 * s + 2 * 8 * n2 + s + N_META + 16) * lane_bytes
    blocks = 4 * 2 * 2 * (s + N_META) * LANE_TILE * 2
    return spectrum + tables + sequence + blocks


def _hyena_conv(xr, xm, col_off, conv_w, conv_b, kre, kim, order, tab, gr, gm, gate_off, skip, tail, *, dims):
    b, s, d, n1, n2 = dims
    if _fused_conv_vmem_bytes(s, n1, n2) <= (VMEM_LIMIT * 7) // 8:
        outs = _fused_conv(xr, xm, col_off, conv_w, conv_b, kre, kim, order, tab, gr, gm, gate_off, skip, tail,
                           b=b, s=s, d=d, n1=n1, n2=n2)
        return outs[0], outs[1]
    kw = dict(b=b, s=s, d=d, n1=n1, n2=n2, dt=LANE_TILE)
    if col_off is not None:
        are, aim, ur4, um4 = _fft_in(xr, xm, col_off, conv_w, conv_b, tab, **kw)
        ur, um = ur4.reshape(b * s, d), um4.reshape(b * N_META, d)
    else:
        are, aim = _fft_in(xr, xm, 0, None, None, tab, **kw)
        ur, um = xr, xm
    bre, bim = _fft_mid(are, aim, kre, kim, order, tab, dt=min(d, FFT_MID_LANES), slabs=8)
    return _fft_out(bre, bim, tab, gr, gm, gate_off, conv_w, conv_b, ur, um, skip, tail, **kw)


def _hyena_layer(streams, fp, g_mix, w_in, conv_w, conv_b, skip, tm):
    outs = []
    d = w_in.shape[0]
    for st in streams:
        b, s = st["b"], st["s"]
        n1, n2 = _fft_split(s)
        seq_len = s + N_META
        n = 2 * s
        tab = _fft_tables(n1, n2)
        r = jnp.arange(n, dtype=jnp.int32)
        tr = min(s, 512)
        circ = _filter_rows(jnp.where(r < s, r, n - r), s // tr, tr, seq_len, fp, d)
        kre, kim = [k.reshape(2 * n1, n2, d)
                    for k in _filter_spectrum(circ, tab, n1=n1, n2=n2, dt=2 * LANE_TILE, slabs=8)]
        a = jnp.arange(16, dtype=jnp.int32)
        tail = _filter_rows(jnp.concatenate([s + a, s - a, s - a, s + a]), 1, 32, seq_len, fp, d)
        tail = jnp.concatenate([tail[:, 0:16], tail[:, 32:48], tail[:, 48:64], tail[:, 16:32]], axis=1)
        pr = _norm_matmul(st["hr"], g_mix, w_in, tm)
        pm = _norm_matmul(st["hm"], g_mix, w_in, st["hm"].shape[0])
        dims = (b, s, d, n1, n2)
        z1r, z1m = _hyena_conv(pr, pm, 0, conv_w, conv_b, kre, kim, 0, tab, pr, pm, d, skip[0], tail[0], dims=dims)
        z2r, z2m = _hyena_conv(z1r, z1m, None, conv_w, conv_b, kre, kim, 1, tab, pr, pm, 2 * d, skip[1], tail[1],
                               dims=dims)
        outs.append((z2r, z2m))
    return outs


def _attention_weights(w_qkv, n_heads):
    groups = n_heads // GROUP
    qd = n_heads * HEAD_DIM
    kd = groups * HEAD_DIM
    w_q = w_qkv[:, :qd] * (HEAD_DIM ** -0.5)
    w_k = w_qkv[:, qd:qd + kd]
    w_v = w_qkv[:, qd + kd:]
    d = w_qkv.shape[0]
    w_vk = jnp.stack([w_v.reshape(d, groups, HEAD_DIM), w_k.reshape(d, groups, HEAD_DIM)], axis=2)
    w_rows = jnp.concatenate([w_q, w_vk.reshape(d, 2 * kd)], axis=1).astype(BF16)
    return w_rows, w_k.T.astype(BF16)


def _meta_keys_transposed(qvk_m, b, n_heads):
    groups = n_heads // GROUP
    qd = n_heads * HEAD_DIM
    k_m = qvk_m[:, qd:].reshape(b, N_META, groups, 2, HEAD_DIM)[:, :, :, 1, :]
    kt = jnp.transpose(k_m.reshape(b, N_META, groups * HEAD_DIM), (0, 2, 1))
    return jnp.pad(kt, ((0, 0), (0, 0), (0, LANE_TILE - N_META)))


def _encoder_pair(x_prompt, x_sample, meta_tokens, norm_mix, norm_mlp, norm_final,
                  hy_w_in, hy_conv_w, hy_conv_b, fps, hy_skip, hy_w_out, hy_b_out,
                  at_w_qkv, at_sink, at_w_o, mlp_w1, mlp_w2, *, n_heads, tm):
    d = x_prompt.shape[-1]
    streams = []
    for x in (x_prompt, x_sample):
        b, s, _ = x.shape
        streams.append(dict(b=b, s=s, hr=x.reshape(b * s, d),
                            hm=jnp.tile(meta_tokens.astype(F32), (b, 1))))
    zeros_d = jnp.zeros((d,), F32)

    conv_w = jnp.pad(hy_conv_w[0], ((0, 5), (0, 0)))
    conv_b = hy_conv_b[0][None, :]
    zs = _hyena_layer(streams, fps[0], norm_mix[0], hy_w_in[0].astype(BF16), conv_w, conv_b, hy_skip[0], tm)
    w_out = hy_w_out[0].astype(BF16)
    w1 = [w.astype(BF16) for w in mlp_w1]
    w2 = [w.astype(BF16) for w in mlp_w2]
    for st, (zr, zm) in zip(streams, zs):
        st["hr"] = _mixer_out_mlp(st["hr"], zr, w_out, hy_b_out[0], norm_mlp[0], w1[0], w2[0], zeros_d, tm, False)
        st["hm"] = _mixer_out_mlp(st["hm"], zm, w_out, hy_b_out[0], norm_mlp[0], w1[0], w2[0], zeros_d,
                                  st["hm"].shape[0], False)

    w_rows, w_kt = _attention_weights(at_w_qkv[0], n_heads)
    w_o = at_w_o[0].astype(BF16)
    outs = []
    for st in streams:
        qvk_r, kt_r = _norm_matmul(st["hr"], norm_mix[1], w_rows, tm, wt=w_kt)
        qvk_m = _norm_matmul(st["hm"], norm_mix[1], w_rows, st["hm"].shape[0])
        kt_m = _meta_keys_transposed(qvk_m, st["b"], n_heads)
        att = _attention(qvk_r, kt_r, qvk_m, kt_m, at_sink[0], b=st["b"], s=st["s"], n_heads=n_heads)
        y = _mixer_out_mlp(st["hr"], att, w_o, zeros_d, norm_mlp[1], w1[1], w2[1], norm_final, tm, True)
        outs.append(y.reshape(st["b"], st["s"], d))
    return tuple(outs)


def kernel(x_prompt, x_sample, meta_tokens, norm_mix, norm_mlp, norm_final, hy_w_in, hy_conv_w, hy_conv_b,
           hy_f_w1, hy_f_b1, hy_f_w2, hy_f_b2, hy_f_w3, hy_f_b3, hy_f_wout, hy_f_freq, hy_skip, hy_w_out,
           hy_b_out, at_w_qkv, at_sink, at_w_o, mlp_w1, mlp_w2):
    fps = [dict(w1=hy_f_w1[j], b1=hy_f_b1[j], w2=hy_f_w2[j], b2=hy_f_b2[j], w3=hy_f_w3[j], b3=hy_f_b3[j],
                wout=hy_f_wout[j], freq=hy_f_freq[j]) for j in range(hy_f_w1.shape[0])]
    n_heads = at_sink.shape[1]
    return _encoder_pair(x_prompt, x_sample, meta_tokens, norm_mix, norm_mlp, norm_final,
                         hy_w_in, hy_conv_w, hy_conv_b, fps, hy_skip, hy_w_out, hy_b_out,
                         at_w_qkv, at_sink, at_w_o, mlp_w1, mlp_w2, n_heads=n_heads, tm=512)
```

```python
import functools
import math

import jax
import jax.numpy as jnp
from jax import lax
from jax.experimental import pallas as pl
from jax.experimental.pallas import tpu as pltpu

F32 = jnp.float32
BF16 = jnp.bfloat16

N_META = 16
RMS_EPS = 1e-6
HY_BANDS = 16
HY_EMB_PAD = 40
HY_FAST_DECAY = 0.3
HY_SLOW_DECAY = 1.5
HY_DECAY_TARGET = 1e-2
ATT_BLOCK = 128
HEAD_DIM = 64
GROUP = 4
MASK_VALUE = -1e30
FF_CHUNK = 1024
LANE_TILE = 128
N2_CHUNK = 32
FUSED_SLABS = 16
FFT_MID_LANES = 256
VMEM_LIMIT = 56 * 1024 * 1024
HIGHEST = lax.Precision.HIGHEST


def _cparams(sem):
    return pltpu.CompilerParams(dimension_semantics=sem, vmem_limit_bytes=VMEM_LIMIT)


def _rms(x, g):
    return x * lax.rsqrt(jnp.mean(x * x, axis=-1, keepdims=True) + RMS_EPS) * g


def _norm_matmul_kernel(x_ref, g_ref, w_ref, *rest):
    u = _rms(x_ref[...], g_ref[...]).astype(BF16)
    if len(rest) == 1:
        (o_ref,) = rest
    else:
        wt_ref, o_ref, ot_ref = rest
        ot_ref[...] = lax.dot_general(wt_ref[...], u, (((1,), (1,)), ((), ())),
                                      preferred_element_type=F32).astype(ot_ref.dtype)
    o_ref[...] = jnp.dot(u, w_ref[...], preferred_element_type=F32).astype(o_ref.dtype)


def _norm_matmul(x, g, w, tm, wt=None):
    rows, d = x.shape
    n = w.shape[1]
    in_specs = [pl.BlockSpec((tm, d), lambda i: (i, 0)),
                pl.BlockSpec((1, d), lambda i: (0, 0)),
                pl.BlockSpec((d, n), lambda i: (0, 0))]
    out_specs = pl.BlockSpec((tm, n), lambda i: (i, 0))
    out_shape = jax.ShapeDtypeStruct((rows, n), BF16)
    args = [x, g.reshape(1, d), w]
    if wt is not None:
        m = wt.shape[0]
        in_specs.append(pl.BlockSpec((m, d), lambda i: (0, 0)))
        out_specs = [out_specs, pl.BlockSpec((m, tm), lambda i: (0, i))]
        out_shape = [out_shape, jax.ShapeDtypeStruct((m, rows), BF16)]
        args.append(wt)
    return pl.pallas_call(
        _norm_matmul_kernel,
        grid=(rows // tm,),
        in_specs=in_specs, out_specs=out_specs, out_shape=out_shape,
        compiler_params=_cparams(("parallel",)),
        name="norm_matmul",
    )(*args)


def _mixer_out_mlp_kernel(h_ref, z_ref, wp_ref, bp_ref, g_ref, w1_ref, w2_ref, gf_ref, o_ref, *, final_norm):
    h = h_ref[...] + jnp.dot(z_ref[...], wp_ref[...], preferred_element_type=F32) + bp_ref[...]
    u = _rms(h, g_ref[...]).astype(BF16)
    acc = h
    d_ff = w1_ref.shape[1]
    for c in range(d_ff // FF_CHUNK):
        a = jnp.dot(u, w1_ref[:, c * FF_CHUNK:(c + 1) * FF_CHUNK], preferred_element_type=F32)
        a = jnp.square(jnp.maximum(a, 0.0)).astype(BF16)
        acc = acc + jnp.dot(a, w2_ref[c * FF_CHUNK:(c + 1) * FF_CHUNK, :], preferred_element_type=F32)
    if final_norm:
        acc = _rms(acc, gf_ref[...])
    o_ref[...] = acc


def _mixer_out_mlp(h, z, wp, bp, g, w1, w2, gf, tm, final_norm):
    rows, d = h.shape
    dz = z.shape[1]
    d_ff = w1.shape[1]
    const = lambda i: (0, 0)
    return pl.pallas_call(
        functools.partial(_mixer_out_mlp_kernel, final_norm=final_norm),
        grid=(rows // tm,),
        in_specs=[pl.BlockSpec((tm, d), lambda i: (i, 0)),
                  pl.BlockSpec((tm, dz), lambda i: (i, 0)),
                  pl.BlockSpec((dz, d), const),
                  pl.BlockSpec((1, d), const),
                  pl.BlockSpec((1, d), const),
                  pl.BlockSpec((d, d_ff), const),
                  pl.BlockSpec((d_ff, d), const),
                  pl.BlockSpec((1, d), const)],
        out_specs=pl.BlockSpec((tm, d), lambda i: (i, 0)),
        out_shape=jax.ShapeDtypeStruct((rows, d), F32),
        compiler_params=_cparams(("parallel",)),
        name="mixer_out_mlp",
    )(h, z, wp, bp.reshape(1, d), g.reshape(1, d), w1, w2, gf.reshape(1, d))


def _split_bf16(x):
    hi = x.astype(BF16)
    return hi, (x - hi.astype(F32)).astype(BF16)


def _dot3(a_hi, a_lo, b_hi, b_lo):
    dot = functools.partial(jnp.dot, preferred_element_type=F32)
    return dot(a_hi, b_hi) + (dot(a_hi, b_lo) + dot(a_lo, b_hi))


def _filter_kernel(z_ref, t_ref, w1_ref, b1_ref, w2_ref, b2_ref, w3_ref, b3_ref, fr_ref, woh_ref, wol_ref,
                   ad_ref, o_ref):
    d = ad_ref.shape[1]
    dot = functools.partial(jnp.dot, precision=HIGHEST, preferred_element_type=F32)
    fr = fr_ref[...]
    h = jnp.sin(fr * (dot(z_ref[...], w1_ref[...]) + b1_ref[...]))
    h = jnp.sin(fr * (dot(h, w2_ref[...]) + b2_ref[...]))
    h = jnp.sin(fr * (dot(h, w3_ref[...]) + b3_ref[...]))
    ho = _dot3(*_split_bf16(h), woh_ref[...], wol_ref[...])
    decay = jnp.exp(-t_ref[...] * ad_ref[...])
    for o in range(2):
        o_ref[o] = ho[:, o * d:(o + 1) * d] * decay


def _filter_rows(lag, first_bwd_tile, tr, seq_len, fp, d):
    rows = lag.shape[0]
    lagf = lag.astype(F32)
    t = lagf / (seq_len - 1)
    w = 2.0 * math.pi * lagf / seq_len
    f = jnp.linspace(1e-4, HY_BANDS - 1, HY_BANDS, dtype=F32)[None, :]
    z = jnp.concatenate([t[:, None], jnp.cos(f * w[:, None]), -jnp.sin(f * w[:, None]),
                         jnp.zeros((rows, HY_EMB_PAD - 2 * HY_BANDS - 1), F32)], axis=-1)
    w1 = jnp.pad(fp["w1"], ((0, HY_EMB_PAD - fp["w1"].shape[0]), (0, 0)))
    hid = w1.shape[1]
    max_decay = math.log(HY_DECAY_TARGET) / HY_FAST_DECAY
    min_decay = math.log(HY_DECAY_TARGET) / HY_SLOW_DECAY
    adel = jnp.abs(jnp.linspace(min_decay, max_decay, d, dtype=F32))[None, :]
    const = lambda i: (0, 0)
    row = lambda i: (i, 0)
    wo = jnp.transpose(fp["wout"].reshape(hid, 2, 2, d), (2, 0, 1, 3)).reshape(2, hid, 2 * d)
    wo_hi, wo_lo = _split_bf16(wo)
    wo_spec = pl.BlockSpec((None, hid, 2 * d), lambda i: ((i >= first_bwd_tile).astype(jnp.int32), 0, 0))
    return pl.pallas_call(
        _filter_kernel,
        grid=(rows // tr,),
        in_specs=[pl.BlockSpec((tr, HY_EMB_PAD), row), pl.BlockSpec((tr, 1), row),
                  pl.BlockSpec((HY_EMB_PAD, hid), const), pl.BlockSpec((1, hid), const),
                  pl.BlockSpec((hid, hid), const), pl.BlockSpec((1, hid), const),
                  pl.BlockSpec((hid, hid), const), pl.BlockSpec((1, hid), const),
                  pl.BlockSpec((1, hid), const), wo_spec, wo_spec,
                  pl.BlockSpec((1, d), const)],
        out_specs=pl.BlockSpec((2, tr, d), lambda i: (0, i, 0)),
        out_shape=jax.ShapeDtypeStruct((2, rows, d), F32),
        compiler_params=_cparams(("parallel",)),
        name="hyena_filter",
    )(z, t[:, None], w1, fp["b1"][None], fp["w2"], fp["b2"][None],
      fp["w3"], fp["b3"][None], fp["freq"][None], wo_hi, wo_lo, adel)


def _cplx_block(re, im):
    return jnp.concatenate([jnp.concatenate([re, -im], axis=-1), jnp.concatenate([im, re], axis=-1)], axis=-2)


def _fft_tables(n1, n2):
    n = n1 * n2
    n1h = n1 // 2
    pad = 8 - 1
    k1 = jnp.arange(n1, dtype=jnp.int32)
    c2 = jnp.arange(n2, dtype=jnp.int32)
    cols = jnp.concatenate([jnp.arange(n1h, dtype=jnp.int32), jnp.array([n1 - 1], jnp.int32)])
    pos = n2 * cols[None, None, :] + c2[:, None, None]
    ang = ((k1[None, :, None] * pos) % n).astype(F32) * (-2.0 * math.pi / n)
    wr = jnp.pad(jnp.cos(ang), ((0, 0), (0, 0), (0, pad)))
    wi = jnp.pad(jnp.sin(ang), ((0, 0), (0, 0), (0, pad)))
    fwd1 = _cplx_block(wr, wi)
    inv1 = _cplx_block(jnp.swapaxes(wr, 1, 2), -jnp.swapaxes(wi, 1, 2)) / n
    posf = n2 * k1[None, None, :] + c2[:, None, None]
    angf = ((k1[None, :, None] * posf) % n).astype(F32) * (-2.0 * math.pi / n)
    fil1 = jnp.concatenate([jnp.cos(angf), jnp.sin(angf)], axis=1)
    ang2 = ((c2[:, None] * c2[None, :]) % n2).astype(F32) * (-2.0 * math.pi / n2)
    fr, fi = jnp.cos(ang2), jnp.sin(ang2)
    fwd2 = _cplx_block(fr, fi)
    return dict(fwd1=fwd1.astype(BF16), inv1=inv1.astype(BF16), fil1=fil1.astype(BF16),
                fwd2=fwd2.astype(BF16), inv2=_cplx_block(fr, -fi).astype(BF16))


def _ordered_short_conv(meta, real, cw_ref, cb_ref, seq_ref):
    s = real.shape[0]
    dt = real.shape[1]
    seq_ref[0:8, :] = jnp.zeros((8, dt), F32)
    seq_ref[8:8 + N_META, :] = meta.astype(F32)
    seq_ref[8 + N_META:8 + N_META + s, :] = real.astype(F32)
    seq_ref[8 + N_META + s:16 + N_META + s, :] = jnp.zeros((8, dt), F32)
    x = seq_ref[...]
    rows = x.shape[0]
    cw = cw_ref[...].astype(F32)
    y = (pltpu.roll(x, 1, axis=0) * cw[0:1] + x * cw[1:2] + pltpu.roll(x, rows - 1, axis=0) * cw[2:3]
         + cb_ref[...].astype(F32))
    return y[8:8 + N_META], y[8 + N_META:8 + N_META + s]


def _fft_in_kernel(*refs, n1, n2, short_conv):
    if short_conv:
        (xr_ref, xm_ref, cw_ref, cb_ref, f_ref, are_ref, aim_ref, vr_ref, vm_ref,
         xs_ref, mp_ref, seq_ref) = refs
    else:
        xr_ref, xm_ref, f_ref, are_ref, aim_ref, xs_ref, mp_ref = refs
    n1h = n1 // 2
    chunk = pl.program_id(2)

    @pl.when(chunk == 0)
    def _prepare():
        for e in range(2):
            if short_conv:
                meta, real = _ordered_short_conv(xm_ref[e], xr_ref[e], cw_ref, cb_ref, seq_ref)
                vr_ref[e] = real.astype(vr_ref.dtype)
                vm_ref[e] = meta.astype(vm_ref.dtype)
            else:
                meta, real = xm_ref[e].astype(F32), xr_ref[e].astype(F32)
            xs_ref[e] = real.reshape(n1h, n2, real.shape[1])
            mp_ref[e] = jnp.zeros(mp_ref.shape[1:], F32)
            for j in range(N_META):
                mp_ref[e, pl.ds(8 * (n2 - N_META + j), 1), :] = meta[j:j + 1]

    base = pl.multiple_of(chunk * N2_CHUNK, N2_CHUNK)
    xt = [jnp.swapaxes(xs_ref[e, :, pl.ds(base, N2_CHUNK), :], 0, 1) for e in range(2)]
    for i in range(N2_CHUNK):
        parts = []
        for e in range(2):
            parts += [xt[e][i], mp_ref[e, pl.ds(pl.multiple_of((base + i) * 8, 8), 8), :]]
        rhs = jnp.concatenate(parts, axis=0).astype(BF16)
        out = jnp.dot(f_ref[i], rhs, preferred_element_type=F32)
        are_ref[i] = out[:n1]
        aim_ref[i] = out[n1:]


def _fft_in(xr, xm, col_off, conv_w, conv_b, tab, *, b, s, d, n1, n2, dt):
    short_conv = conv_w is not None
    pairs = b // 2
    c = xr.shape[1]
    cb0 = col_off // dt
    xr4 = xr.reshape(pairs, 2, s, c)
    xm4 = xm.reshape(pairs, 2, N_META, c)
    kk = tab["fwd1"].shape[2]
    in_specs = [pl.BlockSpec((None, 2, s, dt), lambda p, j, t: (p, 0, 0, cb0 + j)),
                pl.BlockSpec((None, 2, N_META, dt), lambda p, j, t: (p, 0, 0, cb0 + j))]
    args = [xr4, xm4]
    if short_conv:
        in_specs += [pl.BlockSpec((8, dt), lambda p, j, t: (0, cb0 + j)),
                     pl.BlockSpec((1, dt), lambda p, j, t: (0, cb0 + j))]
        args += [conv_w, conv_b]
    in_specs.append(pl.BlockSpec((N2_CHUNK, 2 * n1, kk), lambda p, j, t: (t, 0, 0)))
    args.append(tab["fwd1"])
    a_spec = pl.BlockSpec((None, N2_CHUNK, n1, dt), lambda p, j, t: (p, t, 0, j))
    a_shape = jax.ShapeDtypeStruct((pairs, n2, n1, d), F32)
    out_specs = [a_spec, a_spec]
    out_shape = [a_shape, a_shape]
    scratch = [pltpu.VMEM((2, n1 // 2, n2, dt), F32), pltpu.VMEM((2, 8 * n2, dt), F32)]
    if short_conv:
        out_specs += [pl.BlockSpec((None, 2, s, dt), lambda p, j, t: (p, 0, 0, j)),
                      pl.BlockSpec((None, 2, N_META, dt), lambda p, j, t: (p, 0, 0, j))]
        out_shape += [jax.ShapeDtypeStruct((pairs, 2, s, d), BF16),
                      jax.ShapeDtypeStruct((pairs, 2, N_META, d), BF16)]
        scratch.append(pltpu.VMEM((s + N_META + 16, dt), F32))
    return pl.pallas_call(
        functools.partial(_fft_in_kernel, n1=n1, n2=n2, short_conv=short_conv),
        grid=(pairs, d // dt, n2 // N2_CHUNK),
        in_specs=in_specs, out_specs=out_specs, out_shape=out_shape, scratch_shapes=scratch,
        compiler_params=_cparams(("parallel", "parallel", "arbitrary")),
        name="fft_in",
    )(*args)


def _fft_mid_kernel(are_ref, aim_ref, kre_ref, kim_ref, f_ref, g_ref, bre_ref, bim_ref, *, n2, slabs):
    xr_blk = jnp.swapaxes(are_ref[...], 0, 1)
    xi_blk = jnp.swapaxes(aim_ref[...], 0, 1)
    ys = []
    for i in range(slabs):
        x = jnp.concatenate([xr_blk[i], xi_blk[i]], axis=0).astype(BF16)
        z = jnp.dot(f_ref[...], x, preferred_element_type=F32)
        zr, zi = z[:n2], z[n2:]
        kr, ki = kre_ref[i], kim_ref[i]
        p = jnp.concatenate([zr * kr - zi * ki, zr * ki + zi * kr], axis=0).astype(BF16)
        ys.append(jnp.dot(g_ref[...], p, preferred_element_type=F32))
    yt = jnp.swapaxes(jnp.stack(ys, axis=0), 0, 1)
    bre_ref[...] = yt[:n2]
    bim_ref[...] = yt[n2:]


def _fft_mid(are, aim, kre, kim, order, tab, *, dt, slabs):
    pairs, n2, n1, d = are.shape
    k_blk0 = order * (n1 // slabs)
    blk = (None, n2, slabs, dt)
    amap = lambda j, kb, p: (p, 0, kb, j)
    kmap = lambda j, kb, p: (k_blk0 + kb, 0, j)
    const = lambda j, kb, p: (0, 0)
    shp = jax.ShapeDtypeStruct((pairs, n2, n1, d), F32)
    return pl.pallas_call(
        functools.partial(_fft_mid_kernel, n2=n2, slabs=slabs),
        grid=(d // dt, n1 // slabs, pairs),
        in_specs=[pl.BlockSpec(blk, amap), pl.BlockSpec(blk, amap),
                  pl.BlockSpec((slabs, n2, dt), kmap), pl.BlockSpec((slabs, n2, dt), kmap),
                  pl.BlockSpec((2 * n2, 2 * n2), const), pl.BlockSpec((2 * n2, 2 * n2), const)],
        out_specs=[pl.BlockSpec(blk, amap), pl.BlockSpec(blk, amap)],
        out_shape=[shp, shp],
        compiler_params=_cparams(("parallel", "parallel", "arbitrary")),
        name="fft_mid",
    )(are, aim, kre, kim, tab["fwd2"], tab["inv2"])


def _filter_fft_in_kernel(c_ref, f_ref, are_ref, aim_ref, *, n1, n2):
    base = pl.multiple_of(pl.program_id(2) * N2_CHUNK, N2_CHUNK)
    ct = jnp.swapaxes(c_ref[:, pl.ds(base, N2_CHUNK), :], 0, 1).astype(BF16)
    for i in range(N2_CHUNK):
        out = jnp.dot(f_ref[i], ct[i], preferred_element_type=F32)
        are_ref[i] = out[:n1]
        aim_ref[i] = out[n1:]


def _filter_fft_mid_kernel(are_ref, aim_ref, f_ref, kre_ref, kim_ref, *, n2, slabs):
    xr_blk = jnp.swapaxes(are_ref[...], 0, 1)
    xi_blk = jnp.swapaxes(aim_ref[...], 0, 1)
    for i in range(slabs):
        x = jnp.concatenate([xr_blk[i], xi_blk[i]], axis=0)
        z = jnp.dot(f_ref[...], x.astype(BF16), preferred_element_type=F32)
        kre_ref[i] = z[:n2]
        kim_ref[i] = z[n2:]


def _filter_spectrum(circ, tab, *, n1, n2, dt, slabs):
    orders, n, d = circ.shape
    shp = [jax.ShapeDtypeStruct((orders, n2, n1, d), F32)] * 2
    tab_spec = pl.BlockSpec((N2_CHUNK, 2 * n1, n1), lambda o, j, t: (t, 0, 0))
    lt = LANE_TILE
    are, aim = pl.pallas_call(
        functools.partial(_filter_fft_in_kernel, n1=n1, n2=n2),
        grid=(orders, d // lt, n2 // N2_CHUNK),
        in_specs=[pl.BlockSpec((None, n1, n2, lt), lambda o, j, t: (o, 0, 0, j)), tab_spec],
        out_specs=[pl.BlockSpec((None, N2_CHUNK, n1, lt), lambda o, j, t: (o, t, 0, j))] * 2,
        out_shape=shp,
        compiler_params=_cparams(("parallel", "parallel", "arbitrary")),
        name="filter_fft_in",
    )(circ.reshape(orders, n1, n2, d), tab["fil1"])
    blk = (None, n2, slabs, dt)
    amap = lambda o, kb, j: (o, 0, kb, j)
    f_spec = pl.BlockSpec((2 * n2, 2 * n2), lambda o, kb, j: (0, 0))
    return pl.pallas_call(
        functools.partial(_filter_fft_mid_kernel, n2=n2, slabs=slabs),
        grid=(orders, n1 // slabs, d // dt),
        in_specs=[pl.BlockSpec(blk, amap), pl.BlockSpec(blk, amap), f_spec],
        out_specs=[pl.BlockSpec((None, slabs, n2, dt), lambda o, kb, j: (o, kb, 0, j))] * 2,
        out_shape=[jax.ShapeDtypeStruct((orders, n1, n2, d), F32)] * 2,
        compiler_params=_cparams(("parallel", "parallel", "parallel")),
        name="filter_fft_mid",
    )(are, aim, tab["fwd2"])


def _alias_patch(tail, u_meta, u_last):
    dfw = tail[0:16] - tail[16:32]
    dbw = tail[32:48] - tail[48:64]
    ridx = lax.broadcasted_iota(jnp.int32, dfw.shape, 0)
    real_fix = jnp.zeros_like(dfw)
    meta_fix = jnp.zeros_like(dfw)
    for o in range(16):
        src = dfw if o == 0 else pltpu.roll(dfw, o, axis=0)
        real_fix = real_fix + jnp.where(ridx >= o, src, 0.0) * u_meta[o:o + 1]
    for c in range(1, 16):
        meta_fix = meta_fix + jnp.where(ridx + c <= 15, pltpu.roll(u_last, 16 - c, axis=0), 0.0) * dbw[c:c + 1]
    return real_fix, meta_fix


def _fft_out_kernel(bre_ref, bim_ref, g_ref, gr_ref, gm_ref, cw_ref, cb_ref, ur_ref, um_ref, skip_ref,
                    tail_ref, zr_ref, zm_ref, ys_ref, yp_ref, seq_ref, *, n1, n2):
    n1h = n1 // 2
    kk = n1h + 8
    s = n1h * n2
    chunk = pl.program_id(2)

    base = pl.multiple_of(chunk * N2_CHUNK, N2_CHUNK)
    ys = []
    for i in range(N2_CHUNK):
        rhs = jnp.concatenate([bre_ref[i], bim_ref[i]], axis=0).astype(BF16)
        y = jnp.dot(g_ref[i], rhs, preferred_element_type=F32)
        ys.append(y)
        for e in range(2):
            yp_ref[e, pl.ds(pl.multiple_of((base + i) * 8, 8), 8), :] = y[e * kk + n1h:(e + 1) * kk]
    yt = jnp.swapaxes(jnp.stack(ys, axis=0), 0, 1)
    for e in range(2):
        ys_ref[e, :, pl.ds(base, N2_CHUNK), :] = yt[e * kk:e * kk + n1h]

    @pl.when(chunk == n2 // N2_CHUNK - 1)
    def _gate():
        tail = tail_ref[...]
        skip = skip_ref[...]
        for e in range(2):
            y_meta = jnp.concatenate(
                [yp_ref[e, pl.ds(8 * (n2 - N_META + j), 1), :] for j in range(N_META)], axis=0)
            u_real = ur_ref[e].astype(F32)
            u_meta = um_ref[e].astype(F32)
            real_fix, meta_fix = _alias_patch(tail, u_meta, u_real[s - 16:s])
            ys_ref[e, n1h - 1, n2 - 16:n2, :] = ys_ref[e, n1h - 1, n2 - 16:n2, :] + real_fix
            y_real = ys_ref[e].reshape(s, u_real.shape[1])
            g_meta, g_real = _ordered_short_conv(gm_ref[e], gr_ref[e], cw_ref, cb_ref, seq_ref)
            zr_ref[e] = (g_real * (y_real + skip * u_real)).astype(zr_ref.dtype)
            zm_ref[e] = (g_meta * (y_meta + meta_fix + skip * u_meta)).astype(zm_ref.dtype)


def _fft_out(bre, bim, tab, gr, gm, gate_off, conv_w, conv_b, ur, um, skip, tail, *, b, s, d, n1, n2, dt):
    pairs = b // 2
    c = gr.shape[1]
    gb0 = gate_off // dt
    kk2 = tab["inv1"].shape[1]
    gr4 = gr.reshape(pairs, 2, s, c)
    gm4 = gm.reshape(pairs, 2, N_META, c)
    ur4 = ur.reshape(pairs, 2, s, d)
    um4 = um.reshape(pairs, 2, N_META, d)
    b_spec = pl.BlockSpec((None, N2_CHUNK, n1, dt), lambda p, j, t: (p, t, 0, j))
    zr, zm = pl.pallas_call(
        functools.partial(_fft_out_kernel, n1=n1, n2=n2),
        grid=(pairs, d // dt, n2 // N2_CHUNK),
        in_specs=[b_spec, b_spec,
                  pl.BlockSpec((N2_CHUNK, kk2, 2 * n1), lambda p, j, t: (t, 0, 0)),
                  pl.BlockSpec((None, 2, s, dt), lambda p, j, t: (p, 0, 0, gb0 + j)),
                  pl.BlockSpec((None, 2, N_META, dt), lambda p, j, t: (p, 0, 0, gb0 + j)),
                  pl.BlockSpec((8, dt), lambda p, j, t: (0, gb0 + j)),
                  pl.BlockSpec((1, dt), lambda p, j, t: (0, gb0 + j)),
                  pl.BlockSpec((None, 2, s, dt), lambda p, j, t: (p, 0, 0, j)),
                  pl.BlockSpec((None, 2, N_META, dt), lambda p, j, t: (p, 0, 0, j)),
                  pl.BlockSpec((1, dt), lambda p, j, t: (0, j)),
                  pl.BlockSpec((64, dt), lambda p, j, t: (0, j))],
        out_specs=[pl.BlockSpec((None, 2, s, dt), lambda p, j, t: (p, 0, 0, j)),
                   pl.BlockSpec((None, 2, N_META, dt), lambda p, j, t: (p, 0, 0, j))],
        out_shape=[jax.ShapeDtypeStruct((pairs, 2, s, d), BF16),
                   jax.ShapeDtypeStruct((pairs, 2, N_META, d), BF16)],
        scratch_shapes=[pltpu.VMEM((2, n1 // 2, n2, dt), F32), pltpu.VMEM((2, 8 * n2, dt), F32),
                        pltpu.VMEM((s + N_META + 16, dt), F32)],
        compiler_params=_cparams(("parallel", "parallel", "arbitrary")),
        name="fft_out",
    )(bre, bim, tab["inv1"], gr4, gm4, conv_w, conv_b, ur4, um4, skip.reshape(1, d), tail)
    return zr.reshape(b * s, d), zm.reshape(b * N_META, d)


def _fused_conv_kernel(*refs, n1, n2, short_conv):
    if short_conv:
        (xr_ref, xm_ref, cwx_ref, cbx_ref, f1_ref, kre_ref, kim_ref, f2_ref, g2_ref, g1_ref, gr_ref, gm_ref,
         cwg_ref, cbg_ref, skip_ref, tail_ref, zr_ref, zm_ref, ur_ref, um_ref,
         are_ref, aim_ref, xs_ref, mp_ref, seq_ref) = refs
    else:
        (ur_ref, um_ref, f1_ref, kre_ref, kim_ref, f2_ref, g2_ref, g1_ref, gr_ref, gm_ref,
         cwg_ref, cbg_ref, skip_ref, tail_ref, zr_ref, zm_ref,
         are_ref, aim_ref, xs_ref, mp_ref, seq_ref) = refs
    n1h = n1 // 2
    kk = n1h + 8
    s = n1h * n2
    nkb = n1 // FUSED_SLABS
    dt = xs_ref.shape[3]

    for e in range(2):
        if short_conv:
            meta, real = _ordered_short_conv(xm_ref[e], xr_ref[e], cwx_ref, cbx_ref, seq_ref)
            ur_ref[e] = real.astype(ur_ref.dtype)
            um_ref[e] = meta.astype(um_ref.dtype)
        else:
            meta, real = um_ref[e].astype(F32), ur_ref[e].astype(F32)
        xs_ref[e] = real.reshape(n1h, n2, dt)
        mp_ref[e] = jnp.zeros(mp_ref.shape[1:], F32)
        for j in range(N_META):
            mp_ref[e, pl.ds(8 * (n2 - N_META + j), 1), :] = meta[j:j + 1]

    def stage1(t, carry):
        base = pl.multiple_of(t * N2_CHUNK, N2_CHUNK)
        xt = [jnp.swapaxes(xs_ref[e, :, pl.ds(base, N2_CHUNK), :], 0, 1) for e in range(2)]
        for i in range(N2_CHUNK):
            parts = []
            for e in range(2):
                parts += [xt[e][i], mp_ref[e, pl.ds(pl.multiple_of((base + i) * 8, 8), 8), :]]
            rhs = jnp.concatenate(parts, axis=0).astype(BF16)
            out = jnp.dot(f1_ref[base + i], rhs, preferred_element_type=F32)
            are_ref[:, base + i] = out[:n1].reshape(nkb, FUSED_SLABS, dt)
            aim_ref[:, base + i] = out[n1:].reshape(nkb, FUSED_SLABS, dt)
        return carry

    def stage2(kb, carry):
        k0 = pl.multiple_of(kb * FUSED_SLABS, FUSED_SLABS)
        xr_blk = jnp.swapaxes(are_ref[kb], 0, 1)
        xi_blk = jnp.swapaxes(aim_ref[kb], 0, 1)
        ys = []
        for i in range(0, FUSED_SLABS, 2):
            x = jnp.concatenate([jnp.concatenate([xr_blk[i + h], xi_blk[i + h]], axis=0) for h in range(2)],
                                axis=1).astype(BF16)
            z = jnp.dot(f2_ref[...], x, preferred_element_type=F32)
            zr, zi = z[:n2], z[n2:]
            kr = jnp.concatenate([kre_ref[k0 + i], kre_ref[k0 + i + 1]], axis=1)
            ki = jnp.concatenate([kim_ref[k0 + i], kim_ref[k0 + i + 1]], axis=1)
            p = jnp.concatenate([zr * kr - zi * ki, zr * ki + zi * kr], axis=0).astype(BF16)
            y = jnp.dot(g2_ref[...], p, preferred_element_type=F32)
            ys += [y[:, :dt], y[:, dt:]]
        yt = jnp.swapaxes(jnp.stack(ys, axis=0), 0, 1)
        are_ref[kb] = yt[:n2]
        aim_ref[kb] = yt[n2:]
        return carry

    def stage3(t, carry):
        base = pl.multiple_of(t * N2_CHUNK, N2_CHUNK)
        ys = []
        for i in range(N2_CHUNK):
            rhs = jnp.concatenate([are_ref[:, base + i].reshape(n1, dt), aim_ref[:, base + i].reshape(n1, dt)],
                                  axis=0).astype(BF16)
            y = jnp.dot(g1_ref[base + i], rhs, preferred_element_type=F32)
            ys.append(y)
            for e in range(2):
                mp_ref[e, pl.ds(pl.multiple_of((base + i) * 8, 8), 8), :] = y[e * kk + n1h:(e + 1) * kk]
        yt = jnp.swapaxes(jnp.stack(ys, axis=0), 0, 1)
        for e in range(2):
            xs_ref[e, :, pl.ds(base, N2_CHUNK), :] = yt[e * kk:e * kk + n1h]
        return carry

    lax.fori_loop(0, n2 // N2_CHUNK, stage1, 0)
    lax.fori_loop(0, nkb, stage2, 0)
    lax.fori_loop(0, n2 // N2_CHUNK, stage3, 0)

    tail = tail_ref[...]
    skip = skip_ref[...]
    for e in range(2):
        y_meta = jnp.concatenate(
            [mp_ref[e, pl.ds(8 * (n2 - N_META + j), 1), :] for j in range(N_META)], axis=0)
        u_real = ur_ref[e].astype(F32)
        u_meta = um_ref[e].astype(F32)
        real_fix, meta_fix = _alias_patch(tail, u_meta, u_real[s - 16:s])
        xs_ref[e, n1h - 1, n2 - 16:n2, :] = xs_ref[e, n1h - 1, n2 - 16:n2, :] + real_fix
        y_real = xs_ref[e].reshape(s, dt)
        g_meta, g_real = _ordered_short_conv(gm_ref[e], gr_ref[e], cwg_ref, cbg_ref, seq_ref)
        zr_ref[e] = (g_real * (y_real + skip * u_real)).astype(zr_ref.dtype)
        zm_ref[e] = (g_meta * (y_meta + meta_fix + skip * u_meta)).astype(zm_ref.dtype)


def _fused_conv(xr, xm, col_off, conv_w, conv_b, kre, kim, order, tab, gr, gm, gate_off, skip, tail, *,
                b, s, d, n1, n2):
    short_conv = col_off is not None
    pairs = b // 2
    dt = LANE_TILE
    cx0 = (col_off or 0) // dt
    cg0 = gate_off // dt
    kk2 = tab["fwd1"].shape[2]
    once = dict(pipeline_mode=pl.Buffered(1))
    seq4 = lambda a, rows: a.reshape(pairs, 2, rows, a.shape[1])
    x_spec = lambda rows, c0: pl.BlockSpec((None, 2, rows, dt), lambda j, p: (p, 0, 0, c0 + j))
    row_spec = lambda rows, c0: pl.BlockSpec((rows, dt), lambda j, p: (0, c0 + j))
    const2 = pl.BlockSpec((2 * n2, 2 * n2), lambda j, p: (0, 0), **once)
    k_spec = pl.BlockSpec((n1, n2, dt), lambda j, p: (order, 0, j), **once)
    in_specs = [x_spec(s, cx0), x_spec(N_META, cx0)]
    args = [seq4(xr, s), seq4(xm, N_META)]
    if short_conv:
        in_specs += [row_spec(8, cx0), row_spec(1, cx0)]
        args += [conv_w, conv_b]
    in_specs += [pl.BlockSpec((n2, 2 * n1, kk2), lambda j, p: (0, 0, 0), **once),
                 k_spec, k_spec, const2, const2,
                 pl.BlockSpec((n2, kk2, 2 * n1), lambda j, p: (0, 0, 0), **once),
                 x_spec(s, cg0), x_spec(N_META, cg0), row_spec(8, cg0), row_spec(1, cg0),
                 row_spec(1, 0), row_spec(64, 0)]
    args += [tab["fwd1"], kre, kim, tab["fwd2"], tab["inv2"], tab["inv1"],
             seq4(gr, s), seq4(gm, N_META), conv_w, conv_b, skip.reshape(1, d), tail]
    seq_out = lambda rows: pl.BlockSpec((None, 2, rows, dt), lambda j, p: (p, 0, 0, j))
    out_specs = [seq_out(s), seq_out(N_META)]
    out_shape = [jax.ShapeDtypeStruct((pairs, 2, s, d), BF16), jax.ShapeDtypeStruct((pairs, 2, N_META, d), BF16)]
    if short_conv:
        out_specs = out_specs * 2
        out_shape = out_shape * 2
    scratch = ([pltpu.VMEM((n1 // FUSED_SLABS, n2, FUSED_SLABS, dt), F32)] * 2
               + [pltpu.VMEM((2, n1 // 2, n2, dt), F32), pltpu.VMEM((2, 8 * n2, dt), F32),
                  pltpu.VMEM((s + N_META + 16, dt), F32)])
    outs = pl.pallas_call(
        functools.partial(_fused_conv_kernel, n1=n1, n2=n2, short_conv=short_conv),
        grid=(d // dt, pairs),
        in_specs=in_specs, out_specs=out_specs, out_shape=out_shape, scratch_shapes=scratch,
        compiler_params=_cparams(("parallel", "parallel")),
        name="fused_conv",
    )(*args)
    return [o.reshape(-1, d) for o in outs]


def _attn_kernel(own_ref, vprev_ref, vnext_ref, vmeta_ref, ktp_ref, kto_ref, ktn_ref, ktm_ref,
                 bias_ref, shift_ref, o_ref, *, groups):
    blk = pl.program_id(1).astype(F32)
    gw = GROUP * HEAD_DIM
    qd = groups * gw
    lane = lax.broadcasted_iota(jnp.int32, (1, LANE_TILE), 1)
    low = lane < HEAD_DIM
    pad_rows = jnp.zeros((ATT_BLOCK - N_META, 2 * HEAD_DIM), BF16)
    zero = jnp.zeros((ATT_BLOCK, LANE_TILE), BF16)
    for g in range(groups):
        vk_own = own_ref[:, qd + g * LANE_TILE:qd + (g + 1) * LANE_TILE]
        grp = slice(g * LANE_TILE, (g + 1) * LANE_TILE)
        vk = jnp.concatenate([vprev_ref[:, grp], vk_own, vnext_ref[:, grp], vmeta_ref[:, grp], pad_rows],
                             axis=0)
        v_ones = jnp.where(low, vk, jnp.ones_like(vk))
        kt_rows = slice(g * HEAD_DIM, (g + 1) * HEAD_DIM)
        kt = jnp.concatenate([ktp_ref[kt_rows, :], kto_ref[kt_rows, :], ktn_ref[kt_rows, :], ktm_ref[kt_rows, :]],
                             axis=1)
        kt2 = jnp.concatenate([kt, kt], axis=0)
        parts = []
        for pr in range(GROUP // 2):
            qp = own_ref[:, g * gw + pr * LANE_TILE:g * gw + (pr + 1) * LANE_TILE]
            parts += [jnp.where(low, qp, zero), jnp.where(low, zero, qp)]
        q = jnp.concatenate(parts, axis=0)
        sc = jnp.dot(q, kt2, preferred_element_type=F32) + bias_ref[g]
        t = [sc[:, i * LANE_TILE:(i + 1) * LANE_TILE] for i in range(3)]
        t.append(sc[:, 3 * LANE_TILE:] - shift_ref[g] * blk)
        m = jnp.max(jnp.maximum(jnp.maximum(t[0], t[1]), jnp.maximum(t[2], t[3])), axis=1, keepdims=True)
        p = jnp.concatenate([jnp.exp(x - m) for x in t], axis=1).astype(BF16)
        oa = jnp.dot(p, v_ones, preferred_element_type=F32)
        ob = pltpu.roll(oa, HEAD_DIM, axis=1)
        outs = []
        for pr in range(GROUP // 2):
            ev = slice((2 * pr) * ATT_BLOCK, (2 * pr + 1) * ATT_BLOCK)
            od = slice((2 * pr + 1) * ATT_BLOCK, (2 * pr + 2) * ATT_BLOCK)
            outs.append(jnp.where(low, oa[ev] / ob[ev], ob[od] / oa[od]))
        o_ref[:, g * gw:(g + 1) * gw] = jnp.concatenate(outs, axis=1).astype(o_ref.dtype)


def _attention_tables(n_heads, sink):
    groups = n_heads // GROUP
    slopes = jnp.exp2(-8.0 * jnp.arange(1, n_heads + 1, dtype=F32) / n_heads)
    i = jnp.arange(ATT_BLOCK, dtype=jnp.int32)[:, None]
    c = jnp.arange(4 * ATT_BLOCK, dtype=jnp.int32)[None, :]
    dist = jnp.abs(c - ATT_BLOCK - i)
    key_blk = c // ATT_BLOCK
    in_band = jnp.logical_and(c < 3 * ATT_BLOCK, dist <= ATT_BLOCK)
    meta_col = jnp.logical_and(c >= 3 * ATT_BLOCK, c < 3 * ATT_BLOCK + N_META)
    sink_col = c == 3 * ATT_BLOCK + N_META
    meta_dist = N_META + i - (c - 3 * ATT_BLOCK)
    tables = []
    for drop in (None, 0, 2):
        ok = in_band if drop is None else jnp.logical_and(in_band, key_blk != drop)
        d_eff = jnp.where(ok, dist, jnp.where(meta_col, meta_dist, 0)).astype(F32)
        live = jnp.logical_or(ok, meta_col)
        tab = jnp.where(live[None], -slopes[:, None, None] * d_eff[None], MASK_VALUE)
        tables.append(jnp.where(sink_col[None], sink.astype(F32)[:, None, None], tab))
    bias = jnp.stack(tables, axis=0).reshape(3, groups, GROUP * ATT_BLOCK, 4 * ATT_BLOCK)
    lane = jnp.arange(LANE_TILE)[None, :]
    shift = jnp.where(lane < N_META, jnp.repeat(slopes * ATT_BLOCK, ATT_BLOCK)[:, None], 0.0)
    return bias, shift.reshape(groups, GROUP * ATT_BLOCK, LANE_TILE)


def _attention(qvk_r, kt_r, qvk_m, kt_m, sink, *, b, s, n_heads):
    groups = n_heads // GROUP
    nblk = s // ATT_BLOCK
    qd = n_heads * HEAD_DIM
    width = qvk_r.shape[1]
    vkw = groups * 2 * HEAD_DIM
    vkb = qd // vkw
    bias, shift = _attention_tables(n_heads, sink)

    def variant(j):
        return jnp.where(j == 0, 1, jnp.where(j == nblk - 1, 2, 0))

    prev = lambda i, j: i * nblk + jnp.maximum(j - 1, 0)
    nxt = lambda i, j: i * nblk + jnp.minimum(j + 1, nblk - 1)
    return pl.pallas_call(
        functools.partial(_attn_kernel, groups=groups),
        grid=(b, nblk),
        in_specs=[pl.BlockSpec((ATT_BLOCK, width), lambda i, j: (i * nblk + j, 0)),
                  pl.BlockSpec((ATT_BLOCK, vkw), lambda i, j: (prev(i, j), vkb)),
                  pl.BlockSpec((ATT_BLOCK, vkw), lambda i, j: (nxt(i, j), vkb)),
                  pl.BlockSpec((N_META, vkw), lambda i, j: (i, vkb)),
                  pl.BlockSpec((groups * HEAD_DIM, ATT_BLOCK), lambda i, j: (0, prev(i, j))),
                  pl.BlockSpec((groups * HEAD_DIM, ATT_BLOCK), lambda i, j: (0, i * nblk + j)),
                  pl.BlockSpec((groups * HEAD_DIM, ATT_BLOCK), lambda i, j: (0, nxt(i, j))),
                  pl.BlockSpec((None, groups * HEAD_DIM, LANE_TILE), lambda i, j: (i, 0, 0)),
                  pl.BlockSpec((None, groups, GROUP * ATT_BLOCK, 4 * ATT_BLOCK), lambda i, j: (variant(j), 0, 0, 0)),
                  pl.BlockSpec((groups, GROUP * ATT_BLOCK, LANE_TILE), lambda i, j: (0, 0, 0))],
        out_specs=pl.BlockSpec((ATT_BLOCK, qd), lambda i, j: (i * nblk + j, 0)),
        out_shape=jax.ShapeDtypeStruct((b * s, qd), BF16),
        compiler_params=_cparams(("parallel", "arbitrary")),
        name="window_attention",
    )(qvk_r, qvk_r, qvk_r, qvk_m, kt_r, kt_r, kt_r, kt_m, bias, shift)


def _fft_split(s):
    n2 = 128 if s >= 1024 else 32
    return (2 * s) // n2, n2


def _fused_conv_vmem_bytes(s, n1, n2):
    lane_bytes = LANE_TILE * 4
    spectrum = 4 * n1 * n2 * lane_bytes
    tables = 2 * n2 * 2 * n1 * 2 * (n1 // 2 + 8) * 2
    sequence = (2 * s + 2 * 8 * n2 + s + N_META + 16) * lane_bytes
    blocks = 4 * 2 * 2 * (s + N_META) * LANE_TILE * 2
    return spectrum + tables + sequence + blocks


def _hyena_conv(xr, xm, col_off, conv_w, conv_b, kre, kim, order, tab, gr, gm, gate_off, skip, tail, *, dims):
    b, s, d, n1, n2 = dims
    if _fused_conv_vmem_bytes(s, n1, n2) <= (VMEM_LIMIT * 7) // 8:
        outs = _fused_conv(xr, xm, col_off, conv_w, conv_b, kre, kim, order, tab, gr, gm, gate_off, skip, tail,
                           b=b, s=s, d=d, n1=n1, n2=n2)
        return outs[0], outs[1]
    kw = dict(b=b, s=s, d=d, n1=n1, n2=n2, dt=LANE_TILE)
    if col_off is not None:
        are, aim, ur4, um4 = _fft_in(xr, xm, col_off, conv_w, conv_b, tab, **kw)
        ur, um = ur4.reshape(b * s, d), um4.reshape(b * N_META, d)
    else:
        are, aim = _fft_in(xr, xm, 0, None, None, tab, **kw)
        ur, um = xr, xm
    bre, bim = _fft_mid(are, aim, kre, kim, order, tab, dt=min(d, FFT_MID_LANES), slabs=FUSED_SLABS)
    return _fft_out(bre, bim, tab, gr, gm, gate_off, conv_w, conv_b, ur, um, skip, tail, **kw)


def _hyena_layer(streams, fp, g_mix, w_in, conv_w, conv_b, skip, tm):
    outs = []
    d = w_in.shape[0]
    for st in streams:
        b, s = st["b"], st["s"]
        n1, n2 = _fft_split(s)
        seq_len = s + N_META
        n = 2 * s
        tab = _fft_tables(n1, n2)
        r = jnp.arange(n, dtype=jnp.int32)
        tr = min(s, 512)
        circ = _filter_rows(jnp.where(r < s, r, n - r), s // tr, tr, seq_len, fp, d)
        kre, kim = [k.reshape(2 * n1, n2, d)
                    for k in _filter_spectrum(circ, tab, n1=n1, n2=n2, dt=FFT_MID_LANES, slabs=FUSED_SLABS)]
        a = jnp.arange(16, dtype=jnp.int32)
        tail = _filter_rows(jnp.concatenate([s + a, s - a, s - a, s + a]), 1, 32, seq_len, fp, d)
        tail = jnp.concatenate([tail[:, 0:16], tail[:, 32:48], tail[:, 48:64], tail[:, 16:32]], axis=1)
        pr = _norm_matmul(st["hr"], g_mix, w_in, tm)
        pm = _norm_matmul(st["hm"], g_mix, w_in, st["hm"].shape[0])
        dims = (b, s, d, n1, n2)
        z1r, z1m = _hyena_conv(pr, pm, 0, conv_w, conv_b, kre, kim, 0, tab, pr, pm, d, skip[0], tail[0], dims=dims)
        z2r, z2m = _hyena_conv(z1r, z1m, None, conv_w, conv_b, kre, kim, 1, tab, pr, pm, 2 * d, skip[1], tail[1],
                               dims=dims)
        outs.append((z2r, z2m))
    return outs


def _attention_weights(w_qkv, n_heads):
    groups = n_heads // GROUP
    qd = n_heads * HEAD_DIM
    kd = groups * HEAD_DIM
    w_q = w_qkv[:, :qd] * (HEAD_DIM ** -0.5)
    w_k = w_qkv[:, qd:qd + kd]
    w_v = w_qkv[:, qd + kd:]
    d = w_qkv.shape[0]
    w_vk = jnp.stack([w_v.reshape(d, groups, HEAD_DIM), w_k.reshape(d, groups, HEAD_DIM)], axis=2)
    w_rows = jnp.concatenate([w_q, w_vk.reshape(d, 2 * kd)], axis=1).astype(BF16)
    return w_rows, w_k.T.astype(BF16)


def _meta_keys_transposed(qvk_m, b, n_heads):
    groups = n_heads // GROUP
    qd = n_heads * HEAD_DIM
    k_m = qvk_m[:, qd:].reshape(b, N_META, groups, 2, HEAD_DIM)[:, :, :, 1, :]
    kt = jnp.transpose(k_m.reshape(b, N_META, groups * HEAD_DIM), (0, 2, 1))
    return jnp.pad(kt, ((0, 0), (0, 0), (0, LANE_TILE - N_META)))


def _encoder_pair(x_prompt, x_sample, meta_tokens, norm_mix, norm_mlp, norm_final,
                  hy_w_in, hy_conv_w, hy_conv_b, fps, hy_skip, hy_w_out, hy_b_out,
                  at_w_qkv, at_sink, at_w_o, mlp_w1, mlp_w2, *, n_heads, tm):
    d = x_prompt.shape[-1]
    streams = []
    for x in (x_prompt, x_sample):
        b, s, _ = x.shape
        streams.append(dict(b=b, s=s, hr=x.reshape(b * s, d),
                            hm=jnp.tile(meta_tokens.astype(F32), (b, 1))))
    zeros_d = jnp.zeros((d,), F32)

    conv_w = jnp.pad(hy_conv_w[0], ((0, 5), (0, 0)))
    conv_b = hy_conv_b[0][None, :]
    zs = _hyena_layer(streams, fps[0], norm_mix[0], hy_w_in[0].astype(BF16), conv_w, conv_b, hy_skip[0], tm)
    w_out = hy_w_out[0].astype(BF16)
    w1 = [w.astype(BF16) for w in mlp_w1]
    w2 = [w.astype(BF16) for w in mlp_w2]
    for st, (zr, zm) in zip(streams, zs):
        st["hr"] = _mixer_out_mlp(st["hr"], zr, w_out, hy_b_out[0], norm_mlp[0], w1[0], w2[0], zeros_d, tm, False)
        st["hm"] = _mixer_out_mlp(st["hm"], zm, w_out, hy_b_out[0], norm_mlp[0], w1[0], w2[0], zeros_d,
                                  st["hm"].shape[0], False)

    w_rows, w_kt = _attention_weights(at_w_qkv[0], n_heads)
    w_o = at_w_o[0].astype(BF16)
    outs = []
    for st in streams:
        qvk_r, kt_r = _norm_matmul(st["hr"], norm_mix[1], w_rows, tm, wt=w_kt)
        qvk_m = _norm_matmul(st["hm"], norm_mix[1], w_rows, st["hm"].shape[0])
        kt_m = _meta_keys_transposed(qvk_m, st["b"], n_heads)
        att = _attention(qvk_r, kt_r, qvk_m, kt_m, at_sink[0], b=st["b"], s=st["s"], n_heads=n_heads)
        y = _mixer_out_mlp(st["hr"], att, w_o, zeros_d, norm_mlp[1], w1[1], w2[1], norm_final, tm, True)
        outs.append(y.reshape(st["b"], st["s"], d))
    return tuple(outs)


def kernel(x_prompt, x_sample, meta_tokens, norm_mix, norm_mlp, norm_final, hy_w_in, hy_conv_w, hy_conv_b,
           hy_f_w1, hy_f_b1, hy_f_w2, hy_f_b2, hy_f_w3, hy_f_b3, hy_f_wout, hy_f_freq, hy_skip, hy_w_out,
           hy_b_out, at_w_qkv, at_sink, at_w_o, mlp_w1, mlp_w2):
    fps = [dict(w1=hy_f_w1[j], b1=hy_f_b1[j], w2=hy_f_w2[j], b2=hy_f_b2[j], w3=hy_f_w3[j], b3=hy_f_b3[j],
                wout=hy_f_wout[j], freq=hy_f_freq[j]) for j in range(hy_f_w1.shape[0])]
    n_heads = at_sink.shape[1]
    return _encoder_pair(x_prompt, x_sample, meta_tokens, norm_mix, norm_mlp, norm_final,
                         hy_w_in, hy_conv_w, hy_conv_b, fps, hy_skip, hy_w_out, hy_b_out,
                         at_w_qkv, at_sink, at_w_o, mlp_w1, mlp_w2, n_heads=n_heads, tm=512)
```

```python
import functools
import math

import jax
import jax.numpy as jnp
from jax import lax
from jax.experimental import pallas as pl
from jax.experimental.pallas import tpu as pltpu

F32 = jnp.float32
BF16 = jnp.bfloat16

N_META = 16
RMS_EPS = 1e-6
HY_BANDS = 16
HY_EMB_PAD = 40
HY_FAST_DECAY = 0.3
HY_SLOW_DECAY = 1.5
HY_DECAY_TARGET = 1e-2
ATT_BLOCK = 128
HEAD_DIM = 64
GROUP = 4
MASK_VALUE = -1e30
FF_CHUNK = 1024
LANE_TILE = 128
N2_CHUNK = 32
FUSED_SLABS = 16
FFT_MID_LANES = 256
VMEM_LIMIT = 56 * 1024 * 1024
HIGHEST = lax.Precision.HIGHEST


def _cparams(sem):
    return pltpu.CompilerParams(dimension_semantics=sem, vmem_limit_bytes=VMEM_LIMIT)


def _rms(x, g):
    return x * lax.rsqrt(jnp.mean(x * x, axis=-1, keepdims=True) + RMS_EPS) * g


def _norm_matmul_kernel(x_ref, g_ref, w_ref, *rest):
    u = _rms(x_ref[...], g_ref[...]).astype(BF16)
    if len(rest) == 1:
        (o_ref,) = rest
    else:
        wt_ref, o_ref, ot_ref = rest
        ot_ref[...] = lax.dot_general(wt_ref[...], u, (((1,), (1,)), ((), ())),
                                      preferred_element_type=F32).astype(ot_ref.dtype)
    o_ref[...] = jnp.dot(u, w_ref[...], preferred_element_type=F32).astype(o_ref.dtype)


def _norm_matmul(x, g, w, tm, wt=None):
    rows, d = x.shape
    n = w.shape[1]
    in_specs = [pl.BlockSpec((tm, d), lambda i: (i, 0)),
                pl.BlockSpec((1, d), lambda i: (0, 0)),
                pl.BlockSpec((d, n), lambda i: (0, 0))]
    out_specs = pl.BlockSpec((tm, n), lambda i: (i, 0))
    out_shape = jax.ShapeDtypeStruct((rows, n), BF16)
    args = [x, g.reshape(1, d), w]
    if wt is not None:
        m = wt.shape[0]
        in_specs.append(pl.BlockSpec((m, d), lambda i: (0, 0)))
        out_specs = [out_specs, pl.BlockSpec((m, tm), lambda i: (0, i))]
        out_shape = [out_shape, jax.ShapeDtypeStruct((m, rows), BF16)]
        args.append(wt)
    return pl.pallas_call(
        _norm_matmul_kernel,
        grid=(rows // tm,),
        in_specs=in_specs, out_specs=out_specs, out_shape=out_shape,
        compiler_params=_cparams(("parallel",)),
        name="norm_matmul",
    )(*args)


def _mixer_out_mlp_kernel(h_ref, z_ref, wp_ref, bp_ref, g_ref, w1_ref, w2_ref, gf_ref, o_ref, *, final_norm):
    h = h_ref[...] + jnp.dot(z_ref[...], wp_ref[...], preferred_element_type=F32) + bp_ref[...]
    u = _rms(h, g_ref[...]).astype(BF16)
    acc = h
    d_ff = w1_ref.shape[1]
    for c in range(d_ff // FF_CHUNK):
        a = jnp.dot(u, w1_ref[:, c * FF_CHUNK:(c + 1) * FF_CHUNK], preferred_element_type=F32)
        a = jnp.square(jnp.maximum(a, 0.0)).astype(BF16)
        acc = acc + jnp.dot(a, w2_ref[c * FF_CHUNK:(c + 1) * FF_CHUNK, :], preferred_element_type=F32)
    if final_norm:
        acc = _rms(acc, gf_ref[...])
    o_ref[...] = acc


def _mixer_out_mlp(h, z, wp, bp, g, w1, w2, gf, tm, final_norm):
    rows, d = h.shape
    dz = z.shape[1]
    d_ff = w1.shape[1]
    const = lambda i: (0, 0)
    return pl.pallas_call(
        functools.partial(_mixer_out_mlp_kernel, final_norm=final_norm),
        grid=(rows // tm,),
        in_specs=[pl.BlockSpec((tm, d), lambda i: (i, 0)),
                  pl.BlockSpec((tm, dz), lambda i: (i, 0)),
                  pl.BlockSpec((dz, d), const),
                  pl.BlockSpec((1, d), const),
                  pl.BlockSpec((1, d), const),
                  pl.BlockSpec((d, d_ff), const),
                  pl.BlockSpec((d_ff, d), const),
                  pl.BlockSpec((1, d), const)],
        out_specs=pl.BlockSpec((tm, d), lambda i: (i, 0)),
        out_shape=jax.ShapeDtypeStruct((rows, d), F32),
        compiler_params=_cparams(("parallel",)),
        name="mixer_out_mlp",
    )(h, z, wp, bp.reshape(1, d), g.reshape(1, d), w1, w2, gf.reshape(1, d))


def _split_bf16(x):
    hi = x.astype(BF16)
    return hi, (x - hi.astype(F32)).astype(BF16)


def _dot3(a_hi, a_lo, b_hi, b_lo):
    dot = functools.partial(jnp.dot, preferred_element_type=F32)
    return dot(a_hi, b_hi) + (dot(a_hi, b_lo) + dot(a_lo, b_hi))


def _filter_kernel(z_ref, t_ref, w1_ref, b1_ref, w2_ref, b2_ref, w3_ref, b3_ref, fr_ref, woh_ref, wol_ref,
                   ad_ref, o_ref):
    d = ad_ref.shape[1]
    dot = functools.partial(jnp.dot, precision=HIGHEST, preferred_element_type=F32)
    fr = fr_ref[...]
    h = jnp.sin(fr * (dot(z_ref[...], w1_ref[...]) + b1_ref[...]))
    h = jnp.sin(fr * (dot(h, w2_ref[...]) + b2_ref[...]))
    h = jnp.sin(fr * (dot(h, w3_ref[...]) + b3_ref[...]))
    ho = _dot3(*_split_bf16(h), woh_ref[...], wol_ref[...])
    decay = jnp.exp(-t_ref[...] * ad_ref[...])
    for o in range(2):
        o_ref[o] = ho[:, o * d:(o + 1) * d] * decay


def _filter_rows(lag, first_bwd_tile, tr, seq_len, fp, d):
    rows = lag.shape[0]
    lagf = lag.astype(F32)
    t = lagf / (seq_len - 1)
    w = 2.0 * math.pi * lagf / seq_len
    f = jnp.linspace(1e-4, HY_BANDS - 1, HY_BANDS, dtype=F32)[None, :]
    z = jnp.concatenate([t[:, None], jnp.cos(f * w[:, None]), -jnp.sin(f * w[:, None]),
                         jnp.zeros((rows, HY_EMB_PAD - 2 * HY_BANDS - 1), F32)], axis=-1)
    w1 = jnp.pad(fp["w1"], ((0, HY_EMB_PAD - fp["w1"].shape[0]), (0, 0)))
    hid = w1.shape[1]
    max_decay = math.log(HY_DECAY_TARGET) / HY_FAST_DECAY
    min_decay = math.log(HY_DECAY_TARGET) / HY_SLOW_DECAY
    adel = jnp.abs(jnp.linspace(min_decay, max_decay, d, dtype=F32))[None, :]
    const = lambda i: (0, 0)
    row = lambda i: (i, 0)
    wo = jnp.transpose(fp["wout"].reshape(hid, 2, 2, d), (2, 0, 1, 3)).reshape(2, hid, 2 * d)
    wo_hi, wo_lo = _split_bf16(wo)
    wo_spec = pl.BlockSpec((None, hid, 2 * d), lambda i: ((i >= first_bwd_tile).astype(jnp.int32), 0, 0))
    return pl.pallas_call(
        _filter_kernel,
        grid=(rows // tr,),
        in_specs=[pl.BlockSpec((tr, HY_EMB_PAD), row), pl.BlockSpec((tr, 1), row),
                  pl.BlockSpec((HY_EMB_PAD, hid), const), pl.BlockSpec((1, hid), const),
                  pl.BlockSpec((hid, hid), const), pl.BlockSpec((1, hid), const),
                  pl.BlockSpec((hid, hid), const), pl.BlockSpec((1, hid), const),
                  pl.BlockSpec((1, hid), const), wo_spec, wo_spec,
                  pl.BlockSpec((1, d), const)],
        out_specs=pl.BlockSpec((2, tr, d), lambda i: (0, i, 0)),
        out_shape=jax.ShapeDtypeStruct((2, rows, d), F32),
        compiler_params=_cparams(("parallel",)),
        name="hyena_filter",
    )(z, t[:, None], w1, fp["b1"][None], fp["w2"], fp["b2"][None],
      fp["w3"], fp["b3"][None], fp["freq"][None], wo_hi, wo_lo, adel)


def _cplx_block(re, im):
    return jnp.concatenate([jnp.concatenate([re, -im], axis=-1), jnp.concatenate([im, re], axis=-1)], axis=-2)


def _fft_tables(n1, n2):
    n = n1 * n2
    n1h = n1 // 2
    pad = 8 - 1
    k1 = jnp.arange(n1, dtype=jnp.int32)
    c2 = jnp.arange(n2, dtype=jnp.int32)
    cols = jnp.concatenate([jnp.arange(n1h, dtype=jnp.int32), jnp.array([n1 - 1], jnp.int32)])
    pos = n2 * cols[None, None, :] + c2[:, None, None]
    ang = ((k1[None, :, None] * pos) % n).astype(F32) * (-2.0 * math.pi / n)
    wr = jnp.pad(jnp.cos(ang), ((0, 0), (0, 0), (0, pad)))
    wi = jnp.pad(jnp.sin(ang), ((0, 0), (0, 0), (0, pad)))
    fwd1 = _cplx_block(wr, wi)
    inv1 = _cplx_block(jnp.swapaxes(wr, 1, 2), -jnp.swapaxes(wi, 1, 2)) / n
    posf = n2 * k1[None, None, :] + c2[:, None, None]
    angf = ((k1[None, :, None] * posf) % n).astype(F32) * (-2.0 * math.pi / n)
    fil1 = jnp.concatenate([jnp.cos(angf), jnp.sin(angf)], axis=1)
    ang2 = ((c2[:, None] * c2[None, :]) % n2).astype(F32) * (-2.0 * math.pi / n2)
    fr, fi = jnp.cos(ang2), jnp.sin(ang2)
    fwd2 = _cplx_block(fr, fi)
    return dict(fwd1=fwd1.astype(BF16), inv1=inv1.astype(BF16), fil1=fil1.astype(BF16),
                fwd2=fwd2.astype(BF16), inv2=_cplx_block(fr, -fi).astype(BF16))


def _ordered_short_conv(meta, real, cw_ref, cb_ref, seq_ref):
    s = real.shape[0]
    dt = real.shape[1]
    seq_ref[0:8, :] = jnp.zeros((8, dt), F32)
    seq_ref[8:8 + N_META, :] = meta.astype(F32)
    seq_ref[8 + N_META:8 + N_META + s, :] = real.astype(F32)
    seq_ref[8 + N_META + s:16 + N_META + s, :] = jnp.zeros((8, dt), F32)
    x = seq_ref[...]
    rows = x.shape[0]
    cw = cw_ref[...].astype(F32)
    y = (pltpu.roll(x, 1, axis=0) * cw[0:1] + x * cw[1:2] + pltpu.roll(x, rows - 1, axis=0) * cw[2:3]
         + cb_ref[...].astype(F32))
    return y[8:8 + N_META], y[8 + N_META:8 + N_META + s]


def _fft_in_kernel(*refs, n1, n2, short_conv):
    if short_conv:
        (xr_ref, xm_ref, cw_ref, cb_ref, f_ref, are_ref, aim_ref, vr_ref, vm_ref,
         xs_ref, mp_ref, seq_ref) = refs
    else:
        xr_ref, xm_ref, f_ref, are_ref, aim_ref, xs_ref, mp_ref = refs
    n1h = n1 // 2
    chunk = pl.program_id(2)

    @pl.when(chunk == 0)
    def _prepare():
        for e in range(2):
            if short_conv:
                meta, real = _ordered_short_conv(xm_ref[e], xr_ref[e], cw_ref, cb_ref, seq_ref)
                vr_ref[e] = real.astype(vr_ref.dtype)
                vm_ref[e] = meta.astype(vm_ref.dtype)
            else:
                meta, real = xm_ref[e].astype(F32), xr_ref[e].astype(F32)
            xs_ref[e] = real.reshape(n1h, n2, real.shape[1])
            mp_ref[e] = jnp.zeros(mp_ref.shape[1:], F32)
            for j in range(N_META):
                mp_ref[e, pl.ds(8 * (n2 - N_META + j), 1), :] = meta[j:j + 1]

    base = pl.multiple_of(chunk * N2_CHUNK, N2_CHUNK)
    xt = [jnp.swapaxes(xs_ref[e, :, pl.ds(base, N2_CHUNK), :], 0, 1) for e in range(2)]
    for i in range(N2_CHUNK):
        parts = []
        for e in range(2):
            parts += [xt[e][i], mp_ref[e, pl.ds(pl.multiple_of((base + i) * 8, 8), 8), :]]
        rhs = jnp.concatenate(parts, axis=0).astype(BF16)
        out = jnp.dot(f_ref[i], rhs, preferred_element_type=F32)
        are_ref[i] = out[:n1]
        aim_ref[i] = out[n1:]


def _fft_in(xr, xm, col_off, conv_w, conv_b, tab, *, b, s, d, n1, n2, dt):
    short_conv = conv_w is not None
    pairs = b // 2
    c = xr.shape[1]
    cb0 = col_off // dt
    xr4 = xr.reshape(pairs, 2, s, c)
    xm4 = xm.reshape(pairs, 2, N_META, c)
    kk = tab["fwd1"].shape[2]
    in_specs = [pl.BlockSpec((None, 2, s, dt), lambda p, j, t: (p, 0, 0, cb0 + j)),
                pl.BlockSpec((None, 2, N_META, dt), lambda p, j, t: (p, 0, 0, cb0 + j))]
    args = [xr4, xm4]
    if short_conv:
        in_specs += [pl.BlockSpec((8, dt), lambda p, j, t: (0, cb0 + j)),
                     pl.BlockSpec((1, dt), lambda p, j, t: (0, cb0 + j))]
        args += [conv_w, conv_b]
    in_specs.append(pl.BlockSpec((N2_CHUNK, 2 * n1, kk), lambda p, j, t: (t, 0, 0)))
    args.append(tab["fwd1"])
    a_spec = pl.BlockSpec((None, None, N2_CHUNK, n1, dt), lambda p, j, t: (p, j, t, 0, 0))
    a_shape = jax.ShapeDtypeStruct((pairs, d // dt, n2, n1, dt), F32)
    out_specs = [a_spec, a_spec]
    out_shape = [a_shape, a_shape]
    scratch = [pltpu.VMEM((2, n1 // 2, n2, dt), F32), pltpu.VMEM((2, 8 * n2, dt), F32)]
    if short_conv:
        out_specs += [pl.BlockSpec((None, 2, s, dt), lambda p, j, t: (p, 0, 0, j)),
                      pl.BlockSpec((None, 2, N_META, dt), lambda p, j, t: (p, 0, 0, j))]
        out_shape += [jax.ShapeDtypeStruct((pairs, 2, s, d), BF16),
                      jax.ShapeDtypeStruct((pairs, 2, N_META, d), BF16)]
        scratch.append(pltpu.VMEM((s + N_META + 16, dt), F32))
    return pl.pallas_call(
        functools.partial(_fft_in_kernel, n1=n1, n2=n2, short_conv=short_conv),
        grid=(pairs, d // dt, n2 // N2_CHUNK),
        in_specs=in_specs, out_specs=out_specs, out_shape=out_shape, scratch_shapes=scratch,
        compiler_params=_cparams(("parallel", "parallel", "arbitrary")),
        name="fft_in",
    )(*args)


def _fft_mid_kernel(are_ref, aim_ref, kre_ref, kim_ref, f_ref, g_ref, bre_ref, bim_ref, *, n2, slabs):
    tiles = are_ref.shape[0]
    lt = are_ref.shape[3]
    xr_blk = [jnp.swapaxes(are_ref[h], 0, 1) for h in range(tiles)]
    xi_blk = [jnp.swapaxes(aim_ref[h], 0, 1) for h in range(tiles)]
    ys = []
    for i in range(slabs):
        x = jnp.concatenate([jnp.concatenate([xr_blk[h][i], xi_blk[h][i]], axis=0) for h in range(tiles)],
                            axis=1).astype(BF16)
        z = jnp.dot(f_ref[...], x, preferred_element_type=F32)
        zr, zi = z[:n2], z[n2:]
        kr = jnp.concatenate([kre_ref[h, i] for h in range(tiles)], axis=1)
        ki = jnp.concatenate([kim_ref[h, i] for h in range(tiles)], axis=1)
        p = jnp.concatenate([zr * kr - zi * ki, zr * ki + zi * kr], axis=0).astype(BF16)
        ys.append(jnp.dot(g_ref[...], p, preferred_element_type=F32))
    yt = jnp.swapaxes(jnp.stack(ys, axis=0), 0, 1)
    for h in range(tiles):
        bre_ref[h] = yt[:n2, :, h * lt:(h + 1) * lt]
        bim_ref[h] = yt[n2:, :, h * lt:(h + 1) * lt]


def _fft_mid(are, aim, kre, kim, order, tab, *, dt, slabs):
    pairs, ntiles, n2, n1, lt = are.shape
    d = ntiles * lt
    blk = (None, dt // lt, n2, slabs, lt)
    amap = lambda j, kb, p: (p, j, 0, kb, 0)
    kmap = lambda j, kb, p: (order, j, kb, 0, 0)
    const = lambda j, kb, p: (0, 0)
    shp = jax.ShapeDtypeStruct(are.shape, F32)
    return pl.pallas_call(
        functools.partial(_fft_mid_kernel, n2=n2, slabs=slabs),
        grid=(d // dt, n1 // slabs, pairs),
        in_specs=[pl.BlockSpec(blk, amap), pl.BlockSpec(blk, amap),
                  pl.BlockSpec((None, dt // lt, slabs, n2, lt), kmap),
                  pl.BlockSpec((None, dt // lt, slabs, n2, lt), kmap),
                  pl.BlockSpec((2 * n2, 2 * n2), const), pl.BlockSpec((2 * n2, 2 * n2), const)],
        out_specs=[pl.BlockSpec(blk, amap), pl.BlockSpec(blk, amap)],
        out_shape=[shp, shp],
        compiler_params=_cparams(("parallel", "parallel", "arbitrary")),
        name="fft_mid",
    )(are, aim, kre, kim, tab["fwd2"], tab["inv2"])


def _filter_fft_in_kernel(c_ref, f_ref, are_ref, aim_ref, *, n1, n2):
    base = pl.multiple_of(pl.program_id(2) * N2_CHUNK, N2_CHUNK)
    ct = jnp.swapaxes(c_ref[:, pl.ds(base, N2_CHUNK), :], 0, 1).astype(BF16)
    for i in range(N2_CHUNK):
        out = jnp.dot(f_ref[i], ct[i], preferred_element_type=F32)
        are_ref[i] = out[:n1]
        aim_ref[i] = out[n1:]


def _filter_fft_mid_kernel(are_ref, aim_ref, f_ref, kre_ref, kim_ref, *, n2, slabs):
    tiles = are_ref.shape[0]
    lt = are_ref.shape[3]
    xr_blk = [jnp.swapaxes(are_ref[h], 0, 1) for h in range(tiles)]
    xi_blk = [jnp.swapaxes(aim_ref[h], 0, 1) for h in range(tiles)]
    for i in range(slabs):
        x = jnp.concatenate([jnp.concatenate([xr_blk[h][i], xi_blk[h][i]], axis=0) for h in range(tiles)], axis=1)
        z = jnp.dot(f_ref[...], x.astype(BF16), preferred_element_type=F32)
        for h in range(tiles):
            kre_ref[h, i] = z[:n2, h * lt:(h + 1) * lt]
            kim_ref[h, i] = z[n2:, h * lt:(h + 1) * lt]


def _filter_spectrum(circ, tab, *, n1, n2, dt, slabs):
    orders, n, d = circ.shape
    lt = LANE_TILE
    shp = [jax.ShapeDtypeStruct((orders, d // lt, n2, n1, lt), F32)] * 2
    tab_spec = pl.BlockSpec((N2_CHUNK, 2 * n1, n1), lambda o, j, t: (t, 0, 0))
    are, aim = pl.pallas_call(
        functools.partial(_filter_fft_in_kernel, n1=n1, n2=n2),
        grid=(orders, d // lt, n2 // N2_CHUNK),
        in_specs=[pl.BlockSpec((None, n1, n2, lt), lambda o, j, t: (o, 0, 0, j)), tab_spec],
        out_specs=[pl.BlockSpec((None, None, N2_CHUNK, n1, lt), lambda o, j, t: (o, j, t, 0, 0))] * 2,
        out_shape=shp,
        compiler_params=_cparams(("parallel", "parallel", "arbitrary")),
        name="filter_fft_in",
    )(circ.reshape(orders, n1, n2, d), tab["fil1"])
    blk = (None, dt // lt, n2, slabs, lt)
    amap = lambda o, kb, j: (o, j, 0, kb, 0)
    f_spec = pl.BlockSpec((2 * n2, 2 * n2), lambda o, kb, j: (0, 0))
    return pl.pallas_call(
        functools.partial(_filter_fft_mid_kernel, n2=n2, slabs=slabs),
        grid=(orders, n1 // slabs, d // dt),
        in_specs=[pl.BlockSpec(blk, amap), pl.BlockSpec(blk, amap), f_spec],
        out_specs=[pl.BlockSpec((None, dt // lt, slabs, n2, lt), lambda o, kb, j: (o, j, kb, 0, 0))] * 2,
        out_shape=[jax.ShapeDtypeStruct((orders, d // lt, n1, n2, lt), F32)] * 2,
        compiler_params=_cparams(("parallel", "parallel", "parallel")),
        name="filter_fft_mid",
    )(are, aim, tab["fwd2"])


def _alias_patch(tail, u_meta, u_last):
    dfw = tail[0:16] - tail[16:32]
    dbw = tail[32:48] - tail[48:64]
    ridx = lax.broadcasted_iota(jnp.int32, dfw.shape, 0)
    real_fix = jnp.zeros_like(dfw)
    meta_fix = jnp.zeros_like(dfw)
    for o in range(16):
        src = dfw if o == 0 else pltpu.roll(dfw, o, axis=0)
        real_fix = real_fix + jnp.where(ridx >= o, src, 0.0) * u_meta[o:o + 1]
    for c in range(1, 16):
        meta_fix = meta_fix + jnp.where(ridx + c <= 15, pltpu.roll(u_last, 16 - c, axis=0), 0.0) * dbw[c:c + 1]
    return real_fix, meta_fix


def _fft_out_kernel(bre_ref, bim_ref, g_ref, gr_ref, gm_ref, cw_ref, cb_ref, ur_ref, um_ref, skip_ref,
                    tail_ref, zr_ref, zm_ref, ys_ref, yp_ref, seq_ref, *, n1, n2):
    n1h = n1 // 2
    kk = n1h + 8
    s = n1h * n2
    chunk = pl.program_id(2)

    base = pl.multiple_of(chunk * N2_CHUNK, N2_CHUNK)
    ys = []
    for i in range(N2_CHUNK):
        rhs = jnp.concatenate([bre_ref[i], bim_ref[i]], axis=0).astype(BF16)
        y = jnp.dot(g_ref[i], rhs, preferred_element_type=F32)
        ys.append(y)
        for e in range(2):
            yp_ref[e, pl.ds(pl.multiple_of((base + i) * 8, 8), 8), :] = y[e * kk + n1h:(e + 1) * kk]
    yt = jnp.swapaxes(jnp.stack(ys, axis=0), 0, 1)
    for e in range(2):
        ys_ref[e, :, pl.ds(base, N2_CHUNK), :] = yt[e * kk:e * kk + n1h]

    @pl.when(chunk == n2 // N2_CHUNK - 1)
    def _gate():
        tail = tail_ref[...]
        skip = skip_ref[...]
        for e in range(2):
            y_meta = jnp.concatenate(
                [yp_ref[e, pl.ds(8 * (n2 - N_META + j), 1), :] for j in range(N_META)], axis=0)
            u_real = ur_ref[e].astype(F32)
            u_meta = um_ref[e].astype(F32)
            real_fix, meta_fix = _alias_patch(tail, u_meta, u_real[s - 16:s])
            ys_ref[e, n1h - 1, n2 - 16:n2, :] = ys_ref[e, n1h - 1, n2 - 16:n2, :] + real_fix
            y_real = ys_ref[e].reshape(s, u_real.shape[1])
            g_meta, g_real = _ordered_short_conv(gm_ref[e], gr_ref[e], cw_ref, cb_ref, seq_ref)
            zr_ref[e] = (g_real * (y_real + skip * u_real)).astype(zr_ref.dtype)
            zm_ref[e] = (g_meta * (y_meta + meta_fix + skip * u_meta)).astype(zm_ref.dtype)


def _fft_out(bre, bim, tab, gr, gm, gate_off, conv_w, conv_b, ur, um, skip, tail, *, b, s, d, n1, n2, dt):
    pairs = b // 2
    c = gr.shape[1]
    gb0 = gate_off // dt
    kk2 = tab["inv1"].shape[1]
    gr4 = gr.reshape(pairs, 2, s, c)
    gm4 = gm.reshape(pairs, 2, N_META, c)
    ur4 = ur.reshape(pairs, 2, s, d)
    um4 = um.reshape(pairs, 2, N_META, d)
    b_spec = pl.BlockSpec((None, None, N2_CHUNK, n1, dt), lambda p, j, t: (p, j, t, 0, 0))
    zr, zm = pl.pallas_call(
        functools.partial(_fft_out_kernel, n1=n1, n2=n2),
        grid=(pairs, d // dt, n2 // N2_CHUNK),
        in_specs=[b_spec, b_spec,
                  pl.BlockSpec((N2_CHUNK, kk2, 2 * n1), lambda p, j, t: (t, 0, 0)),
                  pl.BlockSpec((None, 2, s, dt), lambda p, j, t: (p, 0, 0, gb0 + j)),
                  pl.BlockSpec((None, 2, N_META, dt), lambda p, j, t: (p, 0, 0, gb0 + j)),
                  pl.BlockSpec((8, dt), lambda p, j, t: (0, gb0 + j)),
                  pl.BlockSpec((1, dt), lambda p, j, t: (0, gb0 + j)),
                  pl.BlockSpec((None, 2, s, dt), lambda p, j, t: (p, 0, 0, j)),
                  pl.BlockSpec((None, 2, N_META, dt), lambda p, j, t: (p, 0, 0, j)),
                  pl.BlockSpec((1, dt), lambda p, j, t: (0, j)),
                  pl.BlockSpec((64, dt), lambda p, j, t: (0, j))],
        out_specs=[pl.BlockSpec((None, 2, s, dt), lambda p, j, t: (p, 0, 0, j)),
                   pl.BlockSpec((None, 2, N_META, dt), lambda p, j, t: (p, 0, 0, j))],
        out_shape=[jax.ShapeDtypeStruct((pairs, 2, s, d), BF16),
                   jax.ShapeDtypeStruct((pairs, 2, N_META, d), BF16)],
        scratch_shapes=[pltpu.VMEM((2, n1 // 2, n2, dt), F32), pltpu.VMEM((2, 8 * n2, dt), F32),
                        pltpu.VMEM((s + N_META + 16, dt), F32)],
        compiler_params=_cparams(("parallel", "parallel", "arbitrary")),
        name="fft_out",
    )(bre, bim, tab["inv1"], gr4, gm4, conv_w, conv_b, ur4, um4, skip.reshape(1, d), tail)
    return zr.reshape(b * s, d), zm.reshape(b * N_META, d)


def _fused_conv_kernel(*refs, n1, n2, short_conv):
    if short_conv:
        (xr_ref, xm_ref, cwx_ref, cbx_ref, f1_ref, kre_ref, kim_ref, f2_ref, g2_ref, g1_ref, gr_ref, gm_ref,
         cwg_ref, cbg_ref, skip_ref, tail_ref, zr_ref, zm_ref, ur_ref, um_ref,
         are_ref, aim_ref, xs_ref, mp_ref, seq_ref) = refs
    else:
        (ur_ref, um_ref, f1_ref, kre_ref, kim_ref, f2_ref, g2_ref, g1_ref, gr_ref, gm_ref,
         cwg_ref, cbg_ref, skip_ref, tail_ref, zr_ref, zm_ref,
         are_ref, aim_ref, xs_ref, mp_ref, seq_ref) = refs
    n1h = n1 // 2
    kk = n1h + 8
    s = n1h * n2
    nkb = n1 // FUSED_SLABS
    dt = xs_ref.shape[3]

    for e in range(2):
        if short_conv:
            meta, real = _ordered_short_conv(xm_ref[e], xr_ref[e], cwx_ref, cbx_ref, seq_ref)
            ur_ref[e] = real.astype(ur_ref.dtype)
            um_ref[e] = meta.astype(um_ref.dtype)
        else:
            meta, real = um_ref[e].astype(F32), ur_ref[e].astype(F32)
        xs_ref[e] = real.reshape(n1h, n2, dt)
        mp_ref[e] = jnp.zeros(mp_ref.shape[1:], F32)
        for j in range(N_META):
            mp_ref[e, pl.ds(8 * (n2 - N_META + j), 1), :] = meta[j:j + 1]

    def stage1(t, carry):
        base = pl.multiple_of(t * N2_CHUNK, N2_CHUNK)
        xt = [jnp.swapaxes(xs_ref[e, :, pl.ds(base, N2_CHUNK), :], 0, 1) for e in range(2)]
        for i in range(N2_CHUNK):
            parts = []
            for e in range(2):
                parts += [xt[e][i], mp_ref[e, pl.ds(pl.multiple_of((base + i) * 8, 8), 8), :]]
            rhs = jnp.concatenate(parts, axis=0).astype(BF16)
            out = jnp.dot(f1_ref[base + i], rhs, preferred_element_type=F32)
            are_ref[:, base + i] = out[:n1].reshape(nkb, FUSED_SLABS, dt)
            aim_ref[:, base + i] = out[n1:].reshape(nkb, FUSED_SLABS, dt)
        return carry

    def stage2(kb, carry):
        k0 = pl.multiple_of(kb * FUSED_SLABS, FUSED_SLABS)
        xr_blk = jnp.swapaxes(are_ref[kb], 0, 1)
        xi_blk = jnp.swapaxes(aim_ref[kb], 0, 1)
        ys = []
        for i in range(0, FUSED_SLABS, 2):
            x = jnp.concatenate([jnp.concatenate([xr_blk[i + h], xi_blk[i + h]], axis=0) for h in range(2)],
                                axis=1).astype(BF16)
            z = jnp.dot(f2_ref[...], x, preferred_element_type=F32)
            zr, zi = z[:n2], z[n2:]
            kr = jnp.concatenate([kre_ref[k0 + i], kre_ref[k0 + i + 1]], axis=1)
            ki = jnp.concatenate([kim_ref[k0 + i], kim_ref[k0 + i + 1]], axis=1)
            p = jnp.concatenate([zr * kr - zi * ki, zr * ki + zi * kr], axis=0).astype(BF16)
            y = jnp.dot(g2_ref[...], p, preferred_element_type=F32)
            ys += [y[:, :dt], y[:, dt:]]
        yt = jnp.swapaxes(jnp.stack(ys, axis=0), 0, 1)
        are_ref[kb] = yt[:n2]
        aim_ref[kb] = yt[n2:]
        return carry

    def stage3(t, carry):
        base = pl.multiple_of(t * N2_CHUNK, N2_CHUNK)
        ys = []
        for i in range(N2_CHUNK):
            rhs = jnp.concatenate([are_ref[:, base + i].reshape(n1, dt), aim_ref[:, base + i].reshape(n1, dt)],
                                  axis=0).astype(BF16)
            y = jnp.dot(g1_ref[base + i], rhs, preferred_element_type=F32)
            ys.append(y)
            for e in range(2):
                mp_ref[e, pl.ds(pl.multiple_of((base + i) * 8, 8), 8), :] = y[e * kk + n1h:(e + 1) * kk]
        yt = jnp.swapaxes(jnp.stack(ys, axis=0), 0, 1)
        for e in range(2):
            xs_ref[e, :, pl.ds(base, N2_CHUNK), :] = yt[e * kk:e * kk + n1h]
        return carry

    lax.fori_loop(0, n2 // N2_CHUNK, stage1, 0)
    lax.fori_loop(0, nkb, stage2, 0)
    lax.fori_loop(0, n2 // N2_CHUNK, stage3, 0)

    tail = tail_ref[...]
    skip = skip_ref[...]
    for e in range(2):
        y_meta = jnp.concatenate(
            [mp_ref[e, pl.ds(8 * (n2 - N_META + j), 1), :] for j in range(N_META)], axis=0)
        u_real = ur_ref[e].astype(F32)
        u_meta = um_ref[e].astype(F32)
        real_fix, meta_fix = _alias_patch(tail, u_meta, u_real[s - 16:s])
        xs_ref[e, n1h - 1, n2 - 16:n2, :] = xs_ref[e, n1h - 1, n2 - 16:n2, :] + real_fix
        y_real = xs_ref[e].reshape(s, dt)
        g_meta, g_real = _ordered_short_conv(gm_ref[e], gr_ref[e], cwg_ref, cbg_ref, seq_ref)
        zr_ref[e] = (g_real * (y_real + skip * u_real)).astype(zr_ref.dtype)
        zm_ref[e] = (g_meta * (y_meta + meta_fix + skip * u_meta)).astype(zm_ref.dtype)


def _fused_conv(xr, xm, col_off, conv_w, conv_b, kre, kim, order, tab, gr, gm, gate_off, skip, tail, *,
                b, s, d, n1, n2):
    short_conv = col_off is not None
    pairs = b // 2
    dt = LANE_TILE
    cx0 = (col_off or 0) // dt
    cg0 = gate_off // dt
    kk2 = tab["fwd1"].shape[2]
    once = dict(pipeline_mode=pl.Buffered(1))
    seq4 = lambda a, rows: a.reshape(pairs, 2, rows, a.shape[1])
    x_spec = lambda rows, c0: pl.BlockSpec((None, 2, rows, dt), lambda j, p: (p, 0, 0, c0 + j))
    row_spec = lambda rows, c0: pl.BlockSpec((rows, dt), lambda j, p: (0, c0 + j))
    const2 = pl.BlockSpec((2 * n2, 2 * n2), lambda j, p: (0, 0), **once)
    k_spec = pl.BlockSpec((None, None, n1, n2, dt), lambda j, p: (order, j, 0, 0, 0), **once)
    in_specs = [x_spec(s, cx0), x_spec(N_META, cx0)]
    args = [seq4(xr, s), seq4(xm, N_META)]
    if short_conv:
        in_specs += [row_spec(8, cx0), row_spec(1, cx0)]
        args += [conv_w, conv_b]
    in_specs += [pl.BlockSpec((n2, 2 * n1, kk2), lambda j, p: (0, 0, 0), **once),
                 k_spec, k_spec, const2, const2,
                 pl.BlockSpec((n2, kk2, 2 * n1), lambda j, p: (0, 0, 0), **once),
                 x_spec(s, cg0), x_spec(N_META, cg0), row_spec(8, cg0), row_spec(1, cg0),
                 row_spec(1, 0), row_spec(64, 0)]
    args += [tab["fwd1"], kre, kim, tab["fwd2"], tab["inv2"], tab["inv1"],
             seq4(gr, s), seq4(gm, N_META), conv_w, conv_b, skip.reshape(1, d), tail]
    seq_out = lambda rows: pl.BlockSpec((None, 2, rows, dt), lambda j, p: (p, 0, 0, j))
    out_specs = [seq_out(s), seq_out(N_META)]
    out_shape = [jax.ShapeDtypeStruct((pairs, 2, s, d), BF16), jax.ShapeDtypeStruct((pairs, 2, N_META, d), BF16)]
    if short_conv:
        out_specs = out_specs * 2
        out_shape = out_shape * 2
    scratch = ([pltpu.VMEM((n1 // FUSED_SLABS, n2, FUSED_SLABS, dt), F32)] * 2
               + [pltpu.VMEM((2, n1 // 2, n2, dt), F32), pltpu.VMEM((2, 8 * n2, dt), F32),
                  pltpu.VMEM((s + N_META + 16, dt), F32)])
    outs = pl.pallas_call(
        functools.partial(_fused_conv_kernel, n1=n1, n2=n2, short_conv=short_conv),
        grid=(d // dt, pairs),
        in_specs=in_specs, out_specs=out_specs, out_shape=out_shape, scratch_shapes=scratch,
        compiler_params=_cparams(("parallel", "parallel")),
        name="fused_conv",
    )(*args)
    return [o.reshape(-1, d) for o in outs]


def _attn_kernel(own_ref, vprev_ref, vnext_ref, vmeta_ref, ktp_ref, kto_ref, ktn_ref, ktm_ref,
                 bias_ref, shift_ref, o_ref, *, groups):
    blk = pl.program_id(1).astype(F32)
    gw = GROUP * HEAD_DIM
    qd = groups * gw
    lane = lax.broadcasted_iota(jnp.int32, (1, LANE_TILE), 1)
    low = lane < HEAD_DIM
    pad_rows = jnp.zeros((ATT_BLOCK - N_META, 2 * HEAD_DIM), BF16)
    zero = jnp.zeros((ATT_BLOCK, LANE_TILE), BF16)
    for g in range(groups):
        vk_own = own_ref[:, qd + g * LANE_TILE:qd + (g + 1) * LANE_TILE]
        grp = slice(g * LANE_TILE, (g + 1) * LANE_TILE)
        vk = jnp.concatenate([vprev_ref[:, grp], vk_own, vnext_ref[:, grp], vmeta_ref[:, grp], pad_rows],
                             axis=0)
        v_ones = jnp.where(low, vk, jnp.ones_like(vk))
        kt_rows = slice(g * HEAD_DIM, (g + 1) * HEAD_DIM)
        kt = jnp.concatenate([ktp_ref[kt_rows, :], kto_ref[kt_rows, :], ktn_ref[kt_rows, :], ktm_ref[kt_rows, :]],
                             axis=1)
        kt2 = jnp.concatenate([kt, kt], axis=0)
        parts = []
        for pr in range(GROUP // 2):
            qp = own_ref[:, g * gw + pr * LANE_TILE:g * gw + (pr + 1) * LANE_TILE]
            parts += [jnp.where(low, qp, zero), jnp.where(low, zero, qp)]
        q = jnp.concatenate(parts, axis=0)
        sc = jnp.dot(q, kt2, preferred_element_type=F32) + bias_ref[g]
        t = [sc[:, i * LANE_TILE:(i + 1) * LANE_TILE] for i in range(3)]
        t.append(sc[:, 3 * LANE_TILE:] - shift_ref[g] * blk)
        m = jnp.max(jnp.maximum(jnp.maximum(t[0], t[1]), jnp.maximum(t[2], t[3])), axis=1, keepdims=True)
        p = jnp.concatenate([jnp.exp(x - m) for x in t], axis=1).astype(BF16)
        oa = jnp.dot(p, v_ones, preferred_element_type=F32)
        ob = pltpu.roll(oa, HEAD_DIM, axis=1)
        outs = []
        for pr in range(GROUP // 2):
            ev = slice((2 * pr) * ATT_BLOCK, (2 * pr + 1) * ATT_BLOCK)
            od = slice((2 * pr + 1) * ATT_BLOCK, (2 * pr + 2) * ATT_BLOCK)
            outs.append(jnp.where(low, oa[ev] / ob[ev], ob[od] / oa[od]))
        o_ref[:, g * gw:(g + 1) * gw] = jnp.concatenate(outs, axis=1).astype(o_ref.dtype)


def _attention_tables(n_heads, sink):
    groups = n_heads // GROUP
    slopes = jnp.exp2(-8.0 * jnp.arange(1, n_heads + 1, dtype=F32) / n_heads)
    i = jnp.arange(ATT_BLOCK, dtype=jnp.int32)[:, None]
    c = jnp.arange(4 * ATT_BLOCK, dtype=jnp.int32)[None, :]
    dist = jnp.abs(c - ATT_BLOCK - i)
    key_blk = c // ATT_BLOCK
    in_band = jnp.logical_and(c < 3 * ATT_BLOCK, dist <= ATT_BLOCK)
    meta_col = jnp.logical_and(c >= 3 * ATT_BLOCK, c < 3 * ATT_BLOCK + N_META)
    sink_col = c == 3 * ATT_BLOCK + N_META
    meta_dist = N_META + i - (c - 3 * ATT_BLOCK)
    tables = []
    for drop in (None, 0, 2):
        ok = in_band if drop is None else jnp.logical_and(in_band, key_blk != drop)
        d_eff = jnp.where(ok, dist, jnp.where(meta_col, meta_dist, 0)).astype(F32)
        live = jnp.logical_or(ok, meta_col)
        tab = jnp.where(live[None], -slopes[:, None, None] * d_eff[None], MASK_VALUE)
        tables.append(jnp.where(sink_col[None], sink.astype(F32)[:, None, None], tab))
    bias = jnp.stack(tables, axis=0).reshape(3, groups, GROUP * ATT_BLOCK, 4 * ATT_BLOCK)
    lane = jnp.arange(LANE_TILE)[None, :]
    shift = jnp.where(lane < N_META, jnp.repeat(slopes * ATT_BLOCK, ATT_BLOCK)[:, None], 0.0)
    return bias, shift.reshape(groups, GROUP * ATT_BLOCK, LANE_TILE)


def _attention(qvk_r, kt_r, qvk_m, kt_m, sink, *, b, s, n_heads):
    groups = n_heads // GROUP
    nblk = s // ATT_BLOCK
    qd = n_heads * HEAD_DIM
    width = qvk_r.shape[1]
    vkw = groups * 2 * HEAD_DIM
    vkb = qd // vkw
    bias, shift = _attention_tables(n_heads, sink)

    def variant(j):
        return jnp.where(j == 0, 1, jnp.where(j == nblk - 1, 2, 0))

    prev = lambda i, j: i * nblk + jnp.maximum(j - 1, 0)
    nxt = lambda i, j: i * nblk + jnp.minimum(j + 1, nblk - 1)
    return pl.pallas_call(
        functools.partial(_attn_kernel, groups=groups),
        grid=(b, nblk),
        in_specs=[pl.BlockSpec((ATT_BLOCK, width), lambda i, j: (i * nblk + j, 0)),
                  pl.BlockSpec((ATT_BLOCK, vkw), lambda i, j: (prev(i, j), vkb)),
                  pl.BlockSpec((ATT_BLOCK, vkw), lambda i, j: (nxt(i, j), vkb)),
                  pl.BlockSpec((N_META, vkw), lambda i, j: (i, vkb)),
                  pl.BlockSpec((groups * HEAD_DIM, ATT_BLOCK), lambda i, j: (0, prev(i, j))),
                  pl.BlockSpec((groups * HEAD_DIM, ATT_BLOCK), lambda i, j: (0, i * nblk + j)),
                  pl.BlockSpec((groups * HEAD_DIM, ATT_BLOCK), lambda i, j: (0, nxt(i, j))),
                  pl.BlockSpec((None, groups * HEAD_DIM, LANE_TILE), lambda i, j: (i, 0, 0)),
                  pl.BlockSpec((None, groups, GROUP * ATT_BLOCK, 4 * ATT_BLOCK), lambda i, j: (variant(j), 0, 0, 0)),
                  pl.BlockSpec((groups, GROUP * ATT_BLOCK, LANE_TILE), lambda i, j: (0, 0, 0))],
        out_specs=pl.BlockSpec((ATT_BLOCK, qd), lambda i, j: (i * nblk + j, 0)),
        out_shape=jax.ShapeDtypeStruct((b * s, qd), BF16),
        compiler_params=_cparams(("parallel", "arbitrary")),
        name="window_attention",
    )(qvk_r, qvk_r, qvk_r, qvk_m, kt_r, kt_r, kt_r, kt_m, bias, shift)


def _fft_split(s):
    n2 = 128 if s >= 1024 else 32
    return (2 * s) // n2, n2


def _fused_conv_vmem_bytes(s, n1, n2):
    lane_bytes = LANE_TILE * 4
    spectrum = 4 * n1 * n2 * lane_bytes
    tables = 2 * n2 * 2 * n1 * 2 * (n1 // 2 + 8) * 2
    sequence = (2 * s + 2 * 8 * n2 + s + N_META + 16) * lane_bytes
    blocks = 4 * 2 * 2 * (s + N_META) * LANE_TILE * 2
    return spectrum + tables + sequence + blocks


def _hyena_conv(xr, xm, col_off, conv_w, conv_b, kre, kim, order, tab, gr, gm, gate_off, skip, tail, *, dims):
    b, s, d, n1, n2 = dims
    if _fused_conv_vmem_bytes(s, n1, n2) <= (VMEM_LIMIT * 7) // 8:
        outs = _fused_conv(xr, xm, col_off, conv_w, conv_b, kre, kim, order, tab, gr, gm, gate_off, skip, tail,
                           b=b, s=s, d=d, n1=n1, n2=n2)
        return outs[0], outs[1]
    kw = dict(b=b, s=s, d=d, n1=n1, n2=n2, dt=LANE_TILE)
    if col_off is not None:
        are, aim, ur4, um4 = _fft_in(xr, xm, col_off, conv_w, conv_b, tab, **kw)
        ur, um = ur4.reshape(b * s, d), um4.reshape(b * N_META, d)
    else:
        are, aim = _fft_in(xr, xm, 0, None, None, tab, **kw)
        ur, um = xr, xm
    bre, bim = _fft_mid(are, aim, kre, kim, order, tab, dt=min(d, FFT_MID_LANES), slabs=FUSED_SLABS)
    return _fft_out(bre, bim, tab, gr, gm, gate_off, conv_w, conv_b, ur, um, skip, tail, **kw)


def _hyena_layer(streams, fp, g_mix, w_in, conv_w, conv_b, skip, tm):
    outs = []
    d = w_in.shape[0]
    for st in streams:
        b, s = st["b"], st["s"]
        n1, n2 = _fft_split(s)
        seq_len = s + N_META
        n = 2 * s
        tab = _fft_tables(n1, n2)
        r = jnp.arange(n, dtype=jnp.int32)
        tr = min(s, 512)
        circ = _filter_rows(jnp.where(r < s, r, n - r), s // tr, tr, seq_len, fp, d)
        kre, kim = _filter_spectrum(circ, tab, n1=n1, n2=n2, dt=min(d, FFT_MID_LANES), slabs=FUSED_SLABS)
        a = jnp.arange(16, dtype=jnp.int32)
        tail = _filter_rows(jnp.concatenate([s + a, s - a, s - a, s + a]), 1, 32, seq_len, fp, d)
        tail = jnp.concatenate([tail[:, 0:16], tail[:, 32:48], tail[:, 48:64], tail[:, 16:32]], axis=1)
        pr = _norm_matmul(st["hr"], g_mix, w_in, tm)
        pm = _norm_matmul(st["hm"], g_mix, w_in, st["hm"].shape[0])
        dims = (b, s, d, n1, n2)
        z1r, z1m = _hyena_conv(pr, pm, 0, conv_w, conv_b, kre, kim, 0, tab, pr, pm, d, skip[0], tail[0], dims=dims)
        z2r, z2m = _hyena_conv(z1r, z1m, None, conv_w, conv_b, kre, kim, 1, tab, pr, pm, 2 * d, skip[1], tail[1],
                               dims=dims)
        outs.append((z2r, z2m))
    return outs


def _attention_weights(w_qkv, n_heads):
    groups = n_heads // GROUP
    qd = n_heads * HEAD_DIM
    kd = groups * HEAD_DIM
    w_q = w_qkv[:, :qd] * (HEAD_DIM ** -0.5)
    w_k = w_qkv[:, qd:qd + kd]
    w_v = w_qkv[:, qd + kd:]
    d = w_qkv.shape[0]
    w_vk = jnp.stack([w_v.reshape(d, groups, HEAD_DIM), w_k.reshape(d, groups, HEAD_DIM)], axis=2)
    w_rows = jnp.concatenate([w_q, w_vk.reshape(d, 2 * kd)], axis=1).astype(BF16)
    return w_rows, w_k.T.astype(BF16)


def _meta_keys_transposed(qvk_m, b, n_heads):
    groups = n_heads // GROUP
    qd = n_heads * HEAD_DIM
    k_m = qvk_m[:, qd:].reshape(b, N_META, groups, 2, HEAD_DIM)[:, :, :, 1, :]
    kt = jnp.transpose(k_m.reshape(b, N_META, groups * HEAD_DIM), (0, 2, 1))
    return jnp.pad(kt, ((0, 0), (0, 0), (0, LANE_TILE - N_META)))


def _encoder_pair(x_prompt, x_sample, meta_tokens, norm_mix, norm_mlp, norm_final,
                  hy_w_in, hy_conv_w, hy_conv_b, fps, hy_skip, hy_w_out, hy_b_out,
                  at_w_qkv, at_sink, at_w_o, mlp_w1, mlp_w2, *, n_heads, tm):
    d = x_prompt.shape[-1]
    streams = []
    for x in (x_prompt, x_sample):
        b, s, _ = x.shape
        streams.append(dict(b=b, s=s, hr=x.reshape(b * s, d),
                            hm=jnp.tile(meta_tokens.astype(F32), (b, 1))))
    zeros_d = jnp.zeros((d,), F32)

    conv_w = jnp.pad(hy_conv_w[0], ((0, 5), (0, 0)))
    conv_b = hy_conv_b[0][None, :]
    zs = _hyena_layer(streams, fps[0], norm_mix[0], hy_w_in[0].astype(BF16), conv_w, conv_b, hy_skip[0], tm)
    w_out = hy_w_out[0].astype(BF16)
    w1 = [w.astype(BF16) for w in mlp_w1]
    w2 = [w.astype(BF16) for w in mlp_w2]
    for st, (zr, zm) in zip(streams, zs):
        st["hr"] = _mixer_out_mlp(st["hr"], zr, w_out, hy_b_out[0], norm_mlp[0], w1[0], w2[0], zeros_d, tm, False)
        st["hm"] = _mixer_out_mlp(st["hm"], zm, w_out, hy_b_out[0], norm_mlp[0], w1[0], w2[0], zeros_d,
                                  st["hm"].shape[0], False)

    w_rows, w_kt = _attention_weights(at_w_qkv[0], n_heads)
    w_o = at_w_o[0].astype(BF16)
    outs = []
    for st in streams:
        qvk_r, kt_r = _norm_matmul(st["hr"], norm_mix[1], w_rows, tm, wt=w_kt)
        qvk_m = _norm_matmul(st["hm"], norm_mix[1], w_rows, st["hm"].shape[0])
        kt_m = _meta_keys_transposed(qvk_m, st["b"], n_heads)
        att = _attention(qvk_r, kt_r, qvk_m, kt_m, at_sink[0], b=st["b"], s=st["s"], n_heads=n_heads)
        y = _mixer_out_mlp(st["hr"], att, w_o, zeros_d, norm_mlp[1], w1[1], w2[1], norm_final, tm, True)
        outs.append(y.reshape(st["b"], st["s"], d))
    return tuple(outs)


def kernel(x_prompt, x_sample, meta_tokens, norm_mix, norm_mlp, norm_final, hy_w_in, hy_conv_w, hy_conv_b,
           hy_f_w1, hy_f_b1, hy_f_w2, hy_f_b2, hy_f_w3, hy_f_b3, hy_f_wout, hy_f_freq, hy_skip, hy_w_out,
           hy_b_out, at_w_qkv, at_sink, at_w_o, mlp_w1, mlp_w2):
    fps = [dict(w1=hy_f_w1[j], b1=hy_f_b1[j], w2=hy_f_w2[j], b2=hy_f_b2[j], w3=hy_f_w3[j], b3=hy_f_b3[j],
                wout=hy_f_wout[j], freq=hy_f_freq[j]) for j in range(hy_f_w1.shape[0])]
    n_heads = at_sink.shape[1]
    return _encoder_pair(x_prompt, x_sample, meta_tokens, norm_mix, norm_mlp, norm_final,
                         hy_w_in, hy_conv_w, hy_conv_b, fps, hy_skip, hy_w_out, hy_b_out,
                         at_w_qkv, at_sink, at_w_o, mlp_w1, mlp_w2, n_heads=n_heads, tm=512)
```

```python
import functools
import math

import jax
import jax.numpy as jnp
from jax import lax
from jax.experimental import pallas as pl
from jax.experimental.pallas import tpu as pltpu

F32 = jnp.float32
BF16 = jnp.bfloat16

N_META = 16
RMS_EPS = 1e-6
HY_BANDS = 16
HY_EMB_PAD = 40
HY_FAST_DECAY = 0.3
HY_SLOW_DECAY = 1.5
HY_DECAY_TARGET = 1e-2
ATT_BLOCK = 128
HEAD_DIM = 64
GROUP = 4
MASK_VALUE = -1e30
FF_CHUNK = 1024
LANE_TILE = 128
N2_CHUNK = 32
GRID_N2_CHUNK = 16
FUSED_SLABS = 16
FFT_MID_LANES = 256
VMEM_LIMIT = 56 * 1024 * 1024
HIGHEST = lax.Precision.HIGHEST


def _cparams(sem):
    return pltpu.CompilerParams(dimension_semantics=sem, vmem_limit_bytes=VMEM_LIMIT)


def _rms(x, g):
    return x * lax.rsqrt(jnp.mean(x * x, axis=-1, keepdims=True) + RMS_EPS) * g


def _norm_matmul_kernel(x_ref, g_ref, w_ref, *rest):
    u = _rms(x_ref[...], g_ref[...]).astype(BF16)
    if len(rest) == 1:
        (o_ref,) = rest
    else:
        wt_ref, o_ref, ot_ref = rest
        ot_ref[...] = lax.dot_general(wt_ref[...], u, (((1,), (1,)), ((), ())),
                                      preferred_element_type=F32).astype(ot_ref.dtype)
    o_ref[...] = jnp.dot(u, w_ref[...], preferred_element_type=F32).astype(o_ref.dtype)


def _norm_matmul(x, g, w, tm, wt=None):
    rows, d = x.shape
    n = w.shape[1]
    in_specs = [pl.BlockSpec((tm, d), lambda i: (i, 0)),
                pl.BlockSpec((1, d), lambda i: (0, 0)),
                pl.BlockSpec((d, n), lambda i: (0, 0))]
    out_specs = pl.BlockSpec((tm, n), lambda i: (i, 0))
    out_shape = jax.ShapeDtypeStruct((rows, n), BF16)
    args = [x, g.reshape(1, d), w]
    if wt is not None:
        m = wt.shape[0]
        in_specs.append(pl.BlockSpec((m, d), lambda i: (0, 0)))
        out_specs = [out_specs, pl.BlockSpec((m, tm), lambda i: (0, i))]
        out_shape = [out_shape, jax.ShapeDtypeStruct((m, rows), BF16)]
        args.append(wt)
    return pl.pallas_call(
        _norm_matmul_kernel,
        grid=(rows // tm,),
        in_specs=in_specs, out_specs=out_specs, out_shape=out_shape,
        compiler_params=_cparams(("parallel",)),
        name="norm_matmul",
    )(*args)


def _mixer_out_mlp_kernel(h_ref, z_ref, wp_ref, bp_ref, g_ref, w1_ref, w2_ref, gf_ref, o_ref, *, final_norm):
    h = h_ref[...] + jnp.dot(z_ref[...], wp_ref[...], preferred_element_type=F32) + bp_ref[...]
    u = _rms(h, g_ref[...]).astype(BF16)
    acc = h
    d_ff = w1_ref.shape[1]
    for c in range(d_ff // FF_CHUNK):
        a = jnp.dot(u, w1_ref[:, c * FF_CHUNK:(c + 1) * FF_CHUNK], preferred_element_type=F32)
        a = jnp.square(jnp.maximum(a, 0.0)).astype(BF16)
        acc = acc + jnp.dot(a, w2_ref[c * FF_CHUNK:(c + 1) * FF_CHUNK, :], preferred_element_type=F32)
    if final_norm:
        acc = _rms(acc, gf_ref[...])
    o_ref[...] = acc


def _mixer_out_mlp(h, z, wp, bp, g, w1, w2, gf, tm, final_norm):
    rows, d = h.shape
    dz = z.shape[1]
    d_ff = w1.shape[1]
    const = lambda i: (0, 0)
    return pl.pallas_call(
        functools.partial(_mixer_out_mlp_kernel, final_norm=final_norm),
        grid=(rows // tm,),
        in_specs=[pl.BlockSpec((tm, d), lambda i: (i, 0)),
                  pl.BlockSpec((tm, dz), lambda i: (i, 0)),
                  pl.BlockSpec((dz, d), const),
                  pl.BlockSpec((1, d), const),
                  pl.BlockSpec((1, d), const),
                  pl.BlockSpec((d, d_ff), const),
                  pl.BlockSpec((d_ff, d), const),
                  pl.BlockSpec((1, d), const)],
        out_specs=pl.BlockSpec((tm, d), lambda i: (i, 0)),
        out_shape=jax.ShapeDtypeStruct((rows, d), F32),
        compiler_params=_cparams(("parallel",)),
        name="mixer_out_mlp",
    )(h, z, wp, bp.reshape(1, d), g.reshape(1, d), w1, w2, gf.reshape(1, d))


def _split_bf16(x):
    hi = x.astype(BF16)
    return hi, (x - hi.astype(F32)).astype(BF16)


def _dot3(a_hi, a_lo, b_hi, b_lo):
    dot = functools.partial(jnp.dot, preferred_element_type=F32)
    return dot(a_hi, b_hi) + (dot(a_hi, b_lo) + dot(a_lo, b_hi))


def _filter_kernel(z_ref, t_ref, w1_ref, b1_ref, w2_ref, b2_ref, w3_ref, b3_ref, fr_ref, woh_ref, wol_ref,
                   ad_ref, o_ref):
    d = ad_ref.shape[1]
    dot = functools.partial(jnp.dot, precision=HIGHEST, preferred_element_type=F32)
    fr = fr_ref[...]
    h = jnp.sin(fr * (dot(z_ref[...], w1_ref[...]) + b1_ref[...]))
    h = jnp.sin(fr * (dot(h, w2_ref[...]) + b2_ref[...]))
    h = jnp.sin(fr * (dot(h, w3_ref[...]) + b3_ref[...]))
    ho = _dot3(*_split_bf16(h), woh_ref[...], wol_ref[...])
    decay = jnp.exp(-t_ref[...] * ad_ref[...])
    for o in range(2):
        o_ref[o] = ho[:, o * d:(o + 1) * d] * decay


def _filter_rows(lag, first_bwd_tile, tr, seq_len, fp, d):
    rows = lag.shape[0]
    lagf = lag.astype(F32)
    t = lagf / (seq_len - 1)
    w = 2.0 * math.pi * lagf / seq_len
    f = jnp.linspace(1e-4, HY_BANDS - 1, HY_BANDS, dtype=F32)[None, :]
    z = jnp.concatenate([t[:, None], jnp.cos(f * w[:, None]), -jnp.sin(f * w[:, None]),
                         jnp.zeros((rows, HY_EMB_PAD - 2 * HY_BANDS - 1), F32)], axis=-1)
    w1 = jnp.pad(fp["w1"], ((0, HY_EMB_PAD - fp["w1"].shape[0]), (0, 0)))
    hid = w1.shape[1]
    max_decay = math.log(HY_DECAY_TARGET) / HY_FAST_DECAY
    min_decay = math.log(HY_DECAY_TARGET) / HY_SLOW_DECAY
    adel = jnp.abs(jnp.linspace(min_decay, max_decay, d, dtype=F32))[None, :]
    const = lambda i: (0, 0)
    row = lambda i: (i, 0)
    wo = jnp.transpose(fp["wout"].reshape(hid, 2, 2, d), (2, 0, 1, 3)).reshape(2, hid, 2 * d)
    wo_hi, wo_lo = _split_bf16(wo)
    wo_spec = pl.BlockSpec((None, hid, 2 * d), lambda i: ((i >= first_bwd_tile).astype(jnp.int32), 0, 0))
    return pl.pallas_call(
        _filter_kernel,
        grid=(rows // tr,),
        in_specs=[pl.BlockSpec((tr, HY_EMB_PAD), row), pl.BlockSpec((tr, 1), row),
                  pl.BlockSpec((HY_EMB_PAD, hid), const), pl.BlockSpec((1, hid), const),
                  pl.BlockSpec((hid, hid), const), pl.BlockSpec((1, hid), const),
                  pl.BlockSpec((hid, hid), const), pl.BlockSpec((1, hid), const),
                  pl.BlockSpec((1, hid), const), wo_spec, wo_spec,
                  pl.BlockSpec((1, d), const)],
        out_specs=pl.BlockSpec((2, tr, d), lambda i: (0, i, 0)),
        out_shape=jax.ShapeDtypeStruct((2, rows, d), F32),
        compiler_params=_cparams(("parallel",)),
        name="hyena_filter",
    )(z, t[:, None], w1, fp["b1"][None], fp["w2"], fp["b2"][None],
      fp["w3"], fp["b3"][None], fp["freq"][None], wo_hi, wo_lo, adel)


def _cplx_block(re, im):
    return jnp.concatenate([jnp.concatenate([re, -im], axis=-1), jnp.concatenate([im, re], axis=-1)], axis=-2)


def _fft_tables(n1, n2):
    n = n1 * n2
    n1h = n1 // 2
    pad = 8 - 1
    k1 = jnp.arange(n1, dtype=jnp.int32)
    c2 = jnp.arange(n2, dtype=jnp.int32)
    cols = jnp.concatenate([jnp.arange(n1h, dtype=jnp.int32), jnp.array([n1 - 1], jnp.int32)])
    pos = n2 * cols[None, None, :] + c2[:, None, None]
    ang = ((k1[None, :, None] * pos) % n).astype(F32) * (-2.0 * math.pi / n)
    wr = jnp.pad(jnp.cos(ang), ((0, 0), (0, 0), (0, pad)))
    wi = jnp.pad(jnp.sin(ang), ((0, 0), (0, 0), (0, pad)))
    fwd1 = _cplx_block(wr, wi)
    inv1 = _cplx_block(jnp.swapaxes(wr, 1, 2), -jnp.swapaxes(wi, 1, 2)) / n
    posf = n2 * k1[None, None, :] + c2[:, None, None]
    angf = ((k1[None, :, None] * posf) % n).astype(F32) * (-2.0 * math.pi / n)
    fil1 = jnp.concatenate([jnp.cos(angf), jnp.sin(angf)], axis=1)
    ang2 = ((c2[:, None] * c2[None, :]) % n2).astype(F32) * (-2.0 * math.pi / n2)
    fr, fi = jnp.cos(ang2), jnp.sin(ang2)
    fwd2 = _cplx_block(fr, fi)
    return dict(fwd1=fwd1.astype(BF16), inv1=inv1.astype(BF16), fil1=fil1.astype(BF16),
                fwd2=fwd2.astype(BF16), inv2=_cplx_block(fr, -fi).astype(BF16))


def _fill_sequence(meta_ref, real_ref, seq_ref):
    s, dt = real_ref.shape
    seq_ref[0:8, :] = jnp.zeros((8, dt), F32)
    seq_ref[8:8 + N_META, :] = meta_ref[...].astype(F32)
    seq_ref[8 + N_META:8 + N_META + s, :] = real_ref[...].astype(F32)
    seq_ref[8 + N_META + s:16 + N_META + s, :] = jnp.zeros((8, dt), F32)


def _short_conv_rows(seq_ref, cw, cb, start, rows):
    return (seq_ref[pl.ds(start - 1, rows), :] * cw[0:1] + seq_ref[pl.ds(start, rows), :] * cw[1:2]
            + seq_ref[pl.ds(start + 1, rows), :] * cw[2:3] + cb)


def _row_chunk(s):
    return min(s, 1024)


def _load_sequences(x_refs, conv_refs, u_refs, xs_ref, mp_ref, seq_ref, n2):
    n1h, dt = xs_ref.shape[1], xs_ref.shape[3]
    s = n1h * n2
    ch = _row_chunk(s)
    ur_ref, um_ref = u_refs
    for e in range(2):
        if conv_refs is not None:
            xr_ref, xm_ref = x_refs
            cw, cb = conv_refs[0][...].astype(F32), conv_refs[1][...].astype(F32)
            _fill_sequence(xm_ref.at[e], xr_ref.at[e], seq_ref)
            meta = _short_conv_rows(seq_ref, cw, cb, 8, N_META)
            um_ref[e] = meta.astype(um_ref.dtype)
        else:
            meta = um_ref[e].astype(F32)
        for c in range(s // ch):
            if conv_refs is not None:
                real = _short_conv_rows(seq_ref, cw, cb, 8 + N_META + c * ch, ch)
                ur_ref[e, c * ch:(c + 1) * ch, :] = real.astype(ur_ref.dtype)
            else:
                real = ur_ref[e, c * ch:(c + 1) * ch, :].astype(F32)
            xs_ref[e, c * ch // n2:(c + 1) * ch // n2] = real.reshape(ch // n2, n2, dt)
        mp_ref[e] = jnp.zeros(mp_ref.shape[1:], F32)
        for j in range(N_META):
            mp_ref[e, pl.ds(8 * (n2 - N_META + j), 1), :] = meta[j:j + 1]


def _gate_sequences(xs_ref, mp_ref, g_refs, conv_refs, u_refs, skip_ref, tail_ref, z_refs, seq_ref, n2):
    n1h, dt = xs_ref.shape[1], xs_ref.shape[3]
    s = n1h * n2
    ch = _row_chunk(s)
    gr_ref, gm_ref = g_refs
    ur_ref, um_ref = u_refs
    zr_ref, zm_ref = z_refs
    cw, cb = conv_refs[0][...].astype(F32), conv_refs[1][...].astype(F32)
    tail = tail_ref[...]
    skip = skip_ref[...]
    for e in range(2):
        y_meta = jnp.concatenate(
            [mp_ref[e, pl.ds(8 * (n2 - N_META + j), 1), :] for j in range(N_META)], axis=0)
        u_meta = um_ref[e].astype(F32)
        real_fix, meta_fix = _alias_patch(tail, u_meta, ur_ref[e, s - 16:s, :].astype(F32))
        xs_ref[e, n1h - 1, n2 - 16:n2, :] = xs_ref[e, n1h - 1, n2 - 16:n2, :] + real_fix
        _fill_sequence(gm_ref.at[e], gr_ref.at[e], seq_ref)
        g_meta = _short_conv_rows(seq_ref, cw, cb, 8, N_META)
        zm_ref[e] = (g_meta * (y_meta + meta_fix + skip * u_meta)).astype(zm_ref.dtype)
        for c in range(s // ch):
            rows = slice(c * ch, (c + 1) * ch)
            g = _short_conv_rows(seq_ref, cw, cb, 8 + N_META + c * ch, ch)
            y = xs_ref[e, c * ch // n2:(c + 1) * ch // n2].reshape(ch, dt)
            zr_ref[e, rows, :] = (g * (y + skip * ur_ref[e, rows, :].astype(F32))).astype(zr_ref.dtype)


def _fft_in_kernel(*refs, n1, n2, short_conv):
    if short_conv:
        (xr_ref, xm_ref, cw_ref, cb_ref, f_ref, are_ref, aim_ref, vr_ref, vm_ref,
         xs_ref, mp_ref, seq_ref) = refs
    else:
        xr_ref, xm_ref, f_ref, are_ref, aim_ref, xs_ref, mp_ref = refs
    n1h = n1 // 2
    chunk = pl.program_id(2)

    @pl.when(chunk == 0)
    def _prepare():
        if short_conv:
            _load_sequences((xr_ref, xm_ref), (cw_ref, cb_ref), (vr_ref, vm_ref), xs_ref, mp_ref, seq_ref, n2)
        else:
            _load_sequences(None, None, (xr_ref, xm_ref), xs_ref, mp_ref, None, n2)

    base = pl.multiple_of(chunk * GRID_N2_CHUNK, GRID_N2_CHUNK)
    xt = [jnp.swapaxes(xs_ref[e, :, pl.ds(base, GRID_N2_CHUNK), :], 0, 1) for e in range(2)]
    for i in range(GRID_N2_CHUNK):
        parts = []
        for e in range(2):
            parts += [xt[e][i], mp_ref[e, pl.ds(pl.multiple_of((base + i) * 8, 8), 8), :]]
        rhs = jnp.concatenate(parts, axis=0).astype(BF16)
        out = jnp.dot(f_ref[i], rhs, preferred_element_type=F32)
        are_ref[i] = out[:n1]
        aim_ref[i] = out[n1:]


def _fft_in(xr, xm, col_off, conv_w, conv_b, tab, *, b, s, d, n1, n2, dt):
    short_conv = conv_w is not None
    pairs = b // 2
    c = xr.shape[1]
    cb0 = col_off // dt
    xr4 = xr.reshape(pairs, 2, s, c)
    xm4 = xm.reshape(pairs, 2, N_META, c)
    kk = tab["fwd1"].shape[2]
    in_specs = [pl.BlockSpec((None, 2, s, dt), lambda p, j, t: (p, 0, 0, cb0 + j)),
                pl.BlockSpec((None, 2, N_META, dt), lambda p, j, t: (p, 0, 0, cb0 + j))]
    args = [xr4, xm4]
    if short_conv:
        in_specs += [pl.BlockSpec((8, dt), lambda p, j, t: (0, cb0 + j)),
                     pl.BlockSpec((1, dt), lambda p, j, t: (0, cb0 + j))]
        args += [conv_w, conv_b]
    in_specs.append(pl.BlockSpec((GRID_N2_CHUNK, 2 * n1, kk), lambda p, j, t: (t, 0, 0)))
    args.append(tab["fwd1"])
    a_spec = pl.BlockSpec((None, None, GRID_N2_CHUNK, n1, dt), lambda p, j, t: (p, j, t, 0, 0))
    a_shape = jax.ShapeDtypeStruct((pairs, d // dt, n2, n1, dt), F32)
    out_specs = [a_spec, a_spec]
    out_shape = [a_shape, a_shape]
    scratch = [pltpu.VMEM((2, n1 // 2, n2, dt), F32), pltpu.VMEM((2, 8 * n2, dt), F32)]
    if short_conv:
        out_specs += [pl.BlockSpec((None, 2, s, dt), lambda p, j, t: (p, 0, 0, j)),
                      pl.BlockSpec((None, 2, N_META, dt), lambda p, j, t: (p, 0, 0, j))]
        out_shape += [jax.ShapeDtypeStruct((pairs, 2, s, d), BF16),
                      jax.ShapeDtypeStruct((pairs, 2, N_META, d), BF16)]
        scratch.append(pltpu.VMEM((s + N_META + 16, dt), F32))
    return pl.pallas_call(
        functools.partial(_fft_in_kernel, n1=n1, n2=n2, short_conv=short_conv),
        grid=(pairs, d // dt, n2 // GRID_N2_CHUNK),
        in_specs=in_specs, out_specs=out_specs, out_shape=out_shape, scratch_shapes=scratch,
        compiler_params=_cparams(("parallel", "parallel", "arbitrary")),
        name="fft_in",
    )(*args)


def _fft_mid_kernel(are_ref, aim_ref, kre_ref, kim_ref, f_ref, g_ref, bre_ref, bim_ref, *, n2, slabs):
    tiles = are_ref.shape[0]
    lt = are_ref.shape[3]
    xr_blk = [jnp.swapaxes(are_ref[h], 0, 1) for h in range(tiles)]
    xi_blk = [jnp.swapaxes(aim_ref[h], 0, 1) for h in range(tiles)]
    ys = []
    for i in range(slabs):
        x = jnp.concatenate([jnp.concatenate([xr_blk[h][i], xi_blk[h][i]], axis=0) for h in range(tiles)],
                            axis=1).astype(BF16)
        z = jnp.dot(f_ref[...], x, preferred_element_type=F32)
        zr, zi = z[:n2], z[n2:]
        kr = jnp.concatenate([kre_ref[h, i] for h in range(tiles)], axis=1)
        ki = jnp.concatenate([kim_ref[h, i] for h in range(tiles)], axis=1)
        p = jnp.concatenate([zr * kr - zi * ki, zr * ki + zi * kr], axis=0).astype(BF16)
        ys.append(jnp.dot(g_ref[...], p, preferred_element_type=F32))
    yt = jnp.swapaxes(jnp.stack(ys, axis=0), 0, 1)
    for h in range(tiles):
        bre_ref[h] = yt[:n2, :, h * lt:(h + 1) * lt]
        bim_ref[h] = yt[n2:, :, h * lt:(h + 1) * lt]


def _fft_mid(are, aim, kre, kim, order, tab, *, dt, slabs):
    pairs, ntiles, n2, n1, lt = are.shape
    d = ntiles * lt
    blk = (None, dt // lt, n2, slabs, lt)
    amap = lambda j, kb, p: (p, j, 0, kb, 0)
    kmap = lambda j, kb, p: (order, j, kb, 0, 0)
    const = lambda j, kb, p: (0, 0)
    shp = jax.ShapeDtypeStruct(are.shape, F32)
    return pl.pallas_call(
        functools.partial(_fft_mid_kernel, n2=n2, slabs=slabs),
        grid=(d // dt, n1 // slabs, pairs),
        in_specs=[pl.BlockSpec(blk, amap), pl.BlockSpec(blk, amap),
                  pl.BlockSpec((None, dt // lt, slabs, n2, lt), kmap),
                  pl.BlockSpec((None, dt // lt, slabs, n2, lt), kmap),
                  pl.BlockSpec((2 * n2, 2 * n2), const), pl.BlockSpec((2 * n2, 2 * n2), const)],
        out_specs=[pl.BlockSpec(blk, amap), pl.BlockSpec(blk, amap)],
        out_shape=[shp, shp],
        compiler_params=_cparams(("parallel", "parallel", "arbitrary")),
        name="fft_mid",
    )(are, aim, kre, kim, tab["fwd2"], tab["inv2"])


def _filter_fft_in_kernel(c_ref, f_ref, are_ref, aim_ref, *, n1, n2):
    base = pl.multiple_of(pl.program_id(2) * GRID_N2_CHUNK, GRID_N2_CHUNK)
    ct = jnp.swapaxes(c_ref[:, pl.ds(base, GRID_N2_CHUNK), :], 0, 1).astype(BF16)
    for i in range(GRID_N2_CHUNK):
        out = jnp.dot(f_ref[i], ct[i], preferred_element_type=F32)
        are_ref[i] = out[:n1]
        aim_ref[i] = out[n1:]


def _filter_fft_mid_kernel(are_ref, aim_ref, f_ref, kre_ref, kim_ref, *, n2, slabs):
    tiles = are_ref.shape[0]
    lt = are_ref.shape[3]
    xr_blk = [jnp.swapaxes(are_ref[h], 0, 1) for h in range(tiles)]
    xi_blk = [jnp.swapaxes(aim_ref[h], 0, 1) for h in range(tiles)]
    for i in range(slabs):
        x = jnp.concatenate([jnp.concatenate([xr_blk[h][i], xi_blk[h][i]], axis=0) for h in range(tiles)], axis=1)
        z = jnp.dot(f_ref[...], x.astype(BF16), preferred_element_type=F32)
        for h in range(tiles):
            kre_ref[h, i] = z[:n2, h * lt:(h + 1) * lt]
            kim_ref[h, i] = z[n2:, h * lt:(h + 1) * lt]


def _filter_spectrum(circ, tab, *, n1, n2, dt, slabs):
    orders, n, d = circ.shape
    lt = LANE_TILE
    shp = [jax.ShapeDtypeStruct((orders, d // lt, n2, n1, lt), F32)] * 2
    tab_spec = pl.BlockSpec((GRID_N2_CHUNK, 2 * n1, n1), lambda o, j, t: (t, 0, 0))
    are, aim = pl.pallas_call(
        functools.partial(_filter_fft_in_kernel, n1=n1, n2=n2),
        grid=(orders, d // lt, n2 // GRID_N2_CHUNK),
        in_specs=[pl.BlockSpec((None, n1, n2, lt), lambda o, j, t: (o, 0, 0, j)), tab_spec],
        out_specs=[pl.BlockSpec((None, None, GRID_N2_CHUNK, n1, lt), lambda o, j, t: (o, j, t, 0, 0))] * 2,
        out_shape=shp,
        compiler_params=_cparams(("parallel", "parallel", "arbitrary")),
        name="filter_fft_in",
    )(circ.reshape(orders, n1, n2, d), tab["fil1"])
    blk = (None, dt // lt, n2, slabs, lt)
    amap = lambda o, kb, j: (o, j, 0, kb, 0)
    f_spec = pl.BlockSpec((2 * n2, 2 * n2), lambda o, kb, j: (0, 0))
    return pl.pallas_call(
        functools.partial(_filter_fft_mid_kernel, n2=n2, slabs=slabs),
        grid=(orders, n1 // slabs, d // dt),
        in_specs=[pl.BlockSpec(blk, amap), pl.BlockSpec(blk, amap), f_spec],
        out_specs=[pl.BlockSpec((None, dt // lt, slabs, n2, lt), lambda o, kb, j: (o, j, kb, 0, 0))] * 2,
        out_shape=[jax.ShapeDtypeStruct((orders, d // lt, n1, n2, lt), F32)] * 2,
        compiler_params=_cparams(("parallel", "parallel", "parallel")),
        name="filter_fft_mid",
    )(are, aim, tab["fwd2"])


def _alias_patch(tail, u_meta, u_last):
    dfw = tail[0:16] - tail[16:32]
    dbw = tail[32:48] - tail[48:64]
    ridx = lax.broadcasted_iota(jnp.int32, dfw.shape, 0)
    real_fix = jnp.zeros_like(dfw)
    meta_fix = jnp.zeros_like(dfw)
    for o in range(16):
        src = dfw if o == 0 else pltpu.roll(dfw, o, axis=0)
        real_fix = real_fix + jnp.where(ridx >= o, src, 0.0) * u_meta[o:o + 1]
    for c in range(1, 16):
        meta_fix = meta_fix + jnp.where(ridx + c <= 15, pltpu.roll(u_last, 16 - c, axis=0), 0.0) * dbw[c:c + 1]
    return real_fix, meta_fix


def _fft_out_kernel(bre_ref, bim_ref, g_ref, gr_ref, gm_ref, cw_ref, cb_ref, ur_ref, um_ref, skip_ref,
                    tail_ref, zr_ref, zm_ref, ys_ref, yp_ref, seq_ref, *, n1, n2):
    n1h = n1 // 2
    kk = n1h + 8
    s = n1h * n2
    chunk = pl.program_id(2)

    base = pl.multiple_of(chunk * GRID_N2_CHUNK, GRID_N2_CHUNK)
    ys = []
    for i in range(GRID_N2_CHUNK):
        rhs = jnp.concatenate([bre_ref[i], bim_ref[i]], axis=0).astype(BF16)
        y = jnp.dot(g_ref[i], rhs, preferred_element_type=F32)
        ys.append(y)
        for e in range(2):
            yp_ref[e, pl.ds(pl.multiple_of((base + i) * 8, 8), 8), :] = y[e * kk + n1h:(e + 1) * kk]
    yt = jnp.swapaxes(jnp.stack(ys, axis=0), 0, 1)
    for e in range(2):
        ys_ref[e, :, pl.ds(base, GRID_N2_CHUNK), :] = yt[e * kk:e * kk + n1h]

    @pl.when(chunk == n2 // GRID_N2_CHUNK - 1)
    def _gate():
        _gate_sequences(ys_ref, yp_ref, (gr_ref, gm_ref), (cw_ref, cb_ref), (ur_ref, um_ref), skip_ref, tail_ref,
                        (zr_ref, zm_ref), seq_ref, n2)


def _fft_out(bre, bim, tab, gr, gm, gate_off, conv_w, conv_b, ur, um, skip, tail, *, b, s, d, n1, n2, dt):
    pairs = b // 2
    c = gr.shape[1]
    gb0 = gate_off // dt
    kk2 = tab["inv1"].shape[1]
    gr4 = gr.reshape(pairs, 2, s, c)
    gm4 = gm.reshape(pairs, 2, N_META, c)
    ur4 = ur.reshape(pairs, 2, s, d)
    um4 = um.reshape(pairs, 2, N_META, d)
    b_spec = pl.BlockSpec((None, None, GRID_N2_CHUNK, n1, dt), lambda p, j, t: (p, j, t, 0, 0))
    zr, zm = pl.pallas_call(
        functools.partial(_fft_out_kernel, n1=n1, n2=n2),
        grid=(pairs, d // dt, n2 // GRID_N2_CHUNK),
        in_specs=[b_spec, b_spec,
                  pl.BlockSpec((GRID_N2_CHUNK, kk2, 2 * n1), lambda p, j, t: (t, 0, 0)),
                  pl.BlockSpec((None, 2, s, dt), lambda p, j, t: (p, 0, 0, gb0 + j)),
                  pl.BlockSpec((None, 2, N_META, dt), lambda p, j, t: (p, 0, 0, gb0 + j)),
                  pl.BlockSpec((8, dt), lambda p, j, t: (0, gb0 + j)),
                  pl.BlockSpec((1, dt), lambda p, j, t: (0, gb0 + j)),
                  pl.BlockSpec((None, 2, s, dt), lambda p, j, t: (p, 0, 0, j)),
                  pl.BlockSpec((None, 2, N_META, dt), lambda p, j, t: (p, 0, 0, j)),
                  pl.BlockSpec((1, dt), lambda p, j, t: (0, j)),
                  pl.BlockSpec((64, dt), lambda p, j, t: (0, j))],
        out_specs=[pl.BlockSpec((None, 2, s, dt), lambda p, j, t: (p, 0, 0, j)),
                   pl.BlockSpec((None, 2, N_META, dt), lambda p, j, t: (p, 0, 0, j))],
        out_shape=[jax.ShapeDtypeStruct((pairs, 2, s, d), BF16),
                   jax.ShapeDtypeStruct((pairs, 2, N_META, d), BF16)],
        scratch_shapes=[pltpu.VMEM((2, n1 // 2, n2, dt), F32), pltpu.VMEM((2, 8 * n2, dt), F32),
                        pltpu.VMEM((s + N_META + 16, dt), F32)],
        compiler_params=_cparams(("parallel", "parallel", "arbitrary")),
        name="fft_out",
    )(bre, bim, tab["inv1"], gr4, gm4, conv_w, conv_b, ur4, um4, skip.reshape(1, d), tail)
    return zr.reshape(b * s, d), zm.reshape(b * N_META, d)


def _fused_conv_kernel(*refs, n1, n2, short_conv):
    if short_conv:
        (xr_ref, xm_ref, cwx_ref, cbx_ref, f1_ref, kre_ref, kim_ref, f2_ref, g2_ref, g1_ref, gr_ref, gm_ref,
         cwg_ref, cbg_ref, skip_ref, tail_ref, zr_ref, zm_ref, ur_ref, um_ref,
         are_ref, aim_ref, xs_ref, mp_ref, seq_ref) = refs
    else:
        (ur_ref, um_ref, f1_ref, kre_ref, kim_ref, f2_ref, g2_ref, g1_ref, gr_ref, gm_ref,
         cwg_ref, cbg_ref, skip_ref, tail_ref, zr_ref, zm_ref,
         are_ref, aim_ref, xs_ref, mp_ref, seq_ref) = refs
    n1h = n1 // 2
    kk = n1h + 8
    s = n1h * n2
    nkb = n1 // FUSED_SLABS
    dt = xs_ref.shape[3]

    if short_conv:
        _load_sequences((xr_ref, xm_ref), (cwx_ref, cbx_ref), (ur_ref, um_ref), xs_ref, mp_ref, seq_ref, n2)
    else:
        _load_sequences(None, None, (ur_ref, um_ref), xs_ref, mp_ref, None, n2)

    def stage1(t, carry):
        base = pl.multiple_of(t * N2_CHUNK, N2_CHUNK)
        xt = [jnp.swapaxes(xs_ref[e, :, pl.ds(base, N2_CHUNK), :], 0, 1) for e in range(2)]
        for i in range(N2_CHUNK):
            parts = []
            for e in range(2):
                parts += [xt[e][i], mp_ref[e, pl.ds(pl.multiple_of((base + i) * 8, 8), 8), :]]
            rhs = jnp.concatenate(parts, axis=0).astype(BF16)
            out = jnp.dot(f1_ref[base + i], rhs, preferred_element_type=F32).astype(BF16)
            are_ref[:, base + i] = out[:n1].reshape(nkb, FUSED_SLABS, dt)
            aim_ref[:, base + i] = out[n1:].reshape(nkb, FUSED_SLABS, dt)
        return carry

    def stage2(kb, carry):
        k0 = pl.multiple_of(kb * FUSED_SLABS, FUSED_SLABS)
        xr_blk = jnp.swapaxes(are_ref[kb], 0, 1)
        xi_blk = jnp.swapaxes(aim_ref[kb], 0, 1)
        ys = []
        for i in range(0, FUSED_SLABS, 2):
            x = jnp.concatenate([jnp.concatenate([xr_blk[i + h], xi_blk[i + h]], axis=0) for h in range(2)],
                                axis=1)
            z = jnp.dot(f2_ref[...], x, preferred_element_type=F32)
            zr, zi = z[:n2], z[n2:]
            kr = jnp.concatenate([kre_ref[k0 + i], kre_ref[k0 + i + 1]], axis=1)
            ki = jnp.concatenate([kim_ref[k0 + i], kim_ref[k0 + i + 1]], axis=1)
            p = jnp.concatenate([zr * kr - zi * ki, zr * ki + zi * kr], axis=0).astype(BF16)
            y = jnp.dot(g2_ref[...], p, preferred_element_type=F32).astype(BF16)
            ys += [y[:, :dt], y[:, dt:]]
        yt = jnp.swapaxes(jnp.stack(ys, axis=0), 0, 1)
        are_ref[kb] = yt[:n2]
        aim_ref[kb] = yt[n2:]
        return carry

    def stage3(t, carry):
        base = pl.multiple_of(t * N2_CHUNK, N2_CHUNK)
        ys = []
        for i in range(N2_CHUNK):
            rhs = jnp.concatenate([are_ref[:, base + i].reshape(n1, dt), aim_ref[:, base + i].reshape(n1, dt)],
                                  axis=0)
            y = jnp.dot(g1_ref[base + i], rhs, preferred_element_type=F32)
            ys.append(y)
            for e in range(2):
                mp_ref[e, pl.ds(pl.multiple_of((base + i) * 8, 8), 8), :] = y[e * kk + n1h:(e + 1) * kk]
        yt = jnp.swapaxes(jnp.stack(ys, axis=0), 0, 1)
        for e in range(2):
            xs_ref[e, :, pl.ds(base, N2_CHUNK), :] = yt[e * kk:e * kk + n1h]
        return carry

    lax.fori_loop(0, n2 // N2_CHUNK, stage1, 0)
    lax.fori_loop(0, nkb, stage2, 0)
    lax.fori_loop(0, n2 // N2_CHUNK, stage3, 0)

    _gate_sequences(xs_ref, mp_ref, (gr_ref, gm_ref), (cwg_ref, cbg_ref), (ur_ref, um_ref), skip_ref, tail_ref,
                    (zr_ref, zm_ref), seq_ref, n2)


def _fused_conv(xr, xm, col_off, conv_w, conv_b, kre, kim, order, tab, gr, gm, gate_off, skip, tail, *,
                b, s, d, n1, n2):
    short_conv = col_off is not None
    pairs = b // 2
    dt = LANE_TILE
    cx0 = (col_off or 0) // dt
    cg0 = gate_off // dt
    kk2 = tab["fwd1"].shape[2]
    once = dict(pipeline_mode=pl.Buffered(1))
    seq4 = lambda a, rows: a.reshape(pairs, 2, rows, a.shape[1])
    x_spec = lambda rows, c0: pl.BlockSpec((None, 2, rows, dt), lambda j, p: (p, 0, 0, c0 + j))
    row_spec = lambda rows, c0: pl.BlockSpec((rows, dt), lambda j, p: (0, c0 + j))
    const2 = pl.BlockSpec((2 * n2, 2 * n2), lambda j, p: (0, 0), **once)
    k_spec = pl.BlockSpec((None, None, n1, n2, dt), lambda j, p: (order, j, 0, 0, 0), **once)
    in_specs = [x_spec(s, cx0), x_spec(N_META, cx0)]
    args = [seq4(xr, s), seq4(xm, N_META)]
    if short_conv:
        in_specs += [row_spec(8, cx0), row_spec(1, cx0)]
        args += [conv_w, conv_b]
    in_specs += [pl.BlockSpec((n2, 2 * n1, kk2), lambda j, p: (0, 0, 0), **once),
                 k_spec, k_spec, const2, const2,
                 pl.BlockSpec((n2, kk2, 2 * n1), lambda j, p: (0, 0, 0), **once),
                 x_spec(s, cg0), x_spec(N_META, cg0), row_spec(8, cg0), row_spec(1, cg0),
                 row_spec(1, 0), row_spec(64, 0)]
    args += [tab["fwd1"], kre, kim, tab["fwd2"], tab["inv2"], tab["inv1"],
             seq4(gr, s), seq4(gm, N_META), conv_w, conv_b, skip.reshape(1, d), tail]
    seq_out = lambda rows: pl.BlockSpec((None, 2, rows, dt), lambda j, p: (p, 0, 0, j))
    out_specs = [seq_out(s), seq_out(N_META)]
    out_shape = [jax.ShapeDtypeStruct((pairs, 2, s, d), BF16), jax.ShapeDtypeStruct((pairs, 2, N_META, d), BF16)]
    if short_conv:
        out_specs = out_specs * 2
        out_shape = out_shape * 2
    scratch = ([pltpu.VMEM((n1 // FUSED_SLABS, n2, FUSED_SLABS, dt), BF16)] * 2
               + [pltpu.VMEM((2, n1 // 2, n2, dt), F32), pltpu.VMEM((2, 8 * n2, dt), F32),
                  pltpu.VMEM((s + N_META + 16, dt), F32)])
    outs = pl.pallas_call(
        functools.partial(_fused_conv_kernel, n1=n1, n2=n2, short_conv=short_conv),
        grid=(d // dt, pairs),
        in_specs=in_specs, out_specs=out_specs, out_shape=out_shape, scratch_shapes=scratch,
        compiler_params=_cparams(("parallel", "parallel")),
        name="fused_conv",
    )(*args)
    return [o.reshape(-1, d) for o in outs]


def _attn_kernel(own_ref, vprev_ref, vnext_ref, vmeta_ref, ktp_ref, kto_ref, ktn_ref, ktm_ref,
                 bias_ref, shift_ref, o_ref, *, groups):
    blk = pl.program_id(1).astype(F32)
    gw = GROUP * HEAD_DIM
    qd = groups * gw
    lane = lax.broadcasted_iota(jnp.int32, (1, LANE_TILE), 1)
    low = lane < HEAD_DIM
    pad_rows = jnp.zeros((ATT_BLOCK - N_META, 2 * HEAD_DIM), BF16)
    zero = jnp.zeros((ATT_BLOCK, LANE_TILE), BF16)
    for g in range(groups):
        vk_own = own_ref[:, qd + g * LANE_TILE:qd + (g + 1) * LANE_TILE]
        grp = slice(g * LANE_TILE, (g + 1) * LANE_TILE)
        vk = jnp.concatenate([vprev_ref[:, grp], vk_own, vnext_ref[:, grp], vmeta_ref[:, grp], pad_rows],
                             axis=0)
        v_ones = jnp.where(low, vk, jnp.ones_like(vk))
        kt_rows = slice(g * HEAD_DIM, (g + 1) * HEAD_DIM)
        kt = jnp.concatenate([ktp_ref[kt_rows, :], kto_ref[kt_rows, :], ktn_ref[kt_rows, :], ktm_ref[kt_rows, :]],
                             axis=1)
        kt2 = jnp.concatenate([kt, kt], axis=0)
        parts = []
        for pr in range(GROUP // 2):
            qp = own_ref[:, g * gw + pr * LANE_TILE:g * gw + (pr + 1) * LANE_TILE]
            parts += [jnp.where(low, qp, zero), jnp.where(low, zero, qp)]
        q = jnp.concatenate(parts, axis=0)
        sc = jnp.dot(q, kt2, preferred_element_type=F32) + bias_ref[g]
        t = [sc[:, i * LANE_TILE:(i + 1) * LANE_TILE] for i in range(3)]
        t.append(sc[:, 3 * LANE_TILE:] - shift_ref[g] * blk)
        m = jnp.max(jnp.maximum(jnp.maximum(t[0], t[1]), jnp.maximum(t[2], t[3])), axis=1, keepdims=True)
        p = jnp.concatenate([jnp.exp(x - m) for x in t], axis=1).astype(BF16)
        oa = jnp.dot(p, v_ones, preferred_element_type=F32)
        ob = pltpu.roll(oa, HEAD_DIM, axis=1)
        outs = []
        for pr in range(GROUP // 2):
            ev = slice((2 * pr) * ATT_BLOCK, (2 * pr + 1) * ATT_BLOCK)
            od = slice((2 * pr + 1) * ATT_BLOCK, (2 * pr + 2) * ATT_BLOCK)
            outs.append(jnp.where(low, oa[ev] / ob[ev], ob[od] / oa[od]))
        o_ref[:, g * gw:(g + 1) * gw] = jnp.concatenate(outs, axis=1).astype(o_ref.dtype)


def _attention_tables(n_heads, sink):
    groups = n_heads // GROUP
    slopes = jnp.exp2(-8.0 * jnp.arange(1, n_heads + 1, dtype=F32) / n_heads)
    i = jnp.arange(ATT_BLOCK, dtype=jnp.int32)[:, None]
    c = jnp.arange(4 * ATT_BLOCK, dtype=jnp.int32)[None, :]
    dist = jnp.abs(c - ATT_BLOCK - i)
    key_blk = c // ATT_BLOCK
    in_band = jnp.logical_and(c < 3 * ATT_BLOCK, dist <= ATT_BLOCK)
    meta_col = jnp.logical_and(c >= 3 * ATT_BLOCK, c < 3 * ATT_BLOCK + N_META)
    sink_col = c == 3 * ATT_BLOCK + N_META
    meta_dist = N_META + i - (c - 3 * ATT_BLOCK)
    tables = []
    for drop in (None, 0, 2):
        ok = in_band if drop is None else jnp.logical_and(in_band, key_blk != drop)
        d_eff = jnp.where(ok, dist, jnp.where(meta_col, meta_dist, 0)).astype(F32)
        live = jnp.logical_or(ok, meta_col)
        tab = jnp.where(live[None], -slopes[:, None, None] * d_eff[None], MASK_VALUE)
        tables.append(jnp.where(sink_col[None], sink.astype(F32)[:, None, None], tab))
    bias = jnp.stack(tables, axis=0).reshape(3, groups, GROUP * ATT_BLOCK, 4 * ATT_BLOCK)
    lane = jnp.arange(LANE_TILE)[None, :]
    shift = jnp.where(lane < N_META, jnp.repeat(slopes * ATT_BLOCK, ATT_BLOCK)[:, None], 0.0)
    return bias, shift.reshape(groups, GROUP * ATT_BLOCK, LANE_TILE)


def _attention(qvk_r, kt_r, qvk_m, kt_m, sink, *, b, s, n_heads):
    groups = n_heads // GROUP
    nblk = s // ATT_BLOCK
    qd = n_heads * HEAD_DIM
    width = qvk_r.shape[1]
    vkw = groups * 2 * HEAD_DIM
    vkb = qd // vkw
    bias, shift = _attention_tables(n_heads, sink)

    def variant(j):
        return jnp.where(j == 0, 1, jnp.where(j == nblk - 1, 2, 0))

    prev = lambda i, j: i * nblk + jnp.maximum(j - 1, 0)
    nxt = lambda i, j: i * nblk + jnp.minimum(j + 1, nblk - 1)
    return pl.pallas_call(
        functools.partial(_attn_kernel, groups=groups),
        grid=(b, nblk),
        in_specs=[pl.BlockSpec((ATT_BLOCK, width), lambda i, j: (i * nblk + j, 0)),
                  pl.BlockSpec((ATT_BLOCK, vkw), lambda i, j: (prev(i, j), vkb)),
                  pl.BlockSpec((ATT_BLOCK, vkw), lambda i, j: (nxt(i, j), vkb)),
                  pl.BlockSpec((N_META, vkw), lambda i, j: (i, vkb)),
                  pl.BlockSpec((groups * HEAD_DIM, ATT_BLOCK), lambda i, j: (0, prev(i, j))),
                  pl.BlockSpec((groups * HEAD_DIM, ATT_BLOCK), lambda i, j: (0, i * nblk + j)),
                  pl.BlockSpec((groups * HEAD_DIM, ATT_BLOCK), lambda i, j: (0, nxt(i, j))),
                  pl.BlockSpec((None, groups * HEAD_DIM, LANE_TILE), lambda i, j: (i, 0, 0)),
                  pl.BlockSpec((None, groups, GROUP * ATT_BLOCK, 4 * ATT_BLOCK), lambda i, j: (variant(j), 0, 0, 0)),
                  pl.BlockSpec((groups, GROUP * ATT_BLOCK, LANE_TILE), lambda i, j: (0, 0, 0))],
        out_specs=pl.BlockSpec((ATT_BLOCK, qd), lambda i, j: (i * nblk + j, 0)),
        out_shape=jax.ShapeDtypeStruct((b * s, qd), BF16),
        compiler_params=_cparams(("parallel", "arbitrary")),
        name="window_attention",
    )(qvk_r, qvk_r, qvk_r, qvk_m, kt_r, kt_r, kt_r, kt_m, bias, shift)


def _fft_split(s):
    n2 = 128 if s >= 1024 else 32
    return (2 * s) // n2, n2


def _fused_conv_vmem_bytes(s, n1, n2):
    lane_bytes = LANE_TILE * 4
    spectrum = 3 * n1 * n2 * lane_bytes
    tables = 2 * n2 * 2 * n1 * 2 * (n1 // 2 + 8) * 2
    sequence = (2 * s + 2 * 8 * n2 + s + N_META + 16) * lane_bytes
    blocks = 4 * 2 * 2 * (s + N_META) * LANE_TILE * 2
    return spectrum + tables + sequence + blocks


def _hyena_conv(xr, xm, col_off, conv_w, conv_b, kre, kim, order, tab, gr, gm, gate_off, skip, tail, *, dims):
    b, s, d, n1, n2 = dims
    if _fused_conv_vmem_bytes(s, n1, n2) <= (VMEM_LIMIT * 7) // 8:
        outs = _fused_conv(xr, xm, col_off, conv_w, conv_b, kre, kim, order, tab, gr, gm, gate_off, skip, tail,
                           b=b, s=s, d=d, n1=n1, n2=n2)
        return outs[0], outs[1]
    kw = dict(b=b, s=s, d=d, n1=n1, n2=n2, dt=LANE_TILE)
    if col_off is not None:
        are, aim, ur4, um4 = _fft_in(xr, xm, col_off, conv_w, conv_b, tab, **kw)
        ur, um = ur4.reshape(b * s, d), um4.reshape(b * N_META, d)
    else:
        are, aim = _fft_in(xr, xm, 0, None, None, tab, **kw)
        ur, um = xr, xm
    bre, bim = _fft_mid(are, aim, kre, kim, order, tab, dt=min(d, FFT_MID_LANES), slabs=FUSED_SLABS)
    return _fft_out(bre, bim, tab, gr, gm, gate_off, conv_w, conv_b, ur, um, skip, tail, **kw)


def _hyena_layer(streams, fp, g_mix, w_in, conv_w, conv_b, skip, tm):
    outs = []
    d = w_in.shape[0]
    for st in streams:
        b, s = st["b"], st["s"]
        n1, n2 = _fft_split(s)
        seq_len = s + N_META
        n = 2 * s
        tab = _fft_tables(n1, n2)
        r = jnp.arange(n, dtype=jnp.int32)
        tr = min(s, 512)
        circ = _filter_rows(jnp.where(r < s, r, n - r), s // tr, tr, seq_len, fp, d)
        kre, kim = _filter_spectrum(circ, tab, n1=n1, n2=n2, dt=min(d, FFT_MID_LANES), slabs=FUSED_SLABS)
        a = jnp.arange(16, dtype=jnp.int32)
        tail = _filter_rows(jnp.concatenate([s + a, s - a, s - a, s + a]), 1, 32, seq_len, fp, d)
        tail = jnp.concatenate([tail[:, 0:16], tail[:, 32:48], tail[:, 48:64], tail[:, 16:32]], axis=1)
        pr = _norm_matmul(st["hr"], g_mix, w_in, tm)
        pm = _norm_matmul(st["hm"], g_mix, w_in, st["hm"].shape[0])
        dims = (b, s, d, n1, n2)
        z1r, z1m = _hyena_conv(pr, pm, 0, conv_w, conv_b, kre, kim, 0, tab, pr, pm, d, skip[0], tail[0], dims=dims)
        z2r, z2m = _hyena_conv(z1r, z1m, None, conv_w, conv_b, kre, kim, 1, tab, pr, pm, 2 * d, skip[1], tail[1],
                               dims=dims)
        outs.append((z2r, z2m))
    return outs


def _attention_weights(w_qkv, n_heads):
    groups = n_heads // GROUP
    qd = n_heads * HEAD_DIM
    kd = groups * HEAD_DIM
    w_q = w_qkv[:, :qd] * (HEAD_DIM ** -0.5)
    w_k = w_qkv[:, qd:qd + kd]
    w_v = w_qkv[:, qd + kd:]
    d = w_qkv.shape[0]
    w_vk = jnp.stack([w_v.reshape(d, groups, HEAD_DIM), w_k.reshape(d, groups, HEAD_DIM)], axis=2)
    w_rows = jnp.concatenate([w_q, w_vk.reshape(d, 2 * kd)], axis=1).astype(BF16)
    return w_rows, w_k.T.astype(BF16)


def _meta_keys_transposed(qvk_m, b, n_heads):
    groups = n_heads // GROUP
    qd = n_heads * HEAD_DIM
    k_m = qvk_m[:, qd:].reshape(b, N_META, groups, 2, HEAD_DIM)[:, :, :, 1, :]
    kt = jnp.transpose(k_m.reshape(b, N_META, groups * HEAD_DIM), (0, 2, 1))
    return jnp.pad(kt, ((0, 0), (0, 0), (0, LANE_TILE - N_META)))


def _encoder_pair(x_prompt, x_sample, meta_tokens, norm_mix, norm_mlp, norm_final,
                  hy_w_in, hy_conv_w, hy_conv_b, fps, hy_skip, hy_w_out, hy_b_out,
                  at_w_qkv, at_sink, at_w_o, mlp_w1, mlp_w2, *, n_heads, tm):
    d = x_prompt.shape[-1]
    streams = []
    for x in (x_prompt, x_sample):
        b, s, _ = x.shape
        streams.append(dict(b=b, s=s, hr=x.reshape(b * s, d),
                            hm=jnp.tile(meta_tokens.astype(F32), (b, 1))))
    zeros_d = jnp.zeros((d,), F32)

    conv_w = jnp.pad(hy_conv_w[0], ((0, 5), (0, 0)))
    conv_b = hy_conv_b[0][None, :]
    zs = _hyena_layer(streams, fps[0], norm_mix[0], hy_w_in[0].astype(BF16), conv_w, conv_b, hy_skip[0], tm)
    w_out = hy_w_out[0].astype(BF16)
    w1 = [w.astype(BF16) for w in mlp_w1]
    w2 = [w.astype(BF16) for w in mlp_w2]
    for st, (zr, zm) in zip(streams, zs):
        st["hr"] = _mixer_out_mlp(st["hr"], zr, w_out, hy_b_out[0], norm_mlp[0], w1[0], w2[0], zeros_d, tm, False)
        st["hm"] = _mixer_out_mlp(st["hm"], zm, w_out, hy_b_out[0], norm_mlp[0], w1[0], w2[0], zeros_d,
                                  st["hm"].shape[0], False)

    w_rows, w_kt = _attention_weights(at_w_qkv[0], n_heads)
    w_o = at_w_o[0].astype(BF16)
    outs = []
    for st in streams:
        qvk_r, kt_r = _norm_matmul(st["hr"], norm_mix[1], w_rows, tm, wt=w_kt)
        qvk_m = _norm_matmul(st["hm"], norm_mix[1], w_rows, st["hm"].shape[0])
        kt_m = _meta_keys_transposed(qvk_m, st["b"], n_heads)
        att = _attention(qvk_r, kt_r, qvk_m, kt_m, at_sink[0], b=st["b"], s=st["s"], n_heads=n_heads)
        y = _mixer_out_mlp(st["hr"], att, w_o, zeros_d, norm_mlp[1], w1[1], w2[1], norm_final, tm, True)
        outs.append(y.reshape(st["b"], st["s"], d))
    return tuple(outs)


def kernel(x_prompt, x_sample, meta_tokens, norm_mix, norm_mlp, norm_final, hy_w_in, hy_conv_w, hy_conv_b,
           hy_f_w1, hy_f_b1, hy_f_w2, hy_f_b2, hy_f_w3, hy_f_b3, hy_f_wout, hy_f_freq, hy_skip, hy_w_out,
           hy_b_out, at_w_qkv, at_sink, at_w_o, mlp_w1, mlp_w2):
    fps = [dict(w1=hy_f_w1[j], b1=hy_f_b1[j], w2=hy_f_w2[j], b2=hy_f_b2[j], w3=hy_f_w3[j], b3=hy_f_b3[j],
                wout=hy_f_wout[j], freq=hy_f_freq[j]) for j in range(hy_f_w1.shape[0])]
    n_heads = at_sink.shape[1]
    return _encoder_pair(x_prompt, x_sample, meta_tokens, norm_mix, norm_mlp, norm_final,
                         hy_w_in, hy_conv_w, hy_conv_b, fps, hy_skip, hy_w_out, hy_b_out,
                         at_w_qkv, at_sink, at_w_o, mlp_w1, mlp_w2, n_heads=n_heads, tm=512)
```

```python
import functools
import math

import jax
import jax.numpy as jnp
from jax import lax
from jax.experimental import pallas as pl
from jax.experimental.pallas import tpu as pltpu

F32 = jnp.float32
BF16 = jnp.bfloat16

N_META = 16
RMS_EPS = 1e-6
HY_BANDS = 16
HY_EMB_PAD = 40
HY_FAST_DECAY = 0.3
HY_SLOW_DECAY = 1.5
HY_DECAY_TARGET = 1e-2
ATT_BLOCK = 128
HEAD_DIM = 64
GROUP = 4
MASK_VALUE = -1e30
FF_CHUNK = 1024
LANE_TILE = 128
N2_CHUNK = 32
FUSED_SLABS = 16
FFT_MID_LANES = 256
VMEM_LIMIT = 56 * 1024 * 1024
HIGHEST = lax.Precision.HIGHEST


def _cparams(sem):
    return pltpu.CompilerParams(dimension_semantics=sem, vmem_limit_bytes=VMEM_LIMIT)


def _rms(x, g):
    return x * lax.rsqrt(jnp.mean(x * x, axis=-1, keepdims=True) + RMS_EPS) * g


def _norm_matmul_kernel(x_ref, g_ref, w_ref, *rest):
    u = _rms(x_ref[...], g_ref[...]).astype(BF16)
    if len(rest) == 1:
        (o_ref,) = rest
    else:
        wt_ref, o_ref, ot_ref = rest
        ot_ref[...] = lax.dot_general(wt_ref[...], u, (((1,), (1,)), ((), ())),
                                      preferred_element_type=F32).astype(ot_ref.dtype)
    o_ref[...] = jnp.dot(u, w_ref[...], preferred_element_type=F32).astype(o_ref.dtype)


def _norm_matmul(x, g, w, tm, wt=None):
    rows, d = x.shape
    n = w.shape[1]
    in_specs = [pl.BlockSpec((tm, d), lambda i: (i, 0)),
                pl.BlockSpec((1, d), lambda i: (0, 0)),
                pl.BlockSpec((d, n), lambda i: (0, 0))]
    out_specs = pl.BlockSpec((tm, n), lambda i: (i, 0))
    out_shape = jax.ShapeDtypeStruct((rows, n), BF16)
    args = [x, g.reshape(1, d), w]
    if wt is not None:
        m = wt.shape[0]
        in_specs.append(pl.BlockSpec((m, d), lambda i: (0, 0)))
        out_specs = [out_specs, pl.BlockSpec((m, tm), lambda i: (0, i))]
        out_shape = [out_shape, jax.ShapeDtypeStruct((m, rows), BF16)]
        args.append(wt)
    return pl.pallas_call(
        _norm_matmul_kernel,
        grid=(rows // tm,),
        in_specs=in_specs, out_specs=out_specs, out_shape=out_shape,
        compiler_params=_cparams(("parallel",)),
        name="norm_matmul",
    )(*args)


def _mixer_out_mlp_kernel(h_ref, z_ref, wp_ref, bp_ref, g_ref, w1_ref, w2_ref, gf_ref, o_ref, *, final_norm):
    h = h_ref[...] + jnp.dot(z_ref[...], wp_ref[...], preferred_element_type=F32) + bp_ref[...]
    u = _rms(h, g_ref[...]).astype(BF16)
    acc = h
    d_ff = w1_ref.shape[1]
    for c in range(d_ff // FF_CHUNK):
        a = jnp.dot(u, w1_ref[:, c * FF_CHUNK:(c + 1) * FF_CHUNK], preferred_element_type=F32)
        a = jnp.square(jnp.maximum(a, 0.0)).astype(BF16)
        acc = acc + jnp.dot(a, w2_ref[c * FF_CHUNK:(c + 1) * FF_CHUNK, :], preferred_element_type=F32)
    if final_norm:
        acc = _rms(acc, gf_ref[...])
    o_ref[...] = acc


def _mixer_out_mlp(h, z, wp, bp, g, w1, w2, gf, tm, final_norm):
    rows, d = h.shape
    dz = z.shape[1]
    d_ff = w1.shape[1]
    const = lambda i: (0, 0)
    return pl.pallas_call(
        functools.partial(_mixer_out_mlp_kernel, final_norm=final_norm),
        grid=(rows // tm,),
        in_specs=[pl.BlockSpec((tm, d), lambda i: (i, 0)),
                  pl.BlockSpec((tm, dz), lambda i: (i, 0)),
                  pl.BlockSpec((dz, d), const),
                  pl.BlockSpec((1, d), const),
                  pl.BlockSpec((1, d), const),
                  pl.BlockSpec((d, d_ff), const),
                  pl.BlockSpec((d_ff, d), const),
                  pl.BlockSpec((1, d), const)],
        out_specs=pl.BlockSpec((tm, d), lambda i: (i, 0)),
        out_shape=jax.ShapeDtypeStruct((rows, d), F32),
        compiler_params=_cparams(("parallel",)),
        name="mixer_out_mlp",
    )(h, z, wp, bp.reshape(1, d), g.reshape(1, d), w1, w2, gf.reshape(1, d))


def _split_bf16(x):
    hi = x.astype(BF16)
    return hi, (x - hi.astype(F32)).astype(BF16)


def _dot3(a_hi, a_lo, b_hi, b_lo):
    dot = functools.partial(jnp.dot, preferred_element_type=F32)
    return dot(a_hi, b_hi) + (dot(a_hi, b_lo) + dot(a_lo, b_hi))


def _filter_kernel(z_ref, t_ref, w1_ref, b1_ref, w2_ref, b2_ref, w3_ref, b3_ref, fr_ref, woh_ref, wol_ref,
                   ad_ref, o_ref):
    d = ad_ref.shape[1]
    dot = functools.partial(jnp.dot, precision=HIGHEST, preferred_element_type=F32)
    fr = fr_ref[...]
    h = jnp.sin(fr * (dot(z_ref[...], w1_ref[...]) + b1_ref[...]))
    h = jnp.sin(fr * (dot(h, w2_ref[...]) + b2_ref[...]))
    h = jnp.sin(fr * (dot(h, w3_ref[...]) + b3_ref[...]))
    ho = _dot3(*_split_bf16(h), woh_ref[...], wol_ref[...])
    decay = jnp.exp(-t_ref[...] * ad_ref[...])
    for o in range(2):
        o_ref[o] = ho[:, o * d:(o + 1) * d] * decay


def _filter_rows(lag, first_bwd_tile, tr, seq_len, fp, d):
    rows = lag.shape[0]
    lagf = lag.astype(F32)
    t = lagf / (seq_len - 1)
    w = 2.0 * math.pi * lagf / seq_len
    f = jnp.linspace(1e-4, HY_BANDS - 1, HY_BANDS, dtype=F32)[None, :]
    z = jnp.concatenate([t[:, None], jnp.cos(f * w[:, None]), -jnp.sin(f * w[:, None]),
                         jnp.zeros((rows, HY_EMB_PAD - 2 * HY_BANDS - 1), F32)], axis=-1)
    w1 = jnp.pad(fp["w1"], ((0, HY_EMB_PAD - fp["w1"].shape[0]), (0, 0)))
    hid = w1.shape[1]
    max_decay = math.log(HY_DECAY_TARGET) / HY_FAST_DECAY
    min_decay = math.log(HY_DECAY_TARGET) / HY_SLOW_DECAY
    adel = jnp.abs(jnp.linspace(min_decay, max_decay, d, dtype=F32))[None, :]
    const = lambda i: (0, 0)
    row = lambda i: (i, 0)
    wo = jnp.transpose(fp["wout"].reshape(hid, 2, 2, d), (2, 0, 1, 3)).reshape(2, hid, 2 * d)
    wo_hi, wo_lo = _split_bf16(wo)
    wo_spec = pl.BlockSpec((None, hid, 2 * d), lambda i: ((i >= first_bwd_tile).astype(jnp.int32), 0, 0))
    return pl.pallas_call(
        _filter_kernel,
        grid=(rows // tr,),
        in_specs=[pl.BlockSpec((tr, HY_EMB_PAD), row), pl.BlockSpec((tr, 1), row),
                  pl.BlockSpec((HY_EMB_PAD, hid), const), pl.BlockSpec((1, hid), const),
                  pl.BlockSpec((hid, hid), const), pl.BlockSpec((1, hid), const),
                  pl.BlockSpec((hid, hid), const), pl.BlockSpec((1, hid), const),
                  pl.BlockSpec((1, hid), const), wo_spec, wo_spec,
                  pl.BlockSpec((1, d), const)],
        out_specs=pl.BlockSpec((2, tr, d), lambda i: (0, i, 0)),
        out_shape=jax.ShapeDtypeStruct((2, rows, d), F32),
        compiler_params=_cparams(("parallel",)),
        name="hyena_filter",
    )(z, t[:, None], w1, fp["b1"][None], fp["w2"], fp["b2"][None],
      fp["w3"], fp["b3"][None], fp["freq"][None], wo_hi, wo_lo, adel)


def _cplx_block(re, im):
    return jnp.concatenate([jnp.concatenate([re, -im], axis=-1), jnp.concatenate([im, re], axis=-1)], axis=-2)


def _fft_tables(n1, n2):
    n = n1 * n2
    n1h = n1 // 2
    pad = 8 - 1
    k1 = jnp.arange(n1, dtype=jnp.int32)
    c2 = jnp.arange(n2, dtype=jnp.int32)
    cols = jnp.concatenate([jnp.arange(n1h, dtype=jnp.int32), jnp.array([n1 - 1], jnp.int32)])
    pos = n2 * cols[None, None, :] + c2[:, None, None]
    ang = ((k1[None, :, None] * pos) % n).astype(F32) * (-2.0 * math.pi / n)
    wr = jnp.pad(jnp.cos(ang), ((0, 0), (0, 0), (0, pad)))
    wi = jnp.pad(jnp.sin(ang), ((0, 0), (0, 0), (0, pad)))
    fwd1 = _cplx_block(wr, wi)
    inv1 = _cplx_block(jnp.swapaxes(wr, 1, 2), -jnp.swapaxes(wi, 1, 2)) / n
    posf = n2 * k1[None, None, :] + c2[:, None, None]
    angf = ((k1[None, :, None] * posf) % n).astype(F32) * (-2.0 * math.pi / n)
    fil1 = jnp.concatenate([jnp.cos(angf), jnp.sin(angf)], axis=1)
    ang2 = ((c2[:, None] * c2[None, :]) % n2).astype(F32) * (-2.0 * math.pi / n2)
    fr, fi = jnp.cos(ang2), jnp.sin(ang2)
    fwd2 = _cplx_block(fr, fi)
    return dict(fwd1=fwd1.astype(BF16), inv1=inv1.astype(BF16), fil1=fil1.astype(BF16),
                fwd2=fwd2.astype(BF16), inv2=_cplx_block(fr, -fi).astype(BF16))


def _fill_sequence(meta_ref, real_ref, seq_ref):
    s, dt = real_ref.shape
    seq_ref[0:8, :] = jnp.zeros((8, dt), F32)
    seq_ref[8:8 + N_META, :] = meta_ref[...].astype(F32)
    seq_ref[8 + N_META:8 + N_META + s, :] = real_ref[...].astype(F32)
    seq_ref[8 + N_META + s:16 + N_META + s, :] = jnp.zeros((8, dt), F32)


def _short_conv_rows(seq_ref, cw, cb, start, rows):
    return (seq_ref[pl.ds(start - 1, rows), :] * cw[0:1] + seq_ref[pl.ds(start, rows), :] * cw[1:2]
            + seq_ref[pl.ds(start + 1, rows), :] * cw[2:3] + cb)


def _row_chunk(s):
    return min(s, 1024)


def _load_sequences(xr_ref, xm_ref, conv_refs, ut_ref, xs_ref, mp_ref, seq_ref, n2):
    n1h, dt = xs_ref.shape[1], xs_ref.shape[3]
    s = n1h * n2
    ch = _row_chunk(s)
    for e in range(2):
        if conv_refs is not None:
            cw, cb = conv_refs[0][...].astype(F32), conv_refs[1][...].astype(F32)
            _fill_sequence(xm_ref.at[e], xr_ref.at[e], seq_ref)
            meta = _short_conv_rows(seq_ref, cw, cb, 8, N_META)
        else:
            meta = xm_ref[e].astype(F32)
        ut_ref[e, 0:N_META, :] = meta
        for c in range(s // ch):
            if conv_refs is not None:
                real = _short_conv_rows(seq_ref, cw, cb, 8 + N_META + c * ch, ch)
            else:
                real = xr_ref[e, c * ch:(c + 1) * ch, :].astype(F32)
            xs_ref[e, c * ch // n2:(c + 1) * ch // n2] = real.reshape(ch // n2, n2, dt)
        ut_ref[e, N_META:2 * N_META, :] = xs_ref[e, n1h - 1, n2 - 16:n2, :]
        mp_ref[e] = jnp.zeros(mp_ref.shape[1:], F32)
        for j in range(N_META):
            mp_ref[e, pl.ds(8 * (n2 - N_META + j), 1), :] = meta[j:j + 1]


def _gate_sequences(xs_ref, mp_ref, g_refs, conv_refs, ut_ref, tail_ref, z_refs, seq_ref, n2):
    n1h, dt = xs_ref.shape[1], xs_ref.shape[3]
    s = n1h * n2
    ch = _row_chunk(s)
    gr_ref, gm_ref = g_refs
    zr_ref, zm_ref = z_refs
    cw, cb = conv_refs[0][...].astype(F32), conv_refs[1][...].astype(F32)
    tail = tail_ref[...]
    for e in range(2):
        y_meta = jnp.concatenate(
            [mp_ref[e, pl.ds(8 * (n2 - N_META + j), 1), :] for j in range(N_META)], axis=0)
        real_fix, meta_fix = _alias_patch(tail, ut_ref[e, 0:N_META, :], ut_ref[e, N_META:2 * N_META, :])
        xs_ref[e, n1h - 1, n2 - 16:n2, :] = xs_ref[e, n1h - 1, n2 - 16:n2, :] + real_fix
        _fill_sequence(gm_ref.at[e], gr_ref.at[e], seq_ref)
        g_meta = _short_conv_rows(seq_ref, cw, cb, 8, N_META)
        zm_ref[e] = (g_meta * (y_meta + meta_fix)).astype(zm_ref.dtype)
        for c in range(s // ch):
            g = _short_conv_rows(seq_ref, cw, cb, 8 + N_META + c * ch, ch)
            y = xs_ref[e, c * ch // n2:(c + 1) * ch // n2].reshape(ch, dt)
            zr_ref[e, c * ch:(c + 1) * ch, :] = (g * y).astype(zr_ref.dtype)


def _grid_cols(s):
    return N2_CHUNK // 2 if s > 4096 else N2_CHUNK


def _fft_in_kernel(*refs, n1, n2, cols, short_conv):
    if short_conv:
        xr_ref, xm_ref, cw_ref, cb_ref, f_ref, are_ref, aim_ref, ut_ref, xs_ref, mp_ref, seq_ref = refs
    else:
        xr_ref, xm_ref, f_ref, are_ref, aim_ref, ut_ref, xs_ref, mp_ref = refs
    n1h = n1 // 2
    chunk = pl.program_id(2)

    @pl.when(chunk == 0)
    def _prepare():
        if short_conv:
            _load_sequences(xr_ref, xm_ref, (cw_ref, cb_ref), ut_ref, xs_ref, mp_ref, seq_ref, n2)
        else:
            _load_sequences(xr_ref, xm_ref, None, ut_ref, xs_ref, mp_ref, None, n2)

    base = pl.multiple_of(chunk * cols, cols)
    xt = [jnp.swapaxes(xs_ref[e, :, pl.ds(base, cols), :], 0, 1) for e in range(2)]
    for i in range(cols):
        parts = []
        for e in range(2):
            parts += [xt[e][i], mp_ref[e, pl.ds(pl.multiple_of((base + i) * 8, 8), 8), :]]
        rhs = jnp.concatenate(parts, axis=0).astype(BF16)
        out = jnp.dot(f_ref[i], rhs, preferred_element_type=F32)
        are_ref[i] = out[:n1]
        aim_ref[i] = out[n1:]


def _fft_in(xr, xm, col_off, conv_w, conv_b, tab, *, b, s, d, n1, n2, dt):
    short_conv = conv_w is not None
    pairs = b // 2
    c = xr.shape[1]
    cb0 = col_off // dt
    cols = _grid_cols(s)
    xr4 = xr.reshape(pairs, 2, s, c)
    xm4 = xm.reshape(pairs, 2, N_META, c)
    kk = tab["fwd1"].shape[2]
    in_specs = [pl.BlockSpec((None, 2, s, dt), lambda p, j, t: (p, 0, 0, cb0 + j)),
                pl.BlockSpec((None, 2, N_META, dt), lambda p, j, t: (p, 0, 0, cb0 + j))]
    args = [xr4, xm4]
    if short_conv:
        in_specs += [pl.BlockSpec((8, dt), lambda p, j, t: (0, cb0 + j)),
                     pl.BlockSpec((1, dt), lambda p, j, t: (0, cb0 + j))]
        args += [conv_w, conv_b]
    in_specs.append(pl.BlockSpec((cols, 2 * n1, kk), lambda p, j, t: (t, 0, 0)))
    args.append(tab["fwd1"])
    a_spec = pl.BlockSpec((None, None, cols, n1, dt), lambda p, j, t: (p, j, t, 0, 0))
    a_shape = jax.ShapeDtypeStruct((pairs, d // dt, n2, n1, dt), F32)
    out_specs = [a_spec, a_spec, pl.BlockSpec((None, 2, 2 * N_META, dt), lambda p, j, t: (p, 0, 0, j))]
    out_shape = [a_shape, a_shape, jax.ShapeDtypeStruct((pairs, 2, 2 * N_META, d), F32)]
    scratch = [pltpu.VMEM((2, n1 // 2, n2, dt), F32), pltpu.VMEM((2, 8 * n2, dt), F32)]
    if short_conv:
        scratch.append(pltpu.VMEM((s + N_META + 16, dt), F32))
    return pl.pallas_call(
        functools.partial(_fft_in_kernel, n1=n1, n2=n2, cols=cols, short_conv=short_conv),
        grid=(pairs, d // dt, n2 // cols),
        in_specs=in_specs, out_specs=out_specs, out_shape=out_shape, scratch_shapes=scratch,
        compiler_params=_cparams(("parallel", "parallel", "arbitrary")),
        name="fft_in",
    )(*args)


def _fft_mid_kernel(are_ref, aim_ref, kre_ref, kim_ref, f_ref, g_ref, bre_ref, bim_ref, *, n2, slabs):
    tiles = are_ref.shape[0]
    lt = are_ref.shape[3]
    xr_blk = [jnp.swapaxes(are_ref[h], 0, 1) for h in range(tiles)]
    xi_blk = [jnp.swapaxes(aim_ref[h], 0, 1) for h in range(tiles)]
    ys = []
    for i in range(slabs):
        x = jnp.concatenate([jnp.concatenate([xr_blk[h][i], xi_blk[h][i]], axis=0) for h in range(tiles)],
                            axis=1).astype(BF16)
        z = jnp.dot(f_ref[...], x, preferred_element_type=F32)
        zr, zi = z[:n2], z[n2:]
        kr = jnp.concatenate([kre_ref[h, i] for h in range(tiles)], axis=1)
        ki = jnp.concatenate([kim_ref[h, i] for h in range(tiles)], axis=1)
        p = jnp.concatenate([zr * kr - zi * ki, zr * ki + zi * kr], axis=0).astype(BF16)
        ys.append(jnp.dot(g_ref[...], p, preferred_element_type=F32))
    yt = jnp.swapaxes(jnp.stack(ys, axis=0), 0, 1)
    for h in range(tiles):
        bre_ref[h] = yt[:n2, :, h * lt:(h + 1) * lt]
        bim_ref[h] = yt[n2:, :, h * lt:(h + 1) * lt]


def _fft_mid(are, aim, kre, kim, order, tab, *, dt, slabs):
    pairs, ntiles, n2, n1, lt = are.shape
    d = ntiles * lt
    blk = (None, dt // lt, n2, slabs, lt)
    amap = lambda j, kb, p: (p, j, 0, kb, 0)
    kmap = lambda j, kb, p: (order, j, kb, 0, 0)
    const = lambda j, kb, p: (0, 0)
    shp = jax.ShapeDtypeStruct(are.shape, F32)
    return pl.pallas_call(
        functools.partial(_fft_mid_kernel, n2=n2, slabs=slabs),
        grid=(d // dt, n1 // slabs, pairs),
        in_specs=[pl.BlockSpec(blk, amap), pl.BlockSpec(blk, amap),
                  pl.BlockSpec((None, dt // lt, slabs, n2, lt), kmap),
                  pl.BlockSpec((None, dt // lt, slabs, n2, lt), kmap),
                  pl.BlockSpec((2 * n2, 2 * n2), const), pl.BlockSpec((2 * n2, 2 * n2), const)],
        out_specs=[pl.BlockSpec(blk, amap), pl.BlockSpec(blk, amap)],
        out_shape=[shp, shp],
        compiler_params=_cparams(("parallel", "parallel", "arbitrary")),
        name="fft_mid",
    )(are, aim, kre, kim, tab["fwd2"], tab["inv2"])


def _filter_fft_in_kernel(c_ref, f_ref, are_ref, aim_ref, *, n1, n2):
    base = pl.multiple_of(pl.program_id(2) * N2_CHUNK, N2_CHUNK)
    ct = jnp.swapaxes(c_ref[:, pl.ds(base, N2_CHUNK), :], 0, 1).astype(BF16)
    for i in range(N2_CHUNK):
        out = jnp.dot(f_ref[i], ct[i], preferred_element_type=F32)
        are_ref[i] = out[:n1]
        aim_ref[i] = out[n1:]


def _filter_fft_mid_kernel(are_ref, aim_ref, f_ref, kre_ref, kim_ref, *, n2, slabs):
    tiles = are_ref.shape[0]
    lt = are_ref.shape[3]
    xr_blk = [jnp.swapaxes(are_ref[h], 0, 1) for h in range(tiles)]
    xi_blk = [jnp.swapaxes(aim_ref[h], 0, 1) for h in range(tiles)]
    for i in range(slabs):
        x = jnp.concatenate([jnp.concatenate([xr_blk[h][i], xi_blk[h][i]], axis=0) for h in range(tiles)], axis=1)
        z = jnp.dot(f_ref[...], x.astype(BF16), preferred_element_type=F32)
        for h in range(tiles):
            kre_ref[h, i] = z[:n2, h * lt:(h + 1) * lt]
            kim_ref[h, i] = z[n2:, h * lt:(h + 1) * lt]


def _filter_spectrum(circ, tab, *, n1, n2, dt, slabs):
    orders, n, d = circ.shape
    lt = LANE_TILE
    shp = [jax.ShapeDtypeStruct((orders, d // lt, n2, n1, lt), F32)] * 2
    tab_spec = pl.BlockSpec((N2_CHUNK, 2 * n1, n1), lambda o, j, t: (t, 0, 0))
    are, aim = pl.pallas_call(
        functools.partial(_filter_fft_in_kernel, n1=n1, n2=n2),
        grid=(orders, d // lt, n2 // N2_CHUNK),
        in_specs=[pl.BlockSpec((None, n1, n2, lt), lambda o, j, t: (o, 0, 0, j)), tab_spec],
        out_specs=[pl.BlockSpec((None, None, N2_CHUNK, n1, lt), lambda o, j, t: (o, j, t, 0, 0))] * 2,
        out_shape=shp,
        compiler_params=_cparams(("parallel", "parallel", "arbitrary")),
        name="filter_fft_in",
    )(circ.reshape(orders, n1, n2, d), tab["fil1"])
    blk = (None, dt // lt, n2, slabs, lt)
    amap = lambda o, kb, j: (o, j, 0, kb, 0)
    f_spec = pl.BlockSpec((2 * n2, 2 * n2), lambda o, kb, j: (0, 0))
    return pl.pallas_call(
        functools.partial(_filter_fft_mid_kernel, n2=n2, slabs=slabs),
        grid=(orders, n1 // slabs, d // dt),
        in_specs=[pl.BlockSpec(blk, amap), pl.BlockSpec(blk, amap), f_spec],
        out_specs=[pl.BlockSpec((None, dt // lt, slabs, n2, lt), lambda o, kb, j: (o, j, kb, 0, 0))] * 2,
        out_shape=[jax.ShapeDtypeStruct((orders, d // lt, n1, n2, lt), F32)] * 2,
        compiler_params=_cparams(("parallel", "parallel", "parallel")),
        name="filter_fft_mid",
    )(are, aim, tab["fwd2"])


def _alias_patch(tail, u_meta, u_last):
    dfw = tail[0:16] - tail[16:32]
    dbw = tail[32:48] - tail[48:64]
    ridx = lax.broadcasted_iota(jnp.int32, dfw.shape, 0)
    real_fix = jnp.zeros_like(dfw)
    meta_fix = jnp.zeros_like(dfw)
    for o in range(16):
        src = dfw if o == 0 else pltpu.roll(dfw, o, axis=0)
        real_fix = real_fix + jnp.where(ridx >= o, src, 0.0) * u_meta[o:o + 1]
    for c in range(1, 16):
        meta_fix = meta_fix + jnp.where(ridx + c <= 15, pltpu.roll(u_last, 16 - c, axis=0), 0.0) * dbw[c:c + 1]
    return real_fix, meta_fix


def _fft_out_kernel(bre_ref, bim_ref, g_ref, gr_ref, gm_ref, cw_ref, cb_ref, ut_ref, tail_ref, zr_ref, zm_ref,
                    ys_ref, yp_ref, seq_ref, *, n1, n2, cols):
    n1h = n1 // 2
    kk = n1h + 8
    s = n1h * n2
    chunk = pl.program_id(2)

    base = pl.multiple_of(chunk * cols, cols)
    ys = []
    for i in range(cols):
        rhs = jnp.concatenate([bre_ref[i], bim_ref[i]], axis=0).astype(BF16)
        y = jnp.dot(g_ref[i], rhs, preferred_element_type=F32)
        ys.append(y)
        for e in range(2):
            yp_ref[e, pl.ds(pl.multiple_of((base + i) * 8, 8), 8), :] = y[e * kk + n1h:(e + 1) * kk]
    yt = jnp.swapaxes(jnp.stack(ys, axis=0), 0, 1)
    for e in range(2):
        ys_ref[e, :, pl.ds(base, cols), :] = yt[e * kk:e * kk + n1h]

    @pl.when(chunk == n2 // cols - 1)
    def _gate():
        _gate_sequences(ys_ref, yp_ref, (gr_ref, gm_ref), (cw_ref, cb_ref), ut_ref, tail_ref, (zr_ref, zm_ref),
                        seq_ref, n2)


def _fft_out(bre, bim, tab, gr, gm, gate_off, conv_w, conv_b, ut, tail, *, b, s, d, n1, n2, dt):
    pairs = b // 2
    c = gr.shape[1]
    gb0 = gate_off // dt
    kk2 = tab["inv1"].shape[1]
    gr4 = gr.reshape(pairs, 2, s, c)
    gm4 = gm.reshape(pairs, 2, N_META, c)
    cols = _grid_cols(s)
    b_spec = pl.BlockSpec((None, None, cols, n1, dt), lambda p, j, t: (p, j, t, 0, 0))
    zr, zm = pl.pallas_call(
        functools.partial(_fft_out_kernel, n1=n1, n2=n2, cols=cols),
        grid=(pairs, d // dt, n2 // cols),
        in_specs=[b_spec, b_spec,
                  pl.BlockSpec((cols, kk2, 2 * n1), lambda p, j, t: (t, 0, 0)),
                  pl.BlockSpec((None, 2, s, dt), lambda p, j, t: (p, 0, 0, gb0 + j)),
                  pl.BlockSpec((None, 2, N_META, dt), lambda p, j, t: (p, 0, 0, gb0 + j)),
                  pl.BlockSpec((8, dt), lambda p, j, t: (0, gb0 + j)),
                  pl.BlockSpec((1, dt), lambda p, j, t: (0, gb0 + j)),
                  pl.BlockSpec((None, 2, 2 * N_META, dt), lambda p, j, t: (p, 0, 0, j)),
                  pl.BlockSpec((64, dt), lambda p, j, t: (0, j))],
        out_specs=[pl.BlockSpec((None, 2, s, dt), lambda p, j, t: (p, 0, 0, j)),
                   pl.BlockSpec((None, 2, N_META, dt), lambda p, j, t: (p, 0, 0, j))],
        out_shape=[jax.ShapeDtypeStruct((pairs, 2, s, d), BF16),
                   jax.ShapeDtypeStruct((pairs, 2, N_META, d), BF16)],
        scratch_shapes=[pltpu.VMEM((2, n1 // 2, n2, dt), F32), pltpu.VMEM((2, 8 * n2, dt), F32),
                        pltpu.VMEM((s + N_META + 16, dt), F32)],
        compiler_params=_cparams(("parallel", "parallel", "arbitrary")),
        name="fft_out",
    )(bre, bim, tab["inv1"], gr4, gm4, conv_w, conv_b, ut, tail)
    return zr.reshape(b * s, d), zm.reshape(b * N_META, d)


def _fused_conv_kernel(*refs, n1, n2, short_conv):
    if short_conv:
        (xr_ref, xm_ref, cwx_ref, cbx_ref, f1_ref, kre_ref, kim_ref, f2_ref, g2_ref, g1_ref, gr_ref, gm_ref,
         cwg_ref, cbg_ref, tail_ref, zr_ref, zm_ref, are_ref, aim_ref, xs_ref, mp_ref, ut_ref, seq_ref) = refs
        conv_x = (cwx_ref, cbx_ref)
    else:
        (xr_ref, xm_ref, f1_ref, kre_ref, kim_ref, f2_ref, g2_ref, g1_ref, gr_ref, gm_ref,
         cwg_ref, cbg_ref, tail_ref, zr_ref, zm_ref, are_ref, aim_ref, xs_ref, mp_ref, ut_ref, seq_ref) = refs
        conv_x = None
    n1h = n1 // 2
    kk = n1h + 8
    s = n1h * n2
    nkb = n1 // FUSED_SLABS
    dt = xs_ref.shape[3]

    _load_sequences(xr_ref, xm_ref, conv_x, ut_ref, xs_ref, mp_ref, seq_ref, n2)

    def stage1(t, carry):
        base = pl.multiple_of(t * N2_CHUNK, N2_CHUNK)
        xt = [jnp.swapaxes(xs_ref[e, :, pl.ds(base, N2_CHUNK), :], 0, 1) for e in range(2)]
        for i in range(N2_CHUNK):
            parts = []
            for e in range(2):
                parts += [xt[e][i], mp_ref[e, pl.ds(pl.multiple_of((base + i) * 8, 8), 8), :]]
            rhs = jnp.concatenate(parts, axis=0).astype(BF16)
            out = jnp.dot(f1_ref[base + i], rhs, preferred_element_type=F32).astype(BF16)
            are_ref[:, base + i] = out[:n1].reshape(nkb, FUSED_SLABS, dt)
            aim_ref[:, base + i] = out[n1:].reshape(nkb, FUSED_SLABS, dt)
        return carry

    def stage2(kb, carry):
        k0 = pl.multiple_of(kb * FUSED_SLABS, FUSED_SLABS)
        xr_blk = jnp.swapaxes(are_ref[kb], 0, 1)
        xi_blk = jnp.swapaxes(aim_ref[kb], 0, 1)
        ys = []
        for i in range(0, FUSED_SLABS, 2):
            x = jnp.concatenate([jnp.concatenate([xr_blk[i + h], xi_blk[i + h]], axis=0) for h in range(2)],
                                axis=1)
            z = jnp.dot(f2_ref[...], x, preferred_element_type=F32)
            zr, zi = z[:n2], z[n2:]
            kr = jnp.concatenate([kre_ref[k0 + i], kre_ref[k0 + i + 1]], axis=1)
            ki = jnp.concatenate([kim_ref[k0 + i], kim_ref[k0 + i + 1]], axis=1)
            p = jnp.concatenate([zr * kr - zi * ki, zr * ki + zi * kr], axis=0).astype(BF16)
            y = jnp.dot(g2_ref[...], p, preferred_element_type=F32).astype(BF16)
            ys += [y[:, :dt], y[:, dt:]]
        yt = jnp.swapaxes(jnp.stack(ys, axis=0), 0, 1)
        are_ref[kb] = yt[:n2]
        aim_ref[kb] = yt[n2:]
        return carry

    def stage3(t, carry):
        base = pl.multiple_of(t * N2_CHUNK, N2_CHUNK)
        ys = []
        for i in range(N2_CHUNK):
            rhs = jnp.concatenate([are_ref[:, base + i].reshape(n1, dt), aim_ref[:, base + i].reshape(n1, dt)],
                                  axis=0)
            y = jnp.dot(g1_ref[base + i], rhs, preferred_element_type=F32)
            ys.append(y)
            for e in range(2):
                mp_ref[e, pl.ds(pl.multiple_of((base + i) * 8, 8), 8), :] = y[e * kk + n1h:(e + 1) * kk]
        yt = jnp.swapaxes(jnp.stack(ys, axis=0), 0, 1)
        for e in range(2):
            xs_ref[e, :, pl.ds(base, N2_CHUNK), :] = yt[e * kk:e * kk + n1h]
        return carry

    lax.fori_loop(0, n2 // N2_CHUNK, stage1, 0)
    lax.fori_loop(0, nkb, stage2, 0)
    lax.fori_loop(0, n2 // N2_CHUNK, stage3, 0)

    _gate_sequences(xs_ref, mp_ref, (gr_ref, gm_ref), (cwg_ref, cbg_ref), ut_ref, tail_ref, (zr_ref, zm_ref),
                    seq_ref, n2)


def _fused_conv(xr, xm, col_off, conv_w, conv_b, kre, kim, order, tab, gr, gm, gate_off, tail, *,
                b, s, d, n1, n2):
    short_conv = col_off is not None
    pairs = b // 2
    dt = LANE_TILE
    cx0 = (col_off or 0) // dt
    cg0 = gate_off // dt
    kk2 = tab["fwd1"].shape[2]
    once = dict(pipeline_mode=pl.Buffered(1))
    seq4 = lambda a, rows: a.reshape(pairs, 2, rows, a.shape[1])
    x_spec = lambda rows, c0: pl.BlockSpec((None, 2, rows, dt), lambda j, p: (p, 0, 0, c0 + j))
    row_spec = lambda rows, c0: pl.BlockSpec((rows, dt), lambda j, p: (0, c0 + j))
    const2 = pl.BlockSpec((2 * n2, 2 * n2), lambda j, p: (0, 0), **once)
    k_spec = pl.BlockSpec((None, None, n1, n2, dt), lambda j, p: (order, j, 0, 0, 0), **once)
    in_specs = [x_spec(s, cx0), x_spec(N_META, cx0)]
    args = [seq4(xr, s), seq4(xm, N_META)]
    if short_conv:
        in_specs += [row_spec(8, cx0), row_spec(1, cx0)]
        args += [conv_w, conv_b]
    in_specs += [pl.BlockSpec((n2, 2 * n1, kk2), lambda j, p: (0, 0, 0), **once),
                 k_spec, k_spec, const2, const2,
                 pl.BlockSpec((n2, kk2, 2 * n1), lambda j, p: (0, 0, 0), **once),
                 x_spec(s, cg0), x_spec(N_META, cg0), row_spec(8, cg0), row_spec(1, cg0), row_spec(64, 0)]
    args += [tab["fwd1"], kre, kim, tab["fwd2"], tab["inv2"], tab["inv1"],
             seq4(gr, s), seq4(gm, N_META), conv_w, conv_b, tail]
    seq_out = lambda rows: pl.BlockSpec((None, 2, rows, dt), lambda j, p: (p, 0, 0, j))
    out_specs = [seq_out(s), seq_out(N_META)]
    out_shape = [jax.ShapeDtypeStruct((pairs, 2, s, d), BF16), jax.ShapeDtypeStruct((pairs, 2, N_META, d), BF16)]
    scratch = ([pltpu.VMEM((n1 // FUSED_SLABS, n2, FUSED_SLABS, dt), BF16)] * 2
               + [pltpu.VMEM((2, n1 // 2, n2, dt), F32), pltpu.VMEM((2, 8 * n2, dt), F32),
                  pltpu.VMEM((2, 2 * N_META, dt), F32), pltpu.VMEM((s + N_META + 16, dt), F32)])
    outs = pl.pallas_call(
        functools.partial(_fused_conv_kernel, n1=n1, n2=n2, short_conv=short_conv),
        grid=(d // dt, pairs),
        in_specs=in_specs, out_specs=out_specs, out_shape=out_shape, scratch_shapes=scratch,
        compiler_params=_cparams(("parallel", "parallel")),
        name="fused_conv",
    )(*args)
    return [o.reshape(-1, d) for o in outs]


def _attn_kernel(own_ref, vprev_ref, vnext_ref, vmeta_ref, ktp_ref, kto_ref, ktn_ref, ktm_ref,
                 bias_ref, shift_ref, o_ref, *, groups):
    blk = pl.program_id(1).astype(F32)
    gw = GROUP * HEAD_DIM
    qd = groups * gw
    lane = lax.broadcasted_iota(jnp.int32, (1, LANE_TILE), 1)
    low = lane < HEAD_DIM
    pad_rows = jnp.zeros((ATT_BLOCK - N_META, 2 * HEAD_DIM), BF16)
    zero = jnp.zeros((ATT_BLOCK, LANE_TILE), BF16)
    for g in range(groups):
        vk_own = own_ref[:, qd + g * LANE_TILE:qd + (g + 1) * LANE_TILE]
        grp = slice(g * LANE_TILE, (g + 1) * LANE_TILE)
        vk = jnp.concatenate([vprev_ref[:, grp], vk_own, vnext_ref[:, grp], vmeta_ref[:, grp], pad_rows],
                             axis=0)
        v_ones = jnp.where(low, vk, jnp.ones_like(vk))
        kt_rows = slice(g * HEAD_DIM, (g + 1) * HEAD_DIM)
        kt = jnp.concatenate([ktp_ref[kt_rows, :], kto_ref[kt_rows, :], ktn_ref[kt_rows, :], ktm_ref[kt_rows, :]],
                             axis=1)
        kt2 = jnp.concatenate([kt, kt], axis=0)
        parts = []
        for pr in range(GROUP // 2):
            qp = own_ref[:, g * gw + pr * LANE_TILE:g * gw + (pr + 1) * LANE_TILE]
            parts += [jnp.where(low, qp, zero), jnp.where(low, zero, qp)]
        q = jnp.concatenate(parts, axis=0)
        sc = jnp.dot(q, kt2, preferred_element_type=F32) + bias_ref[g]
        t = [sc[:, i * LANE_TILE:(i + 1) * LANE_TILE] for i in range(3)]
        t.append(sc[:, 3 * LANE_TILE:] - shift_ref[g] * blk)
        m = jnp.max(jnp.maximum(jnp.maximum(t[0], t[1]), jnp.maximum(t[2], t[3])), axis=1, keepdims=True)
        p = jnp.concatenate([jnp.exp(x - m) for x in t], axis=1).astype(BF16)
        oa = jnp.dot(p, v_ones, preferred_element_type=F32)
        ob = pltpu.roll(oa, HEAD_DIM, axis=1)
        outs = []
        for pr in range(GROUP // 2):
            ev = slice((2 * pr) * ATT_BLOCK, (2 * pr + 1) * ATT_BLOCK)
            od = slice((2 * pr + 1) * ATT_BLOCK, (2 * pr + 2) * ATT_BLOCK)
            outs.append(jnp.where(low, oa[ev] / ob[ev], ob[od] / oa[od]))
        o_ref[:, g * gw:(g + 1) * gw] = jnp.concatenate(outs, axis=1).astype(o_ref.dtype)


def _attention_tables(n_heads, sink):
    groups = n_heads // GROUP
    slopes = jnp.exp2(-8.0 * jnp.arange(1, n_heads + 1, dtype=F32) / n_heads)
    i = jnp.arange(ATT_BLOCK, dtype=jnp.int32)[:, None]
    c = jnp.arange(4 * ATT_BLOCK, dtype=jnp.int32)[None, :]
    dist = jnp.abs(c - ATT_BLOCK - i)
    key_blk = c // ATT_BLOCK
    in_band = jnp.logical_and(c < 3 * ATT_BLOCK, dist <= ATT_BLOCK)
    meta_col = jnp.logical_and(c >= 3 * ATT_BLOCK, c < 3 * ATT_BLOCK + N_META)
    sink_col = c == 3 * ATT_BLOCK + N_META
    meta_dist = N_META + i - (c - 3 * ATT_BLOCK)
    tables = []
    for drop in (None, 0, 2):
        ok = in_band if drop is None else jnp.logical_and(in_band, key_blk != drop)
        d_eff = jnp.where(ok, dist, jnp.where(meta_col, meta_dist, 0)).astype(F32)
        live = jnp.logical_or(ok, meta_col)
        tab = jnp.where(live[None], -slopes[:, None, None] * d_eff[None], MASK_VALUE)
        tables.append(jnp.where(sink_col[None], sink.astype(F32)[:, None, None], tab))
    bias = jnp.stack(tables, axis=0).reshape(3, groups, GROUP * ATT_BLOCK, 4 * ATT_BLOCK)
    lane = jnp.arange(LANE_TILE)[None, :]
    shift = jnp.where(lane < N_META, jnp.repeat(slopes * ATT_BLOCK, ATT_BLOCK)[:, None], 0.0)
    return bias, shift.reshape(groups, GROUP * ATT_BLOCK, LANE_TILE)


def _attention(qvk_r, kt_r, qvk_m, kt_m, sink, *, b, s, n_heads):
    groups = n_heads // GROUP
    nblk = s // ATT_BLOCK
    qd = n_heads * HEAD_DIM
    width = qvk_r.shape[1]
    vkw = groups * 2 * HEAD_DIM
    vkb = qd // vkw
    bias, shift = _attention_tables(n_heads, sink)

    def variant(j):
        return jnp.where(j == 0, 1, jnp.where(j == nblk - 1, 2, 0))

    prev = lambda i, j: i * nblk + jnp.maximum(j - 1, 0)
    nxt = lambda i, j: i * nblk + jnp.minimum(j + 1, nblk - 1)
    return pl.pallas_call(
        functools.partial(_attn_kernel, groups=groups),
        grid=(b, nblk),
        in_specs=[pl.BlockSpec((ATT_BLOCK, width), lambda i, j: (i * nblk + j, 0)),
                  pl.BlockSpec((ATT_BLOCK, vkw), lambda i, j: (prev(i, j), vkb)),
                  pl.BlockSpec((ATT_BLOCK, vkw), lambda i, j: (nxt(i, j), vkb)),
                  pl.BlockSpec((N_META, vkw), lambda i, j: (i, vkb)),
                  pl.BlockSpec((groups * HEAD_DIM, ATT_BLOCK), lambda i, j: (0, prev(i, j))),
                  pl.BlockSpec((groups * HEAD_DIM, ATT_BLOCK), lambda i, j: (0, i * nblk + j)),
                  pl.BlockSpec((groups * HEAD_DIM, ATT_BLOCK), lambda i, j: (0, nxt(i, j))),
                  pl.BlockSpec((None, groups * HEAD_DIM, LANE_TILE), lambda i, j: (i, 0, 0)),
                  pl.BlockSpec((None, groups, GROUP * ATT_BLOCK, 4 * ATT_BLOCK), lambda i, j: (variant(j), 0, 0, 0)),
                  pl.BlockSpec((groups, GROUP * ATT_BLOCK, LANE_TILE), lambda i, j: (0, 0, 0))],
        out_specs=pl.BlockSpec((ATT_BLOCK, qd), lambda i, j: (i * nblk + j, 0)),
        out_shape=jax.ShapeDtypeStruct((b * s, qd), BF16),
        compiler_params=_cparams(("parallel", "arbitrary")),
        name="window_attention",
    )(qvk_r, qvk_r, qvk_r, qvk_m, kt_r, kt_r, kt_r, kt_m, bias, shift)


def _fft_split(s):
    n2 = 128 if s >= 1024 else 32
    return (2 * s) // n2, n2


def _fused_conv_vmem_bytes(s, n1, n2):
    lane_bytes = LANE_TILE * 4
    spectrum = 3 * n1 * n2 * lane_bytes
    tables = 2 * n2 * 2 * n1 * 2 * (n1 // 2 + 8) * 2
    sequence = (2 * s + 2 * 8 * n2 + s + N_META + 16) * lane_bytes
    blocks = 3 * 2 * 2 * (s + N_META) * LANE_TILE * 2
    return spectrum + tables + sequence + blocks


def _hyena_conv(xr, xm, col_off, conv_w, conv_b, kre, kim, order, tab, gr, gm, gate_off, tail, *, dims):
    b, s, d, n1, n2 = dims
    if _fused_conv_vmem_bytes(s, n1, n2) <= (VMEM_LIMIT * 7) // 8:
        outs = _fused_conv(xr, xm, col_off, conv_w, conv_b, kre, kim, order, tab, gr, gm, gate_off, tail,
                           b=b, s=s, d=d, n1=n1, n2=n2)
        return outs[0], outs[1]
    kw = dict(b=b, s=s, d=d, n1=n1, n2=n2, dt=LANE_TILE)
    if col_off is not None:
        are, aim, ut = _fft_in(xr, xm, col_off, conv_w, conv_b, tab, **kw)
    else:
        are, aim, ut = _fft_in(xr, xm, 0, None, None, tab, **kw)
    bre, bim = _fft_mid(are, aim, kre, kim, order, tab, dt=min(d, FFT_MID_LANES), slabs=FUSED_SLABS)
    return _fft_out(bre, bim, tab, gr, gm, gate_off, conv_w, conv_b, ut, tail, **kw)


def _hyena_layer(streams, fp, g_mix, w_in, conv_w, conv_b, skip, tm):
    outs = []
    d = w_in.shape[0]
    for st in streams:
        b, s = st["b"], st["s"]
        n1, n2 = _fft_split(s)
        seq_len = s + N_META
        n = 2 * s
        tab = _fft_tables(n1, n2)
        r = jnp.arange(n, dtype=jnp.int32)
        tr = min(s, 512)
        circ = _filter_rows(jnp.where(r < s, r, n - r), s // tr, tr, seq_len, fp, d)
        circ = circ.at[:, 0, :].add(skip.astype(F32))
        kre, kim = _filter_spectrum(circ, tab, n1=n1, n2=n2, dt=min(d, FFT_MID_LANES), slabs=FUSED_SLABS)
        a = jnp.arange(16, dtype=jnp.int32)
        tail = _filter_rows(jnp.concatenate([s + a, s - a, s - a, s + a]), 1, 32, seq_len, fp, d)
        tail = jnp.concatenate([tail[:, 0:16], tail[:, 32:48], tail[:, 48:64], tail[:, 16:32]], axis=1)
        pr = _norm_matmul(st["hr"], g_mix, w_in, tm)
        pm = _norm_matmul(st["hm"], g_mix, w_in, st["hm"].shape[0])
        dims = (b, s, d, n1, n2)
        z1r, z1m = _hyena_conv(pr, pm, 0, conv_w, conv_b, kre, kim, 0, tab, pr, pm, d, tail[0], dims=dims)
        z2r, z2m = _hyena_conv(z1r, z1m, None, conv_w, conv_b, kre, kim, 1, tab, pr, pm, 2 * d, tail[1],
                               dims=dims)
        outs.append((z2r, z2m))
    return outs


def _attention_weights(w_qkv, n_heads):
    groups = n_heads // GROUP
    qd = n_heads * HEAD_DIM
    kd = groups * HEAD_DIM
    w_q = w_qkv[:, :qd] * (HEAD_DIM ** -0.5)
    w_k = w_qkv[:, qd:qd + kd]
    w_v = w_qkv[:, qd + kd:]
    d = w_qkv.shape[0]
    w_vk = jnp.stack([w_v.reshape(d, groups, HEAD_DIM), w_k.reshape(d, groups, HEAD_DIM)], axis=2)
    w_rows = jnp.concatenate([w_q, w_vk.reshape(d, 2 * kd)], axis=1).astype(BF16)
    return w_rows, w_k.T.astype(BF16)


def _meta_keys_transposed(qvk_m, b, n_heads):
    groups = n_heads // GROUP
    qd = n_heads * HEAD_DIM
    k_m = qvk_m[:, qd:].reshape(b, N_META, groups, 2, HEAD_DIM)[:, :, :, 1, :]
    kt = jnp.transpose(k_m.reshape(b, N_META, groups * HEAD_DIM), (0, 2, 1))
    return jnp.pad(kt, ((0, 0), (0, 0), (0, LANE_TILE - N_META)))


def _encoder_pair(x_prompt, x_sample, meta_tokens, norm_mix, norm_mlp, norm_final,
                  hy_w_in, hy_conv_w, hy_conv_b, fps, hy_skip, hy_w_out, hy_b_out,
                  at_w_qkv, at_sink, at_w_o, mlp_w1, mlp_w2, *, n_heads, tm):
    d = x_prompt.shape[-1]
    streams = []
    for x in (x_prompt, x_sample):
        b, s, _ = x.shape
        streams.append(dict(b=b, s=s, hr=x.reshape(b * s, d),
                            hm=jnp.tile(meta_tokens.astype(F32), (b, 1))))
    zeros_d = jnp.zeros((d,), F32)

    conv_w = jnp.pad(hy_conv_w[0], ((0, 5), (0, 0)))
    conv_b = hy_conv_b[0][None, :]
    zs = _hyena_layer(streams, fps[0], norm_mix[0], hy_w_in[0].astype(BF16), conv_w, conv_b, hy_skip[0], tm)
    w_out = hy_w_out[0].astype(BF16)
    w1 = [w.astype(BF16) for w in mlp_w1]
    w2 = [w.astype(BF16) for w in mlp_w2]
    for st, (zr, zm) in zip(streams, zs):
        st["hr"] = _mixer_out_mlp(st["hr"], zr, w_out, hy_b_out[0], norm_mlp[0], w1[0], w2[0], zeros_d, tm, False)
        st["hm"] = _mixer_out_mlp(st["hm"], zm, w_out, hy_b_out[0], norm_mlp[0], w1[0], w2[0], zeros_d,
                                  st["hm"].shape[0], False)

    w_rows, w_kt = _attention_weights(at_w_qkv[0], n_heads)
    w_o = at_w_o[0].astype(BF16)
    outs = []
    for st in streams:
        qvk_r, kt_r = _norm_matmul(st["hr"], norm_mix[1], w_rows, tm, wt=w_kt)
        qvk_m = _norm_matmul(st["hm"], norm_mix[1], w_rows, st["hm"].shape[0])
        kt_m = _meta_keys_transposed(qvk_m, st["b"], n_heads)
        att = _attention(qvk_r, kt_r, qvk_m, kt_m, at_sink[0], b=st["b"], s=st["s"], n_heads=n_heads)
        y = _mixer_out_mlp(st["hr"], att, w_o, zeros_d, norm_mlp[1], w1[1], w2[1], norm_final, tm, True)
        outs.append(y.reshape(st["b"], st["s"], d))
    return tuple(outs)


def kernel(x_prompt, x_sample, meta_tokens, norm_mix, norm_mlp, norm_final, hy_w_in, hy_conv_w, hy_conv_b,
           hy_f_w1, hy_f_b1, hy_f_w2, hy_f_b2, hy_f_w3, hy_f_b3, hy_f_wout, hy_f_freq, hy_skip, hy_w_out,
           hy_b_out, at_w_qkv, at_sink, at_w_o, mlp_w1, mlp_w2):
    fps = [dict(w1=hy_f_w1[j], b1=hy_f_b1[j], w2=hy_f_w2[j], b2=hy_f_b2[j], w3=hy_f_w3[j], b3=hy_f_b3[j],
                wout=hy_f_wout[j], freq=hy_f_freq[j]) for j in range(hy_f_w1.shape[0])]
    n_heads = at_sink.shape[1]
    return _encoder_pair(x_prompt, x_sample, meta_tokens, norm_mix, norm_mlp, norm_final,
                         hy_w_in, hy_conv_w, hy_conv_b, fps, hy_skip, hy_w_out, hy_b_out,
                         at_w_qkv, at_sink, at_w_o, mlp_w1, mlp_w2, n_heads=n_heads, tm=512)
```

```python
import functools
import math

import jax
import jax.numpy as jnp
from jax import lax
from jax.experimental import pallas as pl
from jax.experimental.pallas import tpu as pltpu

F32 = jnp.float32
BF16 = jnp.bfloat16

N_META = 16
RMS_EPS = 1e-6
HY_BANDS = 16
HY_EMB_PAD = 40
HY_FAST_DECAY = 0.3
HY_SLOW_DECAY = 1.5
HY_DECAY_TARGET = 1e-2
ATT_BLOCK = 128
HEAD_DIM = 64
GROUP = 4
MASK_VALUE = -1e30
FF_CHUNK = 1024
LANE_TILE = 128
N2_CHUNK = 32
FUSED_SLABS = 16
FFT_MID_LANES = 256
VMEM_LIMIT = 56 * 1024 * 1024
HIGHEST = lax.Precision.HIGHEST


def _cparams(sem):
    return pltpu.CompilerParams(dimension_semantics=sem, vmem_limit_bytes=VMEM_LIMIT)


def _rms(x, g):
    return x * lax.rsqrt(jnp.mean(x * x, axis=-1, keepdims=True) + RMS_EPS) * g


def _norm_matmul_kernel(x_ref, g_ref, w_ref, *rest):
    u = _rms(x_ref[...], g_ref[...]).astype(BF16)
    if len(rest) == 1:
        (o_ref,) = rest
    else:
        wt_ref, o_ref, ot_ref = rest
        ot_ref[...] = lax.dot_general(wt_ref[...], u, (((1,), (1,)), ((), ())),
                                      preferred_element_type=F32).astype(ot_ref.dtype)
    o_ref[...] = jnp.dot(u, w_ref[...], preferred_element_type=F32).astype(o_ref.dtype)


def _norm_matmul(x, g, w, tm, wt=None):
    rows, d = x.shape
    n = w.shape[1]
    in_specs = [pl.BlockSpec((tm, d), lambda i: (i, 0)),
                pl.BlockSpec((1, d), lambda i: (0, 0)),
                pl.BlockSpec((d, n), lambda i: (0, 0))]
    out_specs = pl.BlockSpec((tm, n), lambda i: (i, 0))
    out_shape = jax.ShapeDtypeStruct((rows, n), BF16)
    args = [x, g.reshape(1, d), w]
    if wt is not None:
        m = wt.shape[0]
        in_specs.append(pl.BlockSpec((m, d), lambda i: (0, 0)))
        out_specs = [out_specs, pl.BlockSpec((m, tm), lambda i: (0, i))]
        out_shape = [out_shape, jax.ShapeDtypeStruct((m, rows), BF16)]
        args.append(wt)
    return pl.pallas_call(
        _norm_matmul_kernel,
        grid=(rows // tm,),
        in_specs=in_specs, out_specs=out_specs, out_shape=out_shape,
        compiler_params=_cparams(("parallel",)),
        name="norm_matmul",
    )(*args)


def _mixer_out_mlp_kernel(h_ref, z_ref, wp_ref, bp_ref, g_ref, w1_ref, w2_ref, gf_ref, o_ref, *, final_norm):
    h = h_ref[...] + jnp.dot(z_ref[...], wp_ref[...], preferred_element_type=F32) + bp_ref[...]
    u = _rms(h, g_ref[...]).astype(BF16)
    acc = h
    d_ff = w1_ref.shape[1]
    for c in range(d_ff // FF_CHUNK):
        a = jnp.dot(u, w1_ref[:, c * FF_CHUNK:(c + 1) * FF_CHUNK], preferred_element_type=F32)
        a = jnp.square(jnp.maximum(a, 0.0)).astype(BF16)
        acc = acc + jnp.dot(a, w2_ref[c * FF_CHUNK:(c + 1) * FF_CHUNK, :], preferred_element_type=F32)
    if final_norm:
        acc = _rms(acc, gf_ref[...])
    o_ref[...] = acc


def _mixer_out_mlp(h, z, wp, bp, g, w1, w2, gf, tm, final_norm):
    rows, d = h.shape
    dz = z.shape[1]
    d_ff = w1.shape[1]
    const = lambda i: (0, 0)
    return pl.pallas_call(
        functools.partial(_mixer_out_mlp_kernel, final_norm=final_norm),
        grid=(rows // tm,),
        in_specs=[pl.BlockSpec((tm, d), lambda i: (i, 0)),
                  pl.BlockSpec((tm, dz), lambda i: (i, 0)),
                  pl.BlockSpec((dz, d), const),
                  pl.BlockSpec((1, d), const),
                  pl.BlockSpec((1, d), const),
                  pl.BlockSpec((d, d_ff), const),
                  pl.BlockSpec((d_ff, d), const),
                  pl.BlockSpec((1, d), const)],
        out_specs=pl.BlockSpec((tm, d), lambda i: (i, 0)),
        out_shape=jax.ShapeDtypeStruct((rows, d), F32),
        compiler_params=_cparams(("parallel",)),
        name="mixer_out_mlp",
    )(h, z, wp, bp.reshape(1, d), g.reshape(1, d), w1, w2, gf.reshape(1, d))


def _filter_kernel(z_ref, t_ref, w1_ref, b1_ref, w2_ref, b2_ref, w3_ref, b3_ref, fr_ref, wo_ref, ad_ref, o_ref):
    d = ad_ref.shape[1]
    dot = functools.partial(jnp.dot, precision=HIGHEST, preferred_element_type=F32)
    fr = fr_ref[...]
    h = jnp.sin(fr * (dot(z_ref[...], w1_ref[...]) + b1_ref[...]))
    h = jnp.sin(fr * (dot(h, w2_ref[...]) + b2_ref[...]))
    h = jnp.sin(fr * (dot(h, w3_ref[...]) + b3_ref[...]))
    ho = jnp.dot(h.astype(BF16), wo_ref[...], preferred_element_type=F32)
    decay = jnp.exp(-t_ref[...] * ad_ref[...])
    for o in range(2):
        o_ref[o] = ho[:, o * d:(o + 1) * d] * decay


def _filter_rows(lag, first_bwd_tile, tr, seq_len, fp, d):
    rows = lag.shape[0]
    lagf = lag.astype(F32)
    t = lagf / (seq_len - 1)
    w = 2.0 * math.pi * lagf / seq_len
    f = jnp.linspace(1e-4, HY_BANDS - 1, HY_BANDS, dtype=F32)[None, :]
    z = jnp.concatenate([t[:, None], jnp.cos(f * w[:, None]), -jnp.sin(f * w[:, None]),
                         jnp.zeros((rows, HY_EMB_PAD - 2 * HY_BANDS - 1), F32)], axis=-1)
    w1 = jnp.pad(fp["w1"], ((0, HY_EMB_PAD - fp["w1"].shape[0]), (0, 0)))
    hid = w1.shape[1]
    max_decay = math.log(HY_DECAY_TARGET) / HY_FAST_DECAY
    min_decay = math.log(HY_DECAY_TARGET) / HY_SLOW_DECAY
    adel = jnp.abs(jnp.linspace(min_decay, max_decay, d, dtype=F32))[None, :]
    const = lambda i: (0, 0)
    row = lambda i: (i, 0)
    wo = jnp.transpose(fp["wout"].reshape(hid, 2, 2, d), (2, 0, 1, 3)).reshape(2, hid, 2 * d)
    wo = wo.astype(BF16)
    wo_spec = pl.BlockSpec((None, hid, 2 * d), lambda i: ((i >= first_bwd_tile).astype(jnp.int32), 0, 0))
    return pl.pallas_call(
        _filter_kernel,
        grid=(rows // tr,),
        in_specs=[pl.BlockSpec((tr, HY_EMB_PAD), row), pl.BlockSpec((tr, 1), row),
                  pl.BlockSpec((HY_EMB_PAD, hid), const), pl.BlockSpec((1, hid), const),
                  pl.BlockSpec((hid, hid), const), pl.BlockSpec((1, hid), const),
                  pl.BlockSpec((hid, hid), const), pl.BlockSpec((1, hid), const),
                  pl.BlockSpec((1, hid), const), wo_spec,
                  pl.BlockSpec((1, d), const)],
        out_specs=pl.BlockSpec((2, tr, d), lambda i: (0, i, 0)),
        out_shape=jax.ShapeDtypeStruct((2, rows, d), F32),
        compiler_params=_cparams(("parallel",)),
        name="hyena_filter",
    )(z, t[:, None], w1, fp["b1"][None], fp["w2"], fp["b2"][None],
      fp["w3"], fp["b3"][None], fp["freq"][None], wo, adel)


def _cplx_block(re, im):
    return jnp.concatenate([jnp.concatenate([re, -im], axis=-1), jnp.concatenate([im, re], axis=-1)], axis=-2)


def _fft_tables(n1, n2):
    n = n1 * n2
    n1h = n1 // 2
    pad = 8 - 1
    k1 = jnp.arange(n1, dtype=jnp.int32)
    c2 = jnp.arange(n2, dtype=jnp.int32)
    def phase(idx, period):
        ang = (idx % period).astype(F32) * (-2.0 * math.pi / period)
        return jnp.cos(ang), jnp.sin(ang)

    def twiddled_dft(cols):
        ar, ai = phase(k1[:, None] * cols[None, :], n1)
        br, bi = phase(c2[:, None] * k1[None, :], n)
        return (ar[None] * br[:, :, None] - ai[None] * bi[:, :, None],
                ai[None] * br[:, :, None] + ar[None] * bi[:, :, None])

    cols = jnp.concatenate([jnp.arange(n1h, dtype=jnp.int32), jnp.array([n1 - 1], jnp.int32)])
    wr, wi = twiddled_dft(cols)
    wr = jnp.pad(wr, ((0, 0), (0, 0), (0, pad)))
    wi = jnp.pad(wi, ((0, 0), (0, 0), (0, pad)))
    fwd1 = _cplx_block(wr, wi)
    inv1 = _cplx_block(jnp.swapaxes(wr, 1, 2), -jnp.swapaxes(wi, 1, 2)) / n
    fil1 = jnp.concatenate(twiddled_dft(k1), axis=1)
    fr, fi = phase(c2[:, None] * c2[None, :], n2)
    fwd2 = _cplx_block(fr, fi)
    return dict(fwd1=fwd1.astype(BF16), inv1=inv1.astype(BF16), fil1=fil1.astype(BF16),
                fwd2=fwd2.astype(BF16), inv2=_cplx_block(fr, -fi).astype(BF16))


def _fill_sequence(meta_ref, real_ref, seq_ref):
    s, dt = real_ref.shape
    seq_ref[0:8, :] = jnp.zeros((8, dt), F32)
    seq_ref[8:8 + N_META, :] = meta_ref[...].astype(F32)
    seq_ref[8 + N_META:8 + N_META + s, :] = real_ref[...].astype(F32)
    seq_ref[8 + N_META + s:16 + N_META + s, :] = jnp.zeros((8, dt), F32)


def _short_conv_rows(seq_ref, cw, cb, start, rows):
    return (seq_ref[pl.ds(start - 1, rows), :] * cw[0:1] + seq_ref[pl.ds(start, rows), :] * cw[1:2]
            + seq_ref[pl.ds(start + 1, rows), :] * cw[2:3] + cb)


def _row_chunk(s):
    return min(s, 1024)


def _load_sequences(xr_ref, xm_ref, conv_refs, ut_ref, xs_ref, mp_ref, seq_ref, n2):
    n1h, dt = xs_ref.shape[1], xs_ref.shape[3]
    s = n1h * n2
    ch = _row_chunk(s)
    for e in range(2):
        if conv_refs is not None:
            cw, cb = conv_refs[0][...].astype(F32), conv_refs[1][...].astype(F32)
            _fill_sequence(xm_ref.at[e], xr_ref.at[e], seq_ref)
            meta = _short_conv_rows(seq_ref, cw, cb, 8, N_META)
        else:
            meta = xm_ref[e].astype(F32)
        ut_ref[e, 0:N_META, :] = meta
        for c in range(s // ch):
            if conv_refs is not None:
                real = _short_conv_rows(seq_ref, cw, cb, 8 + N_META + c * ch, ch)
            else:
                real = xr_ref[e, c * ch:(c + 1) * ch, :].astype(F32)
            xs_ref[e, c * ch // n2:(c + 1) * ch // n2] = real.reshape(ch // n2, n2, dt)
        ut_ref[e, N_META:2 * N_META, :] = xs_ref[e, n1h - 1, n2 - 16:n2, :]
        mp_ref[e] = jnp.zeros(mp_ref.shape[1:], F32)
        for j in range(N_META):
            mp_ref[e, pl.ds(8 * (n2 - N_META + j), 1), :] = meta[j:j + 1]


def _gate_sequences(xs_ref, mp_ref, g_refs, conv_refs, ut_ref, tail_ref, z_refs, seq_ref, n2):
    n1h, dt = xs_ref.shape[1], xs_ref.shape[3]
    s = n1h * n2
    ch = _row_chunk(s)
    gr_ref, gm_ref = g_refs
    zr_ref, zm_ref = z_refs
    cw, cb = conv_refs[0][...].astype(F32), conv_refs[1][...].astype(F32)
    tail = tail_ref[...]
    for e in range(2):
        y_meta = jnp.concatenate(
            [mp_ref[e, pl.ds(8 * (n2 - N_META + j), 1), :] for j in range(N_META)], axis=0)
        real_fix, meta_fix = _alias_patch(tail, ut_ref[e, 0:N_META, :], ut_ref[e, N_META:2 * N_META, :])
        xs_ref[e, n1h - 1, n2 - 16:n2, :] = xs_ref[e, n1h - 1, n2 - 16:n2, :] + real_fix
        _fill_sequence(gm_ref.at[e], gr_ref.at[e], seq_ref)
        g_meta = _short_conv_rows(seq_ref, cw, cb, 8, N_META)
        zm_ref[e] = (g_meta * (y_meta + meta_fix)).astype(zm_ref.dtype)
        for c in range(s // ch):
            g = _short_conv_rows(seq_ref, cw, cb, 8 + N_META + c * ch, ch)
            y = xs_ref[e, c * ch // n2:(c + 1) * ch // n2].reshape(ch, dt)
            zr_ref[e, c * ch:(c + 1) * ch, :] = (g * y).astype(zr_ref.dtype)


def _grid_cols(s):
    return N2_CHUNK // 2 if s > 4096 else N2_CHUNK


def _fft_in_kernel(*refs, n1, n2, cols, short_conv):
    if short_conv:
        xr_ref, xm_ref, cw_ref, cb_ref, f_ref, are_ref, aim_ref, ut_ref, xs_ref, mp_ref, seq_ref = refs
    else:
        xr_ref, xm_ref, f_ref, are_ref, aim_ref, ut_ref, xs_ref, mp_ref = refs
    n1h = n1 // 2
    chunk = pl.program_id(2)

    @pl.when(chunk == 0)
    def _prepare():
        if short_conv:
            _load_sequences(xr_ref, xm_ref, (cw_ref, cb_ref), ut_ref, xs_ref, mp_ref, seq_ref, n2)
        else:
            _load_sequences(xr_ref, xm_ref, None, ut_ref, xs_ref, mp_ref, None, n2)

    base = pl.multiple_of(chunk * cols, cols)
    xt = [jnp.swapaxes(xs_ref[e, :, pl.ds(base, cols), :], 0, 1) for e in range(2)]
    for i in range(cols):
        parts = []
        for e in range(2):
            parts += [xt[e][i], mp_ref[e, pl.ds(pl.multiple_of((base + i) * 8, 8), 8), :]]
        rhs = jnp.concatenate(parts, axis=0).astype(BF16)
        out = jnp.dot(f_ref[i], rhs, preferred_element_type=F32)
        are_ref[i] = out[:n1]
        aim_ref[i] = out[n1:]


def _fft_in(xr, xm, col_off, conv_w, conv_b, tab, *, b, s, d, n1, n2, dt):
    short_conv = conv_w is not None
    pairs = b // 2
    c = xr.shape[1]
    cb0 = col_off // dt
    cols = _grid_cols(s)
    xr4 = xr.reshape(pairs, 2, s, c)
    xm4 = xm.reshape(pairs, 2, N_META, c)
    kk = tab["fwd1"].shape[2]
    in_specs = [pl.BlockSpec((None, 2, s, dt), lambda p, j, t: (p, 0, 0, cb0 + j)),
                pl.BlockSpec((None, 2, N_META, dt), lambda p, j, t: (p, 0, 0, cb0 + j))]
    args = [xr4, xm4]
    if short_conv:
        in_specs += [pl.BlockSpec((8, dt), lambda p, j, t: (0, cb0 + j)),
                     pl.BlockSpec((1, dt), lambda p, j, t: (0, cb0 + j))]
        args += [conv_w, conv_b]
    in_specs.append(pl.BlockSpec((cols, 2 * n1, kk), lambda p, j, t: (t, 0, 0)))
    args.append(tab["fwd1"])
    a_spec = pl.BlockSpec((None, None, cols, n1, dt), lambda p, j, t: (p, j, t, 0, 0))
    a_shape = jax.ShapeDtypeStruct((pairs, d // dt, n2, n1, dt), F32)
    out_specs = [a_spec, a_spec, pl.BlockSpec((None, 2, 2 * N_META, dt), lambda p, j, t: (p, 0, 0, j))]
    out_shape = [a_shape, a_shape, jax.ShapeDtypeStruct((pairs, 2, 2 * N_META, d), F32)]
    scratch = [pltpu.VMEM((2, n1 // 2, n2, dt), F32), pltpu.VMEM((2, 8 * n2, dt), F32)]
    if short_conv:
        scratch.append(pltpu.VMEM((s + N_META + 16, dt), F32))
    return pl.pallas_call(
        functools.partial(_fft_in_kernel, n1=n1, n2=n2, cols=cols, short_conv=short_conv),
        grid=(pairs, d // dt, n2 // cols),
        in_specs=in_specs, out_specs=out_specs, out_shape=out_shape, scratch_shapes=scratch,
        compiler_params=_cparams(("parallel", "parallel", "arbitrary")),
        name="fft_in",
    )(*args)


def _fft_mid_kernel(are_ref, aim_ref, kre_ref, kim_ref, f_ref, g_ref, bre_ref, bim_ref, *, n2, slabs):
    tiles = are_ref.shape[0]
    lt = are_ref.shape[3]
    xr_blk = [jnp.swapaxes(are_ref[h], 0, 1) for h in range(tiles)]
    xi_blk = [jnp.swapaxes(aim_ref[h], 0, 1) for h in range(tiles)]
    ys = []
    for i in range(slabs):
        x = jnp.concatenate([jnp.concatenate([xr_blk[h][i], xi_blk[h][i]], axis=0) for h in range(tiles)],
                            axis=1).astype(BF16)
        z = jnp.dot(f_ref[...], x, preferred_element_type=F32)
        zr, zi = z[:n2], z[n2:]
        kr = jnp.concatenate([kre_ref[h, i] for h in range(tiles)], axis=1)
        ki = jnp.concatenate([kim_ref[h, i] for h in range(tiles)], axis=1)
        p = jnp.concatenate([zr * kr - zi * ki, zr * ki + zi * kr], axis=0).astype(BF16)
        ys.append(jnp.dot(g_ref[...], p, preferred_element_type=F32))
    yt = jnp.swapaxes(jnp.stack(ys, axis=0), 0, 1)
    for h in range(tiles):
        bre_ref[h] = yt[:n2, :, h * lt:(h + 1) * lt]
        bim_ref[h] = yt[n2:, :, h * lt:(h + 1) * lt]


def _fft_mid(are, aim, kre, kim, order, tab, *, dt, slabs):
    pairs, ntiles, n2, n1, lt = are.shape
    d = ntiles * lt
    blk = (None, dt // lt, n2, slabs, lt)
    amap = lambda j, kb, p: (p, j, 0, kb, 0)
    kmap = lambda j, kb, p: (order, j, kb, 0, 0)
    const = lambda j, kb, p: (0, 0)
    shp = jax.ShapeDtypeStruct(are.shape, F32)
    return pl.pallas_call(
        functools.partial(_fft_mid_kernel, n2=n2, slabs=slabs),
        grid=(d // dt, n1 // slabs, pairs),
        in_specs=[pl.BlockSpec(blk, amap), pl.BlockSpec(blk, amap),
                  pl.BlockSpec((None, dt // lt, slabs, n2, lt), kmap),
                  pl.BlockSpec((None, dt // lt, slabs, n2, lt), kmap),
                  pl.BlockSpec((2 * n2, 2 * n2), const), pl.BlockSpec((2 * n2, 2 * n2), const)],
        out_specs=[pl.BlockSpec(blk, amap), pl.BlockSpec(blk, amap)],
        out_shape=[shp, shp],
        compiler_params=_cparams(("parallel", "parallel", "arbitrary")),
        name="fft_mid",
    )(are, aim, kre, kim, tab["fwd2"], tab["inv2"])


def _filter_fft_in_kernel(c_ref, f_ref, are_ref, aim_ref, *, n1, n2):
    base = pl.multiple_of(pl.program_id(2) * N2_CHUNK, N2_CHUNK)
    ct = jnp.swapaxes(c_ref[:, pl.ds(base, N2_CHUNK), :], 0, 1).astype(BF16)
    for i in range(N2_CHUNK):
        out = jnp.dot(f_ref[i], ct[i], preferred_element_type=F32)
        are_ref[i] = out[:n1]
        aim_ref[i] = out[n1:]


def _filter_fft_mid_kernel(are_ref, aim_ref, f_ref, kre_ref, kim_ref, *, n2, slabs):
    tiles = are_ref.shape[0]
    lt = are_ref.shape[3]
    xr_blk = [jnp.swapaxes(are_ref[h], 0, 1) for h in range(tiles)]
    xi_blk = [jnp.swapaxes(aim_ref[h], 0, 1) for h in range(tiles)]
    for i in range(slabs):
        x = jnp.concatenate([jnp.concatenate([xr_blk[h][i], xi_blk[h][i]], axis=0) for h in range(tiles)], axis=1)
        z = jnp.dot(f_ref[...], x.astype(BF16), preferred_element_type=F32)
        for h in range(tiles):
            kre_ref[h, i] = z[:n2, h * lt:(h + 1) * lt]
            kim_ref[h, i] = z[n2:, h * lt:(h + 1) * lt]


def _filter_spectrum(circ, tab, *, n1, n2, dt, slabs):
    orders, n, d = circ.shape
    lt = LANE_TILE
    shp = [jax.ShapeDtypeStruct((orders, d // lt, n2, n1, lt), F32)] * 2
    tab_spec = pl.BlockSpec((N2_CHUNK, 2 * n1, n1), lambda o, j, t: (t, 0, 0))
    are, aim = pl.pallas_call(
        functools.partial(_filter_fft_in_kernel, n1=n1, n2=n2),
        grid=(orders, d // lt, n2 // N2_CHUNK),
        in_specs=[pl.BlockSpec((None, n1, n2, lt), lambda o, j, t: (o, 0, 0, j)), tab_spec],
        out_specs=[pl.BlockSpec((None, None, N2_CHUNK, n1, lt), lambda o, j, t: (o, j, t, 0, 0))] * 2,
        out_shape=shp,
        compiler_params=_cparams(("parallel", "parallel", "arbitrary")),
        name="filter_fft_in",
    )(circ.reshape(orders, n1, n2, d), tab["fil1"])
    blk = (None, dt // lt, n2, slabs, lt)
    amap = lambda o, kb, j: (o, j, 0, kb, 0)
    f_spec = pl.BlockSpec((2 * n2, 2 * n2), lambda o, kb, j: (0, 0))
    return pl.pallas_call(
        functools.partial(_filter_fft_mid_kernel, n2=n2, slabs=slabs),
        grid=(orders, n1 // slabs, d // dt),
        in_specs=[pl.BlockSpec(blk, amap), pl.BlockSpec(blk, amap), f_spec],
        out_specs=[pl.BlockSpec((None, dt // lt, slabs, n2, lt), lambda o, kb, j: (o, j, kb, 0, 0))] * 2,
        out_shape=[jax.ShapeDtypeStruct((orders, d // lt, n1, n2, lt), F32)] * 2,
        compiler_params=_cparams(("parallel", "parallel", "parallel")),
        name="filter_fft_mid",
    )(are, aim, tab["fwd2"])


def _alias_patch(tail, u_meta, u_last):
    dfw = tail[0:16] - tail[16:32]
    dbw = tail[32:48] - tail[48:64]
    ridx = lax.broadcasted_iota(jnp.int32, dfw.shape, 0)
    real_fix = jnp.zeros_like(dfw)
    meta_fix = jnp.zeros_like(dfw)
    for o in range(16):
        src = dfw if o == 0 else pltpu.roll(dfw, o, axis=0)
        real_fix = real_fix + jnp.where(ridx >= o, src, 0.0) * u_meta[o:o + 1]
    for c in range(1, 16):
        meta_fix = meta_fix + jnp.where(ridx + c <= 15, pltpu.roll(u_last, 16 - c, axis=0), 0.0) * dbw[c:c + 1]
    return real_fix, meta_fix


def _fft_out_kernel(bre_ref, bim_ref, g_ref, gr_ref, gm_ref, cw_ref, cb_ref, ut_ref, tail_ref, zr_ref, zm_ref,
                    ys_ref, yp_ref, seq_ref, *, n1, n2, cols):
    n1h = n1 // 2
    kk = n1h + 8
    s = n1h * n2
    chunk = pl.program_id(2)

    base = pl.multiple_of(chunk * cols, cols)
    ys = []
    for i in range(cols):
        rhs = jnp.concatenate([bre_ref[i], bim_ref[i]], axis=0).astype(BF16)
        y = jnp.dot(g_ref[i], rhs, preferred_element_type=F32)
        ys.append(y)
        for e in range(2):
            yp_ref[e, pl.ds(pl.multiple_of((base + i) * 8, 8), 8), :] = y[e * kk + n1h:(e + 1) * kk]
    yt = jnp.swapaxes(jnp.stack(ys, axis=0), 0, 1)
    for e in range(2):
        ys_ref[e, :, pl.ds(base, cols), :] = yt[e * kk:e * kk + n1h]

    @pl.when(chunk == n2 // cols - 1)
    def _gate():
        _gate_sequences(ys_ref, yp_ref, (gr_ref, gm_ref), (cw_ref, cb_ref), ut_ref, tail_ref, (zr_ref, zm_ref),
                        seq_ref, n2)


def _fft_out(bre, bim, tab, gr, gm, gate_off, conv_w, conv_b, ut, tail, *, b, s, d, n1, n2, dt):
    pairs = b // 2
    c = gr.shape[1]
    gb0 = gate_off // dt
    kk2 = tab["inv1"].shape[1]
    gr4 = gr.reshape(pairs, 2, s, c)
    gm4 = gm.reshape(pairs, 2, N_META, c)
    cols = _grid_cols(s)
    b_spec = pl.BlockSpec((None, None, cols, n1, dt), lambda p, j, t: (p, j, t, 0, 0))
    zr, zm = pl.pallas_call(
        functools.partial(_fft_out_kernel, n1=n1, n2=n2, cols=cols),
        grid=(pairs, d // dt, n2 // cols),
        in_specs=[b_spec, b_spec,
                  pl.BlockSpec((cols, kk2, 2 * n1), lambda p, j, t: (t, 0, 0)),
                  pl.BlockSpec((None, 2, s, dt), lambda p, j, t: (p, 0, 0, gb0 + j)),
                  pl.BlockSpec((None, 2, N_META, dt), lambda p, j, t: (p, 0, 0, gb0 + j)),
                  pl.BlockSpec((8, dt), lambda p, j, t: (0, gb0 + j)),
                  pl.BlockSpec((1, dt), lambda p, j, t: (0, gb0 + j)),
                  pl.BlockSpec((None, 2, 2 * N_META, dt), lambda p, j, t: (p, 0, 0, j)),
                  pl.BlockSpec((64, dt), lambda p, j, t: (0, j))],
        out_specs=[pl.BlockSpec((None, 2, s, dt), lambda p, j, t: (p, 0, 0, j)),
                   pl.BlockSpec((None, 2, N_META, dt), lambda p, j, t: (p, 0, 0, j))],
        out_shape=[jax.ShapeDtypeStruct((pairs, 2, s, d), BF16),
                   jax.ShapeDtypeStruct((pairs, 2, N_META, d), BF16)],
        scratch_shapes=[pltpu.VMEM((2, n1 // 2, n2, dt), F32), pltpu.VMEM((2, 8 * n2, dt), F32),
                        pltpu.VMEM((s + N_META + 16, dt), F32)],
        compiler_params=_cparams(("parallel", "parallel", "arbitrary")),
        name="fft_out",
    )(bre, bim, tab["inv1"], gr4, gm4, conv_w, conv_b, ut, tail)
    return zr.reshape(b * s, d), zm.reshape(b * N_META, d)


def _fused_conv_kernel(*refs, n1, n2, short_conv):
    if short_conv:
        (xr_ref, xm_ref, cwx_ref, cbx_ref, f1_ref, kre_ref, kim_ref, f2_ref, g2_ref, g1_ref, gr_ref, gm_ref,
         cwg_ref, cbg_ref, tail_ref, zr_ref, zm_ref, are_ref, aim_ref, xs_ref, mp_ref, ut_ref, seq_ref) = refs
        conv_x = (cwx_ref, cbx_ref)
    else:
        (xr_ref, xm_ref, f1_ref, kre_ref, kim_ref, f2_ref, g2_ref, g1_ref, gr_ref, gm_ref,
         cwg_ref, cbg_ref, tail_ref, zr_ref, zm_ref, are_ref, aim_ref, xs_ref, mp_ref, ut_ref, seq_ref) = refs
        conv_x = None
    n1h = n1 // 2
    kk = n1h + 8
    s = n1h * n2
    nkb = n1 // FUSED_SLABS
    dt = xs_ref.shape[3]

    _load_sequences(xr_ref, xm_ref, conv_x, ut_ref, xs_ref, mp_ref, seq_ref, n2)

    def stage1(t, carry):
        base = pl.multiple_of(t * N2_CHUNK, N2_CHUNK)
        xt = [jnp.swapaxes(xs_ref[e, :, pl.ds(base, N2_CHUNK), :], 0, 1) for e in range(2)]
        for i in range(N2_CHUNK):
            parts = []
            for e in range(2):
                parts += [xt[e][i], mp_ref[e, pl.ds(pl.multiple_of((base + i) * 8, 8), 8), :]]
            rhs = jnp.concatenate(parts, axis=0).astype(BF16)
            out = jnp.dot(f1_ref[base + i], rhs, preferred_element_type=F32).astype(BF16)
            are_ref[:, base + i] = out[:n1].reshape(nkb, FUSED_SLABS, dt)
            aim_ref[:, base + i] = out[n1:].reshape(nkb, FUSED_SLABS, dt)
        return carry

    def stage2(kb, carry):
        k0 = pl.multiple_of(kb * FUSED_SLABS, FUSED_SLABS)
        xr_blk = jnp.swapaxes(are_ref[kb], 0, 1)
        xi_blk = jnp.swapaxes(aim_ref[kb], 0, 1)
        ys = []
        for i in range(0, FUSED_SLABS, 2):
            x = jnp.concatenate([jnp.concatenate([xr_blk[i + h], xi_blk[i + h]], axis=0) for h in range(2)],
                                axis=1)
            z = jnp.dot(f2_ref[...], x, preferred_element_type=F32)
            zr, zi = z[:n2], z[n2:]
            kr = jnp.concatenate([kre_ref[k0 + i], kre_ref[k0 + i + 1]], axis=1)
            ki = jnp.concatenate([kim_ref[k0 + i], kim_ref[k0 + i + 1]], axis=1)
            p = jnp.concatenate([zr * kr - zi * ki, zr * ki + zi * kr], axis=0).astype(BF16)
            y = jnp.dot(g2_ref[...], p, preferred_element_type=F32).astype(BF16)
            ys += [y[:, :dt], y[:, dt:]]
        yt = jnp.swapaxes(jnp.stack(ys, axis=0), 0, 1)
        are_ref[kb] = yt[:n2]
        aim_ref[kb] = yt[n2:]
        return carry

    def stage3(t, carry):
        base = pl.multiple_of(t * N2_CHUNK, N2_CHUNK)
        ys = []
        for i in range(N2_CHUNK):
            rhs = jnp.concatenate([are_ref[:, base + i].reshape(n1, dt), aim_ref[:, base + i].reshape(n1, dt)],
                                  axis=0)
            y = jnp.dot(g1_ref[base + i], rhs, preferred_element_type=F32)
            ys.append(y)
            for e in range(2):
                mp_ref[e, pl.ds(pl.multiple_of((base + i) * 8, 8), 8), :] = y[e * kk + n1h:(e + 1) * kk]
        yt = jnp.swapaxes(jnp.stack(ys, axis=0), 0, 1)
        for e in range(2):
            xs_ref[e, :, pl.ds(base, N2_CHUNK), :] = yt[e * kk:e * kk + n1h]
        return carry

    lax.fori_loop(0, n2 // N2_CHUNK, stage1, 0)
    lax.fori_loop(0, nkb, stage2, 0)
    lax.fori_loop(0, n2 // N2_CHUNK, stage3, 0)

    _gate_sequences(xs_ref, mp_ref, (gr_ref, gm_ref), (cwg_ref, cbg_ref), ut_ref, tail_ref, (zr_ref, zm_ref),
                    seq_ref, n2)


def _fused_conv(xr, xm, col_off, conv_w, conv_b, kre, kim, order, tab, gr, gm, gate_off, tail, *,
                b, s, d, n1, n2):
    short_conv = col_off is not None
    pairs = b // 2
    dt = LANE_TILE
    cx0 = (col_off or 0) // dt
    cg0 = gate_off // dt
    kk2 = tab["fwd1"].shape[2]
    once = dict(pipeline_mode=pl.Buffered(1))
    seq4 = lambda a, rows: a.reshape(pairs, 2, rows, a.shape[1])
    x_spec = lambda rows, c0: pl.BlockSpec((None, 2, rows, dt), lambda j, p: (p, 0, 0, c0 + j))
    row_spec = lambda rows, c0: pl.BlockSpec((rows, dt), lambda j, p: (0, c0 + j))
    const2 = pl.BlockSpec((2 * n2, 2 * n2), lambda j, p: (0, 0), **once)
    k_spec = pl.BlockSpec((None, None, n1, n2, dt), lambda j, p: (order, j, 0, 0, 0), **once)
    in_specs = [x_spec(s, cx0), x_spec(N_META, cx0)]
    args = [seq4(xr, s), seq4(xm, N_META)]
    if short_conv:
        in_specs += [row_spec(8, cx0), row_spec(1, cx0)]
        args += [conv_w, conv_b]
    in_specs += [pl.BlockSpec((n2, 2 * n1, kk2), lambda j, p: (0, 0, 0), **once),
                 k_spec, k_spec, const2, const2,
                 pl.BlockSpec((n2, kk2, 2 * n1), lambda j, p: (0, 0, 0), **once),
                 x_spec(s, cg0), x_spec(N_META, cg0), row_spec(8, cg0), row_spec(1, cg0), row_spec(64, 0)]
    args += [tab["fwd1"], kre, kim, tab["fwd2"], tab["inv2"], tab["inv1"],
             seq4(gr, s), seq4(gm, N_META), conv_w, conv_b, tail]
    seq_out = lambda rows: pl.BlockSpec((None, 2, rows, dt), lambda j, p: (p, 0, 0, j))
    out_specs = [seq_out(s), seq_out(N_META)]
    out_shape = [jax.ShapeDtypeStruct((pairs, 2, s, d), BF16), jax.ShapeDtypeStruct((pairs, 2, N_META, d), BF16)]
    scratch = ([pltpu.VMEM((n1 // FUSED_SLABS, n2, FUSED_SLABS, dt), BF16)] * 2
               + [pltpu.VMEM((2, n1 // 2, n2, dt), F32), pltpu.VMEM((2, 8 * n2, dt), F32),
                  pltpu.VMEM((2, 2 * N_META, dt), F32), pltpu.VMEM((s + N_META + 16, dt), F32)])
    outs = pl.pallas_call(
        functools.partial(_fused_conv_kernel, n1=n1, n2=n2, short_conv=short_conv),
        grid=(d // dt, pairs),
        in_specs=in_specs, out_specs=out_specs, out_shape=out_shape, scratch_shapes=scratch,
        compiler_params=_cparams(("parallel", "parallel")),
        name="fused_conv",
    )(*args)
    return [o.reshape(-1, d) for o in outs]


def _attn_kernel(own_ref, vprev_ref, vnext_ref, vmeta_ref, ktp_ref, kto_ref, ktn_ref, ktm_ref,
                 bias_ref, shift_ref, o_ref, *, groups):
    blk = pl.program_id(1).astype(F32)
    gw = GROUP * HEAD_DIM
    qd = groups * gw
    lane = lax.broadcasted_iota(jnp.int32, (1, LANE_TILE), 1)
    low = lane < HEAD_DIM
    pad_rows = jnp.zeros((ATT_BLOCK - N_META, 2 * HEAD_DIM), BF16)
    zero = jnp.zeros((ATT_BLOCK, LANE_TILE), BF16)
    for g in range(groups):
        vk_own = own_ref[:, qd + g * LANE_TILE:qd + (g + 1) * LANE_TILE]
        grp = slice(g * LANE_TILE, (g + 1) * LANE_TILE)
        vk = jnp.concatenate([vprev_ref[:, grp], vk_own, vnext_ref[:, grp], vmeta_ref[:, grp], pad_rows],
                             axis=0)
        v_ones = jnp.where(low, vk, jnp.ones_like(vk))
        kt_rows = slice(g * HEAD_DIM, (g + 1) * HEAD_DIM)
        kt = jnp.concatenate([ktp_ref[kt_rows, :], kto_ref[kt_rows, :], ktn_ref[kt_rows, :], ktm_ref[kt_rows, :]],
                             axis=1)
        kt2 = jnp.concatenate([kt, kt], axis=0)
        parts = []
        for pr in range(GROUP // 2):
            qp = own_ref[:, g * gw + pr * LANE_TILE:g * gw + (pr + 1) * LANE_TILE]
            parts += [jnp.where(low, qp, zero), jnp.where(low, zero, qp)]
        q = jnp.concatenate(parts, axis=0)
        sc = jnp.dot(q, kt2, preferred_element_type=F32) + bias_ref[g]
        t = [sc[:, i * LANE_TILE:(i + 1) * LANE_TILE] for i in range(3)]
        t.append(sc[:, 3 * LANE_TILE:] - shift_ref[g] * blk)
        m = jnp.max(jnp.maximum(jnp.maximum(t[0], t[1]), jnp.maximum(t[2], t[3])), axis=1, keepdims=True)
        p = jnp.concatenate([jnp.exp(x - m) for x in t], axis=1).astype(BF16)
        oa = jnp.dot(p, v_ones, preferred_element_type=F32)
        ob = pltpu.roll(oa, HEAD_DIM, axis=1)
        outs = []
        for pr in range(GROUP // 2):
            ev = slice((2 * pr) * ATT_BLOCK, (2 * pr + 1) * ATT_BLOCK)
            od = slice((2 * pr + 1) * ATT_BLOCK, (2 * pr + 2) * ATT_BLOCK)
            outs.append(jnp.where(low, oa[ev] / ob[ev], ob[od] / oa[od]))
        o_ref[:, g * gw:(g + 1) * gw] = jnp.concatenate(outs, axis=1).astype(o_ref.dtype)


def _attention_tables(n_heads, sink):
    groups = n_heads // GROUP
    slopes = jnp.exp2(-8.0 * jnp.arange(1, n_heads + 1, dtype=F32) / n_heads)
    i = jnp.arange(ATT_BLOCK, dtype=jnp.int32)[:, None]
    c = jnp.arange(4 * ATT_BLOCK, dtype=jnp.int32)[None, :]
    dist = jnp.abs(c - ATT_BLOCK - i)
    key_blk = c // ATT_BLOCK
    in_band = jnp.logical_and(c < 3 * ATT_BLOCK, dist <= ATT_BLOCK)
    meta_col = jnp.logical_and(c >= 3 * ATT_BLOCK, c < 3 * ATT_BLOCK + N_META)
    sink_col = c == 3 * ATT_BLOCK + N_META
    meta_dist = N_META + i - (c - 3 * ATT_BLOCK)
    tables = []
    for drop in (None, 0, 2):
        ok = in_band if drop is None else jnp.logical_and(in_band, key_blk != drop)
        d_eff = jnp.where(ok, dist, jnp.where(meta_col, meta_dist, 0)).astype(F32)
        live = jnp.logical_or(ok, meta_col)
        tab = jnp.where(live[None], -slopes[:, None, None] * d_eff[None], MASK_VALUE)
        tables.append(jnp.where(sink_col[None], sink.astype(F32)[:, None, None], tab))
    bias = jnp.stack(tables, axis=0).reshape(3, groups, GROUP * ATT_BLOCK, 4 * ATT_BLOCK)
    lane = jnp.arange(LANE_TILE)[None, :]
    shift = jnp.where(lane < N_META, jnp.repeat(slopes * ATT_BLOCK, ATT_BLOCK)[:, None], 0.0)
    return bias, shift.reshape(groups, GROUP * ATT_BLOCK, LANE_TILE)


def _attention(qvk_r, kt_r, qvk_m, kt_m, sink, *, b, s, n_heads):
    groups = n_heads // GROUP
    nblk = s // ATT_BLOCK
    qd = n_heads * HEAD_DIM
    width = qvk_r.shape[1]
    vkw = groups * 2 * HEAD_DIM
    vkb = qd // vkw
    bias, shift = _attention_tables(n_heads, sink)

    def variant(j):
        return jnp.where(j == 0, 1, jnp.where(j == nblk - 1, 2, 0))

    prev = lambda i, j: i * nblk + jnp.maximum(j - 1, 0)
    nxt = lambda i, j: i * nblk + jnp.minimum(j + 1, nblk - 1)
    return pl.pallas_call(
        functools.partial(_attn_kernel, groups=groups),
        grid=(b, nblk),
        in_specs=[pl.BlockSpec((ATT_BLOCK, width), lambda i, j: (i * nblk + j, 0)),
                  pl.BlockSpec((ATT_BLOCK, vkw), lambda i, j: (prev(i, j), vkb)),
                  pl.BlockSpec((ATT_BLOCK, vkw), lambda i, j: (nxt(i, j), vkb)),
                  pl.BlockSpec((N_META, vkw), lambda i, j: (i, vkb)),
                  pl.BlockSpec((groups * HEAD_DIM, ATT_BLOCK), lambda i, j: (0, prev(i, j))),
                  pl.BlockSpec((groups * HEAD_DIM, ATT_BLOCK), lambda i, j: (0, i * nblk + j)),
                  pl.BlockSpec((groups * HEAD_DIM, ATT_BLOCK), lambda i, j: (0, nxt(i, j))),
                  pl.BlockSpec((None, groups * HEAD_DIM, LANE_TILE), lambda i, j: (i, 0, 0)),
                  pl.BlockSpec((None, groups, GROUP * ATT_BLOCK, 4 * ATT_BLOCK), lambda i, j: (variant(j), 0, 0, 0)),
                  pl.BlockSpec((groups, GROUP * ATT_BLOCK, LANE_TILE), lambda i, j: (0, 0, 0))],
        out_specs=pl.BlockSpec((ATT_BLOCK, qd), lambda i, j: (i * nblk + j, 0)),
        out_shape=jax.ShapeDtypeStruct((b * s, qd), BF16),
        compiler_params=_cparams(("parallel", "arbitrary")),
        name="window_attention",
    )(qvk_r, qvk_r, qvk_r, qvk_m, kt_r, kt_r, kt_r, kt_m, bias, shift)


def _fft_split(s):
    n2 = 128 if s >= 1024 else 32
    return (2 * s) // n2, n2


def _fused_conv_vmem_bytes(s, n1, n2):
    lane_bytes = LANE_TILE * 4
    spectrum = 3 * n1 * n2 * lane_bytes
    tables = 2 * n2 * 2 * n1 * 2 * (n1 // 2 + 8) * 2
    sequence = (2 * s + 2 * 8 * n2 + s + N_META + 16) * lane_bytes
    blocks = 3 * 2 * 2 * (s + N_META) * LANE_TILE * 2
    return spectrum + tables + sequence + blocks


def _hyena_conv(xr, xm, col_off, conv_w, conv_b, kre, kim, order, tab, gr, gm, gate_off, tail, *, dims):
    b, s, d, n1, n2 = dims
    if _fused_conv_vmem_bytes(s, n1, n2) <= (VMEM_LIMIT * 7) // 8:
        outs = _fused_conv(xr, xm, col_off, conv_w, conv_b, kre, kim, order, tab, gr, gm, gate_off, tail,
                           b=b, s=s, d=d, n1=n1, n2=n2)
        return outs[0], outs[1]
    kw = dict(b=b, s=s, d=d, n1=n1, n2=n2, dt=LANE_TILE)
    if col_off is not None:
        are, aim, ut = _fft_in(xr, xm, col_off, conv_w, conv_b, tab, **kw)
    else:
        are, aim, ut = _fft_in(xr, xm, 0, None, None, tab, **kw)
    bre, bim = _fft_mid(are, aim, kre, kim, order, tab, dt=min(d, FFT_MID_LANES), slabs=FUSED_SLABS)
    return _fft_out(bre, bim, tab, gr, gm, gate_off, conv_w, conv_b, ut, tail, **kw)


def _hyena_layer(streams, fp, g_mix, w_in, conv_w, conv_b, skip, tm):
    outs = []
    d = w_in.shape[0]
    for st in streams:
        b, s = st["b"], st["s"]
        n1, n2 = _fft_split(s)
        seq_len = s + N_META
        n = 2 * s
        tab = _fft_tables(n1, n2)
        r = jnp.arange(n, dtype=jnp.int32)
        tr = min(s, 512)
        circ = _filter_rows(jnp.where(r < s, r, n - r), s // tr, tr, seq_len, fp, d)
        circ = circ.at[:, 0, :].add(skip.astype(F32))
        kre, kim = _filter_spectrum(circ, tab, n1=n1, n2=n2, dt=min(d, FFT_MID_LANES), slabs=FUSED_SLABS)
        a = jnp.arange(16, dtype=jnp.int32)
        tail = _filter_rows(jnp.concatenate([s + a, s - a, s - a, s + a]), 1, 32, seq_len, fp, d)
        tail = jnp.concatenate([tail[:, 0:16], tail[:, 32:48], tail[:, 48:64], tail[:, 16:32]], axis=1)
        pr = _norm_matmul(st["hr"], g_mix, w_in, tm)
        pm = _norm_matmul(st["hm"], g_mix, w_in, st["hm"].shape[0])
        dims = (b, s, d, n1, n2)
        z1r, z1m = _hyena_conv(pr, pm, 0, conv_w, conv_b, kre, kim, 0, tab, pr, pm, d, tail[0], dims=dims)
        z2r, z2m = _hyena_conv(z1r, z1m, None, conv_w, conv_b, kre, kim, 1, tab, pr, pm, 2 * d, tail[1],
                               dims=dims)
        outs.append((z2r, z2m))
    return outs


def _attention_weights(w_qkv, n_heads):
    groups = n_heads // GROUP
    qd = n_heads * HEAD_DIM
    kd = groups * HEAD_DIM
    w_q = w_qkv[:, :qd] * (HEAD_DIM ** -0.5)
    w_k = w_qkv[:, qd:qd + kd]
    w_v = w_qkv[:, qd + kd:]
    d = w_qkv.shape[0]
    w_vk = jnp.stack([w_v.reshape(d, groups, HEAD_DIM), w_k.reshape(d, groups, HEAD_DIM)], axis=2)
    w_rows = jnp.concatenate([w_q, w_vk.reshape(d, 2 * kd)], axis=1).astype(BF16)
    return w_rows, w_k.T.astype(BF16)


def _meta_keys_transposed(qvk_m, b, n_heads):
    groups = n_heads // GROUP
    qd = n_heads * HEAD_DIM
    k_m = qvk_m[:, qd:].reshape(b, N_META, groups, 2, HEAD_DIM)[:, :, :, 1, :]
    kt = jnp.transpose(k_m.reshape(b, N_META, groups * HEAD_DIM), (0, 2, 1))
    return jnp.pad(kt, ((0, 0), (0, 0), (0, LANE_TILE - N_META)))


def _encoder_pair(x_prompt, x_sample, meta_tokens, norm_mix, norm_mlp, norm_final,
                  hy_w_in, hy_conv_w, hy_conv_b, fps, hy_skip, hy_w_out, hy_b_out,
                  at_w_qkv, at_sink, at_w_o, mlp_w1, mlp_w2, *, n_heads, tm):
    d = x_prompt.shape[-1]
    streams = []
    for x in (x_prompt, x_sample):
        b, s, _ = x.shape
        streams.append(dict(b=b, s=s, hr=x.reshape(b * s, d),
                            hm=jnp.tile(meta_tokens.astype(F32), (b, 1))))
    zeros_d = jnp.zeros((d,), F32)

    conv_w = jnp.pad(hy_conv_w[0], ((0, 5), (0, 0)))
    conv_b = hy_conv_b[0][None, :]
    zs = _hyena_layer(streams, fps[0], norm_mix[0], hy_w_in[0].astype(BF16), conv_w, conv_b, hy_skip[0], tm)
    w_out = hy_w_out[0].astype(BF16)
    w1 = [w.astype(BF16) for w in mlp_w1]
    w2 = [w.astype(BF16) for w in mlp_w2]
    for st, (zr, zm) in zip(streams, zs):
        st["hr"] = _mixer_out_mlp(st["hr"], zr, w_out, hy_b_out[0], norm_mlp[0], w1[0], w2[0], zeros_d, tm, False)
        st["hm"] = _mixer_out_mlp(st["hm"], zm, w_out, hy_b_out[0], norm_mlp[0], w1[0], w2[0], zeros_d,
                                  st["hm"].shape[0], False)

    w_rows, w_kt = _attention_weights(at_w_qkv[0], n_heads)
    w_o = at_w_o[0].astype(BF16)
    outs = []
    for st in streams:
        qvk_r, kt_r = _norm_matmul(st["hr"], norm_mix[1], w_rows, tm, wt=w_kt)
        qvk_m = _norm_matmul(st["hm"], norm_mix[1], w_rows, st["hm"].shape[0])
        kt_m = _meta_keys_transposed(qvk_m, st["b"], n_heads)
        att = _attention(qvk_r, kt_r, qvk_m, kt_m, at_sink[0], b=st["b"], s=st["s"], n_heads=n_heads)
        y = _mixer_out_mlp(st["hr"], att, w_o, zeros_d, norm_mlp[1], w1[1], w2[1], norm_final, tm, True)
        outs.append(y.reshape(st["b"], st["s"], d))
    return tuple(outs)


def kernel(x_prompt, x_sample, meta_tokens, norm_mix, norm_mlp, norm_final, hy_w_in, hy_conv_w, hy_conv_b,
           hy_f_w1, hy_f_b1, hy_f_w2, hy_f_b2, hy_f_w3, hy_f_b3, hy_f_wout, hy_f_freq, hy_skip, hy_w_out,
           hy_b_out, at_w_qkv, at_sink, at_w_o, mlp_w1, mlp_w2):
    fps = [dict(w1=hy_f_w1[j], b1=hy_f_b1[j], w2=hy_f_w2[j], b2=hy_f_b2[j], w3=hy_f_w3[j], b3=hy_f_b3[j],
                wout=hy_f_wout[j], freq=hy_f_freq[j]) for j in range(hy_f_w1.shape[0])]
    n_heads = at_sink.shape[1]
    return _encoder_pair(x_prompt, x_sample, meta_tokens, norm_mix, norm_mlp, norm_final,
                         hy_w_in, hy_conv_w, hy_conv_b, fps, hy_skip, hy_w_out, hy_b_out,
                         at_w_qkv, at_sink, at_w_o, mlp_w1, mlp_w2, n_heads=n_heads, tm=512)
```

```python
import functools
import math

import jax
import jax.numpy as jnp
from jax import lax
from jax.experimental import pallas as pl
from jax.experimental.pallas import tpu as pltpu

F32 = jnp.float32
BF16 = jnp.bfloat16

N_META = 16
RMS_EPS = 1e-6
HY_BANDS = 16
HY_EMB_PAD = 40
HY_FAST_DECAY = 0.3
HY_SLOW_DECAY = 1.5
HY_DECAY_TARGET = 1e-2
ATT_BLOCK = 128
HEAD_DIM = 64
GROUP = 4
MASK_VALUE = -1e30
FF_CHUNK = 1024
LANE_TILE = 128
N2_CHUNK = 32
FUSED_SLABS = 16
FFT_MID_LANES = 256
VMEM_LIMIT = 56 * 1024 * 1024
HIGHEST = lax.Precision.HIGHEST


def _cparams(sem):
    return pltpu.CompilerParams(dimension_semantics=sem, vmem_limit_bytes=VMEM_LIMIT)


def _rms(x, g):
    return x * lax.rsqrt(jnp.mean(x * x, axis=-1, keepdims=True) + RMS_EPS) * g


def _norm_matmul_kernel(x_ref, g_ref, w_ref, *rest):
    u = _rms(x_ref[...], g_ref[...]).astype(BF16)
    if len(rest) == 1:
        (o_ref,) = rest
    else:
        wt_ref, o_ref, ot_ref = rest
        ot_ref[...] = lax.dot_general(wt_ref[...], u, (((1,), (1,)), ((), ())),
                                      preferred_element_type=F32).astype(ot_ref.dtype)
    o_ref[...] = jnp.dot(u, w_ref[...], preferred_element_type=F32).astype(o_ref.dtype)


def _norm_matmul(x, g, w, tm, wt=None):
    rows, d = x.shape
    n = w.shape[1]
    in_specs = [pl.BlockSpec((tm, d), lambda i: (i, 0)),
                pl.BlockSpec((1, d), lambda i: (0, 0)),
                pl.BlockSpec((d, n), lambda i: (0, 0))]
    out_specs = pl.BlockSpec((tm, n), lambda i: (i, 0))
    out_shape = jax.ShapeDtypeStruct((rows, n), BF16)
    args = [x, g.reshape(1, d), w]
    if wt is not None:
        m = wt.shape[0]
        in_specs.append(pl.BlockSpec((m, d), lambda i: (0, 0)))
        out_specs = [out_specs, pl.BlockSpec((m, tm), lambda i: (0, i))]
        out_shape = [out_shape, jax.ShapeDtypeStruct((m, rows), BF16)]
        args.append(wt)
    return pl.pallas_call(
        _norm_matmul_kernel,
        grid=(rows // tm,),
        in_specs=in_specs, out_specs=out_specs, out_shape=out_shape,
        compiler_params=_cparams(("parallel",)),
        name="norm_matmul",
    )(*args)


def _mixer_out_mlp_kernel(h_ref, z_ref, wp_ref, bp_ref, g_ref, w1_ref, w2_ref, gf_ref, o_ref, *, final_norm):
    h = h_ref[...] + jnp.dot(z_ref[...], wp_ref[...], preferred_element_type=F32) + bp_ref[...]
    u = _rms(h, g_ref[...]).astype(BF16)
    acc = h
    d_ff = w1_ref.shape[1]
    for c in range(d_ff // FF_CHUNK):
        a = jnp.dot(u, w1_ref[:, c * FF_CHUNK:(c + 1) * FF_CHUNK], preferred_element_type=F32)
        a = jnp.square(jnp.maximum(a, 0.0)).astype(BF16)
        acc = acc + jnp.dot(a, w2_ref[c * FF_CHUNK:(c + 1) * FF_CHUNK, :], preferred_element_type=F32)
    if final_norm:
        acc = _rms(acc, gf_ref[...])
    o_ref[...] = acc


def _mixer_out_mlp(h, z, wp, bp, g, w1, w2, gf, tm, final_norm):
    rows, d = h.shape
    dz = z.shape[1]
    d_ff = w1.shape[1]
    const = lambda i: (0, 0)
    return pl.pallas_call(
        functools.partial(_mixer_out_mlp_kernel, final_norm=final_norm),
        grid=(rows // tm,),
        in_specs=[pl.BlockSpec((tm, d), lambda i: (i, 0)),
                  pl.BlockSpec((tm, dz), lambda i: (i, 0)),
                  pl.BlockSpec((dz, d), const),
                  pl.BlockSpec((1, d), const),
                  pl.BlockSpec((1, d), const),
                  pl.BlockSpec((d, d_ff), const),
                  pl.BlockSpec((d_ff, d), const),
                  pl.BlockSpec((1, d), const)],
        out_specs=pl.BlockSpec((tm, d), lambda i: (i, 0)),
        out_shape=jax.ShapeDtypeStruct((rows, d), F32),
        compiler_params=_cparams(("parallel",)),
        name="mixer_out_mlp",
    )(h, z, wp, bp.reshape(1, d), g.reshape(1, d), w1, w2, gf.reshape(1, d))


def _filter_kernel(z_ref, w1_ref, b1_ref, w2_ref, b2_ref, w3_ref, b3_ref, fr_ref, wo_ref, ad_ref, o_ref):
    d = ad_ref.shape[1]
    dot = functools.partial(jnp.dot, precision=HIGHEST, preferred_element_type=F32)
    fr = fr_ref[...]
    h = jnp.sin(fr * (dot(z_ref[...], w1_ref[...]) + b1_ref[...]))
    h = jnp.sin(fr * (dot(h, w2_ref[...]) + b2_ref[...]))
    h = jnp.sin(fr * (dot(h, w3_ref[...]) + b3_ref[...]))
    ho = jnp.dot(h.astype(BF16), wo_ref[...], preferred_element_type=F32)
    decay = jnp.exp(-z_ref[:, 0:1] * ad_ref[...])
    for o in range(2):
        o_ref[o] = ho[:, o * d:(o + 1) * d] * decay


def _filter_rows(lag, first_bwd_tile, tr, seq_len, fp, d):
    rows = lag.shape[0]
    lagf = lag.astype(F32)
    t = lagf / (seq_len - 1)
    w = 2.0 * math.pi * lagf / seq_len
    f = jnp.linspace(1e-4, HY_BANDS - 1, HY_BANDS, dtype=F32)[None, :]
    z = jnp.concatenate([t[:, None], jnp.cos(f * w[:, None]), -jnp.sin(f * w[:, None]),
                         jnp.zeros((rows, HY_EMB_PAD - 2 * HY_BANDS - 1), F32)], axis=-1)
    w1 = jnp.pad(fp["w1"], ((0, HY_EMB_PAD - fp["w1"].shape[0]), (0, 0)))
    hid = w1.shape[1]
    max_decay = math.log(HY_DECAY_TARGET) / HY_FAST_DECAY
    min_decay = math.log(HY_DECAY_TARGET) / HY_SLOW_DECAY
    adel = jnp.abs(jnp.linspace(min_decay, max_decay, d, dtype=F32))[None, :]
    const = lambda i: (0, 0)
    row = lambda i: (i, 0)
    wo = jnp.transpose(fp["wout"].reshape(hid, 2, 2, d), (2, 0, 1, 3)).reshape(2, hid, 2 * d)
    wo = wo.astype(BF16)
    wo_spec = pl.BlockSpec((None, hid, 2 * d), lambda i: ((i >= first_bwd_tile).astype(jnp.int32), 0, 0))
    return pl.pallas_call(
        _filter_kernel,
        grid=(rows // tr,),
        in_specs=[pl.BlockSpec((tr, HY_EMB_PAD), row),
                  pl.BlockSpec((HY_EMB_PAD, hid), const), pl.BlockSpec((1, hid), const),
                  pl.BlockSpec((hid, hid), const), pl.BlockSpec((1, hid), const),
                  pl.BlockSpec((hid, hid), const), pl.BlockSpec((1, hid), const),
                  pl.BlockSpec((1, hid), const), wo_spec,
                  pl.BlockSpec((1, d), const)],
        out_specs=pl.BlockSpec((2, tr, d), lambda i: (0, i, 0)),
        out_shape=jax.ShapeDtypeStruct((2, rows, d), F32),
        compiler_params=_cparams(("parallel",)),
        name="hyena_filter",
    )(z, w1, fp["b1"][None], fp["w2"], fp["b2"][None],
      fp["w3"], fp["b3"][None], fp["freq"][None], wo, adel)


def _cplx_block(re, im):
    return jnp.concatenate([jnp.concatenate([re, -im], axis=-1), jnp.concatenate([im, re], axis=-1)], axis=-2)


def _fft_tables(n1, n2):
    n = n1 * n2
    n1h = n1 // 2
    pad = 8 - 1
    k1 = jnp.arange(n1, dtype=jnp.int32)
    c2 = jnp.arange(n2, dtype=jnp.int32)
    def phase(idx, period):
        ang = (idx % period).astype(F32) * (-2.0 * math.pi / period)
        return jnp.cos(ang), jnp.sin(ang)

    def twiddled_dft(cols):
        ar, ai = phase(k1[:, None] * cols[None, :], n1)
        br, bi = phase(c2[:, None] * k1[None, :], n)
        return (ar[None] * br[:, :, None] - ai[None] * bi[:, :, None],
                ai[None] * br[:, :, None] + ar[None] * bi[:, :, None])

    cols = jnp.concatenate([jnp.arange(n1h, dtype=jnp.int32), jnp.array([n1 - 1], jnp.int32)])
    wr, wi = twiddled_dft(cols)
    wr = jnp.pad(wr, ((0, 0), (0, 0), (0, pad)))
    wi = jnp.pad(wi, ((0, 0), (0, 0), (0, pad)))
    fwd1 = _cplx_block(wr, wi)
    inv1 = _cplx_block(jnp.swapaxes(wr, 1, 2), -jnp.swapaxes(wi, 1, 2)) / n
    fil1 = jnp.concatenate(twiddled_dft(k1), axis=1)
    fr, fi = phase(c2[:, None] * c2[None, :], n2)
    fwd2 = _cplx_block(fr, fi)
    return dict(fwd1=fwd1.astype(BF16), inv1=inv1.astype(BF16), fil1=fil1.astype(BF16),
                fwd2=fwd2.astype(BF16), inv2=_cplx_block(fr, -fi).astype(BF16))


def _fill_sequence(meta_ref, real_ref, seq_ref):
    s, dt = real_ref.shape
    seq_ref[0:8, :] = jnp.zeros((8, dt), F32)
    seq_ref[8:8 + N_META, :] = meta_ref[...].astype(F32)
    seq_ref[8 + N_META:8 + N_META + s, :] = real_ref[...].astype(F32)
    seq_ref[8 + N_META + s:16 + N_META + s, :] = jnp.zeros((8, dt), F32)


def _short_conv_rows(seq_ref, cw, cb, start, rows):
    return (seq_ref[pl.ds(start - 1, rows), :] * cw[0:1] + seq_ref[pl.ds(start, rows), :] * cw[1:2]
            + seq_ref[pl.ds(start + 1, rows), :] * cw[2:3] + cb)


def _row_chunk(s):
    return min(s, 1024)


def _load_sequences(xr_ref, xm_ref, conv_refs, ut_ref, xs_ref, mp_ref, seq_ref, n2):
    n1h, dt = xs_ref.shape[1], xs_ref.shape[3]
    s = n1h * n2
    ch = _row_chunk(s)
    for e in range(2):
        if conv_refs is not None:
            cw, cb = conv_refs[0][...].astype(F32), conv_refs[1][...].astype(F32)
            _fill_sequence(xm_ref.at[e], xr_ref.at[e], seq_ref)
            meta = _short_conv_rows(seq_ref, cw, cb, 8, N_META)
        else:
            meta = xm_ref[e].astype(F32)
        ut_ref[e, 0:N_META, :] = meta
        for c in range(s // ch):
            if conv_refs is not None:
                real = _short_conv_rows(seq_ref, cw, cb, 8 + N_META + c * ch, ch)
            else:
                real = xr_ref[e, c * ch:(c + 1) * ch, :].astype(F32)
            xs_ref[e, c * ch // n2:(c + 1) * ch // n2] = real.reshape(ch // n2, n2, dt)
        ut_ref[e, N_META:2 * N_META, :] = xs_ref[e, n1h - 1, n2 - 16:n2, :]
        mp_ref[e] = jnp.zeros(mp_ref.shape[1:], F32)
        for j in range(N_META):
            mp_ref[e, pl.ds(8 * (n2 - N_META + j), 1), :] = meta[j:j + 1]


def _gate_sequences(xs_ref, mp_ref, g_refs, conv_refs, ut_ref, tail_ref, z_refs, seq_ref, n2):
    n1h, dt = xs_ref.shape[1], xs_ref.shape[3]
    s = n1h * n2
    ch = _row_chunk(s)
    gr_ref, gm_ref = g_refs
    zr_ref, zm_ref = z_refs
    cw, cb = conv_refs[0][...].astype(F32), conv_refs[1][...].astype(F32)
    tail = tail_ref[...]
    for e in range(2):
        y_meta = jnp.concatenate(
            [mp_ref[e, pl.ds(8 * (n2 - N_META + j), 1), :] for j in range(N_META)], axis=0)
        real_fix, meta_fix = _alias_patch(tail, ut_ref[e, 0:N_META, :], ut_ref[e, N_META:2 * N_META, :])
        xs_ref[e, n1h - 1, n2 - 16:n2, :] = xs_ref[e, n1h - 1, n2 - 16:n2, :] + real_fix
        _fill_sequence(gm_ref.at[e], gr_ref.at[e], seq_ref)
        g_meta = _short_conv_rows(seq_ref, cw, cb, 8, N_META)
        zm_ref[e] = (g_meta * (y_meta + meta_fix)).astype(zm_ref.dtype)
        for c in range(s // ch):
            g = _short_conv_rows(seq_ref, cw, cb, 8 + N_META + c * ch, ch)
            y = xs_ref[e, c * ch // n2:(c + 1) * ch // n2].reshape(ch, dt)
            zr_ref[e, c * ch:(c + 1) * ch, :] = (g * y).astype(zr_ref.dtype)


def _grid_cols(s):
    return N2_CHUNK // 2 if s > 4096 else N2_CHUNK


def _fft_in_kernel(*refs, n1, n2, cols, short_conv):
    if short_conv:
        xr_ref, xm_ref, cw_ref, cb_ref, f_ref, are_ref, aim_ref, ut_ref, xs_ref, mp_ref, seq_ref = refs
    else:
        xr_ref, xm_ref, f_ref, are_ref, aim_ref, ut_ref, xs_ref, mp_ref = refs
    n1h = n1 // 2
    chunk = pl.program_id(2)

    @pl.when(chunk == 0)
    def _prepare():
        if short_conv:
            _load_sequences(xr_ref, xm_ref, (cw_ref, cb_ref), ut_ref, xs_ref, mp_ref, seq_ref, n2)
        else:
            _load_sequences(xr_ref, xm_ref, None, ut_ref, xs_ref, mp_ref, None, n2)

    base = pl.multiple_of(chunk * cols, cols)
    xt = [jnp.swapaxes(xs_ref[e, :, pl.ds(base, cols), :], 0, 1) for e in range(2)]
    for i in range(cols):
        parts = []
        for e in range(2):
            parts += [xt[e][i], mp_ref[e, pl.ds(pl.multiple_of((base + i) * 8, 8), 8), :]]
        rhs = jnp.concatenate(parts, axis=0).astype(BF16)
        out = jnp.dot(f_ref[i], rhs, preferred_element_type=F32)
        are_ref[i] = out[:n1]
        aim_ref[i] = out[n1:]


def _fft_in(xr, xm, col_off, conv_w, conv_b, tab, *, b, s, d, n1, n2, dt):
    short_conv = conv_w is not None
    pairs = b // 2
    c = xr.shape[1]
    cb0 = col_off // dt
    cols = _grid_cols(s)
    xr4 = xr.reshape(pairs, 2, s, c)
    xm4 = xm.reshape(pairs, 2, N_META, c)
    kk = tab["fwd1"].shape[2]
    in_specs = [pl.BlockSpec((None, 2, s, dt), lambda p, j, t: (p, 0, 0, cb0 + j)),
                pl.BlockSpec((None, 2, N_META, dt), lambda p, j, t: (p, 0, 0, cb0 + j))]
    args = [xr4, xm4]
    if short_conv:
        in_specs += [pl.BlockSpec((8, dt), lambda p, j, t: (0, cb0 + j)),
                     pl.BlockSpec((1, dt), lambda p, j, t: (0, cb0 + j))]
        args += [conv_w, conv_b]
    in_specs.append(pl.BlockSpec((cols, 2 * n1, kk), lambda p, j, t: (t, 0, 0)))
    args.append(tab["fwd1"])
    a_spec = pl.BlockSpec((None, None, cols, n1, dt), lambda p, j, t: (p, j, t, 0, 0))
    a_shape = jax.ShapeDtypeStruct((pairs, d // dt, n2, n1, dt), F32)
    out_specs = [a_spec, a_spec, pl.BlockSpec((None, 2, 2 * N_META, dt), lambda p, j, t: (p, 0, 0, j))]
    out_shape = [a_shape, a_shape, jax.ShapeDtypeStruct((pairs, 2, 2 * N_META, d), F32)]
    scratch = [pltpu.VMEM((2, n1 // 2, n2, dt), F32), pltpu.VMEM((2, 8 * n2, dt), F32)]
    if short_conv:
        scratch.append(pltpu.VMEM((s + N_META + 16, dt), F32))
    return pl.pallas_call(
        functools.partial(_fft_in_kernel, n1=n1, n2=n2, cols=cols, short_conv=short_conv),
        grid=(pairs, d // dt, n2 // cols),
        in_specs=in_specs, out_specs=out_specs, out_shape=out_shape, scratch_shapes=scratch,
        compiler_params=_cparams(("parallel", "parallel", "arbitrary")),
        name="fft_in",
    )(*args)


def _fft_mid_kernel(are_ref, aim_ref, kre_ref, kim_ref, f_ref, g_ref, bre_ref, bim_ref, *, n2, slabs):
    tiles = are_ref.shape[0]
    lt = are_ref.shape[3]
    xr_blk = [jnp.swapaxes(are_ref[h], 0, 1) for h in range(tiles)]
    xi_blk = [jnp.swapaxes(aim_ref[h], 0, 1) for h in range(tiles)]
    ys = []
    for i in range(slabs):
        x = jnp.concatenate([jnp.concatenate([xr_blk[h][i], xi_blk[h][i]], axis=0) for h in range(tiles)],
                            axis=1).astype(BF16)
        z = jnp.dot(f_ref[...], x, preferred_element_type=F32)
        zr, zi = z[:n2], z[n2:]
        kr = jnp.concatenate([kre_ref[h, i] for h in range(tiles)], axis=1)
        ki = jnp.concatenate([kim_ref[h, i] for h in range(tiles)], axis=1)
        p = jnp.concatenate([zr * kr - zi * ki, zr * ki + zi * kr], axis=0).astype(BF16)
        ys.append(jnp.dot(g_ref[...], p, preferred_element_type=F32))
    yt = jnp.swapaxes(jnp.stack(ys, axis=0), 0, 1)
    for h in range(tiles):
        bre_ref[h] = yt[:n2, :, h * lt:(h + 1) * lt]
        bim_ref[h] = yt[n2:, :, h * lt:(h + 1) * lt]


def _fft_mid(are, aim, kre, kim, order, tab, *, dt, slabs):
    pairs, ntiles, n2, n1, lt = are.shape
    d = ntiles * lt
    blk = (None, dt // lt, n2, slabs, lt)
    amap = lambda j, kb, p: (p, j, 0, kb, 0)
    kmap = lambda j, kb, p: (order, j, kb, 0, 0)
    const = lambda j, kb, p: (0, 0)
    shp = jax.ShapeDtypeStruct(are.shape, F32)
    return pl.pallas_call(
        functools.partial(_fft_mid_kernel, n2=n2, slabs=slabs),
        grid=(d // dt, n1 // slabs, pairs),
        in_specs=[pl.BlockSpec(blk, amap), pl.BlockSpec(blk, amap),
                  pl.BlockSpec((None, dt // lt, slabs, n2, lt), kmap),
                  pl.BlockSpec((None, dt // lt, slabs, n2, lt), kmap),
                  pl.BlockSpec((2 * n2, 2 * n2), const), pl.BlockSpec((2 * n2, 2 * n2), const)],
        out_specs=[pl.BlockSpec(blk, amap), pl.BlockSpec(blk, amap)],
        out_shape=[shp, shp],
        compiler_params=_cparams(("parallel", "parallel", "arbitrary")),
        name="fft_mid",
    )(are, aim, kre, kim, tab["fwd2"], tab["inv2"])


def _filter_fft_in_kernel(c_ref, f_ref, are_ref, aim_ref, *, n1, n2):
    base = pl.multiple_of(pl.program_id(2) * N2_CHUNK, N2_CHUNK)
    ct = jnp.swapaxes(c_ref[:, pl.ds(base, N2_CHUNK), :], 0, 1).astype(BF16)
    for i in range(N2_CHUNK):
        out = jnp.dot(f_ref[i], ct[i], preferred_element_type=F32)
        are_ref[i] = out[:n1]
        aim_ref[i] = out[n1:]


def _filter_fft_mid_kernel(are_ref, aim_ref, f_ref, kre_ref, kim_ref, *, n2, slabs):
    tiles = are_ref.shape[0]
    lt = are_ref.shape[3]
    xr_blk = [jnp.swapaxes(are_ref[h], 0, 1) for h in range(tiles)]
    xi_blk = [jnp.swapaxes(aim_ref[h], 0, 1) for h in range(tiles)]
    for i in range(slabs):
        x = jnp.concatenate([jnp.concatenate([xr_blk[h][i], xi_blk[h][i]], axis=0) for h in range(tiles)], axis=1)
        z = jnp.dot(f_ref[...], x.astype(BF16), preferred_element_type=F32)
        for h in range(tiles):
            kre_ref[h, i] = z[:n2, h * lt:(h + 1) * lt]
            kim_ref[h, i] = z[n2:, h * lt:(h + 1) * lt]


def _filter_spectrum(circ, tab, *, n1, n2, dt, slabs):
    orders, n, d = circ.shape
    lt = LANE_TILE
    shp = [jax.ShapeDtypeStruct((orders, d // lt, n2, n1, lt), F32)] * 2
    tab_spec = pl.BlockSpec((N2_CHUNK, 2 * n1, n1), lambda o, j, t: (t, 0, 0))
    are, aim = pl.pallas_call(
        functools.partial(_filter_fft_in_kernel, n1=n1, n2=n2),
        grid=(orders, d // lt, n2 // N2_CHUNK),
        in_specs=[pl.BlockSpec((None, n1, n2, lt), lambda o, j, t: (o, 0, 0, j)), tab_spec],
        out_specs=[pl.BlockSpec((None, None, N2_CHUNK, n1, lt), lambda o, j, t: (o, j, t, 0, 0))] * 2,
        out_shape=shp,
        compiler_params=_cparams(("parallel", "parallel", "arbitrary")),
        name="filter_fft_in",
    )(circ.reshape(orders, n1, n2, d), tab["fil1"])
    blk = (None, dt // lt, n2, slabs, lt)
    amap = lambda o, kb, j: (o, j, 0, kb, 0)
    f_spec = pl.BlockSpec((2 * n2, 2 * n2), lambda o, kb, j: (0, 0))
    return pl.pallas_call(
        functools.partial(_filter_fft_mid_kernel, n2=n2, slabs=slabs),
        grid=(orders, n1 // slabs, d // dt),
        in_specs=[pl.BlockSpec(blk, amap), pl.BlockSpec(blk, amap), f_spec],
        out_specs=[pl.BlockSpec((None, dt // lt, slabs, n2, lt), lambda o, kb, j: (o, j, kb, 0, 0))] * 2,
        out_shape=[jax.ShapeDtypeStruct((orders, d // lt, n1, n2, lt), F32)] * 2,
        compiler_params=_cparams(("parallel", "parallel", "parallel")),
        name="filter_fft_mid",
    )(are, aim, tab["fwd2"])


def _alias_patch(tail, u_meta, u_last):
    dfw = tail[0:16] - tail[16:32]
    dbw = tail[32:48] - tail[48:64]
    ridx = lax.broadcasted_iota(jnp.int32, dfw.shape, 0)
    real_fix = jnp.zeros_like(dfw)
    meta_fix = jnp.zeros_like(dfw)
    for o in range(16):
        src = dfw if o == 0 else pltpu.roll(dfw, o, axis=0)
        real_fix = real_fix + jnp.where(ridx >= o, src, 0.0) * u_meta[o:o + 1]
    for c in range(1, 16):
        meta_fix = meta_fix + jnp.where(ridx + c <= 15, pltpu.roll(u_last, 16 - c, axis=0), 0.0) * dbw[c:c + 1]
    return real_fix, meta_fix


def _fft_out_kernel(bre_ref, bim_ref, g_ref, gr_ref, gm_ref, cw_ref, cb_ref, ut_ref, tail_ref, zr_ref, zm_ref,
                    ys_ref, yp_ref, seq_ref, *, n1, n2, cols):
    n1h = n1 // 2
    kk = n1h + 8
    s = n1h * n2
    chunk = pl.program_id(2)

    base = pl.multiple_of(chunk * cols, cols)
    ys = []
    for i in range(cols):
        rhs = jnp.concatenate([bre_ref[i], bim_ref[i]], axis=0).astype(BF16)
        y = jnp.dot(g_ref[i], rhs, preferred_element_type=F32)
        ys.append(y)
        for e in range(2):
            yp_ref[e, pl.ds(pl.multiple_of((base + i) * 8, 8), 8), :] = y[e * kk + n1h:(e + 1) * kk]
    yt = jnp.swapaxes(jnp.stack(ys, axis=0), 0, 1)
    for e in range(2):
        ys_ref[e, :, pl.ds(base, cols), :] = yt[e * kk:e * kk + n1h]

    @pl.when(chunk == n2 // cols - 1)
    def _gate():
        _gate_sequences(ys_ref, yp_ref, (gr_ref, gm_ref), (cw_ref, cb_ref), ut_ref, tail_ref, (zr_ref, zm_ref),
                        seq_ref, n2)


def _fft_out(bre, bim, tab, gr, gm, gate_off, conv_w, conv_b, ut, tail, *, b, s, d, n1, n2, dt):
    pairs = b // 2
    c = gr.shape[1]
    gb0 = gate_off // dt
    kk2 = tab["inv1"].shape[1]
    gr4 = gr.reshape(pairs, 2, s, c)
    gm4 = gm.reshape(pairs, 2, N_META, c)
    cols = _grid_cols(s)
    b_spec = pl.BlockSpec((None, None, cols, n1, dt), lambda p, j, t: (p, j, t, 0, 0))
    zr, zm = pl.pallas_call(
        functools.partial(_fft_out_kernel, n1=n1, n2=n2, cols=cols),
        grid=(pairs, d // dt, n2 // cols),
        in_specs=[b_spec, b_spec,
                  pl.BlockSpec((cols, kk2, 2 * n1), lambda p, j, t: (t, 0, 0)),
                  pl.BlockSpec((None, 2, s, dt), lambda p, j, t: (p, 0, 0, gb0 + j)),
                  pl.BlockSpec((None, 2, N_META, dt), lambda p, j, t: (p, 0, 0, gb0 + j)),
                  pl.BlockSpec((8, dt), lambda p, j, t: (0, gb0 + j)),
                  pl.BlockSpec((1, dt), lambda p, j, t: (0, gb0 + j)),
                  pl.BlockSpec((None, 2, 2 * N_META, dt), lambda p, j, t: (p, 0, 0, j)),
                  pl.BlockSpec((64, dt), lambda p, j, t: (0, j))],
        out_specs=[pl.BlockSpec((None, 2, s, dt), lambda p, j, t: (p, 0, 0, j)),
                   pl.BlockSpec((None, 2, N_META, dt), lambda p, j, t: (p, 0, 0, j))],
        out_shape=[jax.ShapeDtypeStruct((pairs, 2, s, d), BF16),
                   jax.ShapeDtypeStruct((pairs, 2, N_META, d), BF16)],
        scratch_shapes=[pltpu.VMEM((2, n1 // 2, n2, dt), F32), pltpu.VMEM((2, 8 * n2, dt), F32),
                        pltpu.VMEM((s + N_META + 16, dt), F32)],
        compiler_params=_cparams(("parallel", "parallel", "arbitrary")),
        name="fft_out",
    )(bre, bim, tab["inv1"], gr4, gm4, conv_w, conv_b, ut, tail)
    return zr.reshape(b * s, d), zm.reshape(b * N_META, d)


def _fused_conv_kernel(*refs, n1, n2, short_conv):
    if short_conv:
        (xr_ref, xm_ref, cwx_ref, cbx_ref, f1_ref, kre_ref, kim_ref, f2_ref, g2_ref, g1_ref, gr_ref, gm_ref,
         cwg_ref, cbg_ref, tail_ref, zr_ref, zm_ref, are_ref, aim_ref, xs_ref, mp_ref, ut_ref, seq_ref) = refs
        conv_x = (cwx_ref, cbx_ref)
    else:
        (xr_ref, xm_ref, f1_ref, kre_ref, kim_ref, f2_ref, g2_ref, g1_ref, gr_ref, gm_ref,
         cwg_ref, cbg_ref, tail_ref, zr_ref, zm_ref, are_ref, aim_ref, xs_ref, mp_ref, ut_ref, seq_ref) = refs
        conv_x = None
    n1h = n1 // 2
    kk = n1h + 8
    s = n1h * n2
    nkb = n1 // FUSED_SLABS
    dt = xs_ref.shape[3]

    _load_sequences(xr_ref, xm_ref, conv_x, ut_ref, xs_ref, mp_ref, seq_ref, n2)

    def stage1(t, carry):
        base = pl.multiple_of(t * N2_CHUNK, N2_CHUNK)
        xt = [jnp.swapaxes(xs_ref[e, :, pl.ds(base, N2_CHUNK), :], 0, 1) for e in range(2)]
        for i in range(N2_CHUNK):
            parts = []
            for e in range(2):
                parts += [xt[e][i], mp_ref[e, pl.ds(pl.multiple_of((base + i) * 8, 8), 8), :]]
            rhs = jnp.concatenate(parts, axis=0).astype(BF16)
            out = jnp.dot(f1_ref[base + i], rhs, preferred_element_type=F32).astype(BF16)
            are_ref[:, base + i] = out[:n1].reshape(nkb, FUSED_SLABS, dt)
            aim_ref[:, base + i] = out[n1:].reshape(nkb, FUSED_SLABS, dt)
        return carry

    def stage2(kb, carry):
        k0 = pl.multiple_of(kb * FUSED_SLABS, FUSED_SLABS)
        xr_blk = jnp.swapaxes(are_ref[kb], 0, 1)
        xi_blk = jnp.swapaxes(aim_ref[kb], 0, 1)
        ys = []
        for i in range(0, FUSED_SLABS, 2):
            x = jnp.concatenate([jnp.concatenate([xr_blk[i + h], xi_blk[i + h]], axis=0) for h in range(2)],
                                axis=1)
            z = jnp.dot(f2_ref[...], x, preferred_element_type=F32)
            zr, zi = z[:n2], z[n2:]
            kr = jnp.concatenate([kre_ref[k0 + i], kre_ref[k0 + i + 1]], axis=1)
            ki = jnp.concatenate([kim_ref[k0 + i], kim_ref[k0 + i + 1]], axis=1)
            p = jnp.concatenate([zr * kr - zi * ki, zr * ki + zi * kr], axis=0).astype(BF16)
            y = jnp.dot(g2_ref[...], p, preferred_element_type=F32).astype(BF16)
            ys += [y[:, :dt], y[:, dt:]]
        yt = jnp.swapaxes(jnp.stack(ys, axis=0), 0, 1)
        are_ref[kb] = yt[:n2]
        aim_ref[kb] = yt[n2:]
        return carry

    def stage3(t, carry):
        base = pl.multiple_of(t * N2_CHUNK, N2_CHUNK)
        ys = []
        for i in range(N2_CHUNK):
            rhs = jnp.concatenate([are_ref[:, base + i].reshape(n1, dt), aim_ref[:, base + i].reshape(n1, dt)],
                                  axis=0)
            y = jnp.dot(g1_ref[base + i], rhs, preferred_element_type=F32)
            ys.append(y)
            for e in range(2):
                mp_ref[e, pl.ds(pl.multiple_of((base + i) * 8, 8), 8), :] = y[e * kk + n1h:(e + 1) * kk]
        yt = jnp.swapaxes(jnp.stack(ys, axis=0), 0, 1)
        for e in range(2):
            xs_ref[e, :, pl.ds(base, N2_CHUNK), :] = yt[e * kk:e * kk + n1h]
        return carry

    lax.fori_loop(0, n2 // N2_CHUNK, stage1, 0)
    lax.fori_loop(0, nkb, stage2, 0)
    lax.fori_loop(0, n2 // N2_CHUNK, stage3, 0)

    _gate_sequences(xs_ref, mp_ref, (gr_ref, gm_ref), (cwg_ref, cbg_ref), ut_ref, tail_ref, (zr_ref, zm_ref),
                    seq_ref, n2)


def _fused_conv(xr, xm, col_off, conv_w, conv_b, kre, kim, order, tab, gr, gm, gate_off, tail, *,
                b, s, d, n1, n2):
    short_conv = col_off is not None
    pairs = b // 2
    dt = LANE_TILE
    cx0 = (col_off or 0) // dt
    cg0 = gate_off // dt
    kk2 = tab["fwd1"].shape[2]
    once = dict(pipeline_mode=pl.Buffered(1))
    seq4 = lambda a, rows: a.reshape(pairs, 2, rows, a.shape[1])
    x_spec = lambda rows, c0: pl.BlockSpec((None, 2, rows, dt), lambda j, p: (p, 0, 0, c0 + j))
    row_spec = lambda rows, c0: pl.BlockSpec((rows, dt), lambda j, p: (0, c0 + j))
    const2 = pl.BlockSpec((2 * n2, 2 * n2), lambda j, p: (0, 0), **once)
    k_spec = pl.BlockSpec((None, None, n1, n2, dt), lambda j, p: (order, j, 0, 0, 0), **once)
    in_specs = [x_spec(s, cx0), x_spec(N_META, cx0)]
    args = [seq4(xr, s), seq4(xm, N_META)]
    if short_conv:
        in_specs += [row_spec(8, cx0), row_spec(1, cx0)]
        args += [conv_w, conv_b]
    in_specs += [pl.BlockSpec((n2, 2 * n1, kk2), lambda j, p: (0, 0, 0), **once),
                 k_spec, k_spec, const2, const2,
                 pl.BlockSpec((n2, kk2, 2 * n1), lambda j, p: (0, 0, 0), **once),
                 x_spec(s, cg0), x_spec(N_META, cg0), row_spec(8, cg0), row_spec(1, cg0), row_spec(64, 0)]
    args += [tab["fwd1"], kre, kim, tab["fwd2"], tab["inv2"], tab["inv1"],
             seq4(gr, s), seq4(gm, N_META), conv_w, conv_b, tail]
    seq_out = lambda rows: pl.BlockSpec((None, 2, rows, dt), lambda j, p: (p, 0, 0, j))
    out_specs = [seq_out(s), seq_out(N_META)]
    out_shape = [jax.ShapeDtypeStruct((pairs, 2, s, d), BF16), jax.ShapeDtypeStruct((pairs, 2, N_META, d), BF16)]
    scratch = ([pltpu.VMEM((n1 // FUSED_SLABS, n2, FUSED_SLABS, dt), BF16)] * 2
               + [pltpu.VMEM((2, n1 // 2, n2, dt), F32), pltpu.VMEM((2, 8 * n2, dt), F32),
                  pltpu.VMEM((2, 2 * N_META, dt), F32), pltpu.VMEM((s + N_META + 16, dt), F32)])
    outs = pl.pallas_call(
        functools.partial(_fused_conv_kernel, n1=n1, n2=n2, short_conv=short_conv),
        grid=(d // dt, pairs),
        in_specs=in_specs, out_specs=out_specs, out_shape=out_shape, scratch_shapes=scratch,
        compiler_params=_cparams(("parallel", "parallel")),
        name="fused_conv",
    )(*args)
    return [o.reshape(-1, d) for o in outs]


def _attn_kernel(own_ref, vprev_ref, vnext_ref, vmeta_ref, ktp_ref, kto_ref, ktn_ref, ktm_ref,
                 bias_a_ref, bias_b_ref, shift_ref, o_ref, *, groups):
    gw = GROUP * HEAD_DIM
    qd = groups * gw
    lane = lax.broadcasted_iota(jnp.int32, (1, LANE_TILE), 1)
    low = lane < HEAD_DIM
    pad_rows = jnp.zeros((ATT_BLOCK - N_META, 2 * HEAD_DIM), BF16)
    zero = jnp.zeros((ATT_BLOCK, LANE_TILE), BF16)
    first, second = slice(0, ATT_BLOCK), slice(ATT_BLOCK, 2 * ATT_BLOCK)
    for sub, rows, bias_ref in ((0, first, bias_a_ref), (1, second, bias_b_ref)):
        blk = (2 * pl.program_id(1) + sub).astype(F32)
        for g in range(groups):
            grp = slice(g * LANE_TILE, (g + 1) * LANE_TILE)
            own_cols = slice(qd + g * LANE_TILE, qd + (g + 1) * LANE_TILE)
            kt_rows = slice(g * HEAD_DIM, (g + 1) * HEAD_DIM)
            if sub == 0:
                vk_band = [vprev_ref[:, grp], own_ref[first, own_cols], own_ref[second, own_cols]]
                kt_band = [ktp_ref[kt_rows, :], kto_ref[kt_rows, first], kto_ref[kt_rows, second]]
            else:
                vk_band = [own_ref[first, own_cols], own_ref[second, own_cols], vnext_ref[:, grp]]
                kt_band = [kto_ref[kt_rows, first], kto_ref[kt_rows, second], ktn_ref[kt_rows, :]]
            vk = jnp.concatenate(vk_band + [vmeta_ref[:, grp], pad_rows], axis=0)
            v_ones = jnp.where(low, vk, jnp.ones_like(vk))
            kt = jnp.concatenate(kt_band + [ktm_ref[kt_rows, :]], axis=1)
            kt2 = jnp.concatenate([kt, kt], axis=0)
            parts = []
            for pr in range(GROUP // 2):
                qp = own_ref[rows, g * gw + pr * LANE_TILE:g * gw + (pr + 1) * LANE_TILE]
                parts += [jnp.where(low, qp, zero), jnp.where(low, zero, qp)]
            q = jnp.concatenate(parts, axis=0)
            sc = jnp.dot(q, kt2, preferred_element_type=F32) + bias_ref[g]
            t = [sc[:, i * LANE_TILE:(i + 1) * LANE_TILE] for i in range(3)]
            t.append(sc[:, 3 * LANE_TILE:] - shift_ref[g] * blk)
            m = jnp.max(jnp.maximum(jnp.maximum(t[0], t[1]), jnp.maximum(t[2], t[3])), axis=1, keepdims=True)
            p = jnp.concatenate([jnp.exp(x - m) for x in t], axis=1).astype(BF16)
            oa = jnp.dot(p, v_ones, preferred_element_type=F32)
            ob = pltpu.roll(oa, HEAD_DIM, axis=1)
            outs = []
            for pr in range(GROUP // 2):
                ev = slice((2 * pr) * ATT_BLOCK, (2 * pr + 1) * ATT_BLOCK)
                od = slice((2 * pr + 1) * ATT_BLOCK, (2 * pr + 2) * ATT_BLOCK)
                outs.append(jnp.where(low, oa[ev] / ob[ev], ob[od] / oa[od]))
            o_ref[rows, g * gw:(g + 1) * gw] = jnp.concatenate(outs, axis=1).astype(o_ref.dtype)


def _attention_tables(n_heads, sink):
    groups = n_heads // GROUP
    slopes = jnp.exp2(-8.0 * jnp.arange(1, n_heads + 1, dtype=F32) / n_heads)
    i = jnp.arange(ATT_BLOCK, dtype=jnp.int32)[:, None]
    c = jnp.arange(4 * ATT_BLOCK, dtype=jnp.int32)[None, :]
    dist = jnp.abs(c - ATT_BLOCK - i)
    key_blk = c // ATT_BLOCK
    in_band = jnp.logical_and(c < 3 * ATT_BLOCK, dist <= ATT_BLOCK)
    meta_col = jnp.logical_and(c >= 3 * ATT_BLOCK, c < 3 * ATT_BLOCK + N_META)
    sink_col = c == 3 * ATT_BLOCK + N_META
    meta_dist = N_META + i - (c - 3 * ATT_BLOCK)
    tables = []
    for drop in (None, 0, 2):
        ok = in_band if drop is None else jnp.logical_and(in_band, key_blk != drop)
        d_eff = jnp.where(ok, dist, jnp.where(meta_col, meta_dist, 0)).astype(F32)
        live = jnp.logical_or(ok, meta_col)
        tab = jnp.where(live[None], -slopes[:, None, None] * d_eff[None], MASK_VALUE)
        tables.append(jnp.where(sink_col[None], sink.astype(F32)[:, None, None], tab))
    bias = jnp.stack(tables, axis=0).reshape(3, groups, GROUP * ATT_BLOCK, 4 * ATT_BLOCK)
    lane = jnp.arange(LANE_TILE)[None, :]
    shift = jnp.where(lane < N_META, jnp.repeat(slopes * ATT_BLOCK, ATT_BLOCK)[:, None], 0.0)
    return bias, shift.reshape(groups, GROUP * ATT_BLOCK, LANE_TILE)


def _attention(qvk_r, kt_r, qvk_m, kt_m, sink, *, b, s, n_heads):
    groups = n_heads // GROUP
    nblk = s // ATT_BLOCK
    qd = n_heads * HEAD_DIM
    width = qvk_r.shape[1]
    vkw = groups * 2 * HEAD_DIM
    vkb = qd // vkw
    bias, shift = _attention_tables(n_heads, sink)

    steps = nblk // 2
    prev = lambda i, j: i * nblk + jnp.maximum(2 * j - 1, 0)
    nxt = lambda i, j: i * nblk + jnp.minimum(2 * j + 2, nblk - 1)
    bias_blk = (None, groups, GROUP * ATT_BLOCK, 4 * ATT_BLOCK)
    return pl.pallas_call(
        functools.partial(_attn_kernel, groups=groups),
        grid=(b, steps),
        in_specs=[pl.BlockSpec((2 * ATT_BLOCK, width), lambda i, j: (i * steps + j, 0)),
                  pl.BlockSpec((ATT_BLOCK, vkw), lambda i, j: (prev(i, j), vkb)),
                  pl.BlockSpec((ATT_BLOCK, vkw), lambda i, j: (nxt(i, j), vkb)),
                  pl.BlockSpec((N_META, vkw), lambda i, j: (i, vkb)),
                  pl.BlockSpec((groups * HEAD_DIM, ATT_BLOCK), lambda i, j: (0, prev(i, j))),
                  pl.BlockSpec((groups * HEAD_DIM, 2 * ATT_BLOCK), lambda i, j: (0, i * steps + j)),
                  pl.BlockSpec((groups * HEAD_DIM, ATT_BLOCK), lambda i, j: (0, nxt(i, j))),
                  pl.BlockSpec((None, groups * HEAD_DIM, LANE_TILE), lambda i, j: (i, 0, 0)),
                  pl.BlockSpec(bias_blk, lambda i, j: (jnp.where(j == 0, 1, 0), 0, 0, 0)),
                  pl.BlockSpec(bias_blk, lambda i, j: (jnp.where(j == steps - 1, 2, 0), 0, 0, 0)),
                  pl.BlockSpec((groups, GROUP * ATT_BLOCK, LANE_TILE), lambda i, j: (0, 0, 0))],
        out_specs=pl.BlockSpec((2 * ATT_BLOCK, qd), lambda i, j: (i * steps + j, 0)),
        out_shape=jax.ShapeDtypeStruct((b * s, qd), BF16),
        compiler_params=_cparams(("parallel", "arbitrary")),
        name="window_attention",
    )(qvk_r, qvk_r, qvk_r, qvk_m, kt_r, kt_r, kt_r, kt_m, bias, bias, shift)


def _fft_split(s):
    n2 = 128 if s >= 1024 else 32
    return (2 * s) // n2, n2


def _fused_conv_vmem_bytes(s, n1, n2):
    lane_bytes = LANE_TILE * 4
    spectrum = 3 * n1 * n2 * lane_bytes
    tables = 2 * n2 * 2 * n1 * 2 * (n1 // 2 + 8) * 2
    sequence = (2 * s + 2 * 8 * n2 + s + N_META + 16) * lane_bytes
    blocks = 3 * 2 * 2 * (s + N_META) * LANE_TILE * 2
    return spectrum + tables + sequence + blocks


def _hyena_conv(xr, xm, col_off, conv_w, conv_b, kre, kim, order, tab, gr, gm, gate_off, tail, *, dims):
    b, s, d, n1, n2 = dims
    if _fused_conv_vmem_bytes(s, n1, n2) <= (VMEM_LIMIT * 7) // 8:
        outs = _fused_conv(xr, xm, col_off, conv_w, conv_b, kre, kim, order, tab, gr, gm, gate_off, tail,
                           b=b, s=s, d=d, n1=n1, n2=n2)
        return outs[0], outs[1]
    kw = dict(b=b, s=s, d=d, n1=n1, n2=n2, dt=LANE_TILE)
    if col_off is not None:
        are, aim, ut = _fft_in(xr, xm, col_off, conv_w, conv_b, tab, **kw)
    else:
        are, aim, ut = _fft_in(xr, xm, 0, None, None, tab, **kw)
    bre, bim = _fft_mid(are, aim, kre, kim, order, tab, dt=min(d, FFT_MID_LANES), slabs=FUSED_SLABS)
    return _fft_out(bre, bim, tab, gr, gm, gate_off, conv_w, conv_b, ut, tail, **kw)


def _hyena_layer(streams, fp, g_mix, w_in, conv_w, conv_b, skip, tm):
    outs = []
    d = w_in.shape[0]
    for st in streams:
        b, s = st["b"], st["s"]
        n1, n2 = _fft_split(s)
        seq_len = s + N_META
        n = 2 * s
        tab = _fft_tables(n1, n2)
        r = jnp.arange(n, dtype=jnp.int32)
        tr = min(s, 512)
        circ = _filter_rows(jnp.where(r < s, r, n - r), s // tr, tr, seq_len, fp, d)
        circ = circ.at[:, 0, :].add(skip.astype(F32))
        kre, kim = _filter_spectrum(circ, tab, n1=n1, n2=n2, dt=min(d, FFT_MID_LANES), slabs=FUSED_SLABS)
        a = jnp.arange(16, dtype=jnp.int32)
        tail = _filter_rows(jnp.concatenate([s + a, s - a, s - a, s + a]), 1, 32, seq_len, fp, d)
        tail = jnp.concatenate([tail[:, 0:16], tail[:, 32:48], tail[:, 48:64], tail[:, 16:32]], axis=1)
        pr = _norm_matmul(st["hr"], g_mix, w_in, tm)
        pm = _norm_matmul(st["hm"], g_mix, w_in, st["hm"].shape[0])
        dims = (b, s, d, n1, n2)
        z1r, z1m = _hyena_conv(pr, pm, 0, conv_w, conv_b, kre, kim, 0, tab, pr, pm, d, tail[0], dims=dims)
        z2r, z2m = _hyena_conv(z1r, z1m, None, conv_w, conv_b, kre, kim, 1, tab, pr, pm, 2 * d, tail[1],
                               dims=dims)
        outs.append((z2r, z2m))
    return outs


def _attention_weights(w_qkv, n_heads):
    groups = n_heads // GROUP
    qd = n_heads * HEAD_DIM
    kd = groups * HEAD_DIM
    w_q = w_qkv[:, :qd] * (HEAD_DIM ** -0.5)
    w_k = w_qkv[:, qd:qd + kd]
    w_v = w_qkv[:, qd + kd:]
    d = w_qkv.shape[0]
    w_vk = jnp.stack([w_v.reshape(d, groups, HEAD_DIM), w_k.reshape(d, groups, HEAD_DIM)], axis=2)
    w_rows = jnp.concatenate([w_q, w_vk.reshape(d, 2 * kd)], axis=1).astype(BF16)
    return w_rows, w_k.T.astype(BF16)


def _meta_keys_transposed(qvk_m, b, n_heads):
    groups = n_heads // GROUP
    qd = n_heads * HEAD_DIM
    k_m = qvk_m[:, qd:].reshape(b, N_META, groups, 2, HEAD_DIM)[:, :, :, 1, :]
    kt = jnp.transpose(k_m.reshape(b, N_META, groups * HEAD_DIM), (0, 2, 1))
    return jnp.pad(kt, ((0, 0), (0, 0), (0, LANE_TILE - N_META)))


def _encoder_pair(x_prompt, x_sample, meta_tokens, norm_mix, norm_mlp, norm_final,
                  hy_w_in, hy_conv_w, hy_conv_b, fps, hy_skip, hy_w_out, hy_b_out,
                  at_w_qkv, at_sink, at_w_o, mlp_w1, mlp_w2, *, n_heads, tm):
    d = x_prompt.shape[-1]
    streams = []
    for x in (x_prompt, x_sample):
        b, s, _ = x.shape
        streams.append(dict(b=b, s=s, hr=x.reshape(b * s, d),
                            hm=jnp.tile(meta_tokens.astype(F32), (b, 1))))
    zeros_d = jnp.zeros((d,), F32)

    conv_w = jnp.pad(hy_conv_w[0], ((0, 5), (0, 0)))
    conv_b = hy_conv_b[0][None, :]
    zs = _hyena_layer(streams, fps[0], norm_mix[0], hy_w_in[0].astype(BF16), conv_w, conv_b, hy_skip[0], tm)
    w_out = hy_w_out[0].astype(BF16)
    w1 = [w.astype(BF16) for w in mlp_w1]
    w2 = [w.astype(BF16) for w in mlp_w2]
    for st, (zr, zm) in zip(streams, zs):
        st["hr"] = _mixer_out_mlp(st["hr"], zr, w_out, hy_b_out[0], norm_mlp[0], w1[0], w2[0], zeros_d, tm, False)
        st["hm"] = _mixer_out_mlp(st["hm"], zm, w_out, hy_b_out[0], norm_mlp[0], w1[0], w2[0], zeros_d,
                                  st["hm"].shape[0], False)

    w_rows, w_kt = _attention_weights(at_w_qkv[0], n_heads)
    w_o = at_w_o[0].astype(BF16)
    outs = []
    for st in streams:
        qvk_r, kt_r = _norm_matmul(st["hr"], norm_mix[1], w_rows, tm, wt=w_kt)
        qvk_m = _norm_matmul(st["hm"], norm_mix[1], w_rows, st["hm"].shape[0])
        kt_m = _meta_keys_transposed(qvk_m, st["b"], n_heads)
        att = _attention(qvk_r, kt_r, qvk_m, kt_m, at_sink[0], b=st["b"], s=st["s"], n_heads=n_heads)
        y = _mixer_out_mlp(st["hr"], att, w_o, zeros_d, norm_mlp[1], w1[1], w2[1], norm_final, tm, True)
        outs.append(y.reshape(st["b"], st["s"], d))
    return tuple(outs)


def kernel(x_prompt, x_sample, meta_tokens, norm_mix, norm_mlp, norm_final, hy_w_in, hy_conv_w, hy_conv_b,
           hy_f_w1, hy_f_b1, hy_f_w2, hy_f_b2, hy_f_w3, hy_f_b3, hy_f_wout, hy_f_freq, hy_skip, hy_w_out,
           hy_b_out, at_w_qkv, at_sink, at_w_o, mlp_w1, mlp_w2):
    fps = [dict(w1=hy_f_w1[j], b1=hy_f_b1[j], w2=hy_f_w2[j], b2=hy_f_b2[j], w3=hy_f_w3[j], b3=hy_f_b3[j],
                wout=hy_f_wout[j], freq=hy_f_freq[j]) for j in range(hy_f_w1.shape[0])]
    n_heads = at_sink.shape[1]
    return _encoder_pair(x_prompt, x_sample, meta_tokens, norm_mix, norm_mlp, norm_final,
                         hy_w_in, hy_conv_w, hy_conv_b, fps, hy_skip, hy_w_out, hy_b_out,
                         at_w_qkv, at_sink, at_w_o, mlp_w1, mlp_w2, n_heads=n_heads, tm=512)
```

```python
import functools
import math

import jax
import jax.numpy as jnp
from jax import lax
from jax.experimental import pallas as pl
from jax.experimental.pallas import tpu as pltpu

F32 = jnp.float32
BF16 = jnp.bfloat16

N_META = 16
RMS_EPS = 1e-6
HY_BANDS = 16
HY_EMB_PAD = 40
HY_FAST_DECAY = 0.3
HY_SLOW_DECAY = 1.5
HY_DECAY_TARGET = 1e-2
ATT_BLOCK = 128
ATT_QBLOCKS = 4
HEAD_DIM = 64
GROUP = 4
MASK_VALUE = -1e30
FF_CHUNK = 1024
LANE_TILE = 128
N2_CHUNK = 32
FUSED_SLABS = 16
FFT_MID_LANES = 256
VMEM_LIMIT = 56 * 1024 * 1024
HIGHEST = lax.Precision.HIGHEST


def _cparams(sem):
    return pltpu.CompilerParams(dimension_semantics=sem, vmem_limit_bytes=VMEM_LIMIT)


def _rms(x, g):
    return x * lax.rsqrt(jnp.mean(x * x, axis=-1, keepdims=True) + RMS_EPS) * g


def _norm_matmul_kernel(x_ref, g_ref, w_ref, *rest):
    u = _rms(x_ref[...], g_ref[...]).astype(BF16)
    if len(rest) == 1:
        (o_ref,) = rest
    else:
        wt_ref, o_ref, ot_ref = rest
        ot_ref[...] = lax.dot_general(wt_ref[...], u, (((1,), (1,)), ((), ())),
                                      preferred_element_type=F32).astype(ot_ref.dtype)
    o_ref[...] = jnp.dot(u, w_ref[...], preferred_element_type=F32).astype(o_ref.dtype)


def _norm_matmul(x, g, w, tm, wt=None):
    rows, d = x.shape
    n = w.shape[1]
    in_specs = [pl.BlockSpec((tm, d), lambda i: (i, 0)),
                pl.BlockSpec((1, d), lambda i: (0, 0)),
                pl.BlockSpec((d, n), lambda i: (0, 0))]
    out_specs = pl.BlockSpec((tm, n), lambda i: (i, 0))
    out_shape = jax.ShapeDtypeStruct((rows, n), BF16)
    args = [x, g.reshape(1, d), w]
    if wt is not None:
        m = wt.shape[0]
        in_specs.append(pl.BlockSpec((m, d), lambda i: (0, 0)))
        out_specs = [out_specs, pl.BlockSpec((m, tm), lambda i: (0, i))]
        out_shape = [out_shape, jax.ShapeDtypeStruct((m, rows), BF16)]
        args.append(wt)
    return pl.pallas_call(
        _norm_matmul_kernel,
        grid=(rows // tm,),
        in_specs=in_specs, out_specs=out_specs, out_shape=out_shape,
        compiler_params=_cparams(("parallel",)),
        name="norm_matmul",
    )(*args)


def _mixer_out_mlp_kernel(h_ref, z_ref, wp_ref, bp_ref, g_ref, w1_ref, w2_ref, gf_ref, o_ref, *, final_norm):
    h = h_ref[...] + jnp.dot(z_ref[...], wp_ref[...], preferred_element_type=F32) + bp_ref[...]
    u = _rms(h, g_ref[...]).astype(BF16)
    acc = h
    d_ff = w1_ref.shape[1]
    for c in range(d_ff // FF_CHUNK):
        a = jnp.dot(u, w1_ref[:, c * FF_CHUNK:(c + 1) * FF_CHUNK], preferred_element_type=F32)
        a = jnp.square(jnp.maximum(a, 0.0)).astype(BF16)
        acc = acc + jnp.dot(a, w2_ref[c * FF_CHUNK:(c + 1) * FF_CHUNK, :], preferred_element_type=F32)
    if final_norm:
        acc = _rms(acc, gf_ref[...])
    o_ref[...] = acc


def _mixer_out_mlp(h, z, wp, bp, g, w1, w2, gf, tm, final_norm):
    rows, d = h.shape
    dz = z.shape[1]
    d_ff = w1.shape[1]
    const = lambda i: (0, 0)
    return pl.pallas_call(
        functools.partial(_mixer_out_mlp_kernel, final_norm=final_norm),
        grid=(rows // tm,),
        in_specs=[pl.BlockSpec((tm, d), lambda i: (i, 0)),
                  pl.BlockSpec((tm, dz), lambda i: (i, 0)),
                  pl.BlockSpec((dz, d), const),
                  pl.BlockSpec((1, d), const),
                  pl.BlockSpec((1, d), const),
                  pl.BlockSpec((d, d_ff), const),
                  pl.BlockSpec((d_ff, d), const),
                  pl.BlockSpec((1, d), const)],
        out_specs=pl.BlockSpec((tm, d), lambda i: (i, 0)),
        out_shape=jax.ShapeDtypeStruct((rows, d), F32),
        compiler_params=_cparams(("parallel",)),
        name="mixer_out_mlp",
    )(h, z, wp, bp.reshape(1, d), g.reshape(1, d), w1, w2, gf.reshape(1, d))


def _filter_kernel(z_ref, w1_ref, b1_ref, w2_ref, b2_ref, w3_ref, b3_ref, fr_ref, wo_ref, ad_ref, o_ref):
    d = ad_ref.shape[1]
    dot = functools.partial(jnp.dot, precision=HIGHEST, preferred_element_type=F32)
    fr = fr_ref[...]
    h = jnp.sin(fr * (dot(z_ref[...], w1_ref[...]) + b1_ref[...]))
    h = jnp.sin(fr * (dot(h, w2_ref[...]) + b2_ref[...]))
    h = jnp.sin(fr * (dot(h, w3_ref[...]) + b3_ref[...]))
    ho = jnp.dot(h.astype(BF16), wo_ref[...], preferred_element_type=F32)
    decay = jnp.exp(-z_ref[:, 0:1] * ad_ref[...])
    for o in range(2):
        o_ref[o] = ho[:, o * d:(o + 1) * d] * decay


def _filter_rows(lag, first_bwd_tile, tr, seq_len, fp, d):
    rows = lag.shape[0]
    lagf = lag.astype(F32)
    t = lagf / (seq_len - 1)
    w = 2.0 * math.pi * lagf / seq_len
    f = jnp.linspace(1e-4, HY_BANDS - 1, HY_BANDS, dtype=F32)[None, :]
    z = jnp.concatenate([t[:, None], jnp.cos(f * w[:, None]), -jnp.sin(f * w[:, None]),
                         jnp.zeros((rows, HY_EMB_PAD - 2 * HY_BANDS - 1), F32)], axis=-1)
    w1 = jnp.pad(fp["w1"], ((0, HY_EMB_PAD - fp["w1"].shape[0]), (0, 0)))
    hid = w1.shape[1]
    max_decay = math.log(HY_DECAY_TARGET) / HY_FAST_DECAY
    min_decay = math.log(HY_DECAY_TARGET) / HY_SLOW_DECAY
    adel = jnp.abs(jnp.linspace(min_decay, max_decay, d, dtype=F32))[None, :]
    const = lambda i: (0, 0)
    row = lambda i: (i, 0)
    wo = jnp.transpose(fp["wout"].reshape(hid, 2, 2, d), (2, 0, 1, 3)).reshape(2, hid, 2 * d)
    wo = wo.astype(BF16)
    wo_spec = pl.BlockSpec((None, hid, 2 * d), lambda i: ((i >= first_bwd_tile).astype(jnp.int32), 0, 0))
    return pl.pallas_call(
        _filter_kernel,
        grid=(rows // tr,),
        in_specs=[pl.BlockSpec((tr, HY_EMB_PAD), row),
                  pl.BlockSpec((HY_EMB_PAD, hid), const), pl.BlockSpec((1, hid), const),
                  pl.BlockSpec((hid, hid), const), pl.BlockSpec((1, hid), const),
                  pl.BlockSpec((hid, hid), const), pl.BlockSpec((1, hid), const),
                  pl.BlockSpec((1, hid), const), wo_spec,
                  pl.BlockSpec((1, d), const)],
        out_specs=pl.BlockSpec((2, tr, d), lambda i: (0, i, 0)),
        out_shape=jax.ShapeDtypeStruct((2, rows, d), F32),
        compiler_params=_cparams(("parallel",)),
        name="hyena_filter",
    )(z, w1, fp["b1"][None], fp["w2"], fp["b2"][None],
      fp["w3"], fp["b3"][None], fp["freq"][None], wo, adel)


def _cplx_block(re, im):
    return jnp.concatenate([jnp.concatenate([re, -im], axis=-1), jnp.concatenate([im, re], axis=-1)], axis=-2)


def _fft_tables(n1, n2):
    n = n1 * n2
    n1h = n1 // 2
    pad = 8 - 1
    k1 = jnp.arange(n1, dtype=jnp.int32)
    c2 = jnp.arange(n2, dtype=jnp.int32)
    def phase(idx, period):
        ang = (idx % period).astype(F32) * (-2.0 * math.pi / period)
        return jnp.cos(ang), jnp.sin(ang)

    def twiddled_dft(cols):
        ar, ai = phase(k1[:, None] * cols[None, :], n1)
        br, bi = phase(c2[:, None] * k1[None, :], n)
        return (ar[None] * br[:, :, None] - ai[None] * bi[:, :, None],
                ai[None] * br[:, :, None] + ar[None] * bi[:, :, None])

    cols = jnp.concatenate([jnp.arange(n1h, dtype=jnp.int32), jnp.array([n1 - 1], jnp.int32)])
    wr, wi = twiddled_dft(cols)
    wr = jnp.pad(wr, ((0, 0), (0, 0), (0, pad)))
    wi = jnp.pad(wi, ((0, 0), (0, 0), (0, pad)))
    fwd1 = _cplx_block(wr, wi)
    inv1 = _cplx_block(jnp.swapaxes(wr, 1, 2), -jnp.swapaxes(wi, 1, 2)) / n
    fil1 = jnp.concatenate(twiddled_dft(k1), axis=1)
    fr, fi = phase(c2[:, None] * c2[None, :], n2)
    fwd2 = _cplx_block(fr, fi)
    return dict(fwd1=fwd1.astype(BF16), inv1=inv1.astype(BF16), fil1=fil1.astype(BF16),
                fwd2=fwd2.astype(BF16), inv2=_cplx_block(fr, -fi).astype(BF16))


def _fill_sequence(meta_ref, real_ref, seq_ref):
    s, dt = real_ref.shape
    seq_ref[0:8, :] = jnp.zeros((8, dt), F32)
    seq_ref[8:8 + N_META, :] = meta_ref[...].astype(F32)
    seq_ref[8 + N_META:8 + N_META + s, :] = real_ref[...].astype(F32)
    seq_ref[8 + N_META + s:16 + N_META + s, :] = jnp.zeros((8, dt), F32)


def _short_conv_rows(seq_ref, cw, cb, start, rows):
    return (seq_ref[pl.ds(start - 1, rows), :] * cw[0:1] + seq_ref[pl.ds(start, rows), :] * cw[1:2]
            + seq_ref[pl.ds(start + 1, rows), :] * cw[2:3] + cb)


def _row_chunk(s):
    return min(s, 1024)


def _load_sequences(xr_ref, xm_ref, conv_refs, ut_ref, xs_ref, mp_ref, seq_ref, n2):
    n1h, dt = xs_ref.shape[1], xs_ref.shape[3]
    s = n1h * n2
    ch = _row_chunk(s)
    for e in range(2):
        if conv_refs is not None:
            cw, cb = conv_refs[0][...].astype(F32), conv_refs[1][...].astype(F32)
            _fill_sequence(xm_ref.at[e], xr_ref.at[e], seq_ref)
            meta = _short_conv_rows(seq_ref, cw, cb, 8, N_META)
        else:
            meta = xm_ref[e].astype(F32)
        ut_ref[e, 0:N_META, :] = meta
        for c in range(s // ch):
            if conv_refs is not None:
                real = _short_conv_rows(seq_ref, cw, cb, 8 + N_META + c * ch, ch)
            else:
                real = xr_ref[e, c * ch:(c + 1) * ch, :].astype(F32)
            xs_ref[e, c * ch // n2:(c + 1) * ch // n2] = real.reshape(ch // n2, n2, dt)
        ut_ref[e, N_META:2 * N_META, :] = xs_ref[e, n1h - 1, n2 - 16:n2, :]
        mp_ref[e] = jnp.zeros(mp_ref.shape[1:], F32)
        for j in range(N_META):
            mp_ref[e, pl.ds(8 * (n2 - N_META + j), 1), :] = meta[j:j + 1]


def _gate_sequences(xs_ref, mp_ref, g_refs, conv_refs, ut_ref, tail_ref, z_refs, seq_ref, n2):
    n1h, dt = xs_ref.shape[1], xs_ref.shape[3]
    s = n1h * n2
    ch = _row_chunk(s)
    gr_ref, gm_ref = g_refs
    zr_ref, zm_ref = z_refs
    cw, cb = conv_refs[0][...].astype(F32), conv_refs[1][...].astype(F32)
    tail = tail_ref[...]
    for e in range(2):
        y_meta = jnp.concatenate(
            [mp_ref[e, pl.ds(8 * (n2 - N_META + j), 1), :] for j in range(N_META)], axis=0)
        real_fix, meta_fix = _alias_patch(tail, ut_ref[e, 0:N_META, :], ut_ref[e, N_META:2 * N_META, :])
        xs_ref[e, n1h - 1, n2 - 16:n2, :] = xs_ref[e, n1h - 1, n2 - 16:n2, :] + real_fix
        _fill_sequence(gm_ref.at[e], gr_ref.at[e], seq_ref)
        g_meta = _short_conv_rows(seq_ref, cw, cb, 8, N_META)
        zm_ref[e] = (g_meta * (y_meta + meta_fix)).astype(zm_ref.dtype)
        for c in range(s // ch):
            g = _short_conv_rows(seq_ref, cw, cb, 8 + N_META + c * ch, ch)
            y = xs_ref[e, c * ch // n2:(c + 1) * ch // n2].reshape(ch, dt)
            zr_ref[e, c * ch:(c + 1) * ch, :] = (g * y).astype(zr_ref.dtype)


def _grid_cols(s):
    return N2_CHUNK // 2 if s > 4096 else N2_CHUNK


def _fft_in_kernel(*refs, n1, n2, cols, short_conv):
    if short_conv:
        xr_ref, xm_ref, cw_ref, cb_ref, f_ref, are_ref, aim_ref, ut_ref, xs_ref, mp_ref, seq_ref = refs
    else:
        xr_ref, xm_ref, f_ref, are_ref, aim_ref, ut_ref, xs_ref, mp_ref = refs
    n1h = n1 // 2
    chunk = pl.program_id(2)

    @pl.when(chunk == 0)
    def _prepare():
        if short_conv:
            _load_sequences(xr_ref, xm_ref, (cw_ref, cb_ref), ut_ref, xs_ref, mp_ref, seq_ref, n2)
        else:
            _load_sequences(xr_ref, xm_ref, None, ut_ref, xs_ref, mp_ref, None, n2)

    base = pl.multiple_of(chunk * cols, cols)
    xt = [jnp.swapaxes(xs_ref[e, :, pl.ds(base, cols), :], 0, 1) for e in range(2)]
    for i in range(cols):
        parts = []
        for e in range(2):
            parts += [xt[e][i], mp_ref[e, pl.ds(pl.multiple_of((base + i) * 8, 8), 8), :]]
        rhs = jnp.concatenate(parts, axis=0).astype(BF16)
        out = jnp.dot(f_ref[i], rhs, preferred_element_type=F32)
        are_ref[i] = out[:n1]
        aim_ref[i] = out[n1:]


def _fft_in(xr, xm, col_off, conv_w, conv_b, tab, *, b, s, d, n1, n2, dt):
    short_conv = conv_w is not None
    pairs = b // 2
    c = xr.shape[1]
    cb0 = col_off // dt
    cols = _grid_cols(s)
    xr4 = xr.reshape(pairs, 2, s, c)
    xm4 = xm.reshape(pairs, 2, N_META, c)
    kk = tab["fwd1"].shape[2]
    in_specs = [pl.BlockSpec((None, 2, s, dt), lambda p, j, t: (p, 0, 0, cb0 + j)),
                pl.BlockSpec((None, 2, N_META, dt), lambda p, j, t: (p, 0, 0, cb0 + j))]
    args = [xr4, xm4]
    if short_conv:
        in_specs += [pl.BlockSpec((8, dt), lambda p, j, t: (0, cb0 + j)),
                     pl.BlockSpec((1, dt), lambda p, j, t: (0, cb0 + j))]
        args += [conv_w, conv_b]
    in_specs.append(pl.BlockSpec((cols, 2 * n1, kk), lambda p, j, t: (t, 0, 0)))
    args.append(tab["fwd1"])
    a_spec = pl.BlockSpec((None, None, cols, n1, dt), lambda p, j, t: (p, j, t, 0, 0))
    a_shape = jax.ShapeDtypeStruct((pairs, d // dt, n2, n1, dt), F32)
    out_specs = [a_spec, a_spec, pl.BlockSpec((None, 2, 2 * N_META, dt), lambda p, j, t: (p, 0, 0, j))]
    out_shape = [a_shape, a_shape, jax.ShapeDtypeStruct((pairs, 2, 2 * N_META, d), F32)]
    scratch = [pltpu.VMEM((2, n1 // 2, n2, dt), F32), pltpu.VMEM((2, 8 * n2, dt), F32)]
    if short_conv:
        scratch.append(pltpu.VMEM((s + N_META + 16, dt), F32))
    return pl.pallas_call(
        functools.partial(_fft_in_kernel, n1=n1, n2=n2, cols=cols, short_conv=short_conv),
        grid=(pairs, d // dt, n2 // cols),
        in_specs=in_specs, out_specs=out_specs, out_shape=out_shape, scratch_shapes=scratch,
        compiler_params=_cparams(("parallel", "parallel", "arbitrary")),
        name="fft_in",
    )(*args)


def _fft_mid_kernel(are_ref, aim_ref, kre_ref, kim_ref, f_ref, g_ref, bre_ref, bim_ref, *, n2, slabs):
    tiles = are_ref.shape[0]
    lt = are_ref.shape[3]
    xr_blk = [jnp.swapaxes(are_ref[h], 0, 1) for h in range(tiles)]
    xi_blk = [jnp.swapaxes(aim_ref[h], 0, 1) for h in range(tiles)]
    ys = []
    for i in range(slabs):
        x = jnp.concatenate([jnp.concatenate([xr_blk[h][i], xi_blk[h][i]], axis=0) for h in range(tiles)],
                            axis=1).astype(BF16)
        z = jnp.dot(f_ref[...], x, preferred_element_type=F32)
        zr, zi = z[:n2], z[n2:]
        kr = jnp.concatenate([kre_ref[h, i] for h in range(tiles)], axis=1)
        ki = jnp.concatenate([kim_ref[h, i] for h in range(tiles)], axis=1)
        p = jnp.concatenate([zr * kr - zi * ki, zr * ki + zi * kr], axis=0).astype(BF16)
        ys.append(jnp.dot(g_ref[...], p, preferred_element_type=F32))
    yt = jnp.swapaxes(jnp.stack(ys, axis=0), 0, 1)
    for h in range(tiles):
        bre_ref[h] = yt[:n2, :, h * lt:(h + 1) * lt]
        bim_ref[h] = yt[n2:, :, h * lt:(h + 1) * lt]


def _fft_mid(are, aim, kre, kim, order, tab, *, dt, slabs):
    pairs, ntiles, n2, n1, lt = are.shape
    d = ntiles * lt
    blk = (None, dt // lt, n2, slabs, lt)
    amap = lambda j, kb, p: (p, j, 0, kb, 0)
    kmap = lambda j, kb, p: (order, j, kb, 0, 0)
    const = lambda j, kb, p: (0, 0)
    shp = jax.ShapeDtypeStruct(are.shape, F32)
    return pl.pallas_call(
        functools.partial(_fft_mid_kernel, n2=n2, slabs=slabs),
        grid=(d // dt, n1 // slabs, pairs),
        in_specs=[pl.BlockSpec(blk, amap), pl.BlockSpec(blk, amap),
                  pl.BlockSpec((None, dt // lt, slabs, n2, lt), kmap),
                  pl.BlockSpec((None, dt // lt, slabs, n2, lt), kmap),
                  pl.BlockSpec((2 * n2, 2 * n2), const), pl.BlockSpec((2 * n2, 2 * n2), const)],
        out_specs=[pl.BlockSpec(blk, amap), pl.BlockSpec(blk, amap)],
        out_shape=[shp, shp],
        compiler_params=_cparams(("parallel", "parallel", "arbitrary")),
        name="fft_mid",
    )(are, aim, kre, kim, tab["fwd2"], tab["inv2"])


def _filter_fft_in_kernel(c_ref, f_ref, are_ref, aim_ref, *, n1, n2):
    base = pl.multiple_of(pl.program_id(2) * N2_CHUNK, N2_CHUNK)
    ct = jnp.swapaxes(c_ref[:, pl.ds(base, N2_CHUNK), :], 0, 1).astype(BF16)
    for i in range(N2_CHUNK):
        out = jnp.dot(f_ref[i], ct[i], preferred_element_type=F32)
        are_ref[i] = out[:n1]
        aim_ref[i] = out[n1:]


def _filter_fft_mid_kernel(are_ref, aim_ref, f_ref, kre_ref, kim_ref, *, n2, slabs):
    tiles = are_ref.shape[0]
    lt = are_ref.shape[3]
    xr_blk = [jnp.swapaxes(are_ref[h], 0, 1) for h in range(tiles)]
    xi_blk = [jnp.swapaxes(aim_ref[h], 0, 1) for h in range(tiles)]
    for i in range(slabs):
        x = jnp.concatenate([jnp.concatenate([xr_blk[h][i], xi_blk[h][i]], axis=0) for h in range(tiles)], axis=1)
        z = jnp.dot(f_ref[...], x.astype(BF16), preferred_element_type=F32)
        for h in range(tiles):
            kre_ref[h, i] = z[:n2, h * lt:(h + 1) * lt]
            kim_ref[h, i] = z[n2:, h * lt:(h + 1) * lt]


def _filter_spectrum(circ, tab, *, n1, n2, dt, slabs):
    orders, n, d = circ.shape
    lt = LANE_TILE
    shp = [jax.ShapeDtypeStruct((orders, d // lt, n2, n1, lt), F32)] * 2
    tab_spec = pl.BlockSpec((N2_CHUNK, 2 * n1, n1), lambda o, j, t: (t, 0, 0))
    are, aim = pl.pallas_call(
        functools.partial(_filter_fft_in_kernel, n1=n1, n2=n2),
        grid=(orders, d // lt, n2 // N2_CHUNK),
        in_specs=[pl.BlockSpec((None, n1, n2, lt), lambda o, j, t: (o, 0, 0, j)), tab_spec],
        out_specs=[pl.BlockSpec((None, None, N2_CHUNK, n1, lt), lambda o, j, t: (o, j, t, 0, 0))] * 2,
        out_shape=shp,
        compiler_params=_cparams(("parallel", "parallel", "arbitrary")),
        name="filter_fft_in",
    )(circ.reshape(orders, n1, n2, d), tab["fil1"])
    blk = (None, dt // lt, n2, slabs, lt)
    amap = lambda o, kb, j: (o, j, 0, kb, 0)
    f_spec = pl.BlockSpec((2 * n2, 2 * n2), lambda o, kb, j: (0, 0))
    return pl.pallas_call(
        functools.partial(_filter_fft_mid_kernel, n2=n2, slabs=slabs),
        grid=(orders, n1 // slabs, d // dt),
        in_specs=[pl.BlockSpec(blk, amap), pl.BlockSpec(blk, amap), f_spec],
        out_specs=[pl.BlockSpec((None, dt // lt, slabs, n2, lt), lambda o, kb, j: (o, j, kb, 0, 0))] * 2,
        out_shape=[jax.ShapeDtypeStruct((orders, d // lt, n1, n2, lt), F32)] * 2,
        compiler_params=_cparams(("parallel", "parallel", "parallel")),
        name="filter_fft_mid",
    )(are, aim, tab["fwd2"])


def _alias_patch(tail, u_meta, u_last):
    dfw = tail[0:16] - tail[16:32]
    dbw = tail[32:48] - tail[48:64]
    ridx = lax.broadcasted_iota(jnp.int32, dfw.shape, 0)
    real_fix = jnp.zeros_like(dfw)
    meta_fix = jnp.zeros_like(dfw)
    for o in range(16):
        src = dfw if o == 0 else pltpu.roll(dfw, o, axis=0)
        real_fix = real_fix + jnp.where(ridx >= o, src, 0.0) * u_meta[o:o + 1]
    for c in range(1, 16):
        meta_fix = meta_fix + jnp.where(ridx + c <= 15, pltpu.roll(u_last, 16 - c, axis=0), 0.0) * dbw[c:c + 1]
    return real_fix, meta_fix


def _fft_out_kernel(bre_ref, bim_ref, g_ref, gr_ref, gm_ref, cw_ref, cb_ref, ut_ref, tail_ref, zr_ref, zm_ref,
                    ys_ref, yp_ref, seq_ref, *, n1, n2, cols):
    n1h = n1 // 2
    kk = n1h + 8
    s = n1h * n2
    chunk = pl.program_id(2)

    base = pl.multiple_of(chunk * cols, cols)
    ys = []
    for i in range(cols):
        rhs = jnp.concatenate([bre_ref[i], bim_ref[i]], axis=0).astype(BF16)
        y = jnp.dot(g_ref[i], rhs, preferred_element_type=F32)
        ys.append(y)
        for e in range(2):
            yp_ref[e, pl.ds(pl.multiple_of((base + i) * 8, 8), 8), :] = y[e * kk + n1h:(e + 1) * kk]
    yt = jnp.swapaxes(jnp.stack(ys, axis=0), 0, 1)
    for e in range(2):
        ys_ref[e, :, pl.ds(base, cols), :] = yt[e * kk:e * kk + n1h]

    @pl.when(chunk == n2 // cols - 1)
    def _gate():
        _gate_sequences(ys_ref, yp_ref, (gr_ref, gm_ref), (cw_ref, cb_ref), ut_ref, tail_ref, (zr_ref, zm_ref),
                        seq_ref, n2)


def _fft_out(bre, bim, tab, gr, gm, gate_off, conv_w, conv_b, ut, tail, *, b, s, d, n1, n2, dt):
    pairs = b // 2
    c = gr.shape[1]
    gb0 = gate_off // dt
    kk2 = tab["inv1"].shape[1]
    gr4 = gr.reshape(pairs, 2, s, c)
    gm4 = gm.reshape(pairs, 2, N_META, c)
    cols = _grid_cols(s)
    b_spec = pl.BlockSpec((None, None, cols, n1, dt), lambda p, j, t: (p, j, t, 0, 0))
    zr, zm = pl.pallas_call(
        functools.partial(_fft_out_kernel, n1=n1, n2=n2, cols=cols),
        grid=(pairs, d // dt, n2 // cols),
        in_specs=[b_spec, b_spec,
                  pl.BlockSpec((cols, kk2, 2 * n1), lambda p, j, t: (t, 0, 0)),
                  pl.BlockSpec((None, 2, s, dt), lambda p, j, t: (p, 0, 0, gb0 + j)),
                  pl.BlockSpec((None, 2, N_META, dt), lambda p, j, t: (p, 0, 0, gb0 + j)),
                  pl.BlockSpec((8, dt), lambda p, j, t: (0, gb0 + j)),
                  pl.BlockSpec((1, dt), lambda p, j, t: (0, gb0 + j)),
                  pl.BlockSpec((None, 2, 2 * N_META, dt), lambda p, j, t: (p, 0, 0, j)),
                  pl.BlockSpec((64, dt), lambda p, j, t: (0, j))],
        out_specs=[pl.BlockSpec((None, 2, s, dt), lambda p, j, t: (p, 0, 0, j)),
                   pl.BlockSpec((None, 2, N_META, dt), lambda p, j, t: (p, 0, 0, j))],
        out_shape=[jax.ShapeDtypeStruct((pairs, 2, s, d), BF16),
                   jax.ShapeDtypeStruct((pairs, 2, N_META, d), BF16)],
        scratch_shapes=[pltpu.VMEM((2, n1 // 2, n2, dt), F32), pltpu.VMEM((2, 8 * n2, dt), F32),
                        pltpu.VMEM((s + N_META + 16, dt), F32)],
        compiler_params=_cparams(("parallel", "parallel", "arbitrary")),
        name="fft_out",
    )(bre, bim, tab["inv1"], gr4, gm4, conv_w, conv_b, ut, tail)
    return zr.reshape(b * s, d), zm.reshape(b * N_META, d)


def _fused_conv_kernel(*refs, n1, n2, short_conv):
    if short_conv:
        (xr_ref, xm_ref, cwx_ref, cbx_ref, f1_ref, kre_ref, kim_ref, f2_ref, g2_ref, g1_ref, gr_ref, gm_ref,
         cwg_ref, cbg_ref, tail_ref, zr_ref, zm_ref, are_ref, aim_ref, xs_ref, mp_ref, ut_ref, seq_ref) = refs
        conv_x = (cwx_ref, cbx_ref)
    else:
        (xr_ref, xm_ref, f1_ref, kre_ref, kim_ref, f2_ref, g2_ref, g1_ref, gr_ref, gm_ref,
         cwg_ref, cbg_ref, tail_ref, zr_ref, zm_ref, are_ref, aim_ref, xs_ref, mp_ref, ut_ref, seq_ref) = refs
        conv_x = None
    n1h = n1 // 2
    kk = n1h + 8
    s = n1h * n2
    nkb = n1 // FUSED_SLABS
    dt = xs_ref.shape[3]

    _load_sequences(xr_ref, xm_ref, conv_x, ut_ref, xs_ref, mp_ref, seq_ref, n2)

    def stage1(t, carry):
        base = pl.multiple_of(t * N2_CHUNK, N2_CHUNK)
        xt = [jnp.swapaxes(xs_ref[e, :, pl.ds(base, N2_CHUNK), :], 0, 1) for e in range(2)]
        for i in range(N2_CHUNK):
            parts = []
            for e in range(2):
                parts += [xt[e][i], mp_ref[e, pl.ds(pl.multiple_of((base + i) * 8, 8), 8), :]]
            rhs = jnp.concatenate(parts, axis=0).astype(BF16)
            out = jnp.dot(f1_ref[base + i], rhs, preferred_element_type=F32).astype(BF16)
            are_ref[:, base + i] = out[:n1].reshape(nkb, FUSED_SLABS, dt)
            aim_ref[:, base + i] = out[n1:].reshape(nkb, FUSED_SLABS, dt)
        return carry

    def stage2(kb, carry):
        k0 = pl.multiple_of(kb * FUSED_SLABS, FUSED_SLABS)
        xr_blk = jnp.swapaxes(are_ref[kb], 0, 1)
        xi_blk = jnp.swapaxes(aim_ref[kb], 0, 1)
        ys = []
        for i in range(0, FUSED_SLABS, 2):
            x = jnp.concatenate([jnp.concatenate([xr_blk[i + h], xi_blk[i + h]], axis=0) for h in range(2)],
                                axis=1)
            z = jnp.dot(f2_ref[...], x, preferred_element_type=F32)
            zr, zi = z[:n2], z[n2:]
            kr = jnp.concatenate([kre_ref[k0 + i], kre_ref[k0 + i + 1]], axis=1)
            ki = jnp.concatenate([kim_ref[k0 + i], kim_ref[k0 + i + 1]], axis=1)
            p = jnp.concatenate([zr * kr - zi * ki, zr * ki + zi * kr], axis=0).astype(BF16)
            y = jnp.dot(g2_ref[...], p, preferred_element_type=F32).astype(BF16)
            ys += [y[:, :dt], y[:, dt:]]
        yt = jnp.swapaxes(jnp.stack(ys, axis=0), 0, 1)
        are_ref[kb] = yt[:n2]
        aim_ref[kb] = yt[n2:]
        return carry

    def stage3(t, carry):
        base = pl.multiple_of(t * N2_CHUNK, N2_CHUNK)
        ys = []
        for i in range(N2_CHUNK):
            rhs = jnp.concatenate([are_ref[:, base + i].reshape(n1, dt), aim_ref[:, base + i].reshape(n1, dt)],
                                  axis=0)
            y = jnp.dot(g1_ref[base + i], rhs, preferred_element_type=F32)
            ys.append(y)
            for e in range(2):
                mp_ref[e, pl.ds(pl.multiple_of((base + i) * 8, 8), 8), :] = y[e * kk + n1h:(e + 1) * kk]
        yt = jnp.swapaxes(jnp.stack(ys, axis=0), 0, 1)
        for e in range(2):
            xs_ref[e, :, pl.ds(base, N2_CHUNK), :] = yt[e * kk:e * kk + n1h]
        return carry

    lax.fori_loop(0, n2 // N2_CHUNK, stage1, 0)
    lax.fori_loop(0, nkb, stage2, 0)
    lax.fori_loop(0, n2 // N2_CHUNK, stage3, 0)

    _gate_sequences(xs_ref, mp_ref, (gr_ref, gm_ref), (cwg_ref, cbg_ref), ut_ref, tail_ref, (zr_ref, zm_ref),
                    seq_ref, n2)


def _fused_conv(xr, xm, col_off, conv_w, conv_b, kre, kim, order, tab, gr, gm, gate_off, tail, *,
                b, s, d, n1, n2):
    short_conv = col_off is not None
    pairs = b // 2
    dt = LANE_TILE
    cx0 = (col_off or 0) // dt
    cg0 = gate_off // dt
    kk2 = tab["fwd1"].shape[2]
    once = dict(pipeline_mode=pl.Buffered(1))
    seq4 = lambda a, rows: a.reshape(pairs, 2, rows, a.shape[1])
    x_spec = lambda rows, c0: pl.BlockSpec((None, 2, rows, dt), lambda j, p: (p, 0, 0, c0 + j))
    row_spec = lambda rows, c0: pl.BlockSpec((rows, dt), lambda j, p: (0, c0 + j))
    const2 = pl.BlockSpec((2 * n2, 2 * n2), lambda j, p: (0, 0), **once)
    k_spec = pl.BlockSpec((None, None, n1, n2, dt), lambda j, p: (order, j, 0, 0, 0), **once)
    in_specs = [x_spec(s, cx0), x_spec(N_META, cx0)]
    args = [seq4(xr, s), seq4(xm, N_META)]
    if short_conv:
        in_specs += [row_spec(8, cx0), row_spec(1, cx0)]
        args += [conv_w, conv_b]
    in_specs += [pl.BlockSpec((n2, 2 * n1, kk2), lambda j, p: (0, 0, 0), **once),
                 k_spec, k_spec, const2, const2,
                 pl.BlockSpec((n2, kk2, 2 * n1), lambda j, p: (0, 0, 0), **once),
                 x_spec(s, cg0), x_spec(N_META, cg0), row_spec(8, cg0), row_spec(1, cg0), row_spec(64, 0)]
    args += [tab["fwd1"], kre, kim, tab["fwd2"], tab["inv2"], tab["inv1"],
             seq4(gr, s), seq4(gm, N_META), conv_w, conv_b, tail]
    seq_out = lambda rows: pl.BlockSpec((None, 2, rows, dt), lambda j, p: (p, 0, 0, j))
    out_specs = [seq_out(s), seq_out(N_META)]
    out_shape = [jax.ShapeDtypeStruct((pairs, 2, s, d), BF16), jax.ShapeDtypeStruct((pairs, 2, N_META, d), BF16)]
    scratch = ([pltpu.VMEM((n1 // FUSED_SLABS, n2, FUSED_SLABS, dt), BF16)] * 2
               + [pltpu.VMEM((2, n1 // 2, n2, dt), F32), pltpu.VMEM((2, 8 * n2, dt), F32),
                  pltpu.VMEM((2, 2 * N_META, dt), F32), pltpu.VMEM((s + N_META + 16, dt), F32)])
    outs = pl.pallas_call(
        functools.partial(_fused_conv_kernel, n1=n1, n2=n2, short_conv=short_conv),
        grid=(d // dt, pairs),
        in_specs=in_specs, out_specs=out_specs, out_shape=out_shape, scratch_shapes=scratch,
        compiler_params=_cparams(("parallel", "parallel")),
        name="fused_conv",
    )(*args)
    return [o.reshape(-1, d) for o in outs]


def _attn_kernel(own_ref, vprev_ref, vnext_ref, vmeta_ref, ktp_ref, kto_ref, ktn_ref, ktm_ref,
                 bias_a_ref, bias_mid_ref, bias_b_ref, shift_ref, o_ref, *, groups, qblocks):
    gw = GROUP * HEAD_DIM
    qd = groups * gw
    lane = lax.broadcasted_iota(jnp.int32, (1, LANE_TILE), 1)
    low = lane < HEAD_DIM
    pad_rows = jnp.zeros((ATT_BLOCK - N_META, 2 * HEAD_DIM), BF16)
    zero = jnp.zeros((ATT_BLOCK, LANE_TILE), BF16)
    block = lambda i: slice(i * ATT_BLOCK, (i + 1) * ATT_BLOCK)
    for sub in range(qblocks):
        rows = block(sub)
        bias_ref = bias_a_ref if sub == 0 else (bias_b_ref if sub == qblocks - 1 else bias_mid_ref)
        blk = (qblocks * pl.program_id(1) + sub).astype(F32)
        for g in range(groups):
            grp = slice(g * LANE_TILE, (g + 1) * LANE_TILE)
            own_cols = slice(qd + g * LANE_TILE, qd + (g + 1) * LANE_TILE)
            kt_rows = slice(g * HEAD_DIM, (g + 1) * HEAD_DIM)
            vk_band, kt_band = [], []
            for nb in (sub - 1, sub, sub + 1):
                if nb < 0:
                    vk_band.append(vprev_ref[:, grp])
                    kt_band.append(ktp_ref[kt_rows, :])
                elif nb == qblocks:
                    vk_band.append(vnext_ref[:, grp])
                    kt_band.append(ktn_ref[kt_rows, :])
                else:
                    vk_band.append(own_ref[block(nb), own_cols])
                    kt_band.append(kto_ref[kt_rows, block(nb)])
            vk = jnp.concatenate(vk_band + [vmeta_ref[:, grp], pad_rows], axis=0)
            v_ones = jnp.where(low, vk, jnp.ones_like(vk))
            kt = jnp.concatenate(kt_band + [ktm_ref[kt_rows, :]], axis=1)
            kt2 = jnp.concatenate([kt, kt], axis=0)
            parts = []
            for pr in range(GROUP // 2):
                qp = own_ref[rows, g * gw + pr * LANE_TILE:g * gw + (pr + 1) * LANE_TILE]
                parts += [jnp.where(low, qp, zero), jnp.where(low, zero, qp)]
            q = jnp.concatenate(parts, axis=0)
            sc = jnp.dot(q, kt2, preferred_element_type=F32) + bias_ref[g]
            t = [sc[:, i * LANE_TILE:(i + 1) * LANE_TILE] for i in range(3)]
            t.append(sc[:, 3 * LANE_TILE:] - shift_ref[g] * blk)
            m = jnp.max(jnp.maximum(jnp.maximum(t[0], t[1]), jnp.maximum(t[2], t[3])), axis=1, keepdims=True)
            p = jnp.concatenate([jnp.exp(x - m) for x in t], axis=1).astype(BF16)
            oa = jnp.dot(p, v_ones, preferred_element_type=F32)
            ob = pltpu.roll(oa, HEAD_DIM, axis=1)
            outs = []
            for pr in range(GROUP // 2):
                ev = slice((2 * pr) * ATT_BLOCK, (2 * pr + 1) * ATT_BLOCK)
                od = slice((2 * pr + 1) * ATT_BLOCK, (2 * pr + 2) * ATT_BLOCK)
                outs.append(jnp.where(low, oa[ev] / ob[ev], ob[od] / oa[od]))
            o_ref[rows, g * gw:(g + 1) * gw] = jnp.concatenate(outs, axis=1).astype(o_ref.dtype)


def _attention_tables(n_heads, sink):
    groups = n_heads // GROUP
    slopes = jnp.exp2(-8.0 * jnp.arange(1, n_heads + 1, dtype=F32) / n_heads)
    i = jnp.arange(ATT_BLOCK, dtype=jnp.int32)[:, None]
    c = jnp.arange(4 * ATT_BLOCK, dtype=jnp.int32)[None, :]
    dist = jnp.abs(c - ATT_BLOCK - i)
    key_blk = c // ATT_BLOCK
    in_band = jnp.logical_and(c < 3 * ATT_BLOCK, dist <= ATT_BLOCK)
    meta_col = jnp.logical_and(c >= 3 * ATT_BLOCK, c < 3 * ATT_BLOCK + N_META)
    sink_col = c == 3 * ATT_BLOCK + N_META
    meta_dist = N_META + i - (c - 3 * ATT_BLOCK)
    tables = []
    for drop in (None, 0, 2):
        ok = in_band if drop is None else jnp.logical_and(in_band, key_blk != drop)
        d_eff = jnp.where(ok, dist, jnp.where(meta_col, meta_dist, 0)).astype(F32)
        live = jnp.logical_or(ok, meta_col)
        tab = jnp.where(live[None], -slopes[:, None, None] * d_eff[None], MASK_VALUE)
        tables.append(jnp.where(sink_col[None], sink.astype(F32)[:, None, None], tab))
    bias = jnp.stack(tables, axis=0).reshape(3, groups, GROUP * ATT_BLOCK, 4 * ATT_BLOCK)
    lane = jnp.arange(LANE_TILE)[None, :]
    shift = jnp.where(lane < N_META, jnp.repeat(slopes * ATT_BLOCK, ATT_BLOCK)[:, None], 0.0)
    return bias, shift.reshape(groups, GROUP * ATT_BLOCK, LANE_TILE)


def _attention(qvk_r, kt_r, qvk_m, kt_m, sink, *, b, s, n_heads):
    groups = n_heads // GROUP
    nblk = s // ATT_BLOCK
    qd = n_heads * HEAD_DIM
    width = qvk_r.shape[1]
    vkw = groups * 2 * HEAD_DIM
    vkb = qd // vkw
    bias, shift = _attention_tables(n_heads, sink)

    qb = ATT_QBLOCKS
    steps = nblk // qb
    prev = lambda i, j: i * nblk + jnp.maximum(qb * j - 1, 0)
    nxt = lambda i, j: i * nblk + jnp.minimum(qb * j + qb, nblk - 1)
    bias_blk = (None, groups, GROUP * ATT_BLOCK, 4 * ATT_BLOCK)
    return pl.pallas_call(
        functools.partial(_attn_kernel, groups=groups, qblocks=qb),
        grid=(b, steps),
        in_specs=[pl.BlockSpec((qb * ATT_BLOCK, width), lambda i, j: (i * steps + j, 0)),
                  pl.BlockSpec((ATT_BLOCK, vkw), lambda i, j: (prev(i, j), vkb)),
                  pl.BlockSpec((ATT_BLOCK, vkw), lambda i, j: (nxt(i, j), vkb)),
                  pl.BlockSpec((N_META, vkw), lambda i, j: (i, vkb)),
                  pl.BlockSpec((groups * HEAD_DIM, ATT_BLOCK), lambda i, j: (0, prev(i, j))),
                  pl.BlockSpec((groups * HEAD_DIM, qb * ATT_BLOCK), lambda i, j: (0, i * steps + j)),
                  pl.BlockSpec((groups * HEAD_DIM, ATT_BLOCK), lambda i, j: (0, nxt(i, j))),
                  pl.BlockSpec((None, groups * HEAD_DIM, LANE_TILE), lambda i, j: (i, 0, 0)),
                  pl.BlockSpec(bias_blk, lambda i, j: (jnp.where(j == 0, 1, 0), 0, 0, 0)),
                  pl.BlockSpec(bias_blk, lambda i, j: (0, 0, 0, 0)),
                  pl.BlockSpec(bias_blk, lambda i, j: (jnp.where(j == steps - 1, 2, 0), 0, 0, 0)),
                  pl.BlockSpec((groups, GROUP * ATT_BLOCK, LANE_TILE), lambda i, j: (0, 0, 0))],
        out_specs=pl.BlockSpec((qb * ATT_BLOCK, qd), lambda i, j: (i * steps + j, 0)),
        out_shape=jax.ShapeDtypeStruct((b * s, qd), BF16),
        compiler_params=_cparams(("parallel", "arbitrary")),
        name="window_attention",
    )(qvk_r, qvk_r, qvk_r, qvk_m, kt_r, kt_r, kt_r, kt_m, bias, bias, bias, shift)


def _fft_split(s):
    n2 = 128 if s >= 1024 else 32
    return (2 * s) // n2, n2


def _fused_conv_vmem_bytes(s, n1, n2):
    lane_bytes = LANE_TILE * 4
    spectrum = 3 * n1 * n2 * lane_bytes
    tables = 2 * n2 * 2 * n1 * 2 * (n1 // 2 + 8) * 2
    sequence = (2 * s + 2 * 8 * n2 + s + N_META + 16) * lane_bytes
    blocks = 3 * 2 * 2 * (s + N_META) * LANE_TILE * 2
    return spectrum + tables + sequence + blocks


def _hyena_conv(xr, xm, col_off, conv_w, conv_b, kre, kim, order, tab, gr, gm, gate_off, tail, *, dims):
    b, s, d, n1, n2 = dims
    if _fused_conv_vmem_bytes(s, n1, n2) <= (VMEM_LIMIT * 7) // 8:
        outs = _fused_conv(xr, xm, col_off, conv_w, conv_b, kre, kim, order, tab, gr, gm, gate_off, tail,
                           b=b, s=s, d=d, n1=n1, n2=n2)
        return outs[0], outs[1]
    kw = dict(b=b, s=s, d=d, n1=n1, n2=n2, dt=LANE_TILE)
    if col_off is not None:
        are, aim, ut = _fft_in(xr, xm, col_off, conv_w, conv_b, tab, **kw)
    else:
        are, aim, ut = _fft_in(xr, xm, 0, None, None, tab, **kw)
    bre, bim = _fft_mid(are, aim, kre, kim, order, tab, dt=min(d, FFT_MID_LANES), slabs=FUSED_SLABS)
    return _fft_out(bre, bim, tab, gr, gm, gate_off, conv_w, conv_b, ut, tail, **kw)


def _hyena_layer(streams, fp, g_mix, w_in, conv_w, conv_b, skip, tm):
    outs = []
    d = w_in.shape[0]
    for st in streams:
        b, s = st["b"], st["s"]
        n1, n2 = _fft_split(s)
        seq_len = s + N_META
        n = 2 * s
        tab = _fft_tables(n1, n2)
        r = jnp.arange(n, dtype=jnp.int32)
        tr = min(s, 512)
        circ = _filter_rows(jnp.where(r < s, r, n - r), s // tr, tr, seq_len, fp, d)
        circ = circ.at[:, 0, :].add(skip.astype(F32))
        kre, kim = _filter_spectrum(circ, tab, n1=n1, n2=n2, dt=min(d, FFT_MID_LANES), slabs=FUSED_SLABS)
        a = jnp.arange(16, dtype=jnp.int32)
        tail = _filter_rows(jnp.concatenate([s + a, s - a, s - a, s + a]), 1, 32, seq_len, fp, d)
        tail = jnp.concatenate([tail[:, 0:16], tail[:, 32:48], tail[:, 48:64], tail[:, 16:32]], axis=1)
        pr = _norm_matmul(st["hr"], g_mix, w_in, tm)
        pm = _norm_matmul(st["hm"], g_mix, w_in, st["hm"].shape[0])
        dims = (b, s, d, n1, n2)
        z1r, z1m = _hyena_conv(pr, pm, 0, conv_w, conv_b, kre, kim, 0, tab, pr, pm, d, tail[0], dims=dims)
        z2r, z2m = _hyena_conv(z1r, z1m, None, conv_w, conv_b, kre, kim, 1, tab, pr, pm, 2 * d, tail[1],
                               dims=dims)
        outs.append((z2r, z2m))
    return outs


def _attention_weights(w_qkv, n_heads):
    groups = n_heads // GROUP
    qd = n_heads * HEAD_DIM
    kd = groups * HEAD_DIM
    w_q = w_qkv[:, :qd] * (HEAD_DIM ** -0.5)
    w_k = w_qkv[:, qd:qd + kd]
    w_v = w_qkv[:, qd + kd:]
    d = w_qkv.shape[0]
    w_vk = jnp.stack([w_v.reshape(d, groups, HEAD_DIM), w_k.reshape(d, groups, HEAD_DIM)], axis=2)
    w_rows = jnp.concatenate([w_q, w_vk.reshape(d, 2 * kd)], axis=1).astype(BF16)
    return w_rows, w_k.T.astype(BF16)


def _meta_keys_transposed(qvk_m, b, n_heads):
    groups = n_heads // GROUP
    qd = n_heads * HEAD_DIM
    k_m = qvk_m[:, qd:].reshape(b, N_META, groups, 2, HEAD_DIM)[:, :, :, 1, :]
    kt = jnp.transpose(k_m.reshape(b, N_META, groups * HEAD_DIM), (0, 2, 1))
    return jnp.pad(kt, ((0, 0), (0, 0), (0, LANE_TILE - N_META)))


def _encoder_pair(x_prompt, x_sample, meta_tokens, norm_mix, norm_mlp, norm_final,
                  hy_w_in, hy_conv_w, hy_conv_b, fps, hy_skip, hy_w_out, hy_b_out,
                  at_w_qkv, at_sink, at_w_o, mlp_w1, mlp_w2, *, n_heads, tm):
    d = x_prompt.shape[-1]
    streams = []
    for x in (x_prompt, x_sample):
        b, s, _ = x.shape
        streams.append(dict(b=b, s=s, hr=x.reshape(b * s, d),
                            hm=jnp.tile(meta_tokens.astype(F32), (b, 1))))
    zeros_d = jnp.zeros((d,), F32)

    conv_w = jnp.pad(hy_conv_w[0], ((0, 5), (0, 0)))
    conv_b = hy_conv_b[0][None, :]
    zs = _hyena_layer(streams, fps[0], norm_mix[0], hy_w_in[0].astype(BF16), conv_w, conv_b, hy_skip[0], tm)
    w_out = hy_w_out[0].astype(BF16)
    w1 = [w.astype(BF16) for w in mlp_w1]
    w2 = [w.astype(BF16) for w in mlp_w2]
    for st, (zr, zm) in zip(streams, zs):
        st["hr"] = _mixer_out_mlp(st["hr"], zr, w_out, hy_b_out[0], norm_mlp[0], w1[0], w2[0], zeros_d, tm, False)
        st["hm"] = _mixer_out_mlp(st["hm"], zm, w_out, hy_b_out[0], norm_mlp[0], w1[0], w2[0], zeros_d,
                                  st["hm"].shape[0], False)

    w_rows, w_kt = _attention_weights(at_w_qkv[0], n_heads)
    w_o = at_w_o[0].astype(BF16)
    outs = []
    for st in streams:
        qvk_r, kt_r = _norm_matmul(st["hr"], norm_mix[1], w_rows, tm, wt=w_kt)
        qvk_m = _norm_matmul(st["hm"], norm_mix[1], w_rows, st["hm"].shape[0])
        kt_m = _meta_keys_transposed(qvk_m, st["b"], n_heads)
        att = _attention(qvk_r, kt_r, qvk_m, kt_m, at_sink[0], b=st["b"], s=st["s"], n_heads=n_heads)
        y = _mixer_out_mlp(st["hr"], att, w_o, zeros_d, norm_mlp[1], w1[1], w2[1], norm_final, tm, True)
        outs.append(y.reshape(st["b"], st["s"], d))
    return tuple(outs)


def kernel(x_prompt, x_sample, meta_tokens, norm_mix, norm_mlp, norm_final, hy_w_in, hy_conv_w, hy_conv_b,
           hy_f_w1, hy_f_b1, hy_f_w2, hy_f_b2, hy_f_w3, hy_f_b3, hy_f_wout, hy_f_freq, hy_skip, hy_w_out,
           hy_b_out, at_w_qkv, at_sink, at_w_o, mlp_w1, mlp_w2):
    fps = [dict(w1=hy_f_w1[j], b1=hy_f_b1[j], w2=hy_f_w2[j], b2=hy_f_b2[j], w3=hy_f_w3[j], b3=hy_f_b3[j],
                wout=hy_f_wout[j], freq=hy_f_freq[j]) for j in range(hy_f_w1.shape[0])]
    n_heads = at_sink.shape[1]
    return _encoder_pair(x_prompt, x_sample, meta_tokens, norm_mix, norm_mlp, norm_final,
                         hy_w_in, hy_conv_w, hy_conv_b, fps, hy_skip, hy_w_out, hy_b_out,
                         at_w_qkv, at_sink, at_w_o, mlp_w1, mlp_w2, n_heads=n_heads, tm=512)
```

```python
import functools
import math

import jax
import jax.numpy as jnp
from jax import lax
from jax.experimental import pallas as pl
from jax.experimental.pallas import tpu as pltpu

F32 = jnp.float32
BF16 = jnp.bfloat16

N_META = 16
RMS_EPS = 1e-6
HY_BANDS = 16
HY_EMB_PAD = 40
HY_FAST_DECAY = 0.3
HY_SLOW_DECAY = 1.5
HY_DECAY_TARGET = 1e-2
ATT_BLOCK = 128
ATT_QBLOCKS = 8
HEAD_DIM = 64
GROUP = 4
MASK_VALUE = -1e30
FF_CHUNK = 1024
LANE_TILE = 128
N2_CHUNK = 32
FUSED_SLABS = 16
FFT_MID_LANES = 256
VMEM_LIMIT = 56 * 1024 * 1024
HIGHEST = lax.Precision.HIGHEST


def _cparams(sem):
    return pltpu.CompilerParams(dimension_semantics=sem, vmem_limit_bytes=VMEM_LIMIT)


def _rms(x, g):
    return x * lax.rsqrt(jnp.mean(x * x, axis=-1, keepdims=True) + RMS_EPS) * g


def _norm_matmul_kernel(x_ref, g_ref, w_ref, *rest):
    u = _rms(x_ref[...], g_ref[...]).astype(BF16)
    if len(rest) == 1:
        (o_ref,) = rest
    else:
        wt_ref, o_ref, ot_ref = rest
        ot_ref[...] = lax.dot_general(wt_ref[...], u, (((1,), (1,)), ((), ())),
                                      preferred_element_type=F32).astype(ot_ref.dtype)
    o_ref[...] = jnp.dot(u, w_ref[...], preferred_element_type=F32).astype(o_ref.dtype)


def _norm_matmul(x, g, w, tm, wt=None):
    rows, d = x.shape
    n = w.shape[1]
    in_specs = [pl.BlockSpec((tm, d), lambda i: (i, 0)),
                pl.BlockSpec((1, d), lambda i: (0, 0)),
                pl.BlockSpec((d, n), lambda i: (0, 0))]
    out_specs = pl.BlockSpec((tm, n), lambda i: (i, 0))
    out_shape = jax.ShapeDtypeStruct((rows, n), BF16)
    args = [x, g.reshape(1, d), w]
    if wt is not None:
        m = wt.shape[0]
        in_specs.append(pl.BlockSpec((m, d), lambda i: (0, 0)))
        out_specs = [out_specs, pl.BlockSpec((m, tm), lambda i: (0, i))]
        out_shape = [out_shape, jax.ShapeDtypeStruct((m, rows), BF16)]
        args.append(wt)
    return pl.pallas_call(
        _norm_matmul_kernel,
        grid=(rows // tm,),
        in_specs=in_specs, out_specs=out_specs, out_shape=out_shape,
        compiler_params=_cparams(("parallel",)),
        name="norm_matmul",
    )(*args)


def _mixer_out_mlp_kernel(h_ref, z_ref, wp_ref, bp_ref, g_ref, w1_ref, w2_ref, gf_ref, o_ref, *, final_norm):
    h = h_ref[...] + jnp.dot(z_ref[...], wp_ref[...], preferred_element_type=F32) + bp_ref[...]
    u = _rms(h, g_ref[...]).astype(BF16)
    acc = h
    d_ff = w1_ref.shape[1]
    for c in range(d_ff // FF_CHUNK):
        a = jnp.dot(u, w1_ref[:, c * FF_CHUNK:(c + 1) * FF_CHUNK], preferred_element_type=F32)
        a = jnp.square(jnp.maximum(a, 0.0)).astype(BF16)
        acc = acc + jnp.dot(a, w2_ref[c * FF_CHUNK:(c + 1) * FF_CHUNK, :], preferred_element_type=F32)
    if final_norm:
        acc = _rms(acc, gf_ref[...])
    o_ref[...] = acc


def _mixer_out_mlp(h, z, wp, bp, g, w1, w2, gf, tm, final_norm):
    rows, d = h.shape
    dz = z.shape[1]
    d_ff = w1.shape[1]
    const = lambda i: (0, 0)
    return pl.pallas_call(
        functools.partial(_mixer_out_mlp_kernel, final_norm=final_norm),
        grid=(rows // tm,),
        in_specs=[pl.BlockSpec((tm, d), lambda i: (i, 0)),
                  pl.BlockSpec((tm, dz), lambda i: (i, 0)),
                  pl.BlockSpec((dz, d), const),
                  pl.BlockSpec((1, d), const),
                  pl.BlockSpec((1, d), const),
                  pl.BlockSpec((d, d_ff), const),
                  pl.BlockSpec((d_ff, d), const),
                  pl.BlockSpec((1, d), const)],
        out_specs=pl.BlockSpec((tm, d), lambda i: (i, 0)),
        out_shape=jax.ShapeDtypeStruct((rows, d), F32),
        compiler_params=_cparams(("parallel",)),
        name="mixer_out_mlp",
    )(h, z, wp, bp.reshape(1, d), g.reshape(1, d), w1, w2, gf.reshape(1, d))


def _filter_kernel(z_ref, w1_ref, b1_ref, w2_ref, b2_ref, w3_ref, b3_ref, fr_ref, wo_ref, ad_ref, o_ref):
    d = ad_ref.shape[1]
    dot = functools.partial(jnp.dot, precision=HIGHEST, preferred_element_type=F32)
    fr = fr_ref[...]
    h = jnp.sin(fr * (dot(z_ref[...], w1_ref[...]) + b1_ref[...]))
    h = jnp.sin(fr * (dot(h, w2_ref[...]) + b2_ref[...]))
    h = jnp.sin(fr * (dot(h, w3_ref[...]) + b3_ref[...]))
    ho = jnp.dot(h.astype(BF16), wo_ref[...], preferred_element_type=F32)
    decay = jnp.exp(-z_ref[:, 0:1] * ad_ref[...])
    for o in range(2):
        o_ref[o] = ho[:, o * d:(o + 1) * d] * decay


def _filter_rows(lag, first_bwd_tile, tr, seq_len, fp, d):
    rows = lag.shape[0]
    lagf = lag.astype(F32)
    t = lagf / (seq_len - 1)
    w = 2.0 * math.pi * lagf / seq_len
    f = jnp.linspace(1e-4, HY_BANDS - 1, HY_BANDS, dtype=F32)[None, :]
    z = jnp.concatenate([t[:, None], jnp.cos(f * w[:, None]), -jnp.sin(f * w[:, None]),
                         jnp.zeros((rows, HY_EMB_PAD - 2 * HY_BANDS - 1), F32)], axis=-1)
    w1 = jnp.pad(fp["w1"], ((0, HY_EMB_PAD - fp["w1"].shape[0]), (0, 0)))
    hid = w1.shape[1]
    max_decay = math.log(HY_DECAY_TARGET) / HY_FAST_DECAY
    min_decay = math.log(HY_DECAY_TARGET) / HY_SLOW_DECAY
    adel = jnp.abs(jnp.linspace(min_decay, max_decay, d, dtype=F32))[None, :]
    const = lambda i: (0, 0)
    row = lambda i: (i, 0)
    wo = jnp.transpose(fp["wout"].reshape(hid, 2, 2, d), (2, 0, 1, 3)).reshape(2, hid, 2 * d)
    wo = wo.astype(BF16)
    wo_spec = pl.BlockSpec((None, hid, 2 * d), lambda i: ((i >= first_bwd_tile).astype(jnp.int32), 0, 0))
    return pl.pallas_call(
        _filter_kernel,
        grid=(rows // tr,),
        in_specs=[pl.BlockSpec((tr, HY_EMB_PAD), row),
                  pl.BlockSpec((HY_EMB_PAD, hid), const), pl.BlockSpec((1, hid), const),
                  pl.BlockSpec((hid, hid), const), pl.BlockSpec((1, hid), const),
                  pl.BlockSpec((hid, hid), const), pl.BlockSpec((1, hid), const),
                  pl.BlockSpec((1, hid), const), wo_spec,
                  pl.BlockSpec((1, d), const)],
        out_specs=pl.BlockSpec((2, tr, d), lambda i: (0, i, 0)),
        out_shape=jax.ShapeDtypeStruct((2, rows, d), F32),
        compiler_params=_cparams(("parallel",)),
        name="hyena_filter",
    )(z, w1, fp["b1"][None], fp["w2"], fp["b2"][None],
      fp["w3"], fp["b3"][None], fp["freq"][None], wo, adel)


def _cplx_block(re, im):
    return jnp.concatenate([jnp.concatenate([re, -im], axis=-1), jnp.concatenate([im, re], axis=-1)], axis=-2)


def _fft_tables(n1, n2):
    n = n1 * n2
    n1h = n1 // 2
    pad = 8 - 1
    k1 = jnp.arange(n1, dtype=jnp.int32)
    c2 = jnp.arange(n2, dtype=jnp.int32)
    def phase(idx, period):
        ang = (idx % period).astype(F32) * (-2.0 * math.pi / period)
        return jnp.cos(ang), jnp.sin(ang)

    def twiddled_dft(cols):
        ar, ai = phase(k1[:, None] * cols[None, :], n1)
        br, bi = phase(c2[:, None] * k1[None, :], n)
        return (ar[None] * br[:, :, None] - ai[None] * bi[:, :, None],
                ai[None] * br[:, :, None] + ar[None] * bi[:, :, None])

    cols = jnp.concatenate([jnp.arange(n1h, dtype=jnp.int32), jnp.array([n1 - 1], jnp.int32)])
    wr, wi = twiddled_dft(cols)
    wr = jnp.pad(wr, ((0, 0), (0, 0), (0, pad)))
    wi = jnp.pad(wi, ((0, 0), (0, 0), (0, pad)))
    fwd1 = _cplx_block(wr, wi)
    inv1 = _cplx_block(jnp.swapaxes(wr, 1, 2), -jnp.swapaxes(wi, 1, 2)) / n
    fil1 = jnp.concatenate(twiddled_dft(k1), axis=1)
    fr, fi = phase(c2[:, None] * c2[None, :], n2)
    fwd2 = _cplx_block(fr, fi)
    return dict(fwd1=fwd1.astype(BF16), inv1=inv1.astype(BF16), fil1=fil1.astype(BF16),
                fwd2=fwd2.astype(BF16), inv2=_cplx_block(fr, -fi).astype(BF16))


def _fill_sequence(meta_ref, real_ref, seq_ref):
    s, dt = real_ref.shape
    seq_ref[0:8, :] = jnp.zeros((8, dt), F32)
    seq_ref[8:8 + N_META, :] = meta_ref[...].astype(F32)
    seq_ref[8 + N_META:8 + N_META + s, :] = real_ref[...].astype(F32)
    seq_ref[8 + N_META + s:16 + N_META + s, :] = jnp.zeros((8, dt), F32)


def _short_conv_rows(seq_ref, cw, cb, start, rows):
    return (seq_ref[pl.ds(start - 1, rows), :] * cw[0:1] + seq_ref[pl.ds(start, rows), :] * cw[1:2]
            + seq_ref[pl.ds(start + 1, rows), :] * cw[2:3] + cb)


def _row_chunk(s):
    return min(s, 1024)


def _load_sequences(xr_ref, xm_ref, conv_refs, ut_ref, xs_ref, mp_ref, seq_ref, n2):
    n1h, dt = xs_ref.shape[1], xs_ref.shape[3]
    s = n1h * n2
    ch = _row_chunk(s)
    for e in range(2):
        if conv_refs is not None:
            cw, cb = conv_refs[0][...].astype(F32), conv_refs[1][...].astype(F32)
            _fill_sequence(xm_ref.at[e], xr_ref.at[e], seq_ref)
            meta = _short_conv_rows(seq_ref, cw, cb, 8, N_META)
        else:
            meta = xm_ref[e].astype(F32)
        ut_ref[e, 0:N_META, :] = meta
        for c in range(s // ch):
            if conv_refs is not None:
                real = _short_conv_rows(seq_ref, cw, cb, 8 + N_META + c * ch, ch)
            else:
                real = xr_ref[e, c * ch:(c + 1) * ch, :].astype(F32)
            xs_ref[e, c * ch // n2:(c + 1) * ch // n2] = real.reshape(ch // n2, n2, dt)
        ut_ref[e, N_META:2 * N_META, :] = xs_ref[e, n1h - 1, n2 - 16:n2, :]
        mp_ref[e] = jnp.zeros(mp_ref.shape[1:], F32)
        for j in range(N_META):
            mp_ref[e, pl.ds(8 * (n2 - N_META + j), 1), :] = meta[j:j + 1]


def _gate_sequences(xs_ref, mp_ref, g_refs, conv_refs, ut_ref, tail_ref, z_refs, seq_ref, n2):
    n1h, dt = xs_ref.shape[1], xs_ref.shape[3]
    s = n1h * n2
    ch = _row_chunk(s)
    gr_ref, gm_ref = g_refs
    zr_ref, zm_ref = z_refs
    cw, cb = conv_refs[0][...].astype(F32), conv_refs[1][...].astype(F32)
    tail = tail_ref[...]
    for e in range(2):
        y_meta = jnp.concatenate(
            [mp_ref[e, pl.ds(8 * (n2 - N_META + j), 1), :] for j in range(N_META)], axis=0)
        real_fix, meta_fix = _alias_patch(tail, ut_ref[e, 0:N_META, :], ut_ref[e, N_META:2 * N_META, :])
        xs_ref[e, n1h - 1, n2 - 16:n2, :] = xs_ref[e, n1h - 1, n2 - 16:n2, :] + real_fix
        _fill_sequence(gm_ref.at[e], gr_ref.at[e], seq_ref)
        g_meta = _short_conv_rows(seq_ref, cw, cb, 8, N_META)
        zm_ref[e] = (g_meta * (y_meta + meta_fix)).astype(zm_ref.dtype)
        for c in range(s // ch):
            g = _short_conv_rows(seq_ref, cw, cb, 8 + N_META + c * ch, ch)
            y = xs_ref[e, c * ch // n2:(c + 1) * ch // n2].reshape(ch, dt)
            zr_ref[e, c * ch:(c + 1) * ch, :] = (g * y).astype(zr_ref.dtype)


def _grid_cols(s):
    return N2_CHUNK // 2 if s > 4096 else N2_CHUNK


def _fft_in_kernel(*refs, n1, n2, cols, short_conv):
    if short_conv:
        xr_ref, xm_ref, cw_ref, cb_ref, f_ref, are_ref, aim_ref, ut_ref, xs_ref, mp_ref, seq_ref = refs
    else:
        xr_ref, xm_ref, f_ref, are_ref, aim_ref, ut_ref, xs_ref, mp_ref = refs
    n1h = n1 // 2
    chunk = pl.program_id(2)

    @pl.when(chunk == 0)
    def _prepare():
        if short_conv:
            _load_sequences(xr_ref, xm_ref, (cw_ref, cb_ref), ut_ref, xs_ref, mp_ref, seq_ref, n2)
        else:
            _load_sequences(xr_ref, xm_ref, None, ut_ref, xs_ref, mp_ref, None, n2)

    base = pl.multiple_of(chunk * cols, cols)
    xt = [jnp.swapaxes(xs_ref[e, :, pl.ds(base, cols), :], 0, 1) for e in range(2)]
    for i in range(cols):
        parts = []
        for e in range(2):
            parts += [xt[e][i], mp_ref[e, pl.ds(pl.multiple_of((base + i) * 8, 8), 8), :]]
        rhs = jnp.concatenate(parts, axis=0).astype(BF16)
        out = jnp.dot(f_ref[i], rhs, preferred_element_type=F32)
        are_ref[i] = out[:n1]
        aim_ref[i] = out[n1:]


def _fft_in(xr, xm, col_off, conv_w, conv_b, tab, *, b, s, d, n1, n2, dt):
    short_conv = conv_w is not None
    pairs = b // 2
    c = xr.shape[1]
    cb0 = col_off // dt
    cols = _grid_cols(s)
    xr4 = xr.reshape(pairs, 2, s, c)
    xm4 = xm.reshape(pairs, 2, N_META, c)
    kk = tab["fwd1"].shape[2]
    in_specs = [pl.BlockSpec((None, 2, s, dt), lambda p, j, t: (p, 0, 0, cb0 + j)),
                pl.BlockSpec((None, 2, N_META, dt), lambda p, j, t: (p, 0, 0, cb0 + j))]
    args = [xr4, xm4]
    if short_conv:
        in_specs += [pl.BlockSpec((8, dt), lambda p, j, t: (0, cb0 + j)),
                     pl.BlockSpec((1, dt), lambda p, j, t: (0, cb0 + j))]
        args += [conv_w, conv_b]
    in_specs.append(pl.BlockSpec((cols, 2 * n1, kk), lambda p, j, t: (t, 0, 0)))
    args.append(tab["fwd1"])
    a_spec = pl.BlockSpec((None, None, cols, n1, dt), lambda p, j, t: (p, j, t, 0, 0))
    a_shape = jax.ShapeDtypeStruct((pairs, d // dt, n2, n1, dt), F32)
    out_specs = [a_spec, a_spec, pl.BlockSpec((None, 2, 2 * N_META, dt), lambda p, j, t: (p, 0, 0, j))]
    out_shape = [a_shape, a_shape, jax.ShapeDtypeStruct((pairs, 2, 2 * N_META, d), F32)]
    scratch = [pltpu.VMEM((2, n1 // 2, n2, dt), F32), pltpu.VMEM((2, 8 * n2, dt), F32)]
    if short_conv:
        scratch.append(pltpu.VMEM((s + N_META + 16, dt), F32))
    return pl.pallas_call(
        functools.partial(_fft_in_kernel, n1=n1, n2=n2, cols=cols, short_conv=short_conv),
        grid=(pairs, d // dt, n2 // cols),
        in_specs=in_specs, out_specs=out_specs, out_shape=out_shape, scratch_shapes=scratch,
        compiler_params=_cparams(("parallel", "parallel", "arbitrary")),
        name="fft_in",
    )(*args)


def _fft_mid_kernel(are_ref, aim_ref, kre_ref, kim_ref, f_ref, g_ref, bre_ref, bim_ref, *, n2, slabs):
    tiles = are_ref.shape[0]
    lt = are_ref.shape[3]
    xr_blk = [jnp.swapaxes(are_ref[h], 0, 1) for h in range(tiles)]
    xi_blk = [jnp.swapaxes(aim_ref[h], 0, 1) for h in range(tiles)]
    ys = []
    for i in range(slabs):
        x = jnp.concatenate([jnp.concatenate([xr_blk[h][i], xi_blk[h][i]], axis=0) for h in range(tiles)],
                            axis=1).astype(BF16)
        z = jnp.dot(f_ref[...], x, preferred_element_type=F32)
        zr, zi = z[:n2], z[n2:]
        kr = jnp.concatenate([kre_ref[h, i] for h in range(tiles)], axis=1)
        ki = jnp.concatenate([kim_ref[h, i] for h in range(tiles)], axis=1)
        p = jnp.concatenate([zr * kr - zi * ki, zr * ki + zi * kr], axis=0).astype(BF16)
        ys.append(jnp.dot(g_ref[...], p, preferred_element_type=F32))
    yt = jnp.swapaxes(jnp.stack(ys, axis=0), 0, 1)
    for h in range(tiles):
        bre_ref[h] = yt[:n2, :, h * lt:(h + 1) * lt]
        bim_ref[h] = yt[n2:, :, h * lt:(h + 1) * lt]


def _fft_mid(are, aim, kre, kim, order, tab, *, dt, slabs):
    pairs, ntiles, n2, n1, lt = are.shape
    d = ntiles * lt
    blk = (None, dt // lt, n2, slabs, lt)
    amap = lambda j, kb, p: (p, j, 0, kb, 0)
    kmap = lambda j, kb, p: (order, j, kb, 0, 0)
    const = lambda j, kb, p: (0, 0)
    shp = jax.ShapeDtypeStruct(are.shape, F32)
    return pl.pallas_call(
        functools.partial(_fft_mid_kernel, n2=n2, slabs=slabs),
        grid=(d // dt, n1 // slabs, pairs),
        in_specs=[pl.BlockSpec(blk, amap), pl.BlockSpec(blk, amap),
                  pl.BlockSpec((None, dt // lt, slabs, n2, lt), kmap),
                  pl.BlockSpec((None, dt // lt, slabs, n2, lt), kmap),
                  pl.BlockSpec((2 * n2, 2 * n2), const), pl.BlockSpec((2 * n2, 2 * n2), const)],
        out_specs=[pl.BlockSpec(blk, amap), pl.BlockSpec(blk, amap)],
        out_shape=[shp, shp],
        compiler_params=_cparams(("parallel", "parallel", "arbitrary")),
        name="fft_mid",
    )(are, aim, kre, kim, tab["fwd2"], tab["inv2"])


def _filter_fft_in_kernel(c_ref, f_ref, are_ref, aim_ref, *, n1, n2):
    base = pl.multiple_of(pl.program_id(2) * N2_CHUNK, N2_CHUNK)
    ct = jnp.swapaxes(c_ref[:, pl.ds(base, N2_CHUNK), :], 0, 1).astype(BF16)
    for i in range(N2_CHUNK):
        out = jnp.dot(f_ref[i], ct[i], preferred_element_type=F32)
        are_ref[i] = out[:n1]
        aim_ref[i] = out[n1:]


def _filter_fft_mid_kernel(are_ref, aim_ref, f_ref, kre_ref, kim_ref, *, n2, slabs):
    tiles = are_ref.shape[0]
    lt = are_ref.shape[3]
    xr_blk = [jnp.swapaxes(are_ref[h], 0, 1) for h in range(tiles)]
    xi_blk = [jnp.swapaxes(aim_ref[h], 0, 1) for h in range(tiles)]
    for i in range(slabs):
        x = jnp.concatenate([jnp.concatenate([xr_blk[h][i], xi_blk[h][i]], axis=0) for h in range(tiles)], axis=1)
        z = jnp.dot(f_ref[...], x.astype(BF16), preferred_element_type=F32)
        for h in range(tiles):
            kre_ref[h, i] = z[:n2, h * lt:(h + 1) * lt]
            kim_ref[h, i] = z[n2:, h * lt:(h + 1) * lt]


def _filter_spectrum(circ, tab, *, n1, n2, dt, slabs):
    orders, n, d = circ.shape
    lt = LANE_TILE
    shp = [jax.ShapeDtypeStruct((orders, d // lt, n2, n1, lt), F32)] * 2
    tab_spec = pl.BlockSpec((N2_CHUNK, 2 * n1, n1), lambda o, j, t: (t, 0, 0))
    are, aim = pl.pallas_call(
        functools.partial(_filter_fft_in_kernel, n1=n1, n2=n2),
        grid=(orders, d // lt, n2 // N2_CHUNK),
        in_specs=[pl.BlockSpec((None, n1, n2, lt), lambda o, j, t: (o, 0, 0, j)), tab_spec],
        out_specs=[pl.BlockSpec((None, None, N2_CHUNK, n1, lt), lambda o, j, t: (o, j, t, 0, 0))] * 2,
        out_shape=shp,
        compiler_params=_cparams(("parallel", "parallel", "arbitrary")),
        name="filter_fft_in",
    )(circ.reshape(orders, n1, n2, d), tab["fil1"])
    blk = (None, dt // lt, n2, slabs, lt)
    amap = lambda o, kb, j: (o, j, 0, kb, 0)
    f_spec = pl.BlockSpec((2 * n2, 2 * n2), lambda o, kb, j: (0, 0))
    return pl.pallas_call(
        functools.partial(_filter_fft_mid_kernel, n2=n2, slabs=slabs),
        grid=(orders, n1 // slabs, d // dt),
        in_specs=[pl.BlockSpec(blk, amap), pl.BlockSpec(blk, amap), f_spec],
        out_specs=[pl.BlockSpec((None, dt // lt, slabs, n2, lt), lambda o, kb, j: (o, j, kb, 0, 0))] * 2,
        out_shape=[jax.ShapeDtypeStruct((orders, d // lt, n1, n2, lt), F32)] * 2,
        compiler_params=_cparams(("parallel", "parallel", "parallel")),
        name="filter_fft_mid",
    )(are, aim, tab["fwd2"])


def _alias_patch(tail, u_meta, u_last):
    dfw = tail[0:16] - tail[16:32]
    dbw = tail[32:48] - tail[48:64]
    ridx = lax.broadcasted_iota(jnp.int32, dfw.shape, 0)
    real_fix = jnp.zeros_like(dfw)
    meta_fix = jnp.zeros_like(dfw)
    for o in range(16):
        src = dfw if o == 0 else pltpu.roll(dfw, o, axis=0)
        real_fix = real_fix + jnp.where(ridx >= o, src, 0.0) * u_meta[o:o + 1]
    for c in range(1, 16):
        meta_fix = meta_fix + jnp.where(ridx + c <= 15, pltpu.roll(u_last, 16 - c, axis=0), 0.0) * dbw[c:c + 1]
    return real_fix, meta_fix


def _fft_out_kernel(bre_ref, bim_ref, g_ref, gr_ref, gm_ref, cw_ref, cb_ref, ut_ref, tail_ref, zr_ref, zm_ref,
                    ys_ref, yp_ref, seq_ref, *, n1, n2, cols):
    n1h = n1 // 2
    kk = n1h + 8
    s = n1h * n2
    chunk = pl.program_id(2)

    base = pl.multiple_of(chunk * cols, cols)
    ys = []
    for i in range(cols):
        rhs = jnp.concatenate([bre_ref[i], bim_ref[i]], axis=0).astype(BF16)
        y = jnp.dot(g_ref[i], rhs, preferred_element_type=F32)
        ys.append(y)
        for e in range(2):
            yp_ref[e, pl.ds(pl.multiple_of((base + i) * 8, 8), 8), :] = y[e * kk + n1h:(e + 1) * kk]
    yt = jnp.swapaxes(jnp.stack(ys, axis=0), 0, 1)
    for e in range(2):
        ys_ref[e, :, pl.ds(base, cols), :] = yt[e * kk:e * kk + n1h]

    @pl.when(chunk == n2 // cols - 1)
    def _gate():
        _gate_sequences(ys_ref, yp_ref, (gr_ref, gm_ref), (cw_ref, cb_ref), ut_ref, tail_ref, (zr_ref, zm_ref),
                        seq_ref, n2)


def _fft_out(bre, bim, tab, gr, gm, gate_off, conv_w, conv_b, ut, tail, *, b, s, d, n1, n2, dt):
    pairs = b // 2
    c = gr.shape[1]
    gb0 = gate_off // dt
    kk2 = tab["inv1"].shape[1]
    gr4 = gr.reshape(pairs, 2, s, c)
    gm4 = gm.reshape(pairs, 2, N_META, c)
    cols = _grid_cols(s)
    b_spec = pl.BlockSpec((None, None, cols, n1, dt), lambda p, j, t: (p, j, t, 0, 0))
    zr, zm = pl.pallas_call(
        functools.partial(_fft_out_kernel, n1=n1, n2=n2, cols=cols),
        grid=(pairs, d // dt, n2 // cols),
        in_specs=[b_spec, b_spec,
                  pl.BlockSpec((cols, kk2, 2 * n1), lambda p, j, t: (t, 0, 0)),
                  pl.BlockSpec((None, 2, s, dt), lambda p, j, t: (p, 0, 0, gb0 + j)),
                  pl.BlockSpec((None, 2, N_META, dt), lambda p, j, t: (p, 0, 0, gb0 + j)),
                  pl.BlockSpec((8, dt), lambda p, j, t: (0, gb0 + j)),
                  pl.BlockSpec((1, dt), lambda p, j, t: (0, gb0 + j)),
                  pl.BlockSpec((None, 2, 2 * N_META, dt), lambda p, j, t: (p, 0, 0, j)),
                  pl.BlockSpec((64, dt), lambda p, j, t: (0, j))],
        out_specs=[pl.BlockSpec((None, 2, s, dt), lambda p, j, t: (p, 0, 0, j)),
                   pl.BlockSpec((None, 2, N_META, dt), lambda p, j, t: (p, 0, 0, j))],
        out_shape=[jax.ShapeDtypeStruct((pairs, 2, s, d), BF16),
                   jax.ShapeDtypeStruct((pairs, 2, N_META, d), BF16)],
        scratch_shapes=[pltpu.VMEM((2, n1 // 2, n2, dt), F32), pltpu.VMEM((2, 8 * n2, dt), F32),
                        pltpu.VMEM((s + N_META + 16, dt), F32)],
        compiler_params=_cparams(("parallel", "parallel", "arbitrary")),
        name="fft_out",
    )(bre, bim, tab["inv1"], gr4, gm4, conv_w, conv_b, ut, tail)
    return zr.reshape(b * s, d), zm.reshape(b * N_META, d)


def _fused_conv_kernel(*refs, n1, n2, short_conv):
    if short_conv:
        (xr_ref, xm_ref, cwx_ref, cbx_ref, f1_ref, kre_ref, kim_ref, f2_ref, g2_ref, g1_ref, gr_ref, gm_ref,
         cwg_ref, cbg_ref, tail_ref, zr_ref, zm_ref, are_ref, aim_ref, xs_ref, mp_ref, ut_ref, seq_ref) = refs
        conv_x = (cwx_ref, cbx_ref)
    else:
        (xr_ref, xm_ref, f1_ref, kre_ref, kim_ref, f2_ref, g2_ref, g1_ref, gr_ref, gm_ref,
         cwg_ref, cbg_ref, tail_ref, zr_ref, zm_ref, are_ref, aim_ref, xs_ref, mp_ref, ut_ref, seq_ref) = refs
        conv_x = None
    n1h = n1 // 2
    kk = n1h + 8
    s = n1h * n2
    nkb = n1 // FUSED_SLABS
    dt = xs_ref.shape[3]

    _load_sequences(xr_ref, xm_ref, conv_x, ut_ref, xs_ref, mp_ref, seq_ref, n2)

    def stage1(t, carry):
        base = pl.multiple_of(t * N2_CHUNK, N2_CHUNK)
        xt = [jnp.swapaxes(xs_ref[e, :, pl.ds(base, N2_CHUNK), :], 0, 1) for e in range(2)]
        for i in range(N2_CHUNK):
            parts = []
            for e in range(2):
                parts += [xt[e][i], mp_ref[e, pl.ds(pl.multiple_of((base + i) * 8, 8), 8), :]]
            rhs = jnp.concatenate(parts, axis=0).astype(BF16)
            out = jnp.dot(f1_ref[base + i], rhs, preferred_element_type=F32).astype(BF16)
            are_ref[:, base + i] = out[:n1].reshape(nkb, FUSED_SLABS, dt)
            aim_ref[:, base + i] = out[n1:].reshape(nkb, FUSED_SLABS, dt)
        return carry

    def stage2(kb, carry):
        k0 = pl.multiple_of(kb * FUSED_SLABS, FUSED_SLABS)
        xr_blk = jnp.swapaxes(are_ref[kb], 0, 1)
        xi_blk = jnp.swapaxes(aim_ref[kb], 0, 1)
        ys = []
        for i in range(0, FUSED_SLABS, 2):
            x = jnp.concatenate([jnp.concatenate([xr_blk[i + h], xi_blk[i + h]], axis=0) for h in range(2)],
                                axis=1)
            z = jnp.dot(f2_ref[...], x, preferred_element_type=F32)
            zr, zi = z[:n2], z[n2:]
            kr = jnp.concatenate([kre_ref[k0 + i], kre_ref[k0 + i + 1]], axis=1)
            ki = jnp.concatenate([kim_ref[k0 + i], kim_ref[k0 + i + 1]], axis=1)
            p = jnp.concatenate([zr * kr - zi * ki, zr * ki + zi * kr], axis=0).astype(BF16)
            y = jnp.dot(g2_ref[...], p, preferred_element_type=F32).astype(BF16)
            ys += [y[:, :dt], y[:, dt:]]
        yt = jnp.swapaxes(jnp.stack(ys, axis=0), 0, 1)
        are_ref[kb] = yt[:n2]
        aim_ref[kb] = yt[n2:]
        return carry

    def stage3(t, carry):
        base = pl.multiple_of(t * N2_CHUNK, N2_CHUNK)
        ys = []
        for i in range(N2_CHUNK):
            rhs = jnp.concatenate([are_ref[:, base + i].reshape(n1, dt), aim_ref[:, base + i].reshape(n1, dt)],
                                  axis=0)
            y = jnp.dot(g1_ref[base + i], rhs, preferred_element_type=F32)
            ys.append(y)
            for e in range(2):
                mp_ref[e, pl.ds(pl.multiple_of((base + i) * 8, 8), 8), :] = y[e * kk + n1h:(e + 1) * kk]
        yt = jnp.swapaxes(jnp.stack(ys, axis=0), 0, 1)
        for e in range(2):
            xs_ref[e, :, pl.ds(base, N2_CHUNK), :] = yt[e * kk:e * kk + n1h]
        return carry

    lax.fori_loop(0, n2 // N2_CHUNK, stage1, 0)
    lax.fori_loop(0, nkb, stage2, 0)
    lax.fori_loop(0, n2 // N2_CHUNK, stage3, 0)

    _gate_sequences(xs_ref, mp_ref, (gr_ref, gm_ref), (cwg_ref, cbg_ref), ut_ref, tail_ref, (zr_ref, zm_ref),
                    seq_ref, n2)


def _fused_conv(xr, xm, col_off, conv_w, conv_b, kre, kim, order, tab, gr, gm, gate_off, tail, *,
                b, s, d, n1, n2):
    short_conv = col_off is not None
    pairs = b // 2
    dt = LANE_TILE
    cx0 = (col_off or 0) // dt
    cg0 = gate_off // dt
    kk2 = tab["fwd1"].shape[2]
    once = dict(pipeline_mode=pl.Buffered(1))
    seq4 = lambda a, rows: a.reshape(pairs, 2, rows, a.shape[1])
    x_spec = lambda rows, c0: pl.BlockSpec((None, 2, rows, dt), lambda j, p: (p, 0, 0, c0 + j))
    row_spec = lambda rows, c0: pl.BlockSpec((rows, dt), lambda j, p: (0, c0 + j))
    const2 = pl.BlockSpec((2 * n2, 2 * n2), lambda j, p: (0, 0), **once)
    k_spec = pl.BlockSpec((None, None, n1, n2, dt), lambda j, p: (order, j, 0, 0, 0), **once)
    in_specs = [x_spec(s, cx0), x_spec(N_META, cx0)]
    args = [seq4(xr, s), seq4(xm, N_META)]
    if short_conv:
        in_specs += [row_spec(8, cx0), row_spec(1, cx0)]
        args += [conv_w, conv_b]
    in_specs += [pl.BlockSpec((n2, 2 * n1, kk2), lambda j, p: (0, 0, 0), **once),
                 k_spec, k_spec, const2, const2,
                 pl.BlockSpec((n2, kk2, 2 * n1), lambda j, p: (0, 0, 0), **once),
                 x_spec(s, cg0), x_spec(N_META, cg0), row_spec(8, cg0), row_spec(1, cg0), row_spec(64, 0)]
    args += [tab["fwd1"], kre, kim, tab["fwd2"], tab["inv2"], tab["inv1"],
             seq4(gr, s), seq4(gm, N_META), conv_w, conv_b, tail]
    seq_out = lambda rows: pl.BlockSpec((None, 2, rows, dt), lambda j, p: (p, 0, 0, j))
    out_specs = [seq_out(s), seq_out(N_META)]
    out_shape = [jax.ShapeDtypeStruct((pairs, 2, s, d), BF16), jax.ShapeDtypeStruct((pairs, 2, N_META, d), BF16)]
    scratch = ([pltpu.VMEM((n1 // FUSED_SLABS, n2, FUSED_SLABS, dt), BF16)] * 2
               + [pltpu.VMEM((2, n1 // 2, n2, dt), F32), pltpu.VMEM((2, 8 * n2, dt), F32),
                  pltpu.VMEM((2, 2 * N_META, dt), F32), pltpu.VMEM((s + N_META + 16, dt), F32)])
    outs = pl.pallas_call(
        functools.partial(_fused_conv_kernel, n1=n1, n2=n2, short_conv=short_conv),
        grid=(d // dt, pairs),
        in_specs=in_specs, out_specs=out_specs, out_shape=out_shape, scratch_shapes=scratch,
        compiler_params=_cparams(("parallel", "parallel")),
        name="fused_conv",
    )(*args)
    return [o.reshape(-1, d) for o in outs]


def _attn_kernel(own_ref, vprev_ref, vnext_ref, vmeta_ref, ktp_ref, kto_ref, ktn_ref, ktm_ref,
                 bias_a_ref, bias_mid_ref, bias_b_ref, shift_ref, o_ref, *, groups, qblocks):
    gw = GROUP * HEAD_DIM
    qd = groups * gw
    lane = lax.broadcasted_iota(jnp.int32, (1, LANE_TILE), 1)
    low = lane < HEAD_DIM
    pad_rows = jnp.zeros((ATT_BLOCK - N_META, 2 * HEAD_DIM), BF16)
    zero = jnp.zeros((ATT_BLOCK, LANE_TILE), BF16)
    block = lambda i: slice(i * ATT_BLOCK, (i + 1) * ATT_BLOCK)
    for sub in range(qblocks):
        rows = block(sub)
        bias_ref = bias_a_ref if sub == 0 else (bias_b_ref if sub == qblocks - 1 else bias_mid_ref)
        blk = (qblocks * pl.program_id(1) + sub).astype(F32)
        for g in range(groups):
            grp = slice(g * LANE_TILE, (g + 1) * LANE_TILE)
            own_cols = slice(qd + g * LANE_TILE, qd + (g + 1) * LANE_TILE)
            kt_rows = slice(g * HEAD_DIM, (g + 1) * HEAD_DIM)
            vk_band, kt_band = [], []
            for nb in (sub - 1, sub, sub + 1):
                if nb < 0:
                    vk_band.append(vprev_ref[:, grp])
                    kt_band.append(ktp_ref[kt_rows, :])
                elif nb == qblocks:
                    vk_band.append(vnext_ref[:, grp])
                    kt_band.append(ktn_ref[kt_rows, :])
                else:
                    vk_band.append(own_ref[block(nb), own_cols])
                    kt_band.append(kto_ref[kt_rows, block(nb)])
            vk = jnp.concatenate(vk_band + [vmeta_ref[:, grp], pad_rows], axis=0)
            v_ones = jnp.where(low, vk, jnp.ones_like(vk))
            kt = jnp.concatenate(kt_band + [ktm_ref[kt_rows, :]], axis=1)
            kt2 = jnp.concatenate([kt, kt], axis=0)
            parts = []
            for pr in range(GROUP // 2):
                qp = own_ref[rows, g * gw + pr * LANE_TILE:g * gw + (pr + 1) * LANE_TILE]
                parts += [jnp.where(low, qp, zero), jnp.where(low, zero, qp)]
            q = jnp.concatenate(parts, axis=0)
            sc = jnp.dot(q, kt2, preferred_element_type=F32) + bias_ref[g]
            t = [sc[:, i * LANE_TILE:(i + 1) * LANE_TILE] for i in range(3)]
            t.append(sc[:, 3 * LANE_TILE:] - shift_ref[g] * blk)
            m = jnp.max(jnp.maximum(jnp.maximum(t[0], t[1]), jnp.maximum(t[2], t[3])), axis=1, keepdims=True)
            p = jnp.concatenate([jnp.exp(x - m) for x in t], axis=1).astype(BF16)
            oa = jnp.dot(p, v_ones, preferred_element_type=F32)
            ob = pltpu.roll(oa, HEAD_DIM, axis=1)
            outs = []
            for pr in range(GROUP // 2):
                ev = slice((2 * pr) * ATT_BLOCK, (2 * pr + 1) * ATT_BLOCK)
                od = slice((2 * pr + 1) * ATT_BLOCK, (2 * pr + 2) * ATT_BLOCK)
                outs.append(jnp.where(low, oa[ev] / ob[ev], ob[od] / oa[od]))
            o_ref[rows, g * gw:(g + 1) * gw] = jnp.concatenate(outs, axis=1).astype(o_ref.dtype)


def _attention_tables(n_heads, sink):
    groups = n_heads // GROUP
    slopes = jnp.exp2(-8.0 * jnp.arange(1, n_heads + 1, dtype=F32) / n_heads)
    i = jnp.arange(ATT_BLOCK, dtype=jnp.int32)[:, None]
    c = jnp.arange(4 * ATT_BLOCK, dtype=jnp.int32)[None, :]
    dist = jnp.abs(c - ATT_BLOCK - i)
    key_blk = c // ATT_BLOCK
    in_band = jnp.logical_and(c < 3 * ATT_BLOCK, dist <= ATT_BLOCK)
    meta_col = jnp.logical_and(c >= 3 * ATT_BLOCK, c < 3 * ATT_BLOCK + N_META)
    sink_col = c == 3 * ATT_BLOCK + N_META
    meta_dist = N_META + i - (c - 3 * ATT_BLOCK)
    tables = []
    for drop in (None, 0, 2):
        ok = in_band if drop is None else jnp.logical_and(in_band, key_blk != drop)
        d_eff = jnp.where(ok, dist, jnp.where(meta_col, meta_dist, 0)).astype(F32)
        live = jnp.logical_or(ok, meta_col)
        tab = jnp.where(live[None], -slopes[:, None, None] * d_eff[None], MASK_VALUE)
        tables.append(jnp.where(sink_col[None], sink.astype(F32)[:, None, None], tab))
    bias = jnp.stack(tables, axis=0).reshape(3, groups, GROUP * ATT_BLOCK, 4 * ATT_BLOCK)
    lane = jnp.arange(LANE_TILE)[None, :]
    shift = jnp.where(lane < N_META, jnp.repeat(slopes * ATT_BLOCK, ATT_BLOCK)[:, None], 0.0)
    return bias, shift.reshape(groups, GROUP * ATT_BLOCK, LANE_TILE)


def _attention(qvk_r, kt_r, qvk_m, kt_m, sink, *, b, s, n_heads):
    groups = n_heads // GROUP
    nblk = s // ATT_BLOCK
    qd = n_heads * HEAD_DIM
    width = qvk_r.shape[1]
    vkw = groups * 2 * HEAD_DIM
    vkb = qd // vkw
    bias, shift = _attention_tables(n_heads, sink)

    qb = math.gcd(ATT_QBLOCKS, nblk)
    steps = nblk // qb
    prev = lambda i, j: i * nblk + jnp.maximum(qb * j - 1, 0)
    nxt = lambda i, j: i * nblk + jnp.minimum(qb * j + qb, nblk - 1)
    bias_blk = (None, groups, GROUP * ATT_BLOCK, 4 * ATT_BLOCK)
    return pl.pallas_call(
        functools.partial(_attn_kernel, groups=groups, qblocks=qb),
        grid=(b, steps),
        in_specs=[pl.BlockSpec((qb * ATT_BLOCK, width), lambda i, j: (i * steps + j, 0)),
                  pl.BlockSpec((ATT_BLOCK, vkw), lambda i, j: (prev(i, j), vkb)),
                  pl.BlockSpec((ATT_BLOCK, vkw), lambda i, j: (nxt(i, j), vkb)),
                  pl.BlockSpec((N_META, vkw), lambda i, j: (i, vkb)),
                  pl.BlockSpec((groups * HEAD_DIM, ATT_BLOCK), lambda i, j: (0, prev(i, j))),
                  pl.BlockSpec((groups * HEAD_DIM, qb * ATT_BLOCK), lambda i, j: (0, i * steps + j)),
                  pl.BlockSpec((groups * HEAD_DIM, ATT_BLOCK), lambda i, j: (0, nxt(i, j))),
                  pl.BlockSpec((None, groups * HEAD_DIM, LANE_TILE), lambda i, j: (i, 0, 0)),
                  pl.BlockSpec(bias_blk, lambda i, j: (jnp.where(j == 0, 1, 0), 0, 0, 0)),
                  pl.BlockSpec(bias_blk, lambda i, j: (0, 0, 0, 0)),
                  pl.BlockSpec(bias_blk, lambda i, j: (jnp.where(j == steps - 1, 2, 0), 0, 0, 0)),
                  pl.BlockSpec((groups, GROUP * ATT_BLOCK, LANE_TILE), lambda i, j: (0, 0, 0))],
        out_specs=pl.BlockSpec((qb * ATT_BLOCK, qd), lambda i, j: (i * steps + j, 0)),
        out_shape=jax.ShapeDtypeStruct((b * s, qd), BF16),
        compiler_params=_cparams(("parallel", "arbitrary")),
        name="window_attention",
    )(qvk_r, qvk_r, qvk_r, qvk_m, kt_r, kt_r, kt_r, kt_m, bias, bias, bias, shift)


def _fft_split(s):
    n2 = 128 if s >= 1024 else 32
    return (2 * s) // n2, n2


def _fused_conv_vmem_bytes(s, n1, n2):
    lane_bytes = LANE_TILE * 4
    spectrum = 3 * n1 * n2 * lane_bytes
    tables = 2 * n2 * 2 * n1 * 2 * (n1 // 2 + 8) * 2
    sequence = (2 * s + 2 * 8 * n2 + s + N_META + 16) * lane_bytes
    blocks = 3 * 2 * 2 * (s + N_META) * LANE_TILE * 2
    return spectrum + tables + sequence + blocks


def _hyena_conv(xr, xm, col_off, conv_w, conv_b, kre, kim, order, tab, gr, gm, gate_off, tail, *, dims):
    b, s, d, n1, n2 = dims
    if _fused_conv_vmem_bytes(s, n1, n2) <= (VMEM_LIMIT * 7) // 8:
        outs = _fused_conv(xr, xm, col_off, conv_w, conv_b, kre, kim, order, tab, gr, gm, gate_off, tail,
                           b=b, s=s, d=d, n1=n1, n2=n2)
        return outs[0], outs[1]
    kw = dict(b=b, s=s, d=d, n1=n1, n2=n2, dt=LANE_TILE)
    if col_off is not None:
        are, aim, ut = _fft_in(xr, xm, col_off, conv_w, conv_b, tab, **kw)
    else:
        are, aim, ut = _fft_in(xr, xm, 0, None, None, tab, **kw)
    bre, bim = _fft_mid(are, aim, kre, kim, order, tab, dt=min(d, FFT_MID_LANES), slabs=FUSED_SLABS)
    return _fft_out(bre, bim, tab, gr, gm, gate_off, conv_w, conv_b, ut, tail, **kw)


def _hyena_layer(streams, fp, g_mix, w_in, conv_w, conv_b, skip, tm):
    outs = []
    d = w_in.shape[0]
    for st in streams:
        b, s = st["b"], st["s"]
        n1, n2 = _fft_split(s)
        seq_len = s + N_META
        n = 2 * s
        tab = _fft_tables(n1, n2)
        r = jnp.arange(n, dtype=jnp.int32)
        tr = min(s, 512)
        circ = _filter_rows(jnp.where(r < s, r, n - r), s // tr, tr, seq_len, fp, d)
        circ = circ.at[:, 0, :].add(skip.astype(F32))
        kre, kim = _filter_spectrum(circ, tab, n1=n1, n2=n2, dt=min(d, FFT_MID_LANES), slabs=FUSED_SLABS)
        a = jnp.arange(16, dtype=jnp.int32)
        tail = _filter_rows(jnp.concatenate([s + a, s - a, s - a, s + a]), 1, 32, seq_len, fp, d)
        tail = jnp.concatenate([tail[:, 0:16], tail[:, 32:48], tail[:, 48:64], tail[:, 16:32]], axis=1)
        pr = _norm_matmul(st["hr"], g_mix, w_in, 2 * tm)
        pm = _norm_matmul(st["hm"], g_mix, w_in, st["hm"].shape[0])
        dims = (b, s, d, n1, n2)
        z1r, z1m = _hyena_conv(pr, pm, 0, conv_w, conv_b, kre, kim, 0, tab, pr, pm, d, tail[0], dims=dims)
        z2r, z2m = _hyena_conv(z1r, z1m, None, conv_w, conv_b, kre, kim, 1, tab, pr, pm, 2 * d, tail[1],
                               dims=dims)
        outs.append((z2r, z2m))
    return outs


def _attention_weights(w_qkv, n_heads):
    groups = n_heads // GROUP
    qd = n_heads * HEAD_DIM
    kd = groups * HEAD_DIM
    w_q = w_qkv[:, :qd] * (HEAD_DIM ** -0.5)
    w_k = w_qkv[:, qd:qd + kd]
    w_v = w_qkv[:, qd + kd:]
    d = w_qkv.shape[0]
    w_vk = jnp.stack([w_v.reshape(d, groups, HEAD_DIM), w_k.reshape(d, groups, HEAD_DIM)], axis=2)
    w_rows = jnp.concatenate([w_q, w_vk.reshape(d, 2 * kd)], axis=1).astype(BF16)
    return w_rows, w_k.T.astype(BF16)


def _meta_keys_transposed(qvk_m, b, n_heads):
    groups = n_heads // GROUP
    qd = n_heads * HEAD_DIM
    k_m = qvk_m[:, qd:].reshape(b, N_META, groups, 2, HEAD_DIM)[:, :, :, 1, :]
    kt = jnp.transpose(k_m.reshape(b, N_META, groups * HEAD_DIM), (0, 2, 1))
    return jnp.pad(kt, ((0, 0), (0, 0), (0, LANE_TILE - N_META)))


def _encoder_pair(x_prompt, x_sample, meta_tokens, norm_mix, norm_mlp, norm_final,
                  hy_w_in, hy_conv_w, hy_conv_b, fps, hy_skip, hy_w_out, hy_b_out,
                  at_w_qkv, at_sink, at_w_o, mlp_w1, mlp_w2, *, n_heads, tm):
    d = x_prompt.shape[-1]
    streams = []
    for x in (x_prompt, x_sample):
        b, s, _ = x.shape
        streams.append(dict(b=b, s=s, hr=x.reshape(b * s, d),
                            hm=jnp.tile(meta_tokens.astype(F32), (b, 1))))
    zeros_d = jnp.zeros((d,), F32)

    conv_w = jnp.pad(hy_conv_w[0], ((0, 5), (0, 0)))
    conv_b = hy_conv_b[0][None, :]
    zs = _hyena_layer(streams, fps[0], norm_mix[0], hy_w_in[0].astype(BF16), conv_w, conv_b, hy_skip[0], tm)
    w_out = hy_w_out[0].astype(BF16)
    w1 = [w.astype(BF16) for w in mlp_w1]
    w2 = [w.astype(BF16) for w in mlp_w2]
    for st, (zr, zm) in zip(streams, zs):
        st["hr"] = _mixer_out_mlp(st["hr"], zr, w_out, hy_b_out[0], norm_mlp[0], w1[0], w2[0], zeros_d, tm, False)
        st["hm"] = _mixer_out_mlp(st["hm"], zm, w_out, hy_b_out[0], norm_mlp[0], w1[0], w2[0], zeros_d,
                                  st["hm"].shape[0], False)

    w_rows, w_kt = _attention_weights(at_w_qkv[0], n_heads)
    w_o = at_w_o[0].astype(BF16)
    outs = []
    for st in streams:
        qvk_r, kt_r = _norm_matmul(st["hr"], norm_mix[1], w_rows, 2 * tm, wt=w_kt)
        qvk_m = _norm_matmul(st["hm"], norm_mix[1], w_rows, st["hm"].shape[0])
        kt_m = _meta_keys_transposed(qvk_m, st["b"], n_heads)
        att = _attention(qvk_r, kt_r, qvk_m, kt_m, at_sink[0], b=st["b"], s=st["s"], n_heads=n_heads)
        y = _mixer_out_mlp(st["hr"], att, w_o, zeros_d, norm_mlp[1], w1[1], w2[1], norm_final, tm, True)
        outs.append(y.reshape(st["b"], st["s"], d))
    return tuple(outs)


def kernel(x_prompt, x_sample, meta_tokens, norm_mix, norm_mlp, norm_final, hy_w_in, hy_conv_w, hy_conv_b,
           hy_f_w1, hy_f_b1, hy_f_w2, hy_f_b2, hy_f_w3, hy_f_b3, hy_f_wout, hy_f_freq, hy_skip, hy_w_out,
           hy_b_out, at_w_qkv, at_sink, at_w_o, mlp_w1, mlp_w2):
    fps = [dict(w1=hy_f_w1[j], b1=hy_f_b1[j], w2=hy_f_w2[j], b2=hy_f_b2[j], w3=hy_f_w3[j], b3=hy_f_b3[j],
                wout=hy_f_wout[j], freq=hy_f_freq[j]) for j in range(hy_f_w1.shape[0])]
    n_heads = at_sink.shape[1]
    return _encoder_pair(x_prompt, x_sample, meta_tokens, norm_mix, norm_mlp, norm_final,
                         hy_w_in, hy_conv_w, hy_conv_b, fps, hy_skip, hy_w_out, hy_b_out,
                         at_w_qkv, at_sink, at_w_o, mlp_w1, mlp_w2, n_heads=n_heads, tm=512)
```

```python
import functools
import math

import jax
import jax.numpy as jnp
from jax import lax
from jax.experimental import pallas as pl
from jax.experimental.pallas import tpu as pltpu

F32 = jnp.float32
BF16 = jnp.bfloat16

N_META = 16
RMS_EPS = 1e-6
HY_BANDS = 16
HY_EMB_PAD = 40
HY_FAST_DECAY = 0.3
HY_SLOW_DECAY = 1.5
HY_DECAY_TARGET = 1e-2
ATT_BLOCK = 128
ATT_QBLOCKS = 8
HEAD_DIM = 64
GROUP = 4
MASK_VALUE = -1e30
FF_CHUNK = 1024
LANE_TILE = 128
N2_CHUNK = 32
FUSED_SLABS = 16
FFT_MID_LANES = 256
VMEM_LIMIT = 56 * 1024 * 1024
HIGHEST = lax.Precision.HIGHEST


def _cparams(sem):
    return pltpu.CompilerParams(dimension_semantics=sem, vmem_limit_bytes=VMEM_LIMIT)


def _rms(x, g):
    return x * lax.rsqrt(jnp.mean(x * x, axis=-1, keepdims=True) + RMS_EPS) * g


def _norm_matmul_kernel(x_ref, g_ref, w_ref, *rest):
    u = _rms(x_ref[...], g_ref[...]).astype(BF16)
    if len(rest) == 1:
        (o_ref,) = rest
    else:
        wt_ref, o_ref, ot_ref = rest
        ot_ref[...] = lax.dot_general(wt_ref[...], u, (((1,), (1,)), ((), ())),
                                      preferred_element_type=F32).astype(ot_ref.dtype)
    o_ref[...] = jnp.dot(u, w_ref[...], preferred_element_type=F32).astype(o_ref.dtype)


def _norm_matmul(x, g, w, tm, wt=None):
    rows, d = x.shape
    n = w.shape[1]
    in_specs = [pl.BlockSpec((tm, d), lambda i: (i, 0)),
                pl.BlockSpec((1, d), lambda i: (0, 0)),
                pl.BlockSpec((d, n), lambda i: (0, 0))]
    out_specs = pl.BlockSpec((tm, n), lambda i: (i, 0))
    out_shape = jax.ShapeDtypeStruct((rows, n), BF16)
    args = [x, g.reshape(1, d), w]
    if wt is not None:
        m = wt.shape[0]
        in_specs.append(pl.BlockSpec((m, d), lambda i: (0, 0)))
        out_specs = [out_specs, pl.BlockSpec((m, tm), lambda i: (0, i))]
        out_shape = [out_shape, jax.ShapeDtypeStruct((m, rows), BF16)]
        args.append(wt)
    return pl.pallas_call(
        _norm_matmul_kernel,
        grid=(rows // tm,),
        in_specs=in_specs, out_specs=out_specs, out_shape=out_shape,
        compiler_params=_cparams(("parallel",)),
        name="norm_matmul",
    )(*args)


def _mixer_out_mlp_kernel(h_ref, z_ref, wp_ref, bp_ref, g_ref, w1_ref, w2_ref, gf_ref, o_ref, *, final_norm):
    h = h_ref[...] + jnp.dot(z_ref[...], wp_ref[...], preferred_element_type=F32) + bp_ref[...]
    u = _rms(h, g_ref[...]).astype(BF16)
    acc = h
    d_ff = w1_ref.shape[1]
    for c in range(d_ff // FF_CHUNK):
        a = jnp.dot(u, w1_ref[:, c * FF_CHUNK:(c + 1) * FF_CHUNK], preferred_element_type=F32)
        a = jnp.square(jnp.maximum(a, 0.0)).astype(BF16)
        acc = acc + jnp.dot(a, w2_ref[c * FF_CHUNK:(c + 1) * FF_CHUNK, :], preferred_element_type=F32)
    if final_norm:
        acc = _rms(acc, gf_ref[...])
    o_ref[...] = acc


def _mixer_out_mlp(h, z, wp, bp, g, w1, w2, gf, tm, final_norm):
    rows, d = h.shape
    dz = z.shape[1]
    d_ff = w1.shape[1]
    const = lambda i: (0, 0)
    return pl.pallas_call(
        functools.partial(_mixer_out_mlp_kernel, final_norm=final_norm),
        grid=(rows // tm,),
        in_specs=[pl.BlockSpec((tm, d), lambda i: (i, 0)),
                  pl.BlockSpec((tm, dz), lambda i: (i, 0)),
                  pl.BlockSpec((dz, d), const),
                  pl.BlockSpec((1, d), const),
                  pl.BlockSpec((1, d), const),
                  pl.BlockSpec((d, d_ff), const),
                  pl.BlockSpec((d_ff, d), const),
                  pl.BlockSpec((1, d), const)],
        out_specs=pl.BlockSpec((tm, d), lambda i: (i, 0)),
        out_shape=jax.ShapeDtypeStruct((rows, d), F32),
        compiler_params=_cparams(("parallel",)),
        name="mixer_out_mlp",
    )(h, z, wp, bp.reshape(1, d), g.reshape(1, d), w1, w2, gf.reshape(1, d))


def _filter_kernel(z_ref, w1_ref, b1_ref, w2_ref, b2_ref, w3_ref, b3_ref, fr_ref, wo_ref, ad_ref, o_ref):
    d = ad_ref.shape[1]
    dot = functools.partial(jnp.dot, precision=HIGHEST, preferred_element_type=F32)
    fr = fr_ref[...]
    h = jnp.sin(fr * (dot(z_ref[...], w1_ref[...]) + b1_ref[...]))
    h = jnp.sin(fr * (dot(h, w2_ref[...]) + b2_ref[...]))
    h = jnp.sin(fr * (dot(h, w3_ref[...]) + b3_ref[...]))
    ho = jnp.dot(h.astype(BF16), wo_ref[...], preferred_element_type=F32)
    decay = jnp.exp(-z_ref[:, 0:1] * ad_ref[...])
    for o in range(2):
        o_ref[o] = ho[:, o * d:(o + 1) * d] * decay


def _filter_rows(lag, first_bwd_tile, tr, seq_len, fp, d):
    rows = lag.shape[0]
    lagf = lag.astype(F32)
    t = lagf / (seq_len - 1)
    w = 2.0 * math.pi * lagf / seq_len
    f = jnp.linspace(1e-4, HY_BANDS - 1, HY_BANDS, dtype=F32)[None, :]
    z = jnp.concatenate([t[:, None], jnp.cos(f * w[:, None]), -jnp.sin(f * w[:, None]),
                         jnp.zeros((rows, HY_EMB_PAD - 2 * HY_BANDS - 1), F32)], axis=-1)
    w1 = jnp.pad(fp["w1"], ((0, HY_EMB_PAD - fp["w1"].shape[0]), (0, 0)))
    hid = w1.shape[1]
    max_decay = math.log(HY_DECAY_TARGET) / HY_FAST_DECAY
    min_decay = math.log(HY_DECAY_TARGET) / HY_SLOW_DECAY
    adel = jnp.abs(jnp.linspace(min_decay, max_decay, d, dtype=F32))[None, :]
    const = lambda i: (0, 0)
    row = lambda i: (i, 0)
    wo = jnp.transpose(fp["wout"].reshape(hid, 2, 2, d), (2, 0, 1, 3)).reshape(2, hid, 2 * d)
    wo = wo.astype(BF16)
    wo_spec = pl.BlockSpec((None, hid, 2 * d), lambda i: ((i >= first_bwd_tile).astype(jnp.int32), 0, 0))
    return pl.pallas_call(
        _filter_kernel,
        grid=(rows // tr,),
        in_specs=[pl.BlockSpec((tr, HY_EMB_PAD), row),
                  pl.BlockSpec((HY_EMB_PAD, hid), const), pl.BlockSpec((1, hid), const),
                  pl.BlockSpec((hid, hid), const), pl.BlockSpec((1, hid), const),
                  pl.BlockSpec((hid, hid), const), pl.BlockSpec((1, hid), const),
                  pl.BlockSpec((1, hid), const), wo_spec,
                  pl.BlockSpec((1, d), const)],
        out_specs=pl.BlockSpec((2, tr, d), lambda i: (0, i, 0)),
        out_shape=jax.ShapeDtypeStruct((2, rows, d), F32),
        compiler_params=_cparams(("parallel",)),
        name="hyena_filter",
    )(z, w1, fp["b1"][None], fp["w2"], fp["b2"][None],
      fp["w3"], fp["b3"][None], fp["freq"][None], wo, adel)


def _cplx_block(re, im):
    return jnp.concatenate([jnp.concatenate([re, -im], axis=-1), jnp.concatenate([im, re], axis=-1)], axis=-2)


def _fft_tables(n1, n2):
    n = n1 * n2
    n1h = n1 // 2
    pad = 8 - 1
    k1 = jnp.arange(n1, dtype=jnp.int32)
    c2 = jnp.arange(n2, dtype=jnp.int32)
    def phase(idx, period):
        ang = (idx % period).astype(F32) * (-2.0 * math.pi / period)
        return jnp.cos(ang), jnp.sin(ang)

    def twiddled_dft(cols):
        ar, ai = phase(k1[:, None] * cols[None, :], n1)
        br, bi = phase(c2[:, None] * k1[None, :], n)
        return (ar[None] * br[:, :, None] - ai[None] * bi[:, :, None],
                ai[None] * br[:, :, None] + ar[None] * bi[:, :, None])

    cols = jnp.concatenate([jnp.arange(n1h, dtype=jnp.int32), jnp.array([n1 - 1], jnp.int32)])
    wr, wi = twiddled_dft(cols)
    wr = jnp.pad(wr, ((0, 0), (0, 0), (0, pad)))
    wi = jnp.pad(wi, ((0, 0), (0, 0), (0, pad)))
    fwd1 = _cplx_block(wr, wi)
    inv1 = _cplx_block(jnp.swapaxes(wr, 1, 2), -jnp.swapaxes(wi, 1, 2)) / n
    fil1 = jnp.concatenate(twiddled_dft(k1), axis=1)
    fr, fi = phase(c2[:, None] * c2[None, :], n2)
    fwd2 = _cplx_block(fr, fi)
    return dict(fwd1=fwd1.astype(BF16), inv1=inv1.astype(BF16), fil1=fil1.astype(BF16),
                fwd2=fwd2.astype(BF16), inv2=_cplx_block(fr, -fi).astype(BF16))


def _fill_sequence(meta_ref, real_ref, seq_ref):
    s, dt = real_ref.shape
    seq_ref[0:8, :] = jnp.zeros((8, dt), F32)
    seq_ref[8:8 + N_META, :] = meta_ref[...].astype(F32)
    seq_ref[8 + N_META:8 + N_META + s, :] = real_ref[...].astype(F32)
    seq_ref[8 + N_META + s:16 + N_META + s, :] = jnp.zeros((8, dt), F32)


def _short_conv_rows(seq_ref, cw, cb, start, rows):
    return (seq_ref[pl.ds(start - 1, rows), :] * cw[0:1] + seq_ref[pl.ds(start, rows), :] * cw[1:2]
            + seq_ref[pl.ds(start + 1, rows), :] * cw[2:3] + cb)


def _row_chunk(s):
    return min(s, 1024)


def _load_sequences(xr_ref, xm_ref, conv_refs, ut_ref, xs_ref, mp_ref, seq_ref, n2):
    n1h, dt = xs_ref.shape[1], xs_ref.shape[3]
    s = n1h * n2
    ch = _row_chunk(s)
    for e in range(2):
        if conv_refs is not None:
            cw, cb = conv_refs[0][...].astype(F32), conv_refs[1][...].astype(F32)
            _fill_sequence(xm_ref.at[e], xr_ref.at[e], seq_ref)
            meta = _short_conv_rows(seq_ref, cw, cb, 8, N_META)
        else:
            meta = xm_ref[e].astype(F32)
        ut_ref[e, 0:N_META, :] = meta
        for c in range(s // ch):
            if conv_refs is not None:
                real = _short_conv_rows(seq_ref, cw, cb, 8 + N_META + c * ch, ch)
            else:
                real = xr_ref[e, c * ch:(c + 1) * ch, :].astype(F32)
            xs_ref[e, c * ch // n2:(c + 1) * ch // n2] = real.reshape(ch // n2, n2, dt)
        ut_ref[e, N_META:2 * N_META, :] = xs_ref[e, n1h - 1, n2 - 16:n2, :]
        mp_ref[e] = jnp.zeros(mp_ref.shape[1:], F32)
        for j in range(N_META):
            mp_ref[e, pl.ds(8 * (n2 - N_META + j), 1), :] = meta[j:j + 1]


def _gate_sequences(xs_ref, mp_ref, g_refs, conv_refs, ut_ref, tail_ref, z_refs, seq_ref, n2):
    n1h, dt = xs_ref.shape[1], xs_ref.shape[3]
    s = n1h * n2
    ch = _row_chunk(s)
    gr_ref, gm_ref = g_refs
    zr_ref, zm_ref = z_refs
    cw, cb = conv_refs[0][...].astype(F32), conv_refs[1][...].astype(F32)
    tail = tail_ref[...]
    for e in range(2):
        y_meta = jnp.concatenate(
            [mp_ref[e, pl.ds(8 * (n2 - N_META + j), 1), :] for j in range(N_META)], axis=0)
        real_fix, meta_fix = _alias_patch(tail, ut_ref[e, 0:N_META, :], ut_ref[e, N_META:2 * N_META, :])
        xs_ref[e, n1h - 1, n2 - 16:n2, :] = xs_ref[e, n1h - 1, n2 - 16:n2, :] + real_fix
        _fill_sequence(gm_ref.at[e], gr_ref.at[e], seq_ref)
        g_meta = _short_conv_rows(seq_ref, cw, cb, 8, N_META)
        zm_ref[e] = (g_meta * (y_meta + meta_fix)).astype(zm_ref.dtype)
        for c in range(s // ch):
            g = _short_conv_rows(seq_ref, cw, cb, 8 + N_META + c * ch, ch)
            y = xs_ref[e, c * ch // n2:(c + 1) * ch // n2].reshape(ch, dt)
            zr_ref[e, c * ch:(c + 1) * ch, :] = (g * y).astype(zr_ref.dtype)


def _grid_cols(s):
    return N2_CHUNK // 2 if s > 4096 else N2_CHUNK


def _fft_in_kernel(*refs, n1, n2, cols, short_conv):
    if short_conv:
        xr_ref, xm_ref, cw_ref, cb_ref, f_ref, are_ref, aim_ref, ut_ref, xs_ref, mp_ref, seq_ref = refs
    else:
        xr_ref, xm_ref, f_ref, are_ref, aim_ref, ut_ref, xs_ref, mp_ref = refs
    n1h = n1 // 2
    chunk = pl.program_id(2)

    @pl.when(chunk == 0)
    def _prepare():
        if short_conv:
            _load_sequences(xr_ref, xm_ref, (cw_ref, cb_ref), ut_ref, xs_ref, mp_ref, seq_ref, n2)
        else:
            _load_sequences(xr_ref, xm_ref, None, ut_ref, xs_ref, mp_ref, None, n2)

    base = pl.multiple_of(chunk * cols, cols)
    xt = [jnp.swapaxes(xs_ref[e, :, pl.ds(base, cols), :], 0, 1) for e in range(2)]
    for i in range(cols):
        parts = []
        for e in range(2):
            parts += [xt[e][i], mp_ref[e, pl.ds(pl.multiple_of((base + i) * 8, 8), 8), :]]
        rhs = jnp.concatenate(parts, axis=0).astype(BF16)
        out = jnp.dot(f_ref[i], rhs, preferred_element_type=F32)
        are_ref[i] = out[:n1]
        aim_ref[i] = out[n1:]


def _fft_in(xr, xm, col_off, conv_w, conv_b, tab, *, b, s, d, n1, n2, dt):
    short_conv = conv_w is not None
    pairs = b // 2
    c = xr.shape[1]
    cb0 = col_off // dt
    cols = _grid_cols(s)
    xr4 = xr.reshape(pairs, 2, s, c)
    xm4 = xm.reshape(pairs, 2, N_META, c)
    kk = tab["fwd1"].shape[2]
    in_specs = [pl.BlockSpec((None, 2, s, dt), lambda p, j, t: (p, 0, 0, cb0 + j)),
                pl.BlockSpec((None, 2, N_META, dt), lambda p, j, t: (p, 0, 0, cb0 + j))]
    args = [xr4, xm4]
    if short_conv:
        in_specs += [pl.BlockSpec((8, dt), lambda p, j, t: (0, cb0 + j)),
                     pl.BlockSpec((1, dt), lambda p, j, t: (0, cb0 + j))]
        args += [conv_w, conv_b]
    in_specs.append(pl.BlockSpec((cols, 2 * n1, kk), lambda p, j, t: (t, 0, 0)))
    args.append(tab["fwd1"])
    a_spec = pl.BlockSpec((None, None, cols, n1, dt), lambda p, j, t: (p, j, t, 0, 0))
    a_shape = jax.ShapeDtypeStruct((pairs, d // dt, n2, n1, dt), F32)
    out_specs = [a_spec, a_spec, pl.BlockSpec((None, 2, 2 * N_META, dt), lambda p, j, t: (p, 0, 0, j))]
    out_shape = [a_shape, a_shape, jax.ShapeDtypeStruct((pairs, 2, 2 * N_META, d), F32)]
    scratch = [pltpu.VMEM((2, n1 // 2, n2, dt), F32), pltpu.VMEM((2, 8 * n2, dt), F32)]
    if short_conv:
        scratch.append(pltpu.VMEM((s + N_META + 16, dt), F32))
    return pl.pallas_call(
        functools.partial(_fft_in_kernel, n1=n1, n2=n2, cols=cols, short_conv=short_conv),
        grid=(pairs, d // dt, n2 // cols),
        in_specs=in_specs, out_specs=out_specs, out_shape=out_shape, scratch_shapes=scratch,
        compiler_params=_cparams(("parallel", "parallel", "arbitrary")),
        name="fft_in",
    )(*args)


def _fft_mid_kernel(are_ref, aim_ref, kre_ref, kim_ref, f_ref, g_ref, bre_ref, bim_ref, *, n2, slabs):
    tiles = are_ref.shape[0]
    lt = are_ref.shape[3]
    xr_blk = [jnp.swapaxes(are_ref[h], 0, 1) for h in range(tiles)]
    xi_blk = [jnp.swapaxes(aim_ref[h], 0, 1) for h in range(tiles)]
    ys = []
    for i in range(slabs):
        x = jnp.concatenate([jnp.concatenate([xr_blk[h][i], xi_blk[h][i]], axis=0) for h in range(tiles)],
                            axis=1).astype(BF16)
        z = jnp.dot(f_ref[...], x, preferred_element_type=F32)
        zr, zi = z[:n2], z[n2:]
        kr = jnp.concatenate([kre_ref[h, i] for h in range(tiles)], axis=1)
        ki = jnp.concatenate([kim_ref[h, i] for h in range(tiles)], axis=1)
        p = jnp.concatenate([zr * kr - zi * ki, zr * ki + zi * kr], axis=0).astype(BF16)
        ys.append(jnp.dot(g_ref[...], p, preferred_element_type=F32))
    yt = jnp.swapaxes(jnp.stack(ys, axis=0), 0, 1)
    for h in range(tiles):
        bre_ref[h] = yt[:n2, :, h * lt:(h + 1) * lt]
        bim_ref[h] = yt[n2:, :, h * lt:(h + 1) * lt]


def _fft_mid(are, aim, kre, kim, order, tab, *, dt, slabs):
    pairs, ntiles, n2, n1, lt = are.shape
    d = ntiles * lt
    blk = (None, dt // lt, n2, slabs, lt)
    amap = lambda j, kb, p: (p, j, 0, kb, 0)
    kmap = lambda j, kb, p: (order, j, kb, 0, 0)
    const = lambda j, kb, p: (0, 0)
    shp = jax.ShapeDtypeStruct(are.shape, F32)
    return pl.pallas_call(
        functools.partial(_fft_mid_kernel, n2=n2, slabs=slabs),
        grid=(d // dt, n1 // slabs, pairs),
        in_specs=[pl.BlockSpec(blk, amap), pl.BlockSpec(blk, amap),
                  pl.BlockSpec((None, dt // lt, slabs, n2, lt), kmap),
                  pl.BlockSpec((None, dt // lt, slabs, n2, lt), kmap),
                  pl.BlockSpec((2 * n2, 2 * n2), const), pl.BlockSpec((2 * n2, 2 * n2), const)],
        out_specs=[pl.BlockSpec(blk, amap), pl.BlockSpec(blk, amap)],
        out_shape=[shp, shp],
        compiler_params=_cparams(("parallel", "parallel", "arbitrary")),
        name="fft_mid",
    )(are, aim, kre, kim, tab["fwd2"], tab["inv2"])


def _filter_fft_in_kernel(c_ref, f_ref, are_ref, aim_ref, *, n1, n2):
    base = pl.multiple_of(pl.program_id(2) * N2_CHUNK, N2_CHUNK)
    ct = jnp.swapaxes(c_ref[:, pl.ds(base, N2_CHUNK), :], 0, 1).astype(BF16)
    for i in range(N2_CHUNK):
        out = jnp.dot(f_ref[i], ct[i], preferred_element_type=F32)
        are_ref[i] = out[:n1]
        aim_ref[i] = out[n1:]


def _filter_fft_mid_kernel(are_ref, aim_ref, f_ref, kre_ref, kim_ref, *, n2, slabs):
    tiles = are_ref.shape[0]
    lt = are_ref.shape[3]
    xr_blk = [jnp.swapaxes(are_ref[h], 0, 1) for h in range(tiles)]
    xi_blk = [jnp.swapaxes(aim_ref[h], 0, 1) for h in range(tiles)]
    for i in range(slabs):
        x = jnp.concatenate([jnp.concatenate([xr_blk[h][i], xi_blk[h][i]], axis=0) for h in range(tiles)], axis=1)
        z = jnp.dot(f_ref[...], x.astype(BF16), preferred_element_type=F32)
        for h in range(tiles):
            kre_ref[h, i] = z[:n2, h * lt:(h + 1) * lt]
            kim_ref[h, i] = z[n2:, h * lt:(h + 1) * lt]


def _filter_spectrum(circ, tab, *, n1, n2, dt, slabs):
    orders, n, d = circ.shape
    lt = LANE_TILE
    shp = [jax.ShapeDtypeStruct((orders, d // lt, n2, n1, lt), F32)] * 2
    tab_spec = pl.BlockSpec((N2_CHUNK, 2 * n1, n1), lambda o, j, t: (t, 0, 0))
    are, aim = pl.pallas_call(
        functools.partial(_filter_fft_in_kernel, n1=n1, n2=n2),
        grid=(orders, d // lt, n2 // N2_CHUNK),
        in_specs=[pl.BlockSpec((None, n1, n2, lt), lambda o, j, t: (o, 0, 0, j)), tab_spec],
        out_specs=[pl.BlockSpec((None, None, N2_CHUNK, n1, lt), lambda o, j, t: (o, j, t, 0, 0))] * 2,
        out_shape=shp,
        compiler_params=_cparams(("parallel", "parallel", "arbitrary")),
        name="filter_fft_in",
    )(circ.reshape(orders, n1, n2, d), tab["fil1"])
    blk = (None, dt // lt, n2, slabs, lt)
    amap = lambda o, kb, j: (o, j, 0, kb, 0)
    f_spec = pl.BlockSpec((2 * n2, 2 * n2), lambda o, kb, j: (0, 0))
    return pl.pallas_call(
        functools.partial(_filter_fft_mid_kernel, n2=n2, slabs=slabs),
        grid=(orders, n1 // slabs, d // dt),
        in_specs=[pl.BlockSpec(blk, amap), pl.BlockSpec(blk, amap), f_spec],
        out_specs=[pl.BlockSpec((None, dt // lt, slabs, n2, lt), lambda o, kb, j: (o, j, kb, 0, 0))] * 2,
        out_shape=[jax.ShapeDtypeStruct((orders, d // lt, n1, n2, lt), F32)] * 2,
        compiler_params=_cparams(("parallel", "parallel", "parallel")),
        name="filter_fft_mid",
    )(are, aim, tab["fwd2"])


def _alias_patch(tail, u_meta, u_last):
    dfw = tail[0:16] - tail[16:32]
    dbw = tail[32:48] - tail[48:64]
    ridx = lax.broadcasted_iota(jnp.int32, dfw.shape, 0)
    real_fix = jnp.zeros_like(dfw)
    meta_fix = jnp.zeros_like(dfw)
    for o in range(16):
        src = dfw if o == 0 else pltpu.roll(dfw, o, axis=0)
        real_fix = real_fix + jnp.where(ridx >= o, src, 0.0) * u_meta[o:o + 1]
    for c in range(1, 16):
        meta_fix = meta_fix + jnp.where(ridx + c <= 15, pltpu.roll(u_last, 16 - c, axis=0), 0.0) * dbw[c:c + 1]
    return real_fix, meta_fix


def _fft_out_kernel(bre_ref, bim_ref, g_ref, gr_ref, gm_ref, cw_ref, cb_ref, ut_ref, tail_ref, zr_ref, zm_ref,
                    ys_ref, yp_ref, seq_ref, *, n1, n2, cols):
    n1h = n1 // 2
    kk = n1h + 8
    s = n1h * n2
    chunk = pl.program_id(2)

    base = pl.multiple_of(chunk * cols, cols)
    ys = []
    for i in range(cols):
        rhs = jnp.concatenate([bre_ref[i], bim_ref[i]], axis=0).astype(BF16)
        y = jnp.dot(g_ref[i], rhs, preferred_element_type=F32)
        ys.append(y)
        for e in range(2):
            yp_ref[e, pl.ds(pl.multiple_of((base + i) * 8, 8), 8), :] = y[e * kk + n1h:(e + 1) * kk]
    yt = jnp.swapaxes(jnp.stack(ys, axis=0), 0, 1)
    for e in range(2):
        ys_ref[e, :, pl.ds(base, cols), :] = yt[e * kk:e * kk + n1h]

    @pl.when(chunk == n2 // cols - 1)
    def _gate():
        _gate_sequences(ys_ref, yp_ref, (gr_ref, gm_ref), (cw_ref, cb_ref), ut_ref, tail_ref, (zr_ref, zm_ref),
                        seq_ref, n2)


def _fft_out(bre, bim, tab, gr, gm, gate_off, conv_w, conv_b, ut, tail, *, b, s, d, n1, n2, dt):
    pairs = b // 2
    c = gr.shape[1]
    gb0 = gate_off // dt
    kk2 = tab["inv1"].shape[1]
    gr4 = gr.reshape(pairs, 2, s, c)
    gm4 = gm.reshape(pairs, 2, N_META, c)
    cols = _grid_cols(s)
    b_spec = pl.BlockSpec((None, None, cols, n1, dt), lambda p, j, t: (p, j, t, 0, 0))
    zr, zm = pl.pallas_call(
        functools.partial(_fft_out_kernel, n1=n1, n2=n2, cols=cols),
        grid=(pairs, d // dt, n2 // cols),
        in_specs=[b_spec, b_spec,
                  pl.BlockSpec((cols, kk2, 2 * n1), lambda p, j, t: (t, 0, 0)),
                  pl.BlockSpec((None, 2, s, dt), lambda p, j, t: (p, 0, 0, gb0 + j)),
                  pl.BlockSpec((None, 2, N_META, dt), lambda p, j, t: (p, 0, 0, gb0 + j)),
                  pl.BlockSpec((8, dt), lambda p, j, t: (0, gb0 + j)),
                  pl.BlockSpec((1, dt), lambda p, j, t: (0, gb0 + j)),
                  pl.BlockSpec((None, 2, 2 * N_META, dt), lambda p, j, t: (p, 0, 0, j)),
                  pl.BlockSpec((64, dt), lambda p, j, t: (0, j))],
        out_specs=[pl.BlockSpec((None, 2, s, dt), lambda p, j, t: (p, 0, 0, j)),
                   pl.BlockSpec((None, 2, N_META, dt), lambda p, j, t: (p, 0, 0, j))],
        out_shape=[jax.ShapeDtypeStruct((pairs, 2, s, d), BF16),
                   jax.ShapeDtypeStruct((pairs, 2, N_META, d), BF16)],
        scratch_shapes=[pltpu.VMEM((2, n1 // 2, n2, dt), F32), pltpu.VMEM((2, 8 * n2, dt), F32),
                        pltpu.VMEM((s + N_META + 16, dt), F32)],
        compiler_params=_cparams(("parallel", "parallel", "arbitrary")),
        name="fft_out",
    )(bre, bim, tab["inv1"], gr4, gm4, conv_w, conv_b, ut, tail)
    return zr.reshape(b * s, d), zm.reshape(b * N_META, d)


def _fused_conv_kernel(*refs, n1, n2, short_conv):
    if short_conv:
        (xr_ref, xm_ref, cwx_ref, cbx_ref, f1_ref, kre_ref, kim_ref, f2_ref, g2_ref, g1_ref, gr_ref, gm_ref,
         cwg_ref, cbg_ref, tail_ref, zr_ref, zm_ref, are_ref, aim_ref, xs_ref, mp_ref, ut_ref, seq_ref) = refs
        conv_x = (cwx_ref, cbx_ref)
    else:
        (xr_ref, xm_ref, f1_ref, kre_ref, kim_ref, f2_ref, g2_ref, g1_ref, gr_ref, gm_ref,
         cwg_ref, cbg_ref, tail_ref, zr_ref, zm_ref, are_ref, aim_ref, xs_ref, mp_ref, ut_ref, seq_ref) = refs
        conv_x = None
    n1h = n1 // 2
    kk = n1h + 8
    s = n1h * n2
    nkb = n1 // FUSED_SLABS
    dt = xs_ref.shape[3]

    _load_sequences(xr_ref, xm_ref, conv_x, ut_ref, xs_ref, mp_ref, seq_ref, n2)

    def stage1(t, carry):
        base = pl.multiple_of(t * N2_CHUNK, N2_CHUNK)
        xt = [jnp.swapaxes(xs_ref[e, :, pl.ds(base, N2_CHUNK), :], 0, 1) for e in range(2)]
        for i in range(N2_CHUNK):
            parts = []
            for e in range(2):
                parts += [xt[e][i], mp_ref[e, pl.ds(pl.multiple_of((base + i) * 8, 8), 8), :]]
            rhs = jnp.concatenate(parts, axis=0).astype(BF16)
            out = jnp.dot(f1_ref[base + i], rhs, preferred_element_type=F32).astype(BF16)
            are_ref[:, base + i] = out[:n1].reshape(nkb, FUSED_SLABS, dt)
            aim_ref[:, base + i] = out[n1:].reshape(nkb, FUSED_SLABS, dt)
        return carry

    def stage2(kb, carry):
        k0 = pl.multiple_of(kb * FUSED_SLABS, FUSED_SLABS)
        xr_blk = jnp.swapaxes(are_ref[kb], 0, 1)
        xi_blk = jnp.swapaxes(aim_ref[kb], 0, 1)
        ys = []
        for i in range(0, FUSED_SLABS, 2):
            x = jnp.concatenate([jnp.concatenate([xr_blk[i + h], xi_blk[i + h]], axis=0) for h in range(2)],
                                axis=1)
            z = jnp.dot(f2_ref[...], x, preferred_element_type=F32)
            zr, zi = z[:n2], z[n2:]
            kr = jnp.concatenate([kre_ref[k0 + i], kre_ref[k0 + i + 1]], axis=1)
            ki = jnp.concatenate([kim_ref[k0 + i], kim_ref[k0 + i + 1]], axis=1)
            p = jnp.concatenate([zr * kr - zi * ki, zr * ki + zi * kr], axis=0).astype(BF16)
            y = jnp.dot(g2_ref[...], p, preferred_element_type=F32).astype(BF16)
            ys += [y[:, :dt], y[:, dt:]]
        yt = jnp.swapaxes(jnp.stack(ys, axis=0), 0, 1)
        are_ref[kb] = yt[:n2]
        aim_ref[kb] = yt[n2:]
        return carry

    def stage3(t, carry):
        base = pl.multiple_of(t * N2_CHUNK, N2_CHUNK)
        ys = []
        for i in range(N2_CHUNK):
            rhs = jnp.concatenate([are_ref[:, base + i].reshape(n1, dt), aim_ref[:, base + i].reshape(n1, dt)],
                                  axis=0)
            y = jnp.dot(g1_ref[base + i], rhs, preferred_element_type=F32)
            ys.append(y)
            for e in range(2):
                mp_ref[e, pl.ds(pl.multiple_of((base + i) * 8, 8), 8), :] = y[e * kk + n1h:(e + 1) * kk]
        yt = jnp.swapaxes(jnp.stack(ys, axis=0), 0, 1)
        for e in range(2):
            xs_ref[e, :, pl.ds(base, N2_CHUNK), :] = yt[e * kk:e * kk + n1h]
        return carry

    lax.fori_loop(0, n2 // N2_CHUNK, stage1, 0)
    lax.fori_loop(0, nkb, stage2, 0)
    lax.fori_loop(0, n2 // N2_CHUNK, stage3, 0)

    _gate_sequences(xs_ref, mp_ref, (gr_ref, gm_ref), (cwg_ref, cbg_ref), ut_ref, tail_ref, (zr_ref, zm_ref),
                    seq_ref, n2)


def _fused_conv(xr, xm, col_off, conv_w, conv_b, kre, kim, order, tab, gr, gm, gate_off, tail, *,
                b, s, d, n1, n2):
    short_conv = col_off is not None
    pairs = b // 2
    dt = LANE_TILE
    cx0 = (col_off or 0) // dt
    cg0 = gate_off // dt
    kk2 = tab["fwd1"].shape[2]
    once = dict(pipeline_mode=pl.Buffered(1))
    seq4 = lambda a, rows: a.reshape(pairs, 2, rows, a.shape[1])
    x_spec = lambda rows, c0: pl.BlockSpec((None, 2, rows, dt), lambda j, p: (p, 0, 0, c0 + j))
    row_spec = lambda rows, c0: pl.BlockSpec((rows, dt), lambda j, p: (0, c0 + j))
    const2 = pl.BlockSpec((2 * n2, 2 * n2), lambda j, p: (0, 0), **once)
    k_spec = pl.BlockSpec((None, None, n1, n2, dt), lambda j, p: (order, j, 0, 0, 0))
    in_specs = [x_spec(s, cx0), x_spec(N_META, cx0)]
    args = [seq4(xr, s), seq4(xm, N_META)]
    if short_conv:
        in_specs += [row_spec(8, cx0), row_spec(1, cx0)]
        args += [conv_w, conv_b]
    in_specs += [pl.BlockSpec((n2, 2 * n1, kk2), lambda j, p: (0, 0, 0), **once),
                 k_spec, k_spec, const2, const2,
                 pl.BlockSpec((n2, kk2, 2 * n1), lambda j, p: (0, 0, 0), **once),
                 x_spec(s, cg0), x_spec(N_META, cg0), row_spec(8, cg0), row_spec(1, cg0), row_spec(64, 0)]
    args += [tab["fwd1"], kre, kim, tab["fwd2"], tab["inv2"], tab["inv1"],
             seq4(gr, s), seq4(gm, N_META), conv_w, conv_b, tail]
    seq_out = lambda rows: pl.BlockSpec((None, 2, rows, dt), lambda j, p: (p, 0, 0, j))
    out_specs = [seq_out(s), seq_out(N_META)]
    out_shape = [jax.ShapeDtypeStruct((pairs, 2, s, d), BF16), jax.ShapeDtypeStruct((pairs, 2, N_META, d), BF16)]
    scratch = ([pltpu.VMEM((n1 // FUSED_SLABS, n2, FUSED_SLABS, dt), BF16)] * 2
               + [pltpu.VMEM((2, n1 // 2, n2, dt), F32), pltpu.VMEM((2, 8 * n2, dt), F32),
                  pltpu.VMEM((2, 2 * N_META, dt), F32), pltpu.VMEM((s + N_META + 16, dt), F32)])
    outs = pl.pallas_call(
        functools.partial(_fused_conv_kernel, n1=n1, n2=n2, short_conv=short_conv),
        grid=(d // dt, pairs),
        in_specs=in_specs, out_specs=out_specs, out_shape=out_shape, scratch_shapes=scratch,
        compiler_params=_cparams(("parallel", "parallel")),
        name="fused_conv",
    )(*args)
    return [o.reshape(-1, d) for o in outs]


def _attn_kernel(own_ref, vprev_ref, vnext_ref, vmeta_ref, ktp_ref, kto_ref, ktn_ref, ktm_ref,
                 bias_a_ref, bias_mid_ref, bias_b_ref, shift_ref, o_ref, *, groups, qblocks):
    gw = GROUP * HEAD_DIM
    qd = groups * gw
    lane = lax.broadcasted_iota(jnp.int32, (1, LANE_TILE), 1)
    low = lane < HEAD_DIM
    pad_rows = jnp.zeros((ATT_BLOCK - N_META, 2 * HEAD_DIM), BF16)
    zero = jnp.zeros((ATT_BLOCK, LANE_TILE), BF16)
    block = lambda i: slice(i * ATT_BLOCK, (i + 1) * ATT_BLOCK)
    for sub in range(qblocks):
        rows = block(sub)
        bias_ref = bias_a_ref if sub == 0 else (bias_b_ref if sub == qblocks - 1 else bias_mid_ref)
        blk = (qblocks * pl.program_id(1) + sub).astype(F32)
        for g in range(groups):
            grp = slice(g * LANE_TILE, (g + 1) * LANE_TILE)
            own_cols = slice(qd + g * LANE_TILE, qd + (g + 1) * LANE_TILE)
            kt_rows = slice(g * HEAD_DIM, (g + 1) * HEAD_DIM)
            vk_band, kt_band = [], []
            for nb in (sub - 1, sub, sub + 1):
                if nb < 0:
                    vk_band.append(vprev_ref[:, grp])
                    kt_band.append(ktp_ref[kt_rows, :])
                elif nb == qblocks:
                    vk_band.append(vnext_ref[:, grp])
                    kt_band.append(ktn_ref[kt_rows, :])
                else:
                    vk_band.append(own_ref[block(nb), own_cols])
                    kt_band.append(kto_ref[kt_rows, block(nb)])
            vk = jnp.concatenate(vk_band + [vmeta_ref[:, grp], pad_rows], axis=0)
            v_ones = jnp.where(low, vk, jnp.ones_like(vk))
            kt = jnp.concatenate(kt_band + [ktm_ref[kt_rows, :]], axis=1)
            kt2 = jnp.concatenate([kt, kt], axis=0)
            parts = []
            for pr in range(GROUP // 2):
                qp = own_ref[rows, g * gw + pr * LANE_TILE:g * gw + (pr + 1) * LANE_TILE]
                parts += [jnp.where(low, qp, zero), jnp.where(low, zero, qp)]
            q = jnp.concatenate(parts, axis=0)
            sc = jnp.dot(q, kt2, preferred_element_type=F32) + bias_ref[g]
            t = [sc[:, i * LANE_TILE:(i + 1) * LANE_TILE] for i in range(3)]
            t.append(sc[:, 3 * LANE_TILE:] - shift_ref[g] * blk)
            m = jnp.max(jnp.maximum(jnp.maximum(t[0], t[1]), jnp.maximum(t[2], t[3])), axis=1, keepdims=True)
            p = jnp.concatenate([jnp.exp(x - m) for x in t], axis=1).astype(BF16)
            oa = jnp.dot(p, v_ones, preferred_element_type=F32)
            ob = pltpu.roll(oa, HEAD_DIM, axis=1)
            outs = []
            for pr in range(GROUP // 2):
                ev = slice((2 * pr) * ATT_BLOCK, (2 * pr + 1) * ATT_BLOCK)
                od = slice((2 * pr + 1) * ATT_BLOCK, (2 * pr + 2) * ATT_BLOCK)
                outs.append(jnp.where(low, oa[ev] / ob[ev], ob[od] / oa[od]))
            o_ref[rows, g * gw:(g + 1) * gw] = jnp.concatenate(outs, axis=1).astype(o_ref.dtype)


def _attention_tables(n_heads, sink):
    groups = n_heads // GROUP
    slopes = jnp.exp2(-8.0 * jnp.arange(1, n_heads + 1, dtype=F32) / n_heads)
    i = jnp.arange(ATT_BLOCK, dtype=jnp.int32)[:, None]
    c = jnp.arange(4 * ATT_BLOCK, dtype=jnp.int32)[None, :]
    dist = jnp.abs(c - ATT_BLOCK - i)
    key_blk = c // ATT_BLOCK
    in_band = jnp.logical_and(c < 3 * ATT_BLOCK, dist <= ATT_BLOCK)
    meta_col = jnp.logical_and(c >= 3 * ATT_BLOCK, c < 3 * ATT_BLOCK + N_META)
    sink_col = c == 3 * ATT_BLOCK + N_META
    meta_dist = N_META + i - (c - 3 * ATT_BLOCK)
    tables = []
    for drop in (None, 0, 2):
        ok = in_band if drop is None else jnp.logical_and(in_band, key_blk != drop)
        d_eff = jnp.where(ok, dist, jnp.where(meta_col, meta_dist, 0)).astype(F32)
        live = jnp.logical_or(ok, meta_col)
        tab = jnp.where(live[None], -slopes[:, None, None] * d_eff[None], MASK_VALUE)
        tables.append(jnp.where(sink_col[None], sink.astype(F32)[:, None, None], tab))
    bias = jnp.stack(tables, axis=0).reshape(3, groups, GROUP * ATT_BLOCK, 4 * ATT_BLOCK)
    lane = jnp.arange(LANE_TILE)[None, :]
    shift = jnp.where(lane < N_META, jnp.repeat(slopes * ATT_BLOCK, ATT_BLOCK)[:, None], 0.0)
    return bias, shift.reshape(groups, GROUP * ATT_BLOCK, LANE_TILE)


def _attention(qvk_r, kt_r, qvk_m, kt_m, sink, *, b, s, n_heads):
    groups = n_heads // GROUP
    nblk = s // ATT_BLOCK
    qd = n_heads * HEAD_DIM
    width = qvk_r.shape[1]
    vkw = groups * 2 * HEAD_DIM
    vkb = qd // vkw
    bias, shift = _attention_tables(n_heads, sink)

    qb = math.gcd(ATT_QBLOCKS, nblk)
    steps = nblk // qb
    prev = lambda i, j: i * nblk + jnp.maximum(qb * j - 1, 0)
    nxt = lambda i, j: i * nblk + jnp.minimum(qb * j + qb, nblk - 1)
    bias_blk = (None, groups, GROUP * ATT_BLOCK, 4 * ATT_BLOCK)
    return pl.pallas_call(
        functools.partial(_attn_kernel, groups=groups, qblocks=qb),
        grid=(b, steps),
        in_specs=[pl.BlockSpec((qb * ATT_BLOCK, width), lambda i, j: (i * steps + j, 0)),
                  pl.BlockSpec((ATT_BLOCK, vkw), lambda i, j: (prev(i, j), vkb)),
                  pl.BlockSpec((ATT_BLOCK, vkw), lambda i, j: (nxt(i, j), vkb)),
                  pl.BlockSpec((N_META, vkw), lambda i, j: (i, vkb)),
                  pl.BlockSpec((groups * HEAD_DIM, ATT_BLOCK), lambda i, j: (0, prev(i, j))),
                  pl.BlockSpec((groups * HEAD_DIM, qb * ATT_BLOCK), lambda i, j: (0, i * steps + j)),
                  pl.BlockSpec((groups * HEAD_DIM, ATT_BLOCK), lambda i, j: (0, nxt(i, j))),
                  pl.BlockSpec((None, groups * HEAD_DIM, LANE_TILE), lambda i, j: (i, 0, 0)),
                  pl.BlockSpec(bias_blk, lambda i, j: (jnp.where(j == 0, 1, 0), 0, 0, 0)),
                  pl.BlockSpec(bias_blk, lambda i, j: (0, 0, 0, 0)),
                  pl.BlockSpec(bias_blk, lambda i, j: (jnp.where(j == steps - 1, 2, 0), 0, 0, 0)),
                  pl.BlockSpec((groups, GROUP * ATT_BLOCK, LANE_TILE), lambda i, j: (0, 0, 0))],
        out_specs=pl.BlockSpec((qb * ATT_BLOCK, qd), lambda i, j: (i * steps + j, 0)),
        out_shape=jax.ShapeDtypeStruct((b * s, qd), BF16),
        compiler_params=_cparams(("parallel", "arbitrary")),
        name="window_attention",
    )(qvk_r, qvk_r, qvk_r, qvk_m, kt_r, kt_r, kt_r, kt_m, bias, bias, bias, shift)


def _fft_split(s):
    n2 = 128 if s >= 1024 else 32
    return (2 * s) // n2, n2


def _fused_conv_vmem_bytes(s, n1, n2):
    lane_bytes = LANE_TILE * 4
    spectrum = 5 * n1 * n2 * lane_bytes
    tables = 2 * n2 * 2 * n1 * 2 * (n1 // 2 + 8) * 2
    sequence = (2 * s + 2 * 8 * n2 + s + N_META + 16) * lane_bytes
    blocks = 3 * 2 * 2 * (s + N_META) * LANE_TILE * 2
    return spectrum + tables + sequence + blocks


def _hyena_conv(xr, xm, col_off, conv_w, conv_b, kre, kim, order, tab, gr, gm, gate_off, tail, *, dims):
    b, s, d, n1, n2 = dims
    if _fused_conv_vmem_bytes(s, n1, n2) <= (VMEM_LIMIT * 7) // 8:
        outs = _fused_conv(xr, xm, col_off, conv_w, conv_b, kre, kim, order, tab, gr, gm, gate_off, tail,
                           b=b, s=s, d=d, n1=n1, n2=n2)
        return outs[0], outs[1]
    kw = dict(b=b, s=s, d=d, n1=n1, n2=n2, dt=LANE_TILE)
    if col_off is not None:
        are, aim, ut = _fft_in(xr, xm, col_off, conv_w, conv_b, tab, **kw)
    else:
        are, aim, ut = _fft_in(xr, xm, 0, None, None, tab, **kw)
    bre, bim = _fft_mid(are, aim, kre, kim, order, tab, dt=min(d, FFT_MID_LANES), slabs=FUSED_SLABS)
    return _fft_out(bre, bim, tab, gr, gm, gate_off, conv_w, conv_b, ut, tail, **kw)


def _hyena_layer(streams, fp, g_mix, w_in, conv_w, conv_b, skip, tm):
    outs = []
    d = w_in.shape[0]
    for st in streams:
        b, s = st["b"], st["s"]
        n1, n2 = _fft_split(s)
        seq_len = s + N_META
        n = 2 * s
        tab = _fft_tables(n1, n2)
        r = jnp.arange(n, dtype=jnp.int32)
        tr = min(s, 512)
        circ = _filter_rows(jnp.where(r < s, r, n - r), s // tr, tr, seq_len, fp, d)
        circ = circ.at[:, 0, :].add(skip.astype(F32))
        kre, kim = _filter_spectrum(circ, tab, n1=n1, n2=n2, dt=min(d, FFT_MID_LANES), slabs=FUSED_SLABS)
        a = jnp.arange(16, dtype=jnp.int32)
        tail = _filter_rows(jnp.concatenate([s + a, s - a, s - a, s + a]), 1, 32, seq_len, fp, d)
        tail = jnp.concatenate([tail[:, 0:16], tail[:, 32:48], tail[:, 48:64], tail[:, 16:32]], axis=1)
        pr = _norm_matmul(st["hr"], g_mix, w_in, 2 * tm)
        pm = _norm_matmul(st["hm"], g_mix, w_in, st["hm"].shape[0])
        dims = (b, s, d, n1, n2)
        z1r, z1m = _hyena_conv(pr, pm, 0, conv_w, conv_b, kre, kim, 0, tab, pr, pm, d, tail[0], dims=dims)
        z2r, z2m = _hyena_conv(z1r, z1m, None, conv_w, conv_b, kre, kim, 1, tab, pr, pm, 2 * d, tail[1],
                               dims=dims)
        outs.append((z2r, z2m))
    return outs


def _attention_weights(w_qkv, n_heads):
    groups = n_heads // GROUP
    qd = n_heads * HEAD_DIM
    kd = groups * HEAD_DIM
    w_q = w_qkv[:, :qd] * (HEAD_DIM ** -0.5)
    w_k = w_qkv[:, qd:qd + kd]
    w_v = w_qkv[:, qd + kd:]
    d = w_qkv.shape[0]
    w_vk = jnp.stack([w_v.reshape(d, groups, HEAD_DIM), w_k.reshape(d, groups, HEAD_DIM)], axis=2)
    w_rows = jnp.concatenate([w_q, w_vk.reshape(d, 2 * kd)], axis=1).astype(BF16)
    return w_rows, w_k.T.astype(BF16)


def _meta_keys_transposed(qvk_m, b, n_heads):
    groups = n_heads // GROUP
    qd = n_heads * HEAD_DIM
    k_m = qvk_m[:, qd:].reshape(b, N_META, groups, 2, HEAD_DIM)[:, :, :, 1, :]
    kt = jnp.transpose(k_m.reshape(b, N_META, groups * HEAD_DIM), (0, 2, 1))
    return jnp.pad(kt, ((0, 0), (0, 0), (0, LANE_TILE - N_META)))


def _encoder_pair(x_prompt, x_sample, meta_tokens, norm_mix, norm_mlp, norm_final,
                  hy_w_in, hy_conv_w, hy_conv_b, fps, hy_skip, hy_w_out, hy_b_out,
                  at_w_qkv, at_sink, at_w_o, mlp_w1, mlp_w2, *, n_heads, tm):
    d = x_prompt.shape[-1]
    streams = []
    for x in (x_prompt, x_sample):
        b, s, _ = x.shape
        streams.append(dict(b=b, s=s, hr=x.reshape(b * s, d),
                            hm=jnp.tile(meta_tokens.astype(F32), (b, 1))))
    zeros_d = jnp.zeros((d,), F32)

    conv_w = jnp.pad(hy_conv_w[0], ((0, 5), (0, 0)))
    conv_b = hy_conv_b[0][None, :]
    zs = _hyena_layer(streams, fps[0], norm_mix[0], hy_w_in[0].astype(BF16), conv_w, conv_b, hy_skip[0], tm)
    w_out = hy_w_out[0].astype(BF16)
    w1 = [w.astype(BF16) for w in mlp_w1]
    w2 = [w.astype(BF16) for w in mlp_w2]
    for st, (zr, zm) in zip(streams, zs):
        st["hr"] = _mixer_out_mlp(st["hr"], zr, w_out, hy_b_out[0], norm_mlp[0], w1[0], w2[0], zeros_d, tm, False)
        st["hm"] = _mixer_out_mlp(st["hm"], zm, w_out, hy_b_out[0], norm_mlp[0], w1[0], w2[0], zeros_d,
                                  st["hm"].shape[0], False)

    w_rows, w_kt = _attention_weights(at_w_qkv[0], n_heads)
    w_o = at_w_o[0].astype(BF16)
    outs = []
    for st in streams:
        qvk_r, kt_r = _norm_matmul(st["hr"], norm_mix[1], w_rows, 2 * tm, wt=w_kt)
        qvk_m = _norm_matmul(st["hm"], norm_mix[1], w_rows, st["hm"].shape[0])
        kt_m = _meta_keys_transposed(qvk_m, st["b"], n_heads)
        att = _attention(qvk_r, kt_r, qvk_m, kt_m, at_sink[0], b=st["b"], s=st["s"], n_heads=n_heads)
        y = _mixer_out_mlp(st["hr"], att, w_o, zeros_d, norm_mlp[1], w1[1], w2[1], norm_final, tm, True)
        outs.append(y.reshape(st["b"], st["s"], d))
    return tuple(outs)


def kernel(x_prompt, x_sample, meta_tokens, norm_mix, norm_mlp, norm_final, hy_w_in, hy_conv_w, hy_conv_b,
           hy_f_w1, hy_f_b1, hy_f_w2, hy_f_b2, hy_f_w3, hy_f_b3, hy_f_wout, hy_f_freq, hy_skip, hy_w_out,
           hy_b_out, at_w_qkv, at_sink, at_w_o, mlp_w1, mlp_w2):
    fps = [dict(w1=hy_f_w1[j], b1=hy_f_b1[j], w2=hy_f_w2[j], b2=hy_f_b2[j], w3=hy_f_w3[j], b3=hy_f_b3[j],
                wout=hy_f_wout[j], freq=hy_f_freq[j]) for j in range(hy_f_w1.shape[0])]
    n_heads = at_sink.shape[1]
    return _encoder_pair(x_prompt, x_sample, meta_tokens, norm_mix, norm_mlp, norm_final,
                         hy_w_in, hy_conv_w, hy_conv_b, fps, hy_skip, hy_w_out, hy_b_out,
                         at_w_qkv, at_sink, at_w_o, mlp_w1, mlp_w2, n_heads=n_heads, tm=512)
```

```python
import functools
import math

import jax
import jax.numpy as jnp
from jax import lax
from jax.experimental import pallas as pl
from jax.experimental.pallas import tpu as pltpu

F32 = jnp.float32
BF16 = jnp.bfloat16

N_META = 16
RMS_EPS = 1e-6
HY_BANDS = 16
HY_EMB_PAD = 40
HY_FAST_DECAY = 0.3
HY_SLOW_DECAY = 1.5
HY_DECAY_TARGET = 1e-2
ATT_BLOCK = 128
ATT_QBLOCKS = 8
HEAD_DIM = 64
GROUP = 4
MASK_VALUE = -1e30
FF_CHUNK = 1024
LANE_TILE = 128
N2_CHUNK = 32
FUSED_SLABS = 16
FFT_MID_LANES = 256
VMEM_LIMIT = 56 * 1024 * 1024
HIGHEST = lax.Precision.HIGHEST


def _cparams(sem):
    return pltpu.CompilerParams(dimension_semantics=sem, vmem_limit_bytes=VMEM_LIMIT)


def _rms(x, g):
    return x * lax.rsqrt(jnp.mean(x * x, axis=-1, keepdims=True) + RMS_EPS) * g


def _norm_matmul_kernel(x_ref, g_ref, w_ref, *rest):
    u = _rms(x_ref[...], g_ref[...]).astype(BF16)
    if len(rest) == 1:
        (o_ref,) = rest
    else:
        wt_ref, o_ref, ot_ref = rest
        ot_ref[...] = lax.dot_general(wt_ref[...], u, (((1,), (1,)), ((), ())),
                                      preferred_element_type=F32).astype(ot_ref.dtype)
    o_ref[...] = jnp.dot(u, w_ref[...], preferred_element_type=F32).astype(o_ref.dtype)


def _norm_matmul(x, g, w, tm, wt=None):
    rows, d = x.shape
    n = w.shape[1]
    in_specs = [pl.BlockSpec((tm, d), lambda i: (i, 0)),
                pl.BlockSpec((1, d), lambda i: (0, 0)),
                pl.BlockSpec((d, n), lambda i: (0, 0))]
    out_specs = pl.BlockSpec((tm, n), lambda i: (i, 0))
    out_shape = jax.ShapeDtypeStruct((rows, n), BF16)
    args = [x, g.reshape(1, d), w]
    if wt is not None:
        m = wt.shape[0]
        in_specs.append(pl.BlockSpec((m, d), lambda i: (0, 0)))
        out_specs = [out_specs, pl.BlockSpec((m, tm), lambda i: (0, i))]
        out_shape = [out_shape, jax.ShapeDtypeStruct((m, rows), BF16)]
        args.append(wt)
    return pl.pallas_call(
        _norm_matmul_kernel,
        grid=(rows // tm,),
        in_specs=in_specs, out_specs=out_specs, out_shape=out_shape,
        compiler_params=_cparams(("parallel",)),
        name="norm_matmul",
    )(*args)


def _mixer_out_mlp_kernel(h_ref, z_ref, wp_ref, bp_ref, g_ref, w1_ref, w2_ref, gf_ref, o_ref, *, final_norm):
    h = h_ref[...] + jnp.dot(z_ref[...], wp_ref[...], preferred_element_type=F32) + bp_ref[...]
    u = _rms(h, g_ref[...]).astype(BF16)
    acc = h
    d_ff = w1_ref.shape[1]
    for c in range(d_ff // FF_CHUNK):
        a = jnp.dot(u, w1_ref[:, c * FF_CHUNK:(c + 1) * FF_CHUNK], preferred_element_type=F32)
        a = jnp.square(jnp.maximum(a, 0.0)).astype(BF16)
        acc = acc + jnp.dot(a, w2_ref[c * FF_CHUNK:(c + 1) * FF_CHUNK, :], preferred_element_type=F32)
    if final_norm:
        acc = _rms(acc, gf_ref[...])
    o_ref[...] = acc


def _mixer_out_mlp(h, z, wp, bp, g, w1, w2, gf, tm, final_norm):
    rows, d = h.shape
    dz = z.shape[1]
    d_ff = w1.shape[1]
    const = lambda i: (0, 0)
    once = dict(pipeline_mode=pl.Buffered(1))
    return pl.pallas_call(
        functools.partial(_mixer_out_mlp_kernel, final_norm=final_norm),
        grid=(rows // tm,),
        in_specs=[pl.BlockSpec((tm, d), lambda i: (i, 0)),
                  pl.BlockSpec((tm, dz), lambda i: (i, 0)),
                  pl.BlockSpec((dz, d), const, **once),
                  pl.BlockSpec((1, d), const),
                  pl.BlockSpec((1, d), const),
                  pl.BlockSpec((d, d_ff), const, **once),
                  pl.BlockSpec((d_ff, d), const, **once),
                  pl.BlockSpec((1, d), const)],
        out_specs=pl.BlockSpec((tm, d), lambda i: (i, 0)),
        out_shape=jax.ShapeDtypeStruct((rows, d), F32),
        compiler_params=_cparams(("parallel",)),
        name="mixer_out_mlp",
    )(h, z, wp, bp.reshape(1, d), g.reshape(1, d), w1, w2, gf.reshape(1, d))


def _filter_kernel(z_ref, w1_ref, b1_ref, w2_ref, b2_ref, w3_ref, b3_ref, fr_ref, wo_ref, ad_ref, o_ref):
    d = ad_ref.shape[1]
    dot = functools.partial(jnp.dot, precision=HIGHEST, preferred_element_type=F32)
    fr = fr_ref[...]
    h = jnp.sin(fr * (dot(z_ref[...], w1_ref[...]) + b1_ref[...]))
    h = jnp.sin(fr * (dot(h, w2_ref[...]) + b2_ref[...]))
    h = jnp.sin(fr * (dot(h, w3_ref[...]) + b3_ref[...]))
    ho = jnp.dot(h.astype(BF16), wo_ref[...], preferred_element_type=F32)
    decay = jnp.exp(-z_ref[:, 0:1] * ad_ref[...])
    for o in range(2):
        o_ref[o] = ho[:, o * d:(o + 1) * d] * decay


def _filter_rows(lag, first_bwd_tile, tr, seq_len, fp, d):
    rows = lag.shape[0]
    lagf = lag.astype(F32)
    t = lagf / (seq_len - 1)
    w = 2.0 * math.pi * lagf / seq_len
    f = jnp.linspace(1e-4, HY_BANDS - 1, HY_BANDS, dtype=F32)[None, :]
    z = jnp.concatenate([t[:, None], jnp.cos(f * w[:, None]), -jnp.sin(f * w[:, None]),
                         jnp.zeros((rows, HY_EMB_PAD - 2 * HY_BANDS - 1), F32)], axis=-1)
    w1 = jnp.pad(fp["w1"], ((0, HY_EMB_PAD - fp["w1"].shape[0]), (0, 0)))
    hid = w1.shape[1]
    max_decay = math.log(HY_DECAY_TARGET) / HY_FAST_DECAY
    min_decay = math.log(HY_DECAY_TARGET) / HY_SLOW_DECAY
    adel = jnp.abs(jnp.linspace(min_decay, max_decay, d, dtype=F32))[None, :]
    const = lambda i: (0, 0)
    row = lambda i: (i, 0)
    wo = jnp.transpose(fp["wout"].reshape(hid, 2, 2, d), (2, 0, 1, 3)).reshape(2, hid, 2 * d)
    wo = wo.astype(BF16)
    wo_spec = pl.BlockSpec((None, hid, 2 * d), lambda i: ((i >= first_bwd_tile).astype(jnp.int32), 0, 0))
    return pl.pallas_call(
        _filter_kernel,
        grid=(rows // tr,),
        in_specs=[pl.BlockSpec((tr, HY_EMB_PAD), row),
                  pl.BlockSpec((HY_EMB_PAD, hid), const), pl.BlockSpec((1, hid), const),
                  pl.BlockSpec((hid, hid), const), pl.BlockSpec((1, hid), const),
                  pl.BlockSpec((hid, hid), const), pl.BlockSpec((1, hid), const),
                  pl.BlockSpec((1, hid), const), wo_spec,
                  pl.BlockSpec((1, d), const)],
        out_specs=pl.BlockSpec((2, tr, d), lambda i: (0, i, 0)),
        out_shape=jax.ShapeDtypeStruct((2, rows, d), F32),
        compiler_params=_cparams(("parallel",)),
        name="hyena_filter",
    )(z, w1, fp["b1"][None], fp["w2"], fp["b2"][None],
      fp["w3"], fp["b3"][None], fp["freq"][None], wo, adel)


def _cplx_block(re, im):
    return jnp.concatenate([jnp.concatenate([re, -im], axis=-1), jnp.concatenate([im, re], axis=-1)], axis=-2)


def _fft_tables(n1, n2):
    n = n1 * n2
    n1h = n1 // 2
    pad = 8 - 1
    k1 = jnp.arange(n1, dtype=jnp.int32)
    c2 = jnp.arange(n2, dtype=jnp.int32)
    def phase(idx, period):
        ang = (idx % period).astype(F32) * (-2.0 * math.pi / period)
        return jnp.cos(ang), jnp.sin(ang)

    def twiddled_dft(cols):
        ar, ai = phase(k1[:, None] * cols[None, :], n1)
        br, bi = phase(c2[:, None] * k1[None, :], n)
        return (ar[None] * br[:, :, None] - ai[None] * bi[:, :, None],
                ai[None] * br[:, :, None] + ar[None] * bi[:, :, None])

    cols = jnp.concatenate([jnp.arange(n1h, dtype=jnp.int32), jnp.array([n1 - 1], jnp.int32)])
    wr, wi = twiddled_dft(cols)
    wr = jnp.pad(wr, ((0, 0), (0, 0), (0, pad)))
    wi = jnp.pad(wi, ((0, 0), (0, 0), (0, pad)))
    fwd1 = _cplx_block(wr, wi)
    inv1 = _cplx_block(jnp.swapaxes(wr, 1, 2), -jnp.swapaxes(wi, 1, 2)) / n
    fil1 = jnp.concatenate(twiddled_dft(k1), axis=1)
    fr, fi = phase(c2[:, None] * c2[None, :], n2)
    fwd2 = _cplx_block(fr, fi)
    return dict(fwd1=fwd1.astype(BF16), inv1=inv1.astype(BF16), fil1=fil1.astype(BF16),
                fwd2=fwd2.astype(BF16), inv2=_cplx_block(fr, -fi).astype(BF16))


def _fill_sequence(meta_ref, real_ref, seq_ref):
    s, dt = real_ref.shape
    seq_ref[0:8, :] = jnp.zeros((8, dt), F32)
    seq_ref[8:8 + N_META, :] = meta_ref[...].astype(F32)
    seq_ref[8 + N_META:8 + N_META + s, :] = real_ref[...].astype(F32)
    seq_ref[8 + N_META + s:16 + N_META + s, :] = jnp.zeros((8, dt), F32)


def _short_conv_rows(seq_ref, cw, cb, start, rows):
    return (seq_ref[pl.ds(start - 1, rows), :] * cw[0:1] + seq_ref[pl.ds(start, rows), :] * cw[1:2]
            + seq_ref[pl.ds(start + 1, rows), :] * cw[2:3] + cb)


def _row_chunk(s):
    return min(s, 1024)


def _load_sequences(xr_ref, xm_ref, conv_refs, ut_ref, xs_ref, mp_ref, seq_ref, n2):
    n1h, dt = xs_ref.shape[1], xs_ref.shape[3]
    s = n1h * n2
    ch = _row_chunk(s)
    for e in range(2):
        if conv_refs is not None:
            cw, cb = conv_refs[0][...].astype(F32), conv_refs[1][...].astype(F32)
            _fill_sequence(xm_ref.at[e], xr_ref.at[e], seq_ref)
            meta = _short_conv_rows(seq_ref, cw, cb, 8, N_META)
        else:
            meta = xm_ref[e].astype(F32)
        ut_ref[e, 0:N_META, :] = meta
        for c in range(s // ch):
            if conv_refs is not None:
                real = _short_conv_rows(seq_ref, cw, cb, 8 + N_META + c * ch, ch)
            else:
                real = xr_ref[e, c * ch:(c + 1) * ch, :].astype(F32)
            xs_ref[e, c * ch // n2:(c + 1) * ch // n2] = real.reshape(ch // n2, n2, dt)
        ut_ref[e, N_META:2 * N_META, :] = xs_ref[e, n1h - 1, n2 - 16:n2, :]
        mp_ref[e] = jnp.zeros(mp_ref.shape[1:], F32)
        for j in range(N_META):
            mp_ref[e, pl.ds(8 * (n2 - N_META + j), 1), :] = meta[j:j + 1]


def _gate_sequences(xs_ref, mp_ref, g_refs, conv_refs, ut_ref, tail_ref, z_refs, seq_ref, n2):
    n1h, dt = xs_ref.shape[1], xs_ref.shape[3]
    s = n1h * n2
    ch = _row_chunk(s)
    gr_ref, gm_ref = g_refs
    zr_ref, zm_ref = z_refs
    cw, cb = conv_refs[0][...].astype(F32), conv_refs[1][...].astype(F32)
    tail = tail_ref[...]
    for e in range(2):
        y_meta = jnp.concatenate(
            [mp_ref[e, pl.ds(8 * (n2 - N_META + j), 1), :] for j in range(N_META)], axis=0)
        real_fix, meta_fix = _alias_patch(tail, ut_ref[e, 0:N_META, :], ut_ref[e, N_META:2 * N_META, :])
        xs_ref[e, n1h - 1, n2 - 16:n2, :] = xs_ref[e, n1h - 1, n2 - 16:n2, :] + real_fix
        _fill_sequence(gm_ref.at[e], gr_ref.at[e], seq_ref)
        g_meta = _short_conv_rows(seq_ref, cw, cb, 8, N_META)
        zm_ref[e] = (g_meta * (y_meta + meta_fix)).astype(zm_ref.dtype)
        for c in range(s // ch):
            g = _short_conv_rows(seq_ref, cw, cb, 8 + N_META + c * ch, ch)
            y = xs_ref[e, c * ch // n2:(c + 1) * ch // n2].reshape(ch, dt)
            zr_ref[e, c * ch:(c + 1) * ch, :] = (g * y).astype(zr_ref.dtype)


def _grid_cols(s):
    return N2_CHUNK // 2 if s > 4096 else N2_CHUNK


def _fft_in_kernel(*refs, n1, n2, cols, short_conv):
    if short_conv:
        xr_ref, xm_ref, cw_ref, cb_ref, f_ref, are_ref, aim_ref, ut_ref, xs_ref, mp_ref, seq_ref = refs
    else:
        xr_ref, xm_ref, f_ref, are_ref, aim_ref, ut_ref, xs_ref, mp_ref = refs
    n1h = n1 // 2
    chunk = pl.program_id(2)

    @pl.when(chunk == 0)
    def _prepare():
        if short_conv:
            _load_sequences(xr_ref, xm_ref, (cw_ref, cb_ref), ut_ref, xs_ref, mp_ref, seq_ref, n2)
        else:
            _load_sequences(xr_ref, xm_ref, None, ut_ref, xs_ref, mp_ref, None, n2)

    base = pl.multiple_of(chunk * cols, cols)
    xt = [jnp.swapaxes(xs_ref[e, :, pl.ds(base, cols), :], 0, 1) for e in range(2)]
    for i in range(cols):
        parts = []
        for e in range(2):
            parts += [xt[e][i], mp_ref[e, pl.ds(pl.multiple_of((base + i) * 8, 8), 8), :]]
        rhs = jnp.concatenate(parts, axis=0).astype(BF16)
        out = jnp.dot(f_ref[i], rhs, preferred_element_type=F32)
        are_ref[i] = out[:n1]
        aim_ref[i] = out[n1:]


def _fft_in(xr, xm, col_off, conv_w, conv_b, tab, *, b, s, d, n1, n2, dt):
    short_conv = conv_w is not None
    pairs = b // 2
    c = xr.shape[1]
    cb0 = col_off // dt
    cols = _grid_cols(s)
    xr4 = xr.reshape(pairs, 2, s, c)
    xm4 = xm.reshape(pairs, 2, N_META, c)
    kk = tab["fwd1"].shape[2]
    in_specs = [pl.BlockSpec((None, 2, s, dt), lambda p, j, t: (p, 0, 0, cb0 + j)),
                pl.BlockSpec((None, 2, N_META, dt), lambda p, j, t: (p, 0, 0, cb0 + j))]
    args = [xr4, xm4]
    if short_conv:
        in_specs += [pl.BlockSpec((8, dt), lambda p, j, t: (0, cb0 + j)),
                     pl.BlockSpec((1, dt), lambda p, j, t: (0, cb0 + j))]
        args += [conv_w, conv_b]
    in_specs.append(pl.BlockSpec((cols, 2 * n1, kk), lambda p, j, t: (t, 0, 0)))
    args.append(tab["fwd1"])
    a_spec = pl.BlockSpec((None, None, cols, n1, dt), lambda p, j, t: (p, j, t, 0, 0))
    a_shape = jax.ShapeDtypeStruct((pairs, d // dt, n2, n1, dt), F32)
    out_specs = [a_spec, a_spec, pl.BlockSpec((None, 2, 2 * N_META, dt), lambda p, j, t: (p, 0, 0, j))]
    out_shape = [a_shape, a_shape, jax.ShapeDtypeStruct((pairs, 2, 2 * N_META, d), F32)]
    scratch = [pltpu.VMEM((2, n1 // 2, n2, dt), F32), pltpu.VMEM((2, 8 * n2, dt), F32)]
    if short_conv:
        scratch.append(pltpu.VMEM((s + N_META + 16, dt), F32))
    return pl.pallas_call(
        functools.partial(_fft_in_kernel, n1=n1, n2=n2, cols=cols, short_conv=short_conv),
        grid=(pairs, d // dt, n2 // cols),
        in_specs=in_specs, out_specs=out_specs, out_shape=out_shape, scratch_shapes=scratch,
        compiler_params=_cparams(("parallel", "parallel", "arbitrary")),
        name="fft_in",
    )(*args)


def _fft_mid_kernel(are_ref, aim_ref, kre_ref, kim_ref, f_ref, g_ref, bre_ref, bim_ref, *, n2, slabs):
    tiles = are_ref.shape[0]
    lt = are_ref.shape[3]
    xr_blk = [jnp.swapaxes(are_ref[h], 0, 1) for h in range(tiles)]
    xi_blk = [jnp.swapaxes(aim_ref[h], 0, 1) for h in range(tiles)]
    ys = []
    for i in range(slabs):
        x = jnp.concatenate([jnp.concatenate([xr_blk[h][i], xi_blk[h][i]], axis=0) for h in range(tiles)],
                            axis=1).astype(BF16)
        z = jnp.dot(f_ref[...], x, preferred_element_type=F32)
        zr, zi = z[:n2], z[n2:]
        kr = jnp.concatenate([kre_ref[h, i] for h in range(tiles)], axis=1)
        ki = jnp.concatenate([kim_ref[h, i] for h in range(tiles)], axis=1)
        p = jnp.concatenate([zr * kr - zi * ki, zr * ki + zi * kr], axis=0).astype(BF16)
        ys.append(jnp.dot(g_ref[...], p, preferred_element_type=F32))
    yt = jnp.swapaxes(jnp.stack(ys, axis=0), 0, 1)
    for h in range(tiles):
        bre_ref[h] = yt[:n2, :, h * lt:(h + 1) * lt]
        bim_ref[h] = yt[n2:, :, h * lt:(h + 1) * lt]


def _fft_mid(are, aim, kre, kim, order, tab, *, dt, slabs):
    pairs, ntiles, n2, n1, lt = are.shape
    d = ntiles * lt
    blk = (None, dt // lt, n2, slabs, lt)
    amap = lambda j, kb, p: (p, j, 0, kb, 0)
    kmap = lambda j, kb, p: (order, j, kb, 0, 0)
    const = lambda j, kb, p: (0, 0)
    shp = jax.ShapeDtypeStruct(are.shape, F32)
    return pl.pallas_call(
        functools.partial(_fft_mid_kernel, n2=n2, slabs=slabs),
        grid=(d // dt, n1 // slabs, pairs),
        in_specs=[pl.BlockSpec(blk, amap), pl.BlockSpec(blk, amap),
                  pl.BlockSpec((None, dt // lt, slabs, n2, lt), kmap),
                  pl.BlockSpec((None, dt // lt, slabs, n2, lt), kmap),
                  pl.BlockSpec((2 * n2, 2 * n2), const), pl.BlockSpec((2 * n2, 2 * n2), const)],
        out_specs=[pl.BlockSpec(blk, amap), pl.BlockSpec(blk, amap)],
        out_shape=[shp, shp],
        compiler_params=_cparams(("parallel", "parallel", "arbitrary")),
        name="fft_mid",
    )(are, aim, kre, kim, tab["fwd2"], tab["inv2"])


def _filter_fft_in_kernel(c_ref, f_ref, are_ref, aim_ref, *, n1, n2):
    base = pl.multiple_of(pl.program_id(2) * N2_CHUNK, N2_CHUNK)
    ct = jnp.swapaxes(c_ref[:, pl.ds(base, N2_CHUNK), :], 0, 1).astype(BF16)
    for i in range(N2_CHUNK):
        out = jnp.dot(f_ref[i], ct[i], preferred_element_type=F32)
        are_ref[i] = out[:n1]
        aim_ref[i] = out[n1:]


def _filter_fft_mid_kernel(are_ref, aim_ref, f_ref, kre_ref, kim_ref, *, n2, slabs):
    tiles = are_ref.shape[0]
    lt = are_ref.shape[3]
    xr_blk = [jnp.swapaxes(are_ref[h], 0, 1) for h in range(tiles)]
    xi_blk = [jnp.swapaxes(aim_ref[h], 0, 1) for h in range(tiles)]
    for i in range(slabs):
        x = jnp.concatenate([jnp.concatenate([xr_blk[h][i], xi_blk[h][i]], axis=0) for h in range(tiles)], axis=1)
        z = jnp.dot(f_ref[...], x.astype(BF16), preferred_element_type=F32)
        for h in range(tiles):
            kre_ref[h, i] = z[:n2, h * lt:(h + 1) * lt]
            kim_ref[h, i] = z[n2:, h * lt:(h + 1) * lt]


def _filter_spectrum(circ, tab, *, n1, n2, dt, slabs):
    orders, n, d = circ.shape
    lt = LANE_TILE
    shp = [jax.ShapeDtypeStruct((orders, d // lt, n2, n1, lt), F32)] * 2
    tab_spec = pl.BlockSpec((N2_CHUNK, 2 * n1, n1), lambda o, j, t: (t, 0, 0))
    are, aim = pl.pallas_call(
        functools.partial(_filter_fft_in_kernel, n1=n1, n2=n2),
        grid=(orders, d // lt, n2 // N2_CHUNK),
        in_specs=[pl.BlockSpec((None, n1, n2, lt), lambda o, j, t: (o, 0, 0, j)), tab_spec],
        out_specs=[pl.BlockSpec((None, None, N2_CHUNK, n1, lt), lambda o, j, t: (o, j, t, 0, 0))] * 2,
        out_shape=shp,
        compiler_params=_cparams(("parallel", "parallel", "arbitrary")),
        name="filter_fft_in",
    )(circ.reshape(orders, n1, n2, d), tab["fil1"])
    blk = (None, dt // lt, n2, slabs, lt)
    amap = lambda o, kb, j: (o, j, 0, kb, 0)
    f_spec = pl.BlockSpec((2 * n2, 2 * n2), lambda o, kb, j: (0, 0))
    return pl.pallas_call(
        functools.partial(_filter_fft_mid_kernel, n2=n2, slabs=slabs),
        grid=(orders, n1 // slabs, d // dt),
        in_specs=[pl.BlockSpec(blk, amap), pl.BlockSpec(blk, amap), f_spec],
        out_specs=[pl.BlockSpec((None, dt // lt, slabs, n2, lt), lambda o, kb, j: (o, j, kb, 0, 0))] * 2,
        out_shape=[jax.ShapeDtypeStruct((orders, d // lt, n1, n2, lt), F32)] * 2,
        compiler_params=_cparams(("parallel", "parallel", "parallel")),
        name="filter_fft_mid",
    )(are, aim, tab["fwd2"])


def _alias_patch(tail, u_meta, u_last):
    dfw = tail[0:16] - tail[16:32]
    dbw = tail[32:48] - tail[48:64]
    ridx = lax.broadcasted_iota(jnp.int32, dfw.shape, 0)
    real_fix = jnp.zeros_like(dfw)
    meta_fix = jnp.zeros_like(dfw)
    for o in range(16):
        src = dfw if o == 0 else pltpu.roll(dfw, o, axis=0)
        real_fix = real_fix + jnp.where(ridx >= o, src, 0.0) * u_meta[o:o + 1]
    for c in range(1, 16):
        meta_fix = meta_fix + jnp.where(ridx + c <= 15, pltpu.roll(u_last, 16 - c, axis=0), 0.0) * dbw[c:c + 1]
    return real_fix, meta_fix


def _fft_out_kernel(bre_ref, bim_ref, g_ref, gr_ref, gm_ref, cw_ref, cb_ref, ut_ref, tail_ref, zr_ref, zm_ref,
                    ys_ref, yp_ref, seq_ref, *, n1, n2, cols):
    n1h = n1 // 2
    kk = n1h + 8
    s = n1h * n2
    chunk = pl.program_id(2)

    base = pl.multiple_of(chunk * cols, cols)
    ys = []
    for i in range(cols):
        rhs = jnp.concatenate([bre_ref[i], bim_ref[i]], axis=0).astype(BF16)
        y = jnp.dot(g_ref[i], rhs, preferred_element_type=F32)
        ys.append(y)
        for e in range(2):
            yp_ref[e, pl.ds(pl.multiple_of((base + i) * 8, 8), 8), :] = y[e * kk + n1h:(e + 1) * kk]
    yt = jnp.swapaxes(jnp.stack(ys, axis=0), 0, 1)
    for e in range(2):
        ys_ref[e, :, pl.ds(base, cols), :] = yt[e * kk:e * kk + n1h]

    @pl.when(chunk == n2 // cols - 1)
    def _gate():
        _gate_sequences(ys_ref, yp_ref, (gr_ref, gm_ref), (cw_ref, cb_ref), ut_ref, tail_ref, (zr_ref, zm_ref),
                        seq_ref, n2)


def _fft_out(bre, bim, tab, gr, gm, gate_off, conv_w, conv_b, ut, tail, *, b, s, d, n1, n2, dt):
    pairs = b // 2
    c = gr.shape[1]
    gb0 = gate_off // dt
    kk2 = tab["inv1"].shape[1]
    gr4 = gr.reshape(pairs, 2, s, c)
    gm4 = gm.reshape(pairs, 2, N_META, c)
    cols = _grid_cols(s)
    b_spec = pl.BlockSpec((None, None, cols, n1, dt), lambda p, j, t: (p, j, t, 0, 0))
    zr, zm = pl.pallas_call(
        functools.partial(_fft_out_kernel, n1=n1, n2=n2, cols=cols),
        grid=(pairs, d // dt, n2 // cols),
        in_specs=[b_spec, b_spec,
                  pl.BlockSpec((cols, kk2, 2 * n1), lambda p, j, t: (t, 0, 0)),
                  pl.BlockSpec((None, 2, s, dt), lambda p, j, t: (p, 0, 0, gb0 + j)),
                  pl.BlockSpec((None, 2, N_META, dt), lambda p, j, t: (p, 0, 0, gb0 + j)),
                  pl.BlockSpec((8, dt), lambda p, j, t: (0, gb0 + j)),
                  pl.BlockSpec((1, dt), lambda p, j, t: (0, gb0 + j)),
                  pl.BlockSpec((None, 2, 2 * N_META, dt), lambda p, j, t: (p, 0, 0, j)),
                  pl.BlockSpec((64, dt), lambda p, j, t: (0, j))],
        out_specs=[pl.BlockSpec((None, 2, s, dt), lambda p, j, t: (p, 0, 0, j)),
                   pl.BlockSpec((None, 2, N_META, dt), lambda p, j, t: (p, 0, 0, j))],
        out_shape=[jax.ShapeDtypeStruct((pairs, 2, s, d), BF16),
                   jax.ShapeDtypeStruct((pairs, 2, N_META, d), BF16)],
        scratch_shapes=[pltpu.VMEM((2, n1 // 2, n2, dt), F32), pltpu.VMEM((2, 8 * n2, dt), F32),
                        pltpu.VMEM((s + N_META + 16, dt), F32)],
        compiler_params=_cparams(("parallel", "parallel", "arbitrary")),
        name="fft_out",
    )(bre, bim, tab["inv1"], gr4, gm4, conv_w, conv_b, ut, tail)
    return zr.reshape(b * s, d), zm.reshape(b * N_META, d)


def _fused_conv_kernel(*refs, n1, n2, short_conv):
    if short_conv:
        (xr_ref, xm_ref, cwx_ref, cbx_ref, f1_ref, kre_ref, kim_ref, f2_ref, g2_ref, g1_ref, gr_ref, gm_ref,
         cwg_ref, cbg_ref, tail_ref, zr_ref, zm_ref, are_ref, aim_ref, xs_ref, mp_ref, ut_ref, seq_ref) = refs
        conv_x = (cwx_ref, cbx_ref)
    else:
        (xr_ref, xm_ref, f1_ref, kre_ref, kim_ref, f2_ref, g2_ref, g1_ref, gr_ref, gm_ref,
         cwg_ref, cbg_ref, tail_ref, zr_ref, zm_ref, are_ref, aim_ref, xs_ref, mp_ref, ut_ref, seq_ref) = refs
        conv_x = None
    n1h = n1 // 2
    kk = n1h + 8
    s = n1h * n2
    nkb = n1 // FUSED_SLABS
    dt = xs_ref.shape[3]

    _load_sequences(xr_ref, xm_ref, conv_x, ut_ref, xs_ref, mp_ref, seq_ref, n2)

    def stage1(t, carry):
        base = pl.multiple_of(t * N2_CHUNK, N2_CHUNK)
        xt = [jnp.swapaxes(xs_ref[e, :, pl.ds(base, N2_CHUNK), :], 0, 1) for e in range(2)]
        for i in range(N2_CHUNK):
            parts = []
            for e in range(2):
                parts += [xt[e][i], mp_ref[e, pl.ds(pl.multiple_of((base + i) * 8, 8), 8), :]]
            rhs = jnp.concatenate(parts, axis=0).astype(BF16)
            out = jnp.dot(f1_ref[base + i], rhs, preferred_element_type=F32).astype(BF16)
            are_ref[:, base + i] = out[:n1].reshape(nkb, FUSED_SLABS, dt)
            aim_ref[:, base + i] = out[n1:].reshape(nkb, FUSED_SLABS, dt)
        return carry

    def stage2(kb, carry):
        k0 = pl.multiple_of(kb * FUSED_SLABS, FUSED_SLABS)
        xr_blk = jnp.swapaxes(are_ref[kb], 0, 1)
        xi_blk = jnp.swapaxes(aim_ref[kb], 0, 1)
        ys = []
        for i in range(0, FUSED_SLABS, 2):
            x = jnp.concatenate([jnp.concatenate([xr_blk[i + h], xi_blk[i + h]], axis=0) for h in range(2)],
                                axis=1)
            z = jnp.dot(f2_ref[...], x, preferred_element_type=F32)
            zr, zi = z[:n2], z[n2:]
            kr = jnp.concatenate([kre_ref[k0 + i], kre_ref[k0 + i + 1]], axis=1)
            ki = jnp.concatenate([kim_ref[k0 + i], kim_ref[k0 + i + 1]], axis=1)
            p = jnp.concatenate([zr * kr - zi * ki, zr * ki + zi * kr], axis=0).astype(BF16)
            y = jnp.dot(g2_ref[...], p, preferred_element_type=F32).astype(BF16)
            ys += [y[:, :dt], y[:, dt:]]
        yt = jnp.swapaxes(jnp.stack(ys, axis=0), 0, 1)
        are_ref[kb] = yt[:n2]
        aim_ref[kb] = yt[n2:]
        return carry

    def stage3(t, carry):
        base = pl.multiple_of(t * N2_CHUNK, N2_CHUNK)
        ys = []
        for i in range(N2_CHUNK):
            rhs = jnp.concatenate([are_ref[:, base + i].reshape(n1, dt), aim_ref[:, base + i].reshape(n1, dt)],
                                  axis=0)
            y = jnp.dot(g1_ref[base + i], rhs, preferred_element_type=F32)
            ys.append(y)
            for e in range(2):
                mp_ref[e, pl.ds(pl.multiple_of((base + i) * 8, 8), 8), :] = y[e * kk + n1h:(e + 1) * kk]
        yt = jnp.swapaxes(jnp.stack(ys, axis=0), 0, 1)
        for e in range(2):
            xs_ref[e, :, pl.ds(base, N2_CHUNK), :] = yt[e * kk:e * kk + n1h]
        return carry

    lax.fori_loop(0, n2 // N2_CHUNK, stage1, 0)
    lax.fori_loop(0, nkb, stage2, 0)
    lax.fori_loop(0, n2 // N2_CHUNK, stage3, 0)

    _gate_sequences(xs_ref, mp_ref, (gr_ref, gm_ref), (cwg_ref, cbg_ref), ut_ref, tail_ref, (zr_ref, zm_ref),
                    seq_ref, n2)


def _fused_conv(xr, xm, col_off, conv_w, conv_b, kre, kim, order, tab, gr, gm, gate_off, tail, *,
                b, s, d, n1, n2):
    short_conv = col_off is not None
    pairs = b // 2
    dt = LANE_TILE
    cx0 = (col_off or 0) // dt
    cg0 = gate_off // dt
    kk2 = tab["fwd1"].shape[2]
    once = dict(pipeline_mode=pl.Buffered(1))
    seq4 = lambda a, rows: a.reshape(pairs, 2, rows, a.shape[1])
    x_spec = lambda rows, c0: pl.BlockSpec((None, 2, rows, dt), lambda j, p: (p, 0, 0, c0 + j))
    row_spec = lambda rows, c0: pl.BlockSpec((rows, dt), lambda j, p: (0, c0 + j))
    const2 = pl.BlockSpec((2 * n2, 2 * n2), lambda j, p: (0, 0), **once)
    k_spec = pl.BlockSpec((None, None, n1, n2, dt), lambda j, p: (order, j, 0, 0, 0))
    in_specs = [x_spec(s, cx0), x_spec(N_META, cx0)]
    args = [seq4(xr, s), seq4(xm, N_META)]
    if short_conv:
        in_specs += [row_spec(8, cx0), row_spec(1, cx0)]
        args += [conv_w, conv_b]
    in_specs += [pl.BlockSpec((n2, 2 * n1, kk2), lambda j, p: (0, 0, 0), **once),
                 k_spec, k_spec, const2, const2,
                 pl.BlockSpec((n2, kk2, 2 * n1), lambda j, p: (0, 0, 0), **once),
                 x_spec(s, cg0), x_spec(N_META, cg0), row_spec(8, cg0), row_spec(1, cg0), row_spec(64, 0)]
    args += [tab["fwd1"], kre, kim, tab["fwd2"], tab["inv2"], tab["inv1"],
             seq4(gr, s), seq4(gm, N_META), conv_w, conv_b, tail]
    seq_out = lambda rows: pl.BlockSpec((None, 2, rows, dt), lambda j, p: (p, 0, 0, j))
    out_specs = [seq_out(s), seq_out(N_META)]
    out_shape = [jax.ShapeDtypeStruct((pairs, 2, s, d), BF16), jax.ShapeDtypeStruct((pairs, 2, N_META, d), BF16)]
    scratch = ([pltpu.VMEM((n1 // FUSED_SLABS, n2, FUSED_SLABS, dt), BF16)] * 2
               + [pltpu.VMEM((2, n1 // 2, n2, dt), F32), pltpu.VMEM((2, 8 * n2, dt), F32),
                  pltpu.VMEM((2, 2 * N_META, dt), F32), pltpu.VMEM((s + N_META + 16, dt), F32)])
    outs = pl.pallas_call(
        functools.partial(_fused_conv_kernel, n1=n1, n2=n2, short_conv=short_conv),
        grid=(d // dt, pairs),
        in_specs=in_specs, out_specs=out_specs, out_shape=out_shape, scratch_shapes=scratch,
        compiler_params=_cparams(("parallel", "parallel")),
        name="fused_conv",
    )(*args)
    return [o.reshape(-1, d) for o in outs]


def _attn_kernel(own_ref, vprev_ref, vnext_ref, vmeta_ref, ktp_ref, kto_ref, ktn_ref, ktm_ref,
                 bias_a_ref, bias_mid_ref, bias_b_ref, shift_ref, o_ref, *, groups, qblocks):
    gw = GROUP * HEAD_DIM
    qd = groups * gw
    lane = lax.broadcasted_iota(jnp.int32, (1, LANE_TILE), 1)
    low = lane < HEAD_DIM
    pad_rows = jnp.zeros((ATT_BLOCK - N_META, 2 * HEAD_DIM), BF16)
    zero = jnp.zeros((ATT_BLOCK, LANE_TILE), BF16)
    block = lambda i: slice(i * ATT_BLOCK, (i + 1) * ATT_BLOCK)
    for sub in range(qblocks):
        rows = block(sub)
        bias_ref = bias_a_ref if sub == 0 else (bias_b_ref if sub == qblocks - 1 else bias_mid_ref)
        blk = (qblocks * pl.program_id(1) + sub).astype(F32)
        for g in range(groups):
            grp = slice(g * LANE_TILE, (g + 1) * LANE_TILE)
            own_cols = slice(qd + g * LANE_TILE, qd + (g + 1) * LANE_TILE)
            kt_rows = slice(g * HEAD_DIM, (g + 1) * HEAD_DIM)
            vk_band, kt_band = [], []
            for nb in (sub - 1, sub, sub + 1):
                if nb < 0:
                    vk_band.append(vprev_ref[:, grp])
                    kt_band.append(ktp_ref[kt_rows, :])
                elif nb == qblocks:
                    vk_band.append(vnext_ref[:, grp])
                    kt_band.append(ktn_ref[kt_rows, :])
                else:
                    vk_band.append(own_ref[block(nb), own_cols])
                    kt_band.append(kto_ref[kt_rows, block(nb)])
            vk = jnp.concatenate(vk_band + [vmeta_ref[:, grp], pad_rows], axis=0)
            v_ones = jnp.where(low, vk, jnp.ones_like(vk))
            kt = jnp.concatenate(kt_band + [ktm_ref[kt_rows, :]], axis=1)
            kt2 = jnp.concatenate([kt, kt], axis=0)
            parts = []
            for pr in range(GROUP // 2):
                qp = own_ref[rows, g * gw + pr * LANE_TILE:g * gw + (pr + 1) * LANE_TILE]
                parts += [jnp.where(low, qp, zero), jnp.where(low, zero, qp)]
            q = jnp.concatenate(parts, axis=0)
            sc = jnp.dot(q, kt2, preferred_element_type=F32) + bias_ref[g]
            t = [sc[:, i * LANE_TILE:(i + 1) * LANE_TILE] for i in range(3)]
            t.append(sc[:, 3 * LANE_TILE:] - shift_ref[g] * blk)
            m = jnp.max(jnp.maximum(jnp.maximum(t[0], t[1]), jnp.maximum(t[2], t[3])), axis=1, keepdims=True)
            p = jnp.concatenate([jnp.exp(x - m) for x in t], axis=1).astype(BF16)
            oa = jnp.dot(p, v_ones, preferred_element_type=F32)
            ob = pltpu.roll(oa, HEAD_DIM, axis=1)
            outs = []
            for pr in range(GROUP // 2):
                ev = slice((2 * pr) * ATT_BLOCK, (2 * pr + 1) * ATT_BLOCK)
                od = slice((2 * pr + 1) * ATT_BLOCK, (2 * pr + 2) * ATT_BLOCK)
                outs.append(jnp.where(low, oa[ev] / ob[ev], ob[od] / oa[od]))
            o_ref[rows, g * gw:(g + 1) * gw] = jnp.concatenate(outs, axis=1).astype(o_ref.dtype)


def _attention_tables(n_heads, sink):
    groups = n_heads // GROUP
    slopes = jnp.exp2(-8.0 * jnp.arange(1, n_heads + 1, dtype=F32) / n_heads)
    i = jnp.arange(ATT_BLOCK, dtype=jnp.int32)[:, None]
    c = jnp.arange(4 * ATT_BLOCK, dtype=jnp.int32)[None, :]
    dist = jnp.abs(c - ATT_BLOCK - i)
    key_blk = c // ATT_BLOCK
    in_band = jnp.logical_and(c < 3 * ATT_BLOCK, dist <= ATT_BLOCK)
    meta_col = jnp.logical_and(c >= 3 * ATT_BLOCK, c < 3 * ATT_BLOCK + N_META)
    sink_col = c == 3 * ATT_BLOCK + N_META
    meta_dist = N_META + i - (c - 3 * ATT_BLOCK)
    tables = []
    for drop in (None, 0, 2):
        ok = in_band if drop is None else jnp.logical_and(in_band, key_blk != drop)
        d_eff = jnp.where(ok, dist, jnp.where(meta_col, meta_dist, 0)).astype(F32)
        live = jnp.logical_or(ok, meta_col)
        tab = jnp.where(live[None], -slopes[:, None, None] * d_eff[None], MASK_VALUE)
        tables.append(jnp.where(sink_col[None], sink.astype(F32)[:, None, None], tab))
    bias = jnp.stack(tables, axis=0).reshape(3, groups, GROUP * ATT_BLOCK, 4 * ATT_BLOCK)
    lane = jnp.arange(LANE_TILE)[None, :]
    shift = jnp.where(lane < N_META, jnp.repeat(slopes * ATT_BLOCK, ATT_BLOCK)[:, None], 0.0)
    return bias, shift.reshape(groups, GROUP * ATT_BLOCK, LANE_TILE)


def _attention(qvk_r, kt_r, qvk_m, kt_m, sink, *, b, s, n_heads):
    groups = n_heads // GROUP
    nblk = s // ATT_BLOCK
    qd = n_heads * HEAD_DIM
    width = qvk_r.shape[1]
    vkw = groups * 2 * HEAD_DIM
    vkb = qd // vkw
    bias, shift = _attention_tables(n_heads, sink)

    qb = math.gcd(ATT_QBLOCKS, nblk)
    steps = nblk // qb
    prev = lambda i, j: i * nblk + jnp.maximum(qb * j - 1, 0)
    nxt = lambda i, j: i * nblk + jnp.minimum(qb * j + qb, nblk - 1)
    bias_blk = (None, groups, GROUP * ATT_BLOCK, 4 * ATT_BLOCK)
    return pl.pallas_call(
        functools.partial(_attn_kernel, groups=groups, qblocks=qb),
        grid=(b, steps),
        in_specs=[pl.BlockSpec((qb * ATT_BLOCK, width), lambda i, j: (i * steps + j, 0)),
                  pl.BlockSpec((ATT_BLOCK, vkw), lambda i, j: (prev(i, j), vkb)),
                  pl.BlockSpec((ATT_BLOCK, vkw), lambda i, j: (nxt(i, j), vkb)),
                  pl.BlockSpec((N_META, vkw), lambda i, j: (i, vkb)),
                  pl.BlockSpec((groups * HEAD_DIM, ATT_BLOCK), lambda i, j: (0, prev(i, j))),
                  pl.BlockSpec((groups * HEAD_DIM, qb * ATT_BLOCK), lambda i, j: (0, i * steps + j)),
                  pl.BlockSpec((groups * HEAD_DIM, ATT_BLOCK), lambda i, j: (0, nxt(i, j))),
                  pl.BlockSpec((None, groups * HEAD_DIM, LANE_TILE), lambda i, j: (i, 0, 0)),
                  pl.BlockSpec(bias_blk, lambda i, j: (jnp.where(j == 0, 1, 0), 0, 0, 0)),
                  pl.BlockSpec(bias_blk, lambda i, j: (0, 0, 0, 0)),
                  pl.BlockSpec(bias_blk, lambda i, j: (jnp.where(j == steps - 1, 2, 0), 0, 0, 0)),
                  pl.BlockSpec((groups, GROUP * ATT_BLOCK, LANE_TILE), lambda i, j: (0, 0, 0))],
        out_specs=pl.BlockSpec((qb * ATT_BLOCK, qd), lambda i, j: (i * steps + j, 0)),
        out_shape=jax.ShapeDtypeStruct((b * s, qd), BF16),
        compiler_params=_cparams(("parallel", "arbitrary")),
        name="window_attention",
    )(qvk_r, qvk_r, qvk_r, qvk_m, kt_r, kt_r, kt_r, kt_m, bias, bias, bias, shift)


def _fft_split(s):
    n2 = 128 if s >= 1024 else 32
    return (2 * s) // n2, n2


def _fused_conv_vmem_bytes(s, n1, n2):
    lane_bytes = LANE_TILE * 4
    spectrum = 5 * n1 * n2 * lane_bytes
    tables = 2 * n2 * 2 * n1 * 2 * (n1 // 2 + 8) * 2
    sequence = (2 * s + 2 * 8 * n2 + s + N_META + 16) * lane_bytes
    blocks = 3 * 2 * 2 * (s + N_META) * LANE_TILE * 2
    return spectrum + tables + sequence + blocks


def _hyena_conv(xr, xm, col_off, conv_w, conv_b, kre, kim, order, tab, gr, gm, gate_off, tail, *, dims):
    b, s, d, n1, n2 = dims
    if _fused_conv_vmem_bytes(s, n1, n2) <= (VMEM_LIMIT * 7) // 8:
        outs = _fused_conv(xr, xm, col_off, conv_w, conv_b, kre, kim, order, tab, gr, gm, gate_off, tail,
                           b=b, s=s, d=d, n1=n1, n2=n2)
        return outs[0], outs[1]
    kw = dict(b=b, s=s, d=d, n1=n1, n2=n2, dt=LANE_TILE)
    if col_off is not None:
        are, aim, ut = _fft_in(xr, xm, col_off, conv_w, conv_b, tab, **kw)
    else:
        are, aim, ut = _fft_in(xr, xm, 0, None, None, tab, **kw)
    bre, bim = _fft_mid(are, aim, kre, kim, order, tab, dt=min(d, FFT_MID_LANES), slabs=FUSED_SLABS)
    return _fft_out(bre, bim, tab, gr, gm, gate_off, conv_w, conv_b, ut, tail, **kw)


def _hyena_layer(streams, fp, g_mix, w_in, conv_w, conv_b, skip, tm):
    outs = []
    d = w_in.shape[0]
    for st in streams:
        b, s = st["b"], st["s"]
        n1, n2 = _fft_split(s)
        seq_len = s + N_META
        n = 2 * s
        tab = _fft_tables(n1, n2)
        r = jnp.arange(n, dtype=jnp.int32)
        tr = min(s, 512)
        circ = _filter_rows(jnp.where(r < s, r, n - r), s // tr, tr, seq_len, fp, d)
        circ = circ.at[:, 0, :].add(skip.astype(F32))
        kre, kim = _filter_spectrum(circ, tab, n1=n1, n2=n2, dt=min(d, FFT_MID_LANES), slabs=FUSED_SLABS)
        a = jnp.arange(16, dtype=jnp.int32)
        tail = _filter_rows(jnp.concatenate([s + a, s - a, s - a, s + a]), 1, 32, seq_len, fp, d)
        tail = jnp.concatenate([tail[:, 0:16], tail[:, 32:48], tail[:, 48:64], tail[:, 16:32]], axis=1)
        pr = _norm_matmul(st["hr"], g_mix, w_in, 2 * tm)
        pm = _norm_matmul(st["hm"], g_mix, w_in, st["hm"].shape[0])
        dims = (b, s, d, n1, n2)
        z1r, z1m = _hyena_conv(pr, pm, 0, conv_w, conv_b, kre, kim, 0, tab, pr, pm, d, tail[0], dims=dims)
        z2r, z2m = _hyena_conv(z1r, z1m, None, conv_w, conv_b, kre, kim, 1, tab, pr, pm, 2 * d, tail[1],
                               dims=dims)
        outs.append((z2r, z2m))
    return outs


def _attention_weights(w_qkv, n_heads):
    groups = n_heads // GROUP
    qd = n_heads * HEAD_DIM
    kd = groups * HEAD_DIM
    w_q = w_qkv[:, :qd] * (HEAD_DIM ** -0.5)
    w_k = w_qkv[:, qd:qd + kd]
    w_v = w_qkv[:, qd + kd:]
    d = w_qkv.shape[0]
    w_vk = jnp.stack([w_v.reshape(d, groups, HEAD_DIM), w_k.reshape(d, groups, HEAD_DIM)], axis=2)
    w_rows = jnp.concatenate([w_q, w_vk.reshape(d, 2 * kd)], axis=1).astype(BF16)
    return w_rows, w_k.T.astype(BF16)


def _meta_keys_transposed(qvk_m, b, n_heads):
    groups = n_heads // GROUP
    qd = n_heads * HEAD_DIM
    k_m = qvk_m[:, qd:].reshape(b, N_META, groups, 2, HEAD_DIM)[:, :, :, 1, :]
    kt = jnp.transpose(k_m.reshape(b, N_META, groups * HEAD_DIM), (0, 2, 1))
    return jnp.pad(kt, ((0, 0), (0, 0), (0, LANE_TILE - N_META)))


def _encoder_pair(x_prompt, x_sample, meta_tokens, norm_mix, norm_mlp, norm_final,
                  hy_w_in, hy_conv_w, hy_conv_b, fps, hy_skip, hy_w_out, hy_b_out,
                  at_w_qkv, at_sink, at_w_o, mlp_w1, mlp_w2, *, n_heads, tm):
    d = x_prompt.shape[-1]
    streams = []
    for x in (x_prompt, x_sample):
        b, s, _ = x.shape
        streams.append(dict(b=b, s=s, hr=x.reshape(b * s, d),
                            hm=jnp.tile(meta_tokens.astype(F32), (b, 1))))
    zeros_d = jnp.zeros((d,), F32)

    conv_w = jnp.pad(hy_conv_w[0], ((0, 5), (0, 0)))
    conv_b = hy_conv_b[0][None, :]
    zs = _hyena_layer(streams, fps[0], norm_mix[0], hy_w_in[0].astype(BF16), conv_w, conv_b, hy_skip[0], tm)
    w_out = hy_w_out[0].astype(BF16)
    w1 = [w.astype(BF16) for w in mlp_w1]
    w2 = [w.astype(BF16) for w in mlp_w2]
    for st, (zr, zm) in zip(streams, zs):
        st["hr"] = _mixer_out_mlp(st["hr"], zr, w_out, hy_b_out[0], norm_mlp[0], w1[0], w2[0], zeros_d, 2 * tm, False)
        st["hm"] = _mixer_out_mlp(st["hm"], zm, w_out, hy_b_out[0], norm_mlp[0], w1[0], w2[0], zeros_d,
                                  st["hm"].shape[0], False)

    w_rows, w_kt = _attention_weights(at_w_qkv[0], n_heads)
    w_o = at_w_o[0].astype(BF16)
    outs = []
    for st in streams:
        qvk_r, kt_r = _norm_matmul(st["hr"], norm_mix[1], w_rows, 2 * tm, wt=w_kt)
        qvk_m = _norm_matmul(st["hm"], norm_mix[1], w_rows, st["hm"].shape[0])
        kt_m = _meta_keys_transposed(qvk_m, st["b"], n_heads)
        att = _attention(qvk_r, kt_r, qvk_m, kt_m, at_sink[0], b=st["b"], s=st["s"], n_heads=n_heads)
        y = _mixer_out_mlp(st["hr"], att, w_o, zeros_d, norm_mlp[1], w1[1], w2[1], norm_final, 2 * tm, True)
        outs.append(y.reshape(st["b"], st["s"], d))
    return tuple(outs)


def kernel(x_prompt, x_sample, meta_tokens, norm_mix, norm_mlp, norm_final, hy_w_in, hy_conv_w, hy_conv_b,
           hy_f_w1, hy_f_b1, hy_f_w2, hy_f_b2, hy_f_w3, hy_f_b3, hy_f_wout, hy_f_freq, hy_skip, hy_w_out,
           hy_b_out, at_w_qkv, at_sink, at_w_o, mlp_w1, mlp_w2):
    fps = [dict(w1=hy_f_w1[j], b1=hy_f_b1[j], w2=hy_f_w2[j], b2=hy_f_b2[j], w3=hy_f_w3[j], b3=hy_f_b3[j],
                wout=hy_f_wout[j], freq=hy_f_freq[j]) for j in range(hy_f_w1.shape[0])]
    n_heads = at_sink.shape[1]
    return _encoder_pair(x_prompt, x_sample, meta_tokens, norm_mix, norm_mlp, norm_final,
                         hy_w_in, hy_conv_w, hy_conv_b, fps, hy_skip, hy_w_out, hy_b_out,
                         at_w_qkv, at_sink, at_w_o, mlp_w1, mlp_w2, n_heads=n_heads, tm=512)
```
